```python
import jax, jax.numpy as jnp
from jax import lax
import numpy as np

D_MODEL = 1024
BATCH = 8
SEQ = 4096
DEPTH = 2

D_FF = 2816
GDN_HEADS = 8
GDN_DK = 128
GDN_DV = 128
GDN_CONV = 4
CHUNK = 64
CNV_CH = 1024
CNV_K = 31
W_Q = GDN_HEADS * GDN_DK
W_K = GDN_HEADS * GDN_DK
W_V = GDN_HEADS * GDN_DV
W_Z = GDN_HEADS * GDN_DV
W_BETA = GDN_HEADS
W_A = GDN_HEADS
W_GLU = 2 * CNV_CH
W_GATE = 2 * D_MODEL
SPLITS = [int(s) for s in np.cumsum([W_Q + W_K + W_V, W_Z, W_BETA, W_A, W_GLU])]
P_IN = W_Q + W_K + W_V + W_Z + W_BETA + W_A + W_GLU + W_GATE
RMS_EPS = 1e-6
LN_EPS = 1e-5

kernel_name = "hybrid_gdn_conformer_macaron_sandwich"


def rmsnorm(x, w):
    xf = x.astype(jnp.float32)
    y = xf * lax.rsqrt(jnp.mean(xf * xf, axis=-1, keepdims=True) + RMS_EPS)
    return (y * w.astype(jnp.float32)).astype(x.dtype)


def layernorm(x, g, b):
    xf = x.astype(jnp.float32)
    mu = jnp.mean(xf, axis=-1, keepdims=True)
    var = jnp.mean(jnp.square(xf - mu), axis=-1, keepdims=True)
    y = (xf - mu) * lax.rsqrt(var + LN_EPS)
    return (y * g.astype(jnp.float32) + b.astype(jnp.float32)).astype(x.dtype)


def l2norm(x):
    return x * lax.rsqrt(jnp.sum(x * x, axis=-1, keepdims=True) + 1e-6)


def causal_dwconv(x, w):
    k, c = w.shape
    return lax.conv_general_dilated(
        x, w[:, None, :].astype(x.dtype), window_strides=(1,), padding=[(k - 1, 0)],
        dimension_numbers=("NWC", "WIO", "NWC"), feature_group_count=c)


def swiglu_ffn(h, w_in, w_out):
    gate, up = jnp.split(h @ w_in, 2, axis=-1)
    return (jax.nn.silu(gate) * up) @ w_out


def chunk_gated_delta_rule(q, k, v, g, beta):
    b, l, h, dk = q.shape
    dv = v.shape[-1]
    n = l // CHUNK

    def to_chunks(t):
        return t.reshape(b, n, CHUNK, h, -1).transpose(0, 3, 1, 2, 4)

    q, k, v = to_chunks(q), to_chunks(k), to_chunks(v)
    g = g.reshape(b, n, CHUNK, h).transpose(0, 3, 1, 2)
    beta = beta.reshape(b, n, CHUNK, h).transpose(0, 3, 1, 2)
    G = jnp.cumsum(g, axis=-1)
    causal = jnp.tril(jnp.ones((CHUNK, CHUNK), dtype=bool))
    strict = jnp.tril(jnp.ones((CHUNK, CHUNK), dtype=bool), k=-1)
    diff = G[..., :, None] - G[..., None, :]
    decay = jnp.where(causal, jnp.exp(jnp.where(causal, diff, 0.0)), 0.0)
    kk = jnp.einsum("bhncd,bhnsd->bhncs", k, k)
    a_mat = jnp.where(strict, kk * decay * beta[..., :, None], 0.0)
    lhs = jnp.eye(CHUNK, dtype=jnp.float32) + a_mat
    rhs = jnp.concatenate([v * beta[..., None], k * (beta * jnp.exp(G))[..., None]], axis=-1)
    sol = lax.linalg.triangular_solve(lhs, rhs, left_side=True, lower=True, unit_diagonal=True)
    u, w = sol[..., :dv], sol[..., dv:]
    qk = jnp.einsum("bhncd,bhnsd->bhncs", q, k) * decay
    q_dec = q * jnp.exp(G)[..., None]
    k_dec = k * jnp.exp(G[..., -1:] - G)[..., None]
    chunk_decay = jnp.exp(G[..., -1])

    xs = tuple(jnp.moveaxis(t, 2, 0) for t in (q_dec, k_dec, u, w, qk, chunk_decay))

    def step(state, inp):
        q_c, k_c, u_c, w_c, qk_c, d_c = inp
        v_new = u_c - jnp.einsum("bhck,bhkv->bhcv", w_c, state)
        o_c = jnp.einsum("bhck,bhkv->bhcv", q_c, state) + jnp.einsum("bhcs,bhsv->bhcv", qk_c, v_new)
        state = state * d_c[..., None, None] + jnp.einsum("bhck,bhcv->bhkv", k_c, v_new)
        return state, o_c

    s0 = jnp.zeros((b, h, dk, dv), jnp.float32)
    _, o = lax.scan(step, s0, xs)
    return o.transpose(1, 0, 3, 2, 4).reshape(b, l, h, dv)


def gated_deltanet(qkv, z, beta_logit, a_logit, conv_w, a_log, dt_bias, norm_w, w_o):
    b, l, _ = qkv.shape
    qkv = jax.nn.silu(causal_dwconv(qkv, conv_w))
    q, k, v = jnp.split(qkv.astype(jnp.float32), [W_Q, W_Q + W_K], axis=-1)
    q = l2norm(q.reshape(b, l, GDN_HEADS, GDN_DK)) * (GDN_DK ** -0.5)
    k = l2norm(k.reshape(b, l, GDN_HEADS, GDN_DK))
    v = v.reshape(b, l, GDN_HEADS, GDN_DV)
    beta = jax.nn.sigmoid(beta_logit.astype(jnp.float32))
    g = -jnp.exp(a_log.astype(jnp.float32)) * jax.nn.softplus(
        a_logit.astype(jnp.float32) + dt_bias.astype(jnp.float32))
    o = chunk_gated_delta_rule(q, k, v, g, beta)
    zf = z.astype(jnp.float32).reshape(b, l, GDN_HEADS, GDN_DV)
    o = rmsnorm(o, norm_w) * jax.nn.silu(zf)
    return o.reshape(b, l, GDN_HEADS * GDN_DV).astype(qkv.dtype) @ w_o


def conformer_conv(glu_in, pw1_b, dw_w, dw_b, ln_g, ln_b, w_o, b_o):
    a, gate = jnp.split(glu_in + pw1_b, 2, axis=-1)
    h = a * jax.nn.sigmoid(gate)
    h = causal_dwconv(h, dw_w) + dw_b
    h = jax.nn.silu(layernorm(h, ln_g, ln_b))
    return h @ w_o + b_o


def _fwd_setup_inputs(seed: int = 0) -> dict:
    key = jax.random.key(seed)
    ks = iter(jax.random.split(key, 40))

    def nrm(shape, scale):
        return jax.random.normal(next(ks), shape, jnp.float32) * scale

    def gain(shape):
        return 1.0 + 0.02 * jax.random.normal(next(ks), shape, jnp.float32)

    L = DEPTH
    inp = {}
    inp["x"] = nrm((BATCH, SEQ, D_MODEL), 1.0)
    inp["ffn1_norm_pre"] = gain((L, D_MODEL))
    inp["ffn1_norm_post"] = gain((L, D_MODEL))
    inp["ffn1_w_in"] = nrm((L, D_MODEL, 2 * D_FF), D_MODEL ** -0.5)
    inp["ffn1_w_out"] = nrm((L, D_FF, D_MODEL), D_FF ** -0.5)
    inp["mix_norm_pre"] = gain((L, D_MODEL))
    inp["mix_norm_post"] = gain((L, D_MODEL))
    inp["mix_w_in"] = nrm((L, D_MODEL, P_IN), D_MODEL ** -0.5)
    inp["gdn_conv_w"] = nrm((L, GDN_CONV, W_Q + W_K + W_V), GDN_CONV ** -0.5)
    inp["gdn_a_log"] = jnp.log(jax.random.uniform(next(ks), (L, GDN_HEADS), jnp.float32, 1.0, 16.0))
    dt = jnp.exp(jax.random.uniform(next(ks), (L, GDN_HEADS), jnp.float32, np.log(1e-3), np.log(1e-1)))
    inp["gdn_dt_bias"] = dt + jnp.log(-jnp.expm1(-dt))
    inp["gdn_norm_w"] = gain((L, GDN_DV))
    inp["gdn_w_o"] = nrm((L, GDN_HEADS * GDN_DV, D_MODEL), (GDN_HEADS * GDN_DV) ** -0.5)
    inp["cnv_pw1_b"] = nrm((L, W_GLU), 0.02)
    inp["cnv_dw_w"] = nrm((L, CNV_K, CNV_CH), CNV_K ** -0.5)
    inp["cnv_dw_b"] = nrm((L, CNV_CH), 0.02)
    inp["cnv_ln_g"] = gain((L, CNV_CH))
    inp["cnv_ln_b"] = nrm((L, CNV_CH), 0.02)
    inp["cnv_w_o"] = nrm((L, CNV_CH, D_MODEL), CNV_CH ** -0.5)
    inp["cnv_b_o"] = nrm((L, D_MODEL), 0.02)
    inp["mix_w_out"] = nrm((L, D_MODEL, D_MODEL), D_MODEL ** -0.5)
    inp["ffn2_norm_pre"] = gain((L, D_MODEL))
    inp["ffn2_norm_post"] = gain((L, D_MODEL))
    inp["ffn2_w_in"] = nrm((L, D_MODEL, 2 * D_FF), D_MODEL ** -0.5)
    inp["ffn2_w_out"] = nrm((L, D_FF, D_MODEL), D_FF ** -0.5)
    return inp


def _fwd_reference(x, ffn1_norm_pre, ffn1_norm_post, ffn1_w_in, ffn1_w_out,
              mix_norm_pre, mix_norm_post, mix_w_in,
              gdn_conv_w, gdn_a_log, gdn_dt_bias, gdn_norm_w, gdn_w_o,
              cnv_pw1_b, cnv_dw_w, cnv_dw_b, cnv_ln_g, cnv_ln_b, cnv_w_o, cnv_b_o,
              mix_w_out,
              ffn2_norm_pre, ffn2_norm_post, ffn2_w_in, ffn2_w_out):
    for i in range(DEPTH):
        f = swiglu_ffn(rmsnorm(x, ffn1_norm_pre[i]), ffn1_w_in[i], ffn1_w_out[i])
        x = x + 0.5 * rmsnorm(f, ffn1_norm_post[i])

        h = rmsnorm(x, mix_norm_pre[i])
        p = h @ mix_w_in[i]
        qkv, z, beta_logit, a_logit, glu_in, gates = jnp.split(p, SPLITS, axis=-1)
        y_a = gated_deltanet(qkv, z, beta_logit, a_logit, gdn_conv_w[i], gdn_a_log[i],
                             gdn_dt_bias[i], gdn_norm_w[i], gdn_w_o[i])
        y_b = conformer_conv(glu_in, cnv_pw1_b[i], cnv_dw_w[i], cnv_dw_b[i],
                             cnv_ln_g[i], cnv_ln_b[i], cnv_w_o[i], cnv_b_o[i])
        g_a, g_b = jnp.split(jax.nn.sigmoid(gates), 2, axis=-1)
        y = (g_a * y_a + g_b * y_b) @ mix_w_out[i]
        x = x + rmsnorm(y, mix_norm_post[i])

        f = swiglu_ffn(rmsnorm(x, ffn2_norm_pre[i]), ffn2_w_in[i], ffn2_w_out[i])
        x = x + 0.5 * rmsnorm(f, ffn2_norm_post[i])
    return x


import jax as _jax
import jax.numpy as _jnp

TWIN_FORMAT = 'train_step'
FWD_PARAMS = ['x', 'ffn1_norm_pre', 'ffn1_norm_post', 'ffn1_w_in', 'ffn1_w_out', 'mix_norm_pre', 'mix_norm_post', 'mix_w_in', 'gdn_conv_w', 'gdn_a_log', 'gdn_dt_bias', 'gdn_norm_w', 'gdn_w_o', 'cnv_pw1_b', 'cnv_dw_w', 'cnv_dw_b', 'cnv_ln_g', 'cnv_ln_b', 'cnv_w_o', 'cnv_b_o', 'mix_w_out', 'ffn2_norm_pre', 'ffn2_norm_post', 'ffn2_w_in', 'ffn2_w_out']
TWIN_WEIGHTS = ['ffn1_norm_pre', 'ffn1_norm_post', 'ffn1_w_in', 'ffn1_w_out', 'mix_norm_pre', 'mix_norm_post', 'mix_w_in', 'gdn_conv_w', 'gdn_a_log', 'gdn_dt_bias', 'gdn_norm_w', 'gdn_w_o', 'cnv_pw1_b', 'cnv_dw_w', 'cnv_dw_b', 'cnv_ln_g', 'cnv_ln_b', 'cnv_w_o', 'cnv_b_o', 'mix_w_out', 'ffn2_norm_pre', 'ffn2_norm_post', 'ffn2_w_in', 'ffn2_w_out']
TWIN_DIFF_INPUT = 'x'
TWIN_INPUTS = ['x', 'ffn1_norm_pre', 'ffn1_norm_post', 'ffn1_w_in', 'ffn1_w_out', 'mix_norm_pre', 'mix_norm_post', 'mix_w_in', 'gdn_conv_w', 'gdn_a_log', 'gdn_dt_bias', 'gdn_norm_w', 'gdn_w_o', 'cnv_pw1_b', 'cnv_dw_w', 'cnv_dw_b', 'cnv_ln_g', 'cnv_ln_b', 'cnv_w_o', 'cnv_b_o', 'mix_w_out', 'ffn2_norm_pre', 'ffn2_norm_post', 'ffn2_w_in', 'ffn2_w_out', 'loss_target', 'm_ffn1_norm_pre', 'm_ffn1_norm_post', 'm_ffn1_w_in', 'm_ffn1_w_out', 'm_mix_norm_pre', 'm_mix_norm_post', 'm_mix_w_in', 'm_gdn_conv_w', 'm_gdn_a_log', 'm_gdn_dt_bias', 'm_gdn_norm_w', 'm_gdn_w_o', 'm_cnv_pw1_b', 'm_cnv_dw_w', 'm_cnv_dw_b', 'm_cnv_ln_g', 'm_cnv_ln_b', 'm_cnv_w_o', 'm_cnv_b_o', 'm_mix_w_out', 'm_ffn2_norm_pre', 'm_ffn2_norm_post', 'm_ffn2_w_in', 'm_ffn2_w_out', 'v_ffn1_norm_pre', 'v_ffn1_norm_post', 'v_ffn1_w_in', 'v_ffn1_w_out', 'v_mix_norm_pre', 'v_mix_norm_post', 'v_mix_w_in', 'v_gdn_conv_w', 'v_gdn_a_log', 'v_gdn_dt_bias', 'v_gdn_norm_w', 'v_gdn_w_o', 'v_cnv_pw1_b', 'v_cnv_dw_w', 'v_cnv_dw_b', 'v_cnv_ln_g', 'v_cnv_ln_b', 'v_cnv_w_o', 'v_cnv_b_o', 'v_mix_w_out', 'v_ffn2_norm_pre', 'v_ffn2_norm_post', 'v_ffn2_w_in', 'v_ffn2_w_out']
TWIN_OUTPUTS = ['loss', 'grad_x', 'grad_ffn1_norm_pre', 'grad_ffn1_norm_post', 'grad_ffn1_w_in', 'grad_ffn1_w_out', 'grad_mix_norm_pre', 'grad_mix_norm_post', 'grad_mix_w_in', 'grad_gdn_conv_w', 'grad_gdn_a_log', 'grad_gdn_dt_bias', 'grad_gdn_norm_w', 'grad_gdn_w_o', 'grad_cnv_pw1_b', 'grad_cnv_dw_w', 'grad_cnv_dw_b', 'grad_cnv_ln_g', 'grad_cnv_ln_b', 'grad_cnv_w_o', 'grad_cnv_b_o', 'grad_mix_w_out', 'grad_ffn2_norm_pre', 'grad_ffn2_norm_post', 'grad_ffn2_w_in', 'grad_ffn2_w_out', 'delta_ffn1_norm_pre', 'delta_ffn1_norm_post', 'delta_ffn1_w_in', 'delta_ffn1_w_out', 'delta_mix_norm_pre', 'delta_mix_norm_post', 'delta_mix_w_in', 'delta_gdn_conv_w', 'delta_gdn_a_log', 'delta_gdn_dt_bias', 'delta_gdn_norm_w', 'delta_gdn_w_o', 'delta_cnv_pw1_b', 'delta_cnv_dw_w', 'delta_cnv_dw_b', 'delta_cnv_ln_g', 'delta_cnv_ln_b', 'delta_cnv_w_o', 'delta_cnv_b_o', 'delta_mix_w_out', 'delta_ffn2_norm_pre', 'delta_ffn2_norm_post', 'delta_ffn2_w_in', 'delta_ffn2_w_out', 'new_m_ffn1_norm_pre', 'new_m_ffn1_norm_post', 'new_m_ffn1_w_in', 'new_m_ffn1_w_out', 'new_m_mix_norm_pre', 'new_m_mix_norm_post', 'new_m_mix_w_in', 'new_m_gdn_conv_w', 'new_m_gdn_a_log', 'new_m_gdn_dt_bias', 'new_m_gdn_norm_w', 'new_m_gdn_w_o', 'new_m_cnv_pw1_b', 'new_m_cnv_dw_w', 'new_m_cnv_dw_b', 'new_m_cnv_ln_g', 'new_m_cnv_ln_b', 'new_m_cnv_w_o', 'new_m_cnv_b_o', 'new_m_mix_w_out', 'new_m_ffn2_norm_pre', 'new_m_ffn2_norm_post', 'new_m_ffn2_w_in', 'new_m_ffn2_w_out', 'new_v_ffn1_norm_pre', 'new_v_ffn1_norm_post', 'new_v_ffn1_w_in', 'new_v_ffn1_w_out', 'new_v_mix_norm_pre', 'new_v_mix_norm_post', 'new_v_mix_w_in', 'new_v_gdn_conv_w', 'new_v_gdn_a_log', 'new_v_gdn_dt_bias', 'new_v_gdn_norm_w', 'new_v_gdn_w_o', 'new_v_cnv_pw1_b', 'new_v_cnv_dw_w', 'new_v_cnv_dw_b', 'new_v_cnv_ln_g', 'new_v_cnv_ln_b', 'new_v_cnv_w_o', 'new_v_cnv_b_o', 'new_v_mix_w_out', 'new_v_ffn2_norm_pre', 'new_v_ffn2_norm_post', 'new_v_ffn2_w_in', 'new_v_ffn2_w_out']
TWIN_LEAF_KINDS = {'loss': 'loss', 'grad_x': 'grad_x', 'grad_ffn1_norm_pre': 'grad_w', 'grad_ffn1_norm_post': 'grad_w', 'grad_ffn1_w_in': 'grad_w', 'grad_ffn1_w_out': 'grad_w', 'grad_mix_norm_pre': 'grad_w', 'grad_mix_norm_post': 'grad_w', 'grad_mix_w_in': 'grad_w', 'grad_gdn_conv_w': 'grad_w', 'grad_gdn_a_log': 'grad_w', 'grad_gdn_dt_bias': 'grad_w', 'grad_gdn_norm_w': 'grad_w', 'grad_gdn_w_o': 'grad_w', 'grad_cnv_pw1_b': 'grad_w', 'grad_cnv_dw_w': 'grad_w', 'grad_cnv_dw_b': 'grad_w', 'grad_cnv_ln_g': 'grad_w', 'grad_cnv_ln_b': 'grad_w', 'grad_cnv_w_o': 'grad_w', 'grad_cnv_b_o': 'grad_w', 'grad_mix_w_out': 'grad_w', 'grad_ffn2_norm_pre': 'grad_w', 'grad_ffn2_norm_post': 'grad_w', 'grad_ffn2_w_in': 'grad_w', 'grad_ffn2_w_out': 'grad_w', 'delta_ffn1_norm_pre': 'delta_w', 'delta_ffn1_norm_post': 'delta_w', 'delta_ffn1_w_in': 'delta_w', 'delta_ffn1_w_out': 'delta_w', 'delta_mix_norm_pre': 'delta_w', 'delta_mix_norm_post': 'delta_w', 'delta_mix_w_in': 'delta_w', 'delta_gdn_conv_w': 'delta_w', 'delta_gdn_a_log': 'delta_w', 'delta_gdn_dt_bias': 'delta_w', 'delta_gdn_norm_w': 'delta_w', 'delta_gdn_w_o': 'delta_w', 'delta_cnv_pw1_b': 'delta_w', 'delta_cnv_dw_w': 'delta_w', 'delta_cnv_dw_b': 'delta_w', 'delta_cnv_ln_g': 'delta_w', 'delta_cnv_ln_b': 'delta_w', 'delta_cnv_w_o': 'delta_w', 'delta_cnv_b_o': 'delta_w', 'delta_mix_w_out': 'delta_w', 'delta_ffn2_norm_pre': 'delta_w', 'delta_ffn2_norm_post': 'delta_w', 'delta_ffn2_w_in': 'delta_w', 'delta_ffn2_w_out': 'delta_w', 'new_m_ffn1_norm_pre': 'new_m', 'new_m_ffn1_norm_post': 'new_m', 'new_m_ffn1_w_in': 'new_m', 'new_m_ffn1_w_out': 'new_m', 'new_m_mix_norm_pre': 'new_m', 'new_m_mix_norm_post': 'new_m', 'new_m_mix_w_in': 'new_m', 'new_m_gdn_conv_w': 'new_m', 'new_m_gdn_a_log': 'new_m', 'new_m_gdn_dt_bias': 'new_m', 'new_m_gdn_norm_w': 'new_m', 'new_m_gdn_w_o': 'new_m', 'new_m_cnv_pw1_b': 'new_m', 'new_m_cnv_dw_w': 'new_m', 'new_m_cnv_dw_b': 'new_m', 'new_m_cnv_ln_g': 'new_m', 'new_m_cnv_ln_b': 'new_m', 'new_m_cnv_w_o': 'new_m', 'new_m_cnv_b_o': 'new_m', 'new_m_mix_w_out': 'new_m', 'new_m_ffn2_norm_pre': 'new_m', 'new_m_ffn2_norm_post': 'new_m', 'new_m_ffn2_w_in': 'new_m', 'new_m_ffn2_w_out': 'new_m', 'new_v_ffn1_norm_pre': 'new_v', 'new_v_ffn1_norm_post': 'new_v', 'new_v_ffn1_w_in': 'new_v', 'new_v_ffn1_w_out': 'new_v', 'new_v_mix_norm_pre': 'new_v', 'new_v_mix_norm_post': 'new_v', 'new_v_mix_w_in': 'new_v', 'new_v_gdn_conv_w': 'new_v', 'new_v_gdn_a_log': 'new_v', 'new_v_gdn_dt_bias': 'new_v', 'new_v_gdn_norm_w': 'new_v', 'new_v_gdn_w_o': 'new_v', 'new_v_cnv_pw1_b': 'new_v', 'new_v_cnv_dw_w': 'new_v', 'new_v_cnv_dw_b': 'new_v', 'new_v_cnv_ln_g': 'new_v', 'new_v_cnv_ln_b': 'new_v', 'new_v_cnv_w_o': 'new_v', 'new_v_cnv_b_o': 'new_v', 'new_v_mix_w_out': 'new_v', 'new_v_ffn2_norm_pre': 'new_v', 'new_v_ffn2_norm_post': 'new_v', 'new_v_ffn2_w_in': 'new_v', 'new_v_ffn2_w_out': 'new_v'}


def _forward(args):
    return _fwd_reference(*[args[k] for k in FWD_PARAMS])


def _output_shape():
    def fwd():
        inp = _fwd_setup_inputs(0)
        return _fwd_reference(*[inp[k] for k in FWD_PARAMS])
    out = _jax.eval_shape(fwd)
    return out.shape, out.dtype

N_MICROBATCH = 1
ADAM_LR = 0.001
ADAM_B1 = 0.9
ADAM_B2 = 0.999
ADAM_EPS = 1e-08
ADAM_WD = 0.01
ADAM_STEP = 10
PER_EXAMPLE_BATCH_AXIS = {'x': 0, 'loss_target': 0}
SHARED_INPUTS = []
_WEIGHT_DTYPES = {'ffn1_norm_pre': _jnp.float32, 'ffn1_norm_post': _jnp.float32, 'ffn1_w_in': _jnp.float32, 'ffn1_w_out': _jnp.float32, 'mix_norm_pre': _jnp.float32, 'mix_norm_post': _jnp.float32, 'mix_w_in': _jnp.float32, 'gdn_conv_w': _jnp.float32, 'gdn_a_log': _jnp.float32, 'gdn_dt_bias': _jnp.float32, 'gdn_norm_w': _jnp.float32, 'gdn_w_o': _jnp.float32, 'cnv_pw1_b': _jnp.float32, 'cnv_dw_w': _jnp.float32, 'cnv_dw_b': _jnp.float32, 'cnv_ln_g': _jnp.float32, 'cnv_ln_b': _jnp.float32, 'cnv_w_o': _jnp.float32, 'cnv_b_o': _jnp.float32, 'mix_w_out': _jnp.float32, 'ffn2_norm_pre': _jnp.float32, 'ffn2_norm_post': _jnp.float32, 'ffn2_w_in': _jnp.float32, 'ffn2_w_out': _jnp.float32}
MOMENT_SCALE = {'ffn1_norm_pre': 8.183380e-01, 'ffn1_norm_post': 7.956737e+00, 'ffn1_w_in': 3.056252e-01, 'ffn1_w_out': 5.481581e-01, 'mix_norm_pre': 1.041229e+00, 'mix_norm_post': 3.238952e+01, 'mix_w_in': 3.767221e-01, 'gdn_conv_w': 5.733611e-01, 'gdn_a_log': 2.592002e+00, 'gdn_dt_bias': 2.469824e+00, 'gdn_norm_w': 4.141781e+00, 'gdn_w_o': 1.401788e+00, 'cnv_pw1_b': 3.899197e+00, 'cnv_dw_w': 8.116051e-01, 'cnv_dw_b': 9.927374e+00, 'cnv_ln_g': 3.867416e+00, 'cnv_ln_b': 5.623856e+00, 'cnv_w_o': 2.303940e+00, 'cnv_b_o': 1.125695e+01, 'mix_w_out': 2.762107e+00, 'ffn2_norm_pre': 7.654767e-01, 'ffn2_norm_post': 8.017476e+00, 'ffn2_w_in': 3.061489e-01, 'ffn2_w_out': 6.149579e-01}


def _to_microbatches(a, axis):
    t = _jnp.moveaxis(a, axis, 0)
    t = t.reshape((N_MICROBATCH, t.shape[0] // N_MICROBATCH) + t.shape[1:])
    return _jnp.moveaxis(t, 1, axis + 1)


def setup_inputs(seed: int = 0) -> dict:
    inp = _fwd_setup_inputs(seed)
    key = _jax.random.fold_in(_jax.random.key(seed), 7919)
    shape, _ = _output_shape()
    out = dict(inp)
    out["loss_target"] = _jax.random.normal(_jax.random.fold_in(key, 0), shape, _jnp.float32)
    for i, name in enumerate(TWIN_WEIGHTS):
        w = inp[name].astype(_jnp.float32)
        if MOMENT_SCALE is None:
            s = _jnp.sqrt(_jnp.mean(_jnp.square(w)) + 1e-30)
        else:
            s = MOMENT_SCALE[name]
        km, kv = _jax.random.split(_jax.random.fold_in(key, i + 1))
        out[name] = w
        out["m_" + name] = s * _jax.random.normal(km, w.shape, _jnp.float32)
        out["v_" + name] = (s * s) * _jax.random.uniform(kv, w.shape, _jnp.float32, 0.5, 1.5)
    if N_MICROBATCH > 1:
        for name, axis in PER_EXAMPLE_BATCH_AXIS.items():
            out[name] = _to_microbatches(out[name], axis)
    return {'x': out['x'], 'ffn1_norm_pre': out['ffn1_norm_pre'], 'ffn1_norm_post': out['ffn1_norm_post'], 'ffn1_w_in': out['ffn1_w_in'], 'ffn1_w_out': out['ffn1_w_out'], 'mix_norm_pre': out['mix_norm_pre'], 'mix_norm_post': out['mix_norm_post'], 'mix_w_in': out['mix_w_in'], 'gdn_conv_w': out['gdn_conv_w'], 'gdn_a_log': out['gdn_a_log'], 'gdn_dt_bias': out['gdn_dt_bias'], 'gdn_norm_w': out['gdn_norm_w'], 'gdn_w_o': out['gdn_w_o'], 'cnv_pw1_b': out['cnv_pw1_b'], 'cnv_dw_w': out['cnv_dw_w'], 'cnv_dw_b': out['cnv_dw_b'], 'cnv_ln_g': out['cnv_ln_g'], 'cnv_ln_b': out['cnv_ln_b'], 'cnv_w_o': out['cnv_w_o'], 'cnv_b_o': out['cnv_b_o'], 'mix_w_out': out['mix_w_out'], 'ffn2_norm_pre': out['ffn2_norm_pre'], 'ffn2_norm_post': out['ffn2_norm_post'], 'ffn2_w_in': out['ffn2_w_in'], 'ffn2_w_out': out['ffn2_w_out'], 'loss_target': out['loss_target'], 'm_ffn1_norm_pre': out['m_ffn1_norm_pre'], 'm_ffn1_norm_post': out['m_ffn1_norm_post'], 'm_ffn1_w_in': out['m_ffn1_w_in'], 'm_ffn1_w_out': out['m_ffn1_w_out'], 'm_mix_norm_pre': out['m_mix_norm_pre'], 'm_mix_norm_post': out['m_mix_norm_post'], 'm_mix_w_in': out['m_mix_w_in'], 'm_gdn_conv_w': out['m_gdn_conv_w'], 'm_gdn_a_log': out['m_gdn_a_log'], 'm_gdn_dt_bias': out['m_gdn_dt_bias'], 'm_gdn_norm_w': out['m_gdn_norm_w'], 'm_gdn_w_o': out['m_gdn_w_o'], 'm_cnv_pw1_b': out['m_cnv_pw1_b'], 'm_cnv_dw_w': out['m_cnv_dw_w'], 'm_cnv_dw_b': out['m_cnv_dw_b'], 'm_cnv_ln_g': out['m_cnv_ln_g'], 'm_cnv_ln_b': out['m_cnv_ln_b'], 'm_cnv_w_o': out['m_cnv_w_o'], 'm_cnv_b_o': out['m_cnv_b_o'], 'm_mix_w_out': out['m_mix_w_out'], 'm_ffn2_norm_pre': out['m_ffn2_norm_pre'], 'm_ffn2_norm_post': out['m_ffn2_norm_post'], 'm_ffn2_w_in': out['m_ffn2_w_in'], 'm_ffn2_w_out': out['m_ffn2_w_out'], 'v_ffn1_norm_pre': out['v_ffn1_norm_pre'], 'v_ffn1_norm_post': out['v_ffn1_norm_post'], 'v_ffn1_w_in': out['v_ffn1_w_in'], 'v_ffn1_w_out': out['v_ffn1_w_out'], 'v_mix_norm_pre': out['v_mix_norm_pre'], 'v_mix_norm_post': out['v_mix_norm_post'], 'v_mix_w_in': out['v_mix_w_in'], 'v_gdn_conv_w': out['v_gdn_conv_w'], 'v_gdn_a_log': out['v_gdn_a_log'], 'v_gdn_dt_bias': out['v_gdn_dt_bias'], 'v_gdn_norm_w': out['v_gdn_norm_w'], 'v_gdn_w_o': out['v_gdn_w_o'], 'v_cnv_pw1_b': out['v_cnv_pw1_b'], 'v_cnv_dw_w': out['v_cnv_dw_w'], 'v_cnv_dw_b': out['v_cnv_dw_b'], 'v_cnv_ln_g': out['v_cnv_ln_g'], 'v_cnv_ln_b': out['v_cnv_ln_b'], 'v_cnv_w_o': out['v_cnv_w_o'], 'v_cnv_b_o': out['v_cnv_b_o'], 'v_mix_w_out': out['v_mix_w_out'], 'v_ffn2_norm_pre': out['v_ffn2_norm_pre'], 'v_ffn2_norm_post': out['v_ffn2_norm_post'], 'v_ffn2_w_in': out['v_ffn2_w_in'], 'v_ffn2_w_out': out['v_ffn2_w_out']}


def _loss(weights, diff, rest, loss_target):
    with _jax.named_scope("forward"):
        args = {**rest, TWIN_DIFF_INPUT: diff, **{k: w.astype(_WEIGHT_DTYPES[k]) for k, w in weights.items()}}
        y = _forward(args)
    with _jax.named_scope("loss_head"):
        err = _jnp.square(y.astype(_jnp.float32) - loss_target)
        return 0.5 * _jnp.sum(_jnp.mean(err, axis=-1)) if err.ndim else 0.5 * err


def _adamw(w, g, m, v):
    m = ADAM_B1 * m + (1.0 - ADAM_B1) * g
    v = ADAM_B2 * v + (1.0 - ADAM_B2) * _jnp.square(g)
    m_hat = m / (1.0 - ADAM_B1 ** ADAM_STEP)
    v_hat = v / (1.0 - ADAM_B2 ** ADAM_STEP)
    delta = -ADAM_LR * (m_hat / (_jnp.sqrt(v_hat) + ADAM_EPS) + ADAM_WD * w)
    return delta, m, v


def reference(x, ffn1_norm_pre, ffn1_norm_post, ffn1_w_in, ffn1_w_out, mix_norm_pre, mix_norm_post, mix_w_in, gdn_conv_w, gdn_a_log, gdn_dt_bias, gdn_norm_w, gdn_w_o, cnv_pw1_b, cnv_dw_w, cnv_dw_b, cnv_ln_g, cnv_ln_b, cnv_w_o, cnv_b_o, mix_w_out, ffn2_norm_pre, ffn2_norm_post, ffn2_w_in, ffn2_w_out, loss_target, m_ffn1_norm_pre, m_ffn1_norm_post, m_ffn1_w_in, m_ffn1_w_out, m_mix_norm_pre, m_mix_norm_post, m_mix_w_in, m_gdn_conv_w, m_gdn_a_log, m_gdn_dt_bias, m_gdn_norm_w, m_gdn_w_o, m_cnv_pw1_b, m_cnv_dw_w, m_cnv_dw_b, m_cnv_ln_g, m_cnv_ln_b, m_cnv_w_o, m_cnv_b_o, m_mix_w_out, m_ffn2_norm_pre, m_ffn2_norm_post, m_ffn2_w_in, m_ffn2_w_out, v_ffn1_norm_pre, v_ffn1_norm_post, v_ffn1_w_in, v_ffn1_w_out, v_mix_norm_pre, v_mix_norm_post, v_mix_w_in, v_gdn_conv_w, v_gdn_a_log, v_gdn_dt_bias, v_gdn_norm_w, v_gdn_w_o, v_cnv_pw1_b, v_cnv_dw_w, v_cnv_dw_b, v_cnv_ln_g, v_cnv_ln_b, v_cnv_w_o, v_cnv_b_o, v_mix_w_out, v_ffn2_norm_pre, v_ffn2_norm_post, v_ffn2_w_in, v_ffn2_w_out):
    given = dict(x=x, ffn1_norm_pre=ffn1_norm_pre, ffn1_norm_post=ffn1_norm_post, ffn1_w_in=ffn1_w_in, ffn1_w_out=ffn1_w_out, mix_norm_pre=mix_norm_pre, mix_norm_post=mix_norm_post, mix_w_in=mix_w_in, gdn_conv_w=gdn_conv_w, gdn_a_log=gdn_a_log, gdn_dt_bias=gdn_dt_bias, gdn_norm_w=gdn_norm_w, gdn_w_o=gdn_w_o, cnv_pw1_b=cnv_pw1_b, cnv_dw_w=cnv_dw_w, cnv_dw_b=cnv_dw_b, cnv_ln_g=cnv_ln_g, cnv_ln_b=cnv_ln_b, cnv_w_o=cnv_w_o, cnv_b_o=cnv_b_o, mix_w_out=mix_w_out, ffn2_norm_pre=ffn2_norm_pre, ffn2_norm_post=ffn2_norm_post, ffn2_w_in=ffn2_w_in, ffn2_w_out=ffn2_w_out, loss_target=loss_target, m_ffn1_norm_pre=m_ffn1_norm_pre, m_ffn1_norm_post=m_ffn1_norm_post, m_ffn1_w_in=m_ffn1_w_in, m_ffn1_w_out=m_ffn1_w_out, m_mix_norm_pre=m_mix_norm_pre, m_mix_norm_post=m_mix_norm_post, m_mix_w_in=m_mix_w_in, m_gdn_conv_w=m_gdn_conv_w, m_gdn_a_log=m_gdn_a_log, m_gdn_dt_bias=m_gdn_dt_bias, m_gdn_norm_w=m_gdn_norm_w, m_gdn_w_o=m_gdn_w_o, m_cnv_pw1_b=m_cnv_pw1_b, m_cnv_dw_w=m_cnv_dw_w, m_cnv_dw_b=m_cnv_dw_b, m_cnv_ln_g=m_cnv_ln_g, m_cnv_ln_b=m_cnv_ln_b, m_cnv_w_o=m_cnv_w_o, m_cnv_b_o=m_cnv_b_o, m_mix_w_out=m_mix_w_out, m_ffn2_norm_pre=m_ffn2_norm_pre, m_ffn2_norm_post=m_ffn2_norm_post, m_ffn2_w_in=m_ffn2_w_in, m_ffn2_w_out=m_ffn2_w_out, v_ffn1_norm_pre=v_ffn1_norm_pre, v_ffn1_norm_post=v_ffn1_norm_post, v_ffn1_w_in=v_ffn1_w_in, v_ffn1_w_out=v_ffn1_w_out, v_mix_norm_pre=v_mix_norm_pre, v_mix_norm_post=v_mix_norm_post, v_mix_w_in=v_mix_w_in, v_gdn_conv_w=v_gdn_conv_w, v_gdn_a_log=v_gdn_a_log, v_gdn_dt_bias=v_gdn_dt_bias, v_gdn_norm_w=v_gdn_norm_w, v_gdn_w_o=v_gdn_w_o, v_cnv_pw1_b=v_cnv_pw1_b, v_cnv_dw_w=v_cnv_dw_w, v_cnv_dw_b=v_cnv_dw_b, v_cnv_ln_g=v_cnv_ln_g, v_cnv_ln_b=v_cnv_ln_b, v_cnv_w_o=v_cnv_w_o, v_cnv_b_o=v_cnv_b_o, v_mix_w_out=v_mix_w_out, v_ffn2_norm_pre=v_ffn2_norm_pre, v_ffn2_norm_post=v_ffn2_norm_post, v_ffn2_w_in=v_ffn2_w_in, v_ffn2_w_out=v_ffn2_w_out)
    weights = {n: given[n] for n in TWIN_WEIGHTS}
    shared = {n: given[n] for n in SHARED_INPUTS}
    per_example = {n: given[n] for n in ['x']}
    grad_fn = _jax.value_and_grad(_loss, argnums=(0, 1))

    def one_microbatch(ex, loss_target):
        ex = dict(ex)
        diff = ex.pop(TWIN_DIFF_INPUT)
        return grad_fn(weights, diff, {**shared, **ex}, loss_target)

    if N_MICROBATCH == 1:
        loss, (grad_w, grad_x) = one_microbatch(per_example, given["loss_target"])
    else:
        def body(carry, xs):
            loss_sum, grad_sum = carry
            l_k, (gw_k, gx_k) = one_microbatch(xs[0], xs[1])
            with _jax.named_scope("update"):
                return (loss_sum + l_k, _jax.tree.map(_jnp.add, grad_sum, gw_k)), gx_k

        init = (_jnp.zeros((), _jnp.float32), _jax.tree.map(_jnp.zeros_like, weights))
        (loss, grad_w), grad_x = _jax.lax.scan(body, init, (per_example, given["loss_target"]))
    with _jax.named_scope("update"):
        delta_w, new_m, new_v = {}, {}, {}
        for n in TWIN_WEIGHTS:
            delta_w[n], new_m[n], new_v[n] = _adamw(weights[n], grad_w[n], given["m_" + n], given["v_" + n])
    return (loss, grad_x, *[grad_w[n] for n in TWIN_WEIGHTS], *[delta_w[n] for n in TWIN_WEIGHTS],
            *[new_m[n] for n in TWIN_WEIGHTS], *[new_v[n] for n in TWIN_WEIGHTS])
```

```python
import functools

import jax
import jax.numpy as jnp
from jax import lax
from jax.experimental import pallas as pl
from jax.experimental.pallas import tpu as pltpu

F32 = jnp.float32
BF16 = jnp.bfloat16

GDN_DK = 128
CHUNK = 64
GDN_CONV = 4
CNV_K = 31
RMS_EPS = 1e-6
LN_EPS = 1e-5
L2_EPS = 1e-6
ADAM_LR = 0.001
ADAM_B1 = 0.9
ADAM_B2 = 0.999
ADAM_EPS = 1e-08
ADAM_WD = 0.01
ADAM_STEP = 10

LANES = 128
SUB = 16
VMEM_LIMIT = 56 * 1024 * 1024
N_DEV = 8
MESH = pl.DeviceIdType.MESH
HI = lax.Precision.HIGHEST


def _cparams(sem=None, **kw):
    if sem is not None:
        kw["dimension_semantics"] = sem
    return pltpu.CompilerParams(vmem_limit_bytes=VMEM_LIMIT, **kw)


def _tile(dim, target):
    best = None
    for t in range(LANES, min(dim, target) + 1, LANES):
        if dim % t == 0:
            best = t
    return best if best is not None else dim


def _sigmoid(x):
    return 1.0 / (1.0 + jnp.exp(-x))


def _silu(x):
    return x * _sigmoid(x)


def _dsilu(x):
    s = _sigmoid(x)
    return s * (1.0 + x * (1.0 - s))


def _matmul(a, b, mode, out_dtype, name, tm=512, tn=512):
    if mode == "nn":
        (M, K), N = a.shape, b.shape[1]
    elif mode == "nt":
        (M, K), N = a.shape, b.shape[0]
    else:
        (K, M), N = a.shape, b.shape[1]
    if K > 4096:
        tm = 256
    tm, tn = _tile(M, tm), _tile(N, tn)
    if mode == "nn":
        a_spec = pl.BlockSpec((tm, K), lambda j, i: (i, 0))
        b_spec = pl.BlockSpec((K, tn), lambda j, i: (0, j))
        dn = (((1,), (0,)), ((), ()))
    elif mode == "nt":
        a_spec = pl.BlockSpec((tm, K), lambda j, i: (i, 0))
        b_spec = pl.BlockSpec((tn, K), lambda j, i: (j, 0))
        dn = (((1,), (1,)), ((), ()))
    else:
        a_spec = pl.BlockSpec((K, tm), lambda j, i: (0, i))
        b_spec = pl.BlockSpec((K, tn), lambda j, i: (0, j))
        dn = (((0,), (0,)), ((), ()))

    def body(a_ref, b_ref, o_ref):
        o_ref[...] = lax.dot_general(a_ref[...], b_ref[...], dn, preferred_element_type=F32).astype(out_dtype)

    return pl.pallas_call(
        body, name=name, grid=(N // tn, M // tm), in_specs=[a_spec, b_spec],
        out_specs=pl.BlockSpec((tm, tn), lambda j, i: (i, j)),
        out_shape=jax.ShapeDtypeStruct((M, N), out_dtype),
        compiler_params=_cparams(("parallel", "parallel")),
    )(a, b)


def _rowcall(name, body, T, tb, row_ins, par_ins, row_outs, acc_outs):
    n_ri, n_pi, n_ro = len(row_ins), len(par_ins), len(row_outs)

    def kern(*refs):
        ri, pi = refs[:n_ri], refs[n_ri:n_ri + n_pi]
        ro, ao = refs[n_ri + n_pi:n_ri + n_pi + n_ro], refs[n_ri + n_pi + n_ro:]
        if ao:
            @pl.when(pl.program_id(0) == 0)
            def _():
                for r in ao:
                    r[...] = jnp.zeros_like(r)
        body(ri, pi, ro, ao)

    in_specs = [pl.BlockSpec((tb, w), lambda i, cb=cb: (i, cb)) for (_, w, cb) in row_ins]
    in_specs += [pl.BlockSpec(p.shape, lambda i: (0, 0)) for p in par_ins]
    out_specs = [pl.BlockSpec((tb, w), lambda i: (i, 0)) for (w, _) in row_outs]
    out_specs += [pl.BlockSpec((1, w), lambda i: (0, 0)) for w in acc_outs]
    out_shape = [jax.ShapeDtypeStruct((T, w), dt) for (w, dt) in row_outs]
    out_shape += [jax.ShapeDtypeStruct((1, w), F32) for w in acc_outs]
    return pl.pallas_call(
        kern, name=name, grid=(T // tb,), in_specs=in_specs, out_specs=out_specs, out_shape=out_shape,
        compiler_params=_cparams(("arbitrary",)),
    )(*[a for (a, _, _) in row_ins], *par_ins)


def _rsum(x):
    return jnp.sum(x, axis=0, keepdims=True)


def _rms_rstd(x):
    return lax.rsqrt(jnp.mean(x * x, axis=-1, keepdims=True) + RMS_EPS)


def _rms_fwd(x, w, name):
    T, D = x.shape

    def body(ri, pi, ro, ao):
        xv = ri[0][...]
        ro[0][...] = (xv * _rms_rstd(xv) * pi[0][...]).astype(BF16)

    return _rowcall(name, body, T, 256, [(x, D, 0)], [w], [(D, BF16)], [])[0]


def _rms_bwd_core(dy, x, w):
    rs = _rms_rstd(x)
    xh = x * rs
    gw = dy * w
    dx = rs * (gw - xh * jnp.mean(gw * xh, axis=-1, keepdims=True))
    return dx, dy * xh


def _pre_bwd(dh, x, w, dres, name):
    T, D = x.shape

    def body(ri, pi, ro, ao):
        dx, dwc = _rms_bwd_core(ri[0][...], ri[1][...], pi[0][...])
        ro[0][...] = ri[2][...] + dx
        ao[0][...] += _rsum(dwc)

    return _rowcall(name, body, T, 256, [(dh, D, 0), (x, D, 0), (dres, D, 0)], [w], [(D, F32)], [D])


def _post_fwd(x, f, w, r, name):
    T, D = x.shape

    def body(ri, pi, ro, ao):
        fv = ri[1][...]
        ro[0][...] = ri[0][...] + r * (fv * _rms_rstd(fv) * pi[0][...])

    return _rowcall(name, body, T, 256, [(x, D, 0), (f, D, 0)], [w], [(D, F32)], [])[0]


def _post_bwd(dxn, f, w, r, name):
    T, D = f.shape

    def body(ri, pi, ro, ao):
        df, dwc = _rms_bwd_core(r * ri[0][...], ri[1][...], pi[0][...])
        ro[0][...] = df.astype(BF16)
        ao[0][...] += _rsum(dwc)

    return _rowcall(name, body, T, 256, [(dxn, D, 0), (f, D, 0)], [w], [(D, BF16)], [D])


def _swiglu_fwd(a, name):
    T, F2 = a.shape
    F = F2 // 2

    def body(ri, pi, ro, ao):
        ro[0][...] = (_silu(ri[0][...]) * ri[1][...]).astype(BF16)

    return _rowcall(name, body, T, 256, [(a, F, 0), (a, F, 1)], [], [(F, BF16)], [])[0]


def _swiglu_bwd(ds, a, name):
    T, F2 = a.shape
    F = F2 // 2

    def body(ri, pi, ro, ao):
        dsv, g, u = ri[0][...], ri[1][...], ri[2][...]
        ro[0][:, :F] = (dsv * u * _dsilu(g)).astype(BF16)
        ro[0][:, F:] = (dsv * _silu(g)).astype(BF16)

    return _rowcall(name, body, T, 256, [(ds, F, 0), (a, F, 0), (a, F, 1)], [], [(F2, BF16)], [])[0]


def _loss_fwd_bwd(y, tgt, name):
    T, D = y.shape

    def body(ri, pi, ro, ao):
        e = ri[0][...] - ri[1][...]
        ro[0][...] = e * (1.0 / D)
        tot = jnp.sum(_rsum(e * e), axis=1, keepdims=True) * (0.5 / D)
        ao[0][...] += jnp.broadcast_to(tot, (1, LANES))

    return _rowcall(name, body, T, 256, [(y, D, 0), (tgt, D, 0)], [], [(D, F32)], [LANES])


def _gdn_gate_fwd(o, p, nw, name):
    T, D = o.shape
    H = D // GDN_DK

    def body(ri, pi, ro, ao):
        for h in range(H):
            sl = slice(h * GDN_DK, (h + 1) * GDN_DK)
            oh = ri[0][:, sl]
            ro[0][:, sl] = (oh * _rms_rstd(oh) * pi[0][...] * _silu(ri[1][:, sl])).astype(BF16)

    return _rowcall(name, body, T, 256, [(o, D, 0), (p, D, 3)], [nw], [(D, BF16)], [])[0]


def _gdn_gate_bwd(dog, o, p, nw, name):
    T, D = o.shape
    H = D // GDN_DK

    def body(ri, pi, ro, ao):
        acc = jnp.zeros((1, GDN_DK), F32)
        for h in range(H):
            sl = slice(h * GDN_DK, (h + 1) * GDN_DK)
            dy, oh, z = ri[0][:, sl], ri[1][:, sl], ri[2][:, sl]
            sz = _silu(z)
            do, dwc = _rms_bwd_core(dy * sz, oh, pi[0][...])
            ro[0][:, sl] = do
            ro[1][:, sl] = (dy * oh * _rms_rstd(oh) * pi[0][...] * _dsilu(z)).astype(BF16)
            acc = acc + _rsum(dwc)
        ao[0][...] += acc

    return _rowcall(name, body, T, 256, [(dog, D, 0), (o, D, 0), (p, D, 3)], [nw], [(D, F32), (D, BF16)], [GDN_DK])


def _glu_fwd(p, b, name):
    T = p.shape[0]
    D = b.shape[1] // 2

    def body(ri, pi, ro, ao):
        ro[0][...] = (ri[0][...] + pi[0][:, :D]) * _sigmoid(ri[1][...] + pi[0][:, D:])

    return _rowcall(name, body, T, 256, [(p, D, 4), (p, D, 5)], [b], [(D, F32)], [])[0]


def _glu_bwd(dhc, p, b, name):
    T = p.shape[0]
    D = b.shape[1] // 2

    def body(ri, pi, ro, ao):
        d, a, g = ri[0][...], ri[1][...] + pi[0][:, :D], ri[2][...] + pi[0][:, D:]
        sg = _sigmoid(g)
        da, dg = d * sg, d * a * sg * (1.0 - sg)
        ro[0][:, :D] = da.astype(BF16)
        ro[0][:, D:] = dg.astype(BF16)
        ao[0][:, :D] += _rsum(da)
        ao[0][:, D:] += _rsum(dg)

    return _rowcall(name, body, T, 256, [(dhc, D, 0), (p, D, 4), (p, D, 5)], [b], [(2 * D, BF16)], [2 * D])


def _ln_stats(x):
    mu = jnp.mean(x, axis=-1, keepdims=True)
    xc = x - mu
    rstd = lax.rsqrt(jnp.mean(xc * xc, axis=-1, keepdims=True) + LN_EPS)
    return xc * rstd, rstd


def _ln_silu_fwd(hcv, g, b, name):
    T, D = hcv.shape

    def body(ri, pi, ro, ao):
        xh, _ = _ln_stats(ri[0][...])
        ro[0][...] = _silu(xh * pi[0][...] + pi[1][...]).astype(BF16)

    return _rowcall(name, body, T, 256, [(hcv, D, 0)], [g, b], [(D, BF16)], [])[0]


def _ln_silu_bwd(dhl, hcv, g, b, name):
    T, D = hcv.shape

    def body(ri, pi, ro, ao):
        xh, rstd = _ln_stats(ri[1][...])
        dyl = ri[0][...] * _dsilu(xh * pi[0][...] + pi[1][...])
        dxh = dyl * pi[0][...]
        dx = rstd * (dxh - jnp.mean(dxh, axis=-1, keepdims=True) - xh * jnp.mean(dxh * xh, axis=-1, keepdims=True))
        ro[0][...] = dx
        ao[0][...] += _rsum(dyl * xh)
        ao[1][...] += _rsum(dyl)
        ao[2][...] += _rsum(dx)

    return _rowcall(name, body, T, 256, [(dhl, D, 0), (hcv, D, 0)], [g, b], [(D, F32)], [D, D, D])


def _merge_fwd(p, ya, yb, bo, name):
    T, D = ya.shape

    def body(ri, pi, ro, ao):
        ro[0][...] = (_sigmoid(ri[0][...]) * ri[2][...] + _sigmoid(ri[1][...]) * (ri[3][...] + pi[0][...])).astype(BF16)

    return _rowcall(name, body, T, 256, [(p, D, 6), (p, D, 7), (ya, D, 0), (yb, D, 0)], [bo], [(D, BF16)], [])[0]


def _merge_bwd(dym, p, ya, yb, bo, name):
    T, D = ya.shape

    def body(ri, pi, ro, ao):
        d = ri[0][...]
        ga, gb = _sigmoid(ri[1][...]), _sigmoid(ri[2][...])
        ybv = ri[4][...] + pi[0][...]
        dyb = d * gb
        ro[0][...] = (d * ga).astype(BF16)
        ro[1][...] = dyb.astype(BF16)
        ro[2][:, :D] = (d * ri[3][...] * ga * (1.0 - ga)).astype(BF16)
        ro[2][:, D:] = (d * ybv * gb * (1.0 - gb)).astype(BF16)
        ao[0][...] += _rsum(dyb)

    return _rowcall(name, body, T, 256, [(dym, D, 0), (p, D, 6), (p, D, 7), (ya, D, 0), (yb, D, 0)], [bo],
                    [(D, BF16), (D, BF16), (2 * D, BF16)], [D])


PAD = 32
RC = 256


def _causal_taps(xp_ref, w, K, c0):
    acc = None
    for j in range(K):
        term = w[j:j + 1, :] * xp_ref[pl.ds(PAD - (K - 1) + j + c0, RC), :]
        acc = term if acc is None else acc + term
    return acc


def _anticausal_taps(dp_ref, w, K, c0):
    acc = None
    for j in range(K):
        term = w[j:j + 1, :] * dp_ref[pl.ds((K - 1) - j + c0, RC), :]
        acc = term if acc is None else acc + term
    return acc


def _tap_grads(dw_ref, dc_ref, xp_ref, K, T):
    for j in range(K):
        acc = jnp.zeros((1, LANES), F32)
        for c in range(T // RC):
            acc = acc + _rsum(dc_ref[pl.ds(c * RC, RC), :] * xp_ref[pl.ds(PAD - (K - 1) + j + c * RC, RC), :])
        dw_ref[j:j + 1, :] = acc


def _qkv_conv_fwd(p, cw, H, name):
    T = p.shape[0]
    K = cw.shape[0]

    def body(x_ref, w_ref, o_ref, xp_ref):
        j = pl.program_id(0)
        xp_ref[pl.ds(0, PAD), :] = jnp.zeros((PAD, LANES), F32)
        xp_ref[pl.ds(PAD, T), :] = x_ref[...]
        w = w_ref[...]
        scale = jnp.where(j < H, GDN_DK ** -0.5, 1.0).astype(F32)
        for c in range(T // RC):
            act = _silu(_causal_taps(xp_ref, w, K, c * RC))
            nrm = act * lax.rsqrt(jnp.sum(act * act, axis=-1, keepdims=True) + L2_EPS) * scale
            o_ref[pl.ds(c * RC, RC), :] = jnp.where(j < 2 * H, nrm, act)

    return pl.pallas_call(
        body, name=name, grid=(3 * H,),
        in_specs=[pl.BlockSpec((T, LANES), lambda j: (0, j)), pl.BlockSpec((K, LANES), lambda j: (0, j))],
        out_specs=pl.BlockSpec((T, LANES), lambda j: (0, j)),
        out_shape=jax.ShapeDtypeStruct((T, 3 * H * GDN_DK), F32),
        scratch_shapes=[pltpu.VMEM((T + PAD, LANES), F32)],
        compiler_params=_cparams(("arbitrary",)),
    )(p, cw)


def _qkv_conv_bwd(dn, p, cw, H, name):
    T = p.shape[0]
    K = cw.shape[0]

    def body(dn_ref, x_ref, w_ref, dx_ref, dw_ref, xp_ref, dc_ref):
        j = pl.program_id(0)
        xp_ref[pl.ds(0, PAD), :] = jnp.zeros((PAD, LANES), F32)
        xp_ref[pl.ds(PAD, T), :] = x_ref[...]
        dc_ref[pl.ds(T, PAD), :] = jnp.zeros((PAD, LANES), F32)
        w = w_ref[...]
        scale = jnp.where(j < H, GDN_DK ** -0.5, 1.0).astype(F32)
        for c in range(T // RC):
            pre = _causal_taps(xp_ref, w, K, c * RC)
            act = _silu(pre)
            d = dn_ref[pl.ds(c * RC, RC), :]
            rs = lax.rsqrt(jnp.sum(act * act, axis=-1, keepdims=True) + L2_EPS)
            nh = act * rs
            dact_n = scale * rs * (d - nh * jnp.sum(d * nh, axis=-1, keepdims=True))
            dact = jnp.where(j < 2 * H, dact_n, d)
            dc_ref[pl.ds(c * RC, RC), :] = dact * _dsilu(pre)
        for c in range(T // RC):
            dx_ref[pl.ds(c * RC, RC), :] = _anticausal_taps(dc_ref, w, K, c * RC).astype(BF16)
        _tap_grads(dw_ref, dc_ref, xp_ref, K, T)

    return pl.pallas_call(
        body, name=name, grid=(3 * H,),
        in_specs=[pl.BlockSpec((T, LANES), lambda j: (0, j)), pl.BlockSpec((T, LANES), lambda j: (0, j)),
                  pl.BlockSpec((K, LANES), lambda j: (0, j))],
        out_specs=[pl.BlockSpec((T, LANES), lambda j: (0, j)), pl.BlockSpec((K, LANES), lambda j: (0, j))],
        out_shape=[jax.ShapeDtypeStruct((T, 3 * H * GDN_DK), BF16), jax.ShapeDtypeStruct(cw.shape, F32)],
        scratch_shapes=[pltpu.VMEM((T + PAD, LANES), F32), pltpu.VMEM((T + PAD, LANES), F32)],
        compiler_params=_cparams(("arbitrary",)),
    )(dn, p, cw)


def _dw_conv_fwd(hc, w, b, name):
    T, D = hc.shape
    K = w.shape[0]

    def body(x_ref, w_ref, b_ref, o_ref, xp_ref):
        xp_ref[pl.ds(0, PAD), :] = jnp.zeros((PAD, LANES), F32)
        xp_ref[pl.ds(PAD, T), :] = x_ref[...]
        wv = w_ref[...]
        for c in range(T // RC):
            o_ref[pl.ds(c * RC, RC), :] = _causal_taps(xp_ref, wv, K, c * RC) + b_ref[...]

    return pl.pallas_call(
        body, name=name, grid=(D // LANES,),
        in_specs=[pl.BlockSpec((T, LANES), lambda j: (0, j)), pl.BlockSpec((K, LANES), lambda j: (0, j)),
                  pl.BlockSpec((1, LANES), lambda j: (0, j))],
        out_specs=pl.BlockSpec((T, LANES), lambda j: (0, j)),
        out_shape=jax.ShapeDtypeStruct((T, D), F32),
        scratch_shapes=[pltpu.VMEM((T + PAD, LANES), F32)],
        compiler_params=_cparams(("arbitrary",)),
    )(hc, w, b)


def _dw_conv_bwd(dy, hc, w, name):
    T, D = hc.shape
    K = w.shape[0]

    def body(dy_ref, x_ref, w_ref, dx_ref, dw_ref, xp_ref, dc_ref):
        xp_ref[pl.ds(0, PAD), :] = jnp.zeros((PAD, LANES), F32)
        xp_ref[pl.ds(PAD, T), :] = x_ref[...]
        dc_ref[pl.ds(T, PAD), :] = jnp.zeros((PAD, LANES), F32)
        dc_ref[pl.ds(0, T), :] = dy_ref[...]
        wv = w_ref[...]
        for c in range(T // RC):
            dx_ref[pl.ds(c * RC, RC), :] = _anticausal_taps(dc_ref, wv, K, c * RC)
        _tap_grads(dw_ref, dc_ref, xp_ref, K, T)

    return pl.pallas_call(
        body, name=name, grid=(D // LANES,),
        in_specs=[pl.BlockSpec((T, LANES), lambda j: (0, j)), pl.BlockSpec((T, LANES), lambda j: (0, j)),
                  pl.BlockSpec((K, LANES), lambda j: (0, j))],
        out_specs=[pl.BlockSpec((T, LANES), lambda j: (0, j)), pl.BlockSpec((K, LANES), lambda j: (0, j))],
        out_shape=[jax.ShapeDtypeStruct((T, D), F32), jax.ShapeDtypeStruct(w.shape, F32)],
        scratch_shapes=[pltpu.VMEM((T + PAD, LANES), F32), pltpu.VMEM((T + PAD, LANES), F32)],
        compiler_params=_cparams(("arbitrary",)),
    )(dy, hc, w)


NN = (((1,), (0,)), ((), ()))
NT = (((1,), (1,)), ((), ()))
TN = (((0,), (0,)), ((), ()))


def _dotb(a, b, dn=NN):
    return lax.dot_general(a.astype(BF16), b.astype(BF16), dn, preferred_element_type=F32)


def _dotx(a, b, dn=NN):
    return lax.dot_general(a, b, dn, precision=HI, preferred_element_type=F32)


def _iota2(shape, axis):
    return lax.broadcasted_iota(jnp.int32, shape, axis)


def _to_row(col, eye):
    return jnp.sum(jnp.where(eye, col, 0.0), axis=0, keepdims=True)


def _to_col(row, eye):
    return jnp.sum(jnp.where(eye, row, 0.0), axis=1, keepdims=True)


def _gdn_gates(bl, al, alog, dtb):
    beta = _sigmoid(bl)
    x = al + dtb
    sp = jnp.maximum(x, 0.0) + jnp.log(1.0 + jnp.exp(-jnp.abs(x)))
    g = -jnp.exp(alog) * sp
    r, c = _iota2((CHUNK, CHUNK), 0), _iota2((CHUNK, CHUNK), 1)
    G = _dotx((r >= c).astype(F32), g)
    return beta, g, G, x


def _head_prep(k, Gc, bc):
    r, c = _iota2((CHUNK, CHUNK), 0), _iota2((CHUNK, CHUNK), 1)
    eye = r == c
    Gr, br = _to_row(Gc, eye), _to_row(bc, eye)
    low, up = r >= c, r <= c
    Dm = jnp.where(low, jnp.exp(jnp.where(low, Gc - Gr, 0.0)), 0.0)
    Dt = jnp.where(up, jnp.exp(jnp.where(up, Gr - Gc, 0.0)), 0.0)
    KK = _dotb(k, k, NT)
    M = jnp.where(r > c, KK * Dm, 0.0)
    At = jnp.where(r < c, KK * Dt, 0.0) * br
    return Dm, KK, M, M * bc, At


def _unit_lower_inverses(As, Ats):
    n = len(As)
    nb = CHUNK // SUB
    lane = _iota2((SUB, CHUNK), 1)
    row = _iota2((SUB, CHUNK), 0)
    Atp = []
    for At in Ats:
        acc = jnp.zeros((SUB, CHUNK), F32)
        for b in range(nb):
            acc = jnp.where(lane // SUB == b, At[b * SUB:(b + 1) * SUB, :], acc)
        Atp.append(acc)
    gr, gc = _iota2((CHUNK, CHUNK), 0), _iota2((CHUNK, CHUNK), 1)
    ones_bd = (gr // SUB == gc // SUB).astype(F32)
    stack = jnp.concatenate(
        [jnp.where(lane % SUB == i, Atp[m], 0.0) for i in range(1, SUB) for m in range(n)], axis=0)
    Cm = _dotx(stack, ones_bd)
    Z = [(row == lane % SUB).astype(F32) for _ in range(n)]
    for i in range(1, SUB):
        for m in range(n):
            cm = Cm[((i - 1) * n + m) * SUB:((i - 1) * n + m + 1) * SUB, :]
            new = -jnp.sum(cm * Z[m], axis=0, keepdims=True)
            Z[m] = Z[m] + jnp.where(row == i, new, 0.0)
    out = []
    bd = gr // SUB == gc // SUB
    for m in range(n):
        X = jnp.where(bd, jnp.concatenate([Z[m]] * nb, axis=0), 0.0)
        blk = SUB
        while blk < CHUNK:
            N = jnp.where((gr // (2 * blk) == gc // (2 * blk)) & (gr // blk != gc // blk), As[m], 0.0)
            X = X - _dotx(_dotx(X, N), X)
            blk *= 2
        out.append(X)
    return out


def _gdn_fwd(qkvn, p, alog, dtb, H, name):
    T = qkvn.shape[0]
    D = H * GDN_DK
    N = T // CHUNK
    bblk = 8 * D // LANES

    def body(q_ref, k_ref, v_ref, b_ref, a_ref, alog_ref, dtb_ref, o_ref, t_ref, s_ref, S_scr):
        @pl.when(pl.program_id(0) == 0)
        def _():
            S_scr[...] = jnp.zeros_like(S_scr)

        beta, _, G, _ = _gdn_gates(b_ref[...], a_ref[...], alog_ref[...], dtb_ref[...])
        prep = []
        for h in range(H):
            sl = slice(h * GDN_DK, (h + 1) * GDN_DK)
            prep.append(_head_prep(k_ref[:, sl], G[:, h:h + 1], beta[:, h:h + 1]))
        Ts = _unit_lower_inverses([pr[3] for pr in prep], [pr[4] for pr in prep])
        for h in range(H):
            sl = slice(h * GDN_DK, (h + 1) * GDN_DK)
            q, k, v = q_ref[:, sl], k_ref[:, sl], v_ref[:, sl]
            Gc, bc = G[:, h:h + 1], beta[:, h:h + 1]
            Dm = prep[h][0]
            Tm = Ts[h]
            t_ref[0, h] = Tm
            eG = jnp.exp(Gc)
            Gl = Gc[CHUNK - 1:CHUNK, :]
            u = _dotb(Tm, v * bc)
            w = _dotb(Tm, k * (bc * eG))
            QK = _dotb(q, k, NT) * Dm
            S = S_scr[h]
            s_ref[0, h] = S
            vn = u - _dotb(w, S)
            o_ref[:, sl] = _dotb(q * eG, S) + _dotb(QK, vn)
            S_scr[h] = S * jnp.exp(Gl) + _dotb(k * jnp.exp(Gl - Gc), vn, TN)

    qkv_spec = [pl.BlockSpec((CHUNK, D), lambda n, cb=cb: (n, cb)) for cb in range(3)]
    return pl.pallas_call(
        body, name=name, grid=(N,),
        in_specs=qkv_spec + [pl.BlockSpec((CHUNK, LANES), lambda n: (n, bblk)),
                             pl.BlockSpec((CHUNK, LANES), lambda n: (n, bblk + 1)),
                             pl.BlockSpec((1, LANES), lambda n: (0, 0)), pl.BlockSpec((1, LANES), lambda n: (0, 0))],
        out_specs=[pl.BlockSpec((CHUNK, D), lambda n: (n, 0)),
                   pl.BlockSpec((1, H, CHUNK, CHUNK), lambda n: (n, 0, 0, 0)),
                   pl.BlockSpec((1, H, GDN_DK, GDN_DK), lambda n: (n, 0, 0, 0))],
        out_shape=[jax.ShapeDtypeStruct((T, D), F32), jax.ShapeDtypeStruct((N, H, CHUNK, CHUNK), F32),
                   jax.ShapeDtypeStruct((N, H, GDN_DK, GDN_DK), F32)],
        scratch_shapes=[pltpu.VMEM((H, GDN_DK, GDN_DK), F32)],
        compiler_params=_cparams(("arbitrary",)),
    )(qkvn, qkvn, qkvn, p, p, alog, dtb)


def _gdn_bwd(do, qkvn, p, alog, dtb, Tinv, Sin, H, name):
    T = qkvn.shape[0]
    D = H * GDN_DK
    N = T // CHUNK
    bblk = 8 * D // LANES

    def body(do_ref, q_ref, k_ref, v_ref, b_ref, a_ref, alog_ref, dtb_ref, t_ref, s_ref,
             dqkv_ref, dba_ref, dalog_ref, ddtb_ref, dS_scr):
        @pl.when(pl.program_id(0) == 0)
        def _():
            dS_scr[...] = jnp.zeros_like(dS_scr)
            dalog_ref[...] = jnp.zeros_like(dalog_ref)
            ddtb_ref[...] = jnp.zeros_like(ddtb_ref)

        beta, g, G, x = _gdn_gates(b_ref[...], a_ref[...], alog_ref[...], dtb_ref[...])
        r, c = _iota2((CHUNK, CHUNK), 0), _iota2((CHUNK, CHUNK), 1)
        eye, low, strict = r == c, r >= c, r > c
        lane = _iota2((CHUNK, LANES), 1)
        dG_all = jnp.zeros((CHUNK, LANES), F32)
        dbeta_all = jnp.zeros((CHUNK, LANES), F32)
        for h in range(H):
            sl = slice(h * GDN_DK, (h + 1) * GDN_DK)
            q, k, v, dov = q_ref[:, sl], k_ref[:, sl], v_ref[:, sl], do_ref[:, sl]
            Gc, bc = G[:, h:h + 1], beta[:, h:h + 1]
            Dm, KK, M, _, _ = _head_prep(k, Gc, bc)
            Tm, S, dSo = t_ref[0, h], s_ref[0, h], dS_scr[h]
            eG = jnp.exp(Gc)
            Gl = Gc[CHUNK - 1:CHUNK, :]
            eR, dch = jnp.exp(Gl - Gc), jnp.exp(Gl)
            vb, kb = v * bc, k * (bc * eG)
            u, w = _dotb(Tm, vb), _dotb(Tm, kb)
            QKr = _dotb(q, k, NT)
            QK = QKr * Dm
            qd, kd = q * eG, k * eR
            vn = u - _dotb(w, S)
            dqd = _dotb(dov, S, NT)
            dS = _dotb(qd, dov, TN)
            dQK = jnp.where(low, _dotb(dov, vn, NT), 0.0)
            dvn = _dotb(QK, dov, TN)
            ddch = jnp.sum(jnp.sum(dSo * S, axis=1, keepdims=True), axis=0, keepdims=True)
            dS = dS + dch * dSo
            dkd = _dotb(vn, dSo, NT)
            dvn = dvn + _dotb(kd, dSo)
            dw = -_dotb(dvn, S, NT)
            dS = dS - _dotb(w, dvn, TN)
            dS_scr[h] = dS
            dvb = _dotb(Tm, dvn, TN)
            dkb = _dotb(Tm, dw, TN)
            dA = -jnp.where(strict, _dotb(dvb, u, NT) + _dotb(dkb, w, NT), 0.0)
            rk = jnp.sum(dkb * k, axis=1, keepdims=True)
            dbeta = jnp.sum(dvb * v, axis=1, keepdims=True) + rk * eG + jnp.sum(dA * M, axis=1, keepdims=True)
            deG = rk * bc
            dM = dA * bc
            dKK = dM * Dm
            dQKr = dQK * Dm
            E = dM * M + dQK * QK
            dq = _dotb(dQKr, k) + dqd * eG
            dk = (dkb * (bc * eG) + _dotb(dKK, k) + _dotb(dKK, k, TN) + _dotb(dQKr, q, TN) + dkd * eR)
            deG = deG + jnp.sum(dqd * q, axis=1, keepdims=True)
            deR = jnp.sum(dkd * k, axis=1, keepdims=True)
            dGl = jnp.sum(deR * eR, axis=0, keepdims=True) + ddch * dch
            dGc = (jnp.sum(E, axis=1, keepdims=True) - _to_col(jnp.sum(E, axis=0, keepdims=True), eye)
                   + deG * eG - deR * eR + jnp.where(r[:, :1] == CHUNK - 1, dGl, 0.0))
            dqkv_ref[:, sl] = dq
            dqkv_ref[:, D + h * GDN_DK:D + (h + 1) * GDN_DK] = dk
            dqkv_ref[:, 2 * D + h * GDN_DK:2 * D + (h + 1) * GDN_DK] = dvb * bc
            dG_all = jnp.where(lane == h, dGc, dG_all)
            dbeta_all = jnp.where(lane == h, dbeta, dbeta_all)
        dg = _dotx((r <= c).astype(F32), dG_all)
        da = dg * (-jnp.exp(alog_ref[...])) * _sigmoid(x)
        dba_ref[:, :LANES] = (dbeta_all * beta * (1.0 - beta)).astype(BF16)
        dba_ref[:, LANES:] = da.astype(BF16)
        dalog_ref[...] += _rsum(dg * g)
        ddtb_ref[...] += _rsum(da)

    rev = lambda n: N - 1 - n
    qkv_spec = [pl.BlockSpec((CHUNK, D), lambda n, cb=cb: (rev(n), cb)) for cb in range(3)]
    return pl.pallas_call(
        body, name=name, grid=(N,),
        in_specs=[pl.BlockSpec((CHUNK, D), lambda n: (rev(n), 0))] + qkv_spec + [
            pl.BlockSpec((CHUNK, LANES), lambda n: (rev(n), bblk)),
            pl.BlockSpec((CHUNK, LANES), lambda n: (rev(n), bblk + 1)),
            pl.BlockSpec((1, LANES), lambda n: (0, 0)), pl.BlockSpec((1, LANES), lambda n: (0, 0)),
            pl.BlockSpec((1, H, CHUNK, CHUNK), lambda n: (rev(n), 0, 0, 0)),
            pl.BlockSpec((1, H, GDN_DK, GDN_DK), lambda n: (rev(n), 0, 0, 0))],
        out_specs=[pl.BlockSpec((CHUNK, 3 * D), lambda n: (rev(n), 0)),
                   pl.BlockSpec((CHUNK, 2 * LANES), lambda n: (rev(n), 0)),
                   pl.BlockSpec((1, LANES), lambda n: (0, 0)), pl.BlockSpec((1, LANES), lambda n: (0, 0))],
        out_shape=[jax.ShapeDtypeStruct((T, 3 * D), F32), jax.ShapeDtypeStruct((T, 2 * LANES), BF16),
                   jax.ShapeDtypeStruct((1, LANES), F32), jax.ShapeDtypeStruct((1, LANES), F32)],
        scratch_shapes=[pltpu.VMEM((H, GDN_DK, GDN_DK), F32)],
        compiler_params=_cparams(("arbitrary",)),
    )(do, qkvn, qkvn, qkvn, p, p, alog, dtb, Tinv, Sin)


def _mix_in_reorder(w, D, H):
    o1 = 4 * D
    o2, o3 = o1 + H, o1 + 2 * H
    z = jnp.zeros((w.shape[0], LANES - H), w.dtype)
    return jnp.concatenate([w[:, :o1], w[:, o3:], w[:, o1:o2], z, w[:, o2:o3], z], axis=1)


def _mix_in_restore(dw, D, H):
    b0 = 8 * D
    return jnp.concatenate([dw[:, :4 * D], dw[:, b0:b0 + H], dw[:, b0 + LANES:b0 + LANES + H], dw[:, 4 * D:b0]], axis=1)


def _ffn_fwd(x, W, pre, tag):
    h = _rms_fwd(x, W[pre + "_norm_pre"], tag + "_pre")
    a = _matmul(h, W[pre + "_w_in"], "nn", F32, tag + "_in")
    s = _swiglu_fwd(a, tag + "_act")
    f = _matmul(s, W[pre + "_w_out"], "nn", F32, tag + "_out")
    return _post_fwd(x, f, W[pre + "_norm_post"], 0.5, tag + "_post"), (x, h, a, s, f)


def _ffn_bwd(dxn, saved, W, pre, tag):
    x, h, a, s, f = saved
    df, dpost = _post_bwd(dxn, f, W[pre + "_norm_post"], 0.5, tag + "_dpost")
    ds = _matmul(df, W[pre + "_w_out"], "nt", F32, tag + "_ds")
    dw_out = _matmul(s, df, "tn", F32, tag + "_dwout")
    da = _swiglu_bwd(ds, a, tag + "_dact")
    dh = _matmul(da, W[pre + "_w_in"], "nt", F32, tag + "_dh")
    dw_in = _matmul(h, da, "tn", F32, tag + "_dwin")
    dx, dpre = _pre_bwd(dh, x, W[pre + "_norm_pre"], dxn, tag + "_dpre")
    return dx, {pre + "_norm_pre": dpre, pre + "_norm_post": dpost, pre + "_w_in": dw_in, pre + "_w_out": dw_out}


def _mix_fwd(x, W, H, tag):
    h = _rms_fwd(x, W["mix_norm_pre"], tag + "_pre")
    p = _matmul(h, W["mix_w_in"], "nn", F32, tag + "_in")
    qkvn = _qkv_conv_fwd(p, W["gdn_conv_w"], H, tag + "_qkvconv")
    o, Tinv, Sin = _gdn_fwd(qkvn, p, W["gdn_a_log"], W["gdn_dt_bias"], H, tag + "_gdn")
    og = _gdn_gate_fwd(o, p, W["gdn_norm_w"], tag + "_gdngate")
    ya = _matmul(og, W["gdn_w_o"], "nn", F32, tag + "_gdno")
    hc = _glu_fwd(p, W["cnv_pw1_b"], tag + "_glu")
    hcv = _dw_conv_fwd(hc, W["cnv_dw_w"], W["cnv_dw_b"], tag + "_dwconv")
    hl = _ln_silu_fwd(hcv, W["cnv_ln_g"], W["cnv_ln_b"], tag + "_ln")
    yb = _matmul(hl, W["cnv_w_o"], "nn", F32, tag + "_cnvo")
    ym = _merge_fwd(p, ya, yb, W["cnv_b_o"], tag + "_merge")
    y = _matmul(ym, W["mix_w_out"], "nn", F32, tag + "_out")
    xn = _post_fwd(x, y, W["mix_norm_post"], 1.0, tag + "_post")
    return xn, (x, h, p, qkvn, o, Tinv, Sin, og, ya, hc, hcv, hl, yb, ym, y)


def _mix_bwd(dxn, saved, W, H, tag):
    x, h, p, qkvn, o, Tinv, Sin, og, ya, hc, hcv, hl, yb, ym, y = saved
    g = {}
    dy, g["mix_norm_post"] = _post_bwd(dxn, y, W["mix_norm_post"], 1.0, tag + "_dpost")
    dym = _matmul(dy, W["mix_w_out"], "nt", F32, tag + "_dym")
    g["mix_w_out"] = _matmul(ym, dy, "tn", F32, tag + "_dwout")
    dya, dyb, dgates, g["cnv_b_o"] = _merge_bwd(dym, p, ya, yb, W["cnv_b_o"], tag + "_dmerge")
    dhl = _matmul(dyb, W["cnv_w_o"], "nt", F32, tag + "_dhl")
    g["cnv_w_o"] = _matmul(hl, dyb, "tn", F32, tag + "_dwcnvo")
    dhcv, g["cnv_ln_g"], g["cnv_ln_b"], g["cnv_dw_b"] = _ln_silu_bwd(dhl, hcv, W["cnv_ln_g"], W["cnv_ln_b"], tag + "_dln")
    dhc, g["cnv_dw_w"] = _dw_conv_bwd(dhcv, hc, W["cnv_dw_w"], tag + "_ddwconv")
    dglu, g["cnv_pw1_b"] = _glu_bwd(dhc, p, W["cnv_pw1_b"], tag + "_dglu")
    dog = _matmul(dya, W["gdn_w_o"], "nt", F32, tag + "_dog")
    g["gdn_w_o"] = _matmul(og, dya, "tn", F32, tag + "_dwgdno")
    do, dz, g["gdn_norm_w"] = _gdn_gate_bwd(dog, o, p, W["gdn_norm_w"], tag + "_dgdngate")
    dqkvn, dba, g["gdn_a_log"], g["gdn_dt_bias"] = _gdn_bwd(
        do, qkvn, p, W["gdn_a_log"], W["gdn_dt_bias"], Tinv, Sin, H, tag + "_dgdn")
    dqkv, g["gdn_conv_w"] = _qkv_conv_bwd(dqkvn, p, W["gdn_conv_w"], H, tag + "_dqkvconv")
    dp = jnp.concatenate([dqkv, dz, dglu, dgates, dba], axis=1)
    dh = _matmul(dp, W["mix_w_in"], "nt", F32, tag + "_dh")
    g["mix_w_in"] = _matmul(h, dp, "tn", F32, tag + "_dwin")
    dx, g["mix_norm_pre"] = _pre_bwd(dh, x, W["mix_norm_pre"], dxn, tag + "_dpre")
    return dx, g


def _trunk_fwd_bwd(x, tgt, Ws, H):
    saved = []
    for i, W in enumerate(Ws):
        x, s1 = _ffn_fwd(x, W, "ffn1", f"l{i}_ffn1")
        x, s2 = _mix_fwd(x, W, H, f"l{i}_mix")
        x, s3 = _ffn_fwd(x, W, "ffn2", f"l{i}_ffn2")
        saved.append((s1, s2, s3))
    dx, loss = _loss_fwd_bwd(x, tgt, "loss")
    grads = [None] * len(Ws)
    for i in reversed(range(len(Ws))):
        s1, s2, s3 = saved[i]
        dx, g3 = _ffn_bwd(dx, s3, Ws[i], "ffn2", f"l{i}_ffn2")
        dx, g2 = _mix_bwd(dx, s2, Ws[i], H, f"l{i}_mix")
        dx, g1 = _ffn_bwd(dx, s1, Ws[i], "ffn1", f"l{i}_ffn1")
        grads[i] = {**g1, **g2, **g3}
    return loss, dx, grads


HBM_SPEC = pl.BlockSpec(memory_space=pltpu.HBM)


def _coords():
    return lax.axis_index("x"), lax.axis_index("y"), lax.axis_index("c")


def _all_gather(shards, name):
    n = len(shards)

    def body(*refs):
        ins, outs = refs[:n], refs[n:2 * n]
        send_sems, recv_sems, local_sems = refs[2 * n:]
        x, y, c = _coords()
        me, sibling = (x, y, c), (x, y, 1 - c)
        chips = [(1 - x, y), (x, 1 - y), (1 - x, 1 - y)]

        def copy(w, k, block, to, src=None):
            dst = outs[w].at[4 * block[0] + 2 * block[1] + block[2]]
            return pltpu.make_async_remote_copy(
                src_ref=dst if src is None else src, dst_ref=dst, send_sem=send_sems.at[w, k],
                recv_sem=recv_sems.at[w, k], device_id=to, device_id_type=MESH)

        mine = [pltpu.make_async_copy(ins[w], outs[w].at[4 * x + 2 * y + c], local_sems.at[w]) for w in range(n)]
        first = []
        for w in range(n):
            mine[w].start()
            first.append(copy(w, 0, me, sibling, src=ins[w]))
            first += [copy(w, 1 + j, me, (*chip, c), src=ins[w]) for j, chip in enumerate(chips)]
        for cp in first:
            cp.start()
        passed = []
        for j, chip in enumerate(chips):
            for w in range(n):
                copy(w, 1 + j, (*chip, c), me).wait_recv()
                fwd = copy(w, 4 + j, (*chip, c), sibling)
                fwd.start()
                passed.append(fwd)
        for w in range(n):
            copy(w, 0, sibling, me).wait_recv()
            for j, chip in enumerate(chips):
                copy(w, 4 + j, (*chip, 1 - c), me).wait_recv()
        for cp in first + passed:
            cp.wait_send()
        for cp in mine:
            cp.wait()

    return pl.pallas_call(
        body, name=name,
        out_shape=[jax.ShapeDtypeStruct((N_DEV,) + s.shape, s.dtype) for s in shards],
        in_specs=[HBM_SPEC] * n, out_specs=[HBM_SPEC] * n,
        scratch_shapes=[pltpu.SemaphoreType.DMA((n, 7)), pltpu.SemaphoreType.DMA((n, 7)), pltpu.SemaphoreType.DMA((n,))],
    )(*shards)


def _rs_sibling(Gs, name):
    n = len(Gs)

    def body(*refs):
        ins, outs = refs[:n], refs[n:2 * n]
        send_sems, recv_sems = refs[2 * n:]
        x, y, c = _coords()
        cps = []
        for w in range(n):
            for q in range(4):
                cp = pltpu.make_async_remote_copy(
                    src_ref=ins[w].at[2 * q + (1 - c)], dst_ref=outs[w].at[q], send_sem=send_sems.at[w, q],
                    recv_sem=recv_sems.at[w, q], device_id=(x, y, 1 - c), device_id_type=MESH)
                cp.start()
                cps.append(cp)
        for cp in cps:
            cp.wait()

    return pl.pallas_call(
        body, name=name,
        out_shape=[jax.ShapeDtypeStruct((4,) + g.shape[1:], g.dtype) for g in Gs],
        in_specs=[HBM_SPEC] * n, out_specs=[HBM_SPEC] * n,
        scratch_shapes=[pltpu.SemaphoreType.DMA((n, 4)), pltpu.SemaphoreType.DMA((n, 4))],
    )(*Gs)


def _rs_chips(Ps, name):
    n = len(Ps)

    def body(*refs):
        ins, outs = refs[:n], refs[n:2 * n]
        send_sems, recv_sems, local_sems = refs[2 * n:]
        x, y, c = _coords()
        me_q = 2 * x + y
        chips = [(1 - x, y), (x, 1 - y), (1 - x, 1 - y)]
        cps = []
        for w in range(n):
            loc = pltpu.make_async_copy(ins[w].at[me_q], outs[w].at[me_q], local_sems.at[w])
            loc.start()
            cps.append(loc)
            for j, (px, py) in enumerate(chips):
                cp = pltpu.make_async_remote_copy(
                    src_ref=ins[w].at[2 * px + py], dst_ref=outs[w].at[me_q], send_sem=send_sems.at[w, j],
                    recv_sem=recv_sems.at[w, j], device_id=(px, py, c), device_id_type=MESH)
                cp.start()
                cps.append(cp)
        for cp in cps:
            cp.wait()

    return pl.pallas_call(
        body, name=name,
        out_shape=[jax.ShapeDtypeStruct(p.shape, p.dtype) for p in Ps],
        in_specs=[HBM_SPEC] * n, out_specs=[HBM_SPEC] * n,
        scratch_shapes=[pltpu.SemaphoreType.DMA((n, 3)), pltpu.SemaphoreType.DMA((n, 3)), pltpu.SemaphoreType.DMA((n,))],
    )(*Ps)


def _row_tile(R, target=256):
    best = None
    for t in range(8, min(R, target) + 1, 8):
        if R % t == 0:
            best = t
    return best if best is not None else R


def _pair_add(G, R1, cidx, name):
    _, _, R, C = G.shape
    tb = _row_tile(R)

    def body(c_ref, g_ref, r_ref, o_ref):
        o_ref[...] = (g_ref[...] + r_ref[...]).astype(BF16)

    return pl.pallas_call(
        body, name=name,
        grid_spec=pltpu.PrefetchScalarGridSpec(
            num_scalar_prefetch=1, grid=(4, R // tb),
            in_specs=[pl.BlockSpec((None, None, tb, C), lambda q, i, cr: (q, cr[0], i, 0)),
                      pl.BlockSpec((None, tb, C), lambda q, i, cr: (q, i, 0))],
            out_specs=pl.BlockSpec((None, tb, C), lambda q, i, cr: (q, i, 0))),
        out_shape=jax.ShapeDtypeStruct((4, R, C), BF16),
        compiler_params=_cparams(("arbitrary", "arbitrary")),
    )(cidx, G, R1)


def _sum_parts(parts, name):
    P, R, C = parts.shape

    def body(p_ref, o_ref):
        acc = p_ref[0]
        for j in range(1, P):
            acc = acc + p_ref[j]
        o_ref[...] = acc

    return pl.pallas_call(
        body, name=name, out_shape=jax.ShapeDtypeStruct((R, C), F32),
        in_specs=[pl.BlockSpec(memory_space=pltpu.VMEM)], out_specs=pl.BlockSpec(memory_space=pltpu.VMEM),
        compiler_params=_cparams(),
    )(parts)


def _adamw(w, m, v, parts, name):
    R, C = w.shape
    P = parts.shape[0]
    tb = _row_tile(R)
    c1 = 1.0 / (1.0 - ADAM_B1 ** ADAM_STEP)
    c2 = 1.0 / (1.0 - ADAM_B2 ** ADAM_STEP)

    def body(w_ref, m_ref, v_ref, p_ref, g_ref, d_ref, nm_ref, nv_ref):
        g = p_ref[0].astype(F32)
        for j in range(1, P):
            g = g + p_ref[j].astype(F32)
        nm = ADAM_B1 * m_ref[...] + (1.0 - ADAM_B1) * g
        nv = ADAM_B2 * v_ref[...] + (1.0 - ADAM_B2) * (g * g)
        g_ref[...] = g
        nm_ref[...] = nm
        nv_ref[...] = nv
        d_ref[...] = -ADAM_LR * ((nm * c1) / (jnp.sqrt(nv * c2) + ADAM_EPS) + ADAM_WD * w_ref[...])

    blk = pl.BlockSpec((tb, C), lambda i: (i, 0))
    return pl.pallas_call(
        body, name=name, grid=(R // tb,),
        in_specs=[blk, blk, blk, pl.BlockSpec((P, tb, C), lambda i: (0, i, 0))],
        out_specs=[blk] * 4, out_shape=[jax.ShapeDtypeStruct((R, C), F32)] * 4,
        compiler_params=_cparams(("parallel",)),
    )(w, m, v, parts)


BIG = ("ffn1_w_in", "ffn1_w_out", "mix_w_in", "gdn_w_o", "cnv_w_o", "mix_w_out", "ffn2_w_in", "ffn2_w_out")
COL_SHARDED = ("ffn1_w_in", "mix_w_in", "ffn2_w_in")
SMALL_SHARDED = ("gdn_conv_w", "cnv_dw_w")
NAMES = ("ffn1_norm_pre", "ffn1_norm_post", "ffn1_w_in", "ffn1_w_out", "mix_norm_pre", "mix_norm_post", "mix_w_in",
         "gdn_conv_w", "gdn_a_log", "gdn_dt_bias", "gdn_norm_w", "gdn_w_o", "cnv_pw1_b", "cnv_dw_w", "cnv_dw_b",
         "cnv_ln_g", "cnv_ln_b", "cnv_w_o", "cnv_b_o", "mix_w_out", "ffn2_norm_pre", "ffn2_norm_post", "ffn2_w_in",
         "ffn2_w_out")
SMALL = tuple(n for n in NAMES if n not in BIG)


def _gathered_layer_weights(gath, params, i, D, H):
    W = {}
    for n in BIG:
        g = gath[n][:, i]
        if n in COL_SHARDED:
            g = jnp.transpose(g, (1, 0, 2)).reshape(g.shape[1], -1)
            if n == "mix_w_in":
                g = _mix_in_reorder(g, D, H)
        else:
            g = g.reshape(-1, g.shape[-1])
        W[n] = g
    for n in SMALL_SHARDED:
        g = gath[n][:, i]
        W[n] = jnp.transpose(g, (1, 0, 2)).reshape(g.shape[1], -1)
    for n in SMALL:
        if n in SMALL_SHARDED:
            continue
        v = params[n][i]
        if n in ("gdn_a_log", "gdn_dt_bias"):
            v = jnp.pad(v, (0, LANES - H))
        W[n] = v.reshape(1, -1)
    return W


def kernel(x, ffn1_norm_pre, ffn1_norm_post, ffn1_w_in, ffn1_w_out, mix_norm_pre, mix_norm_post, mix_w_in, gdn_conv_w, gdn_a_log, gdn_dt_bias, gdn_norm_w, gdn_w_o, cnv_pw1_b, cnv_dw_w, cnv_dw_b, cnv_ln_g, cnv_ln_b, cnv_w_o, cnv_b_o, mix_w_out, ffn2_norm_pre, ffn2_norm_post, ffn2_w_in, ffn2_w_out, loss_target, m_ffn1_norm_pre, m_ffn1_norm_post, m_ffn1_w_in, m_ffn1_w_out, m_mix_norm_pre, m_mix_norm_post, m_mix_w_in, m_gdn_conv_w, m_gdn_a_log, m_gdn_dt_bias, m_gdn_norm_w, m_gdn_w_o, m_cnv_pw1_b, m_cnv_dw_w, m_cnv_dw_b, m_cnv_ln_g, m_cnv_ln_b, m_cnv_w_o, m_cnv_b_o, m_mix_w_out, m_ffn2_norm_pre, m_ffn2_norm_post, m_ffn2_w_in, m_ffn2_w_out, v_ffn1_norm_pre, v_ffn1_norm_post, v_ffn1_w_in, v_ffn1_w_out, v_mix_norm_pre, v_mix_norm_post, v_mix_w_in, v_gdn_conv_w, v_gdn_a_log, v_gdn_dt_bias, v_gdn_norm_w, v_gdn_w_o, v_cnv_pw1_b, v_cnv_dw_w, v_cnv_dw_b, v_cnv_ln_g, v_cnv_ln_b, v_cnv_w_o, v_cnv_b_o, v_mix_w_out, v_ffn2_norm_pre, v_ffn2_norm_post, v_ffn2_w_in, v_ffn2_w_out):
    params = dict(zip(NAMES, (ffn1_norm_pre, ffn1_norm_post, ffn1_w_in, ffn1_w_out, mix_norm_pre, mix_norm_post, mix_w_in, gdn_conv_w, gdn_a_log, gdn_dt_bias, gdn_norm_w, gdn_w_o, cnv_pw1_b, cnv_dw_w, cnv_dw_b, cnv_ln_g, cnv_ln_b, cnv_w_o, cnv_b_o, mix_w_out, ffn2_norm_pre, ffn2_norm_post, ffn2_w_in, ffn2_w_out)))
    mom1 = dict(zip(NAMES, (m_ffn1_norm_pre, m_ffn1_norm_post, m_ffn1_w_in, m_ffn1_w_out, m_mix_norm_pre, m_mix_norm_post, m_mix_w_in, m_gdn_conv_w, m_gdn_a_log, m_gdn_dt_bias, m_gdn_norm_w, m_gdn_w_o, m_cnv_pw1_b, m_cnv_dw_w, m_cnv_dw_b, m_cnv_ln_g, m_cnv_ln_b, m_cnv_w_o, m_cnv_b_o, m_mix_w_out, m_ffn2_norm_pre, m_ffn2_norm_post, m_ffn2_w_in, m_ffn2_w_out)))
    mom2 = dict(zip(NAMES, (v_ffn1_norm_pre, v_ffn1_norm_post, v_ffn1_w_in, v_ffn1_w_out, v_mix_norm_pre, v_mix_norm_post, v_mix_w_in, v_gdn_conv_w, v_gdn_a_log, v_gdn_dt_bias, v_gdn_norm_w, v_gdn_w_o, v_cnv_pw1_b, v_cnv_dw_w, v_cnv_dw_b, v_cnv_ln_g, v_cnv_ln_b, v_cnv_w_o, v_cnv_b_o, v_mix_w_out, v_ffn2_norm_pre, v_ffn2_norm_post, v_ffn2_w_in, v_ffn2_w_out)))
    T, D = x.shape[1], x.shape[2]
    H = D // GDN_DK
    L = ffn1_norm_pre.shape[0]
    xi, yi, ci = _coords()
    dev = 4 * xi + 2 * yi + ci

    ag_names = BIG + SMALL_SHARDED
    gathered = _all_gather([params[n].astype(BF16) if n in BIG else params[n] for n in ag_names], "ag_weights")
    gath = dict(zip(ag_names, gathered))
    Ws = [_gathered_layer_weights(gath, params, i, D, H) for i in range(L)]

    loss_row, dx, grads = _trunk_fwd_bwd(x[0], loss_target[0], Ws, H)
    loss = lax.psum(loss_row[0, 0], ("x", "y", "c"))

    Gs = []
    for n in BIG:
        per_layer = []
        for i in range(L):
            g = grads[i][n]
            if n == "mix_w_in":
                g = _mix_in_restore(g, D, H)
            if n in COL_SHARDED:
                g = jnp.transpose(g.reshape(g.shape[0], N_DEV, -1), (1, 0, 2))
            else:
                g = g.reshape(N_DEV, -1, g.shape[-1])
            per_layer.append(g)
        Gs.append(jnp.stack(per_layer, axis=1))
    R1s = _rs_sibling(Gs, "rs_sibling")
    cidx = jnp.reshape(ci, (1,)).astype(jnp.int32)
    Ps = []
    for n, G, R1 in zip(BIG, Gs, R1s):
        rows, cols = G.shape[1] * G.shape[2], G.shape[3]
        Ps.append(_pair_add(G.reshape(4, 2, rows, cols), R1.reshape(4, rows, cols), cidx, "pair_add_" + n))
    R2s = dict(zip(BIG, _rs_chips(Ps, "rs_chips")))

    pieces = []
    for i in range(L):
        for n in SMALL:
            pieces.append(grads[i][n].reshape(-1, LANES))
    rows = sum(p.shape[0] for p in pieces)
    pad = (-rows) % 8
    if pad:
        pieces.append(jnp.zeros((pad, LANES), F32))
    small_all = _all_gather([jnp.concatenate(pieces, axis=0)], "ag_small_grads")[0]
    small_sum = _sum_parts(small_all, "sum_small_grads")
    small_g = {n: [None] * L for n in SMALL}
    off = 0
    for i in range(L):
        for n in SMALL:
            shape = grads[i][n].shape
            cnt = shape[0] * shape[1] // LANES
            g = small_sum[off:off + cnt].reshape(shape)
            off += cnt
            if n in ("gdn_a_log", "gdn_dt_bias"):
                g = g[:, :H]
            if n in SMALL_SHARDED:
                wloc = params[n].shape[-1]
                g = lax.dynamic_slice_in_dim(g, dev * wloc, wloc, axis=1)
            small_g[n][i] = g

    outs = {}
    for n in NAMES:
        w, m, v = params[n], mom1[n], mom2[n]
        if n in BIG:
            rows, cols = w.shape[0] * w.shape[1], w.shape[2]
            parts = R2s[n]
        else:
            rows, cols = (w.shape[0] * w.shape[1], w.shape[2]) if w.ndim == 3 else w.shape
            parts = jnp.stack(small_g[n], axis=0).reshape(1, rows, cols)
        res = _adamw(w.reshape(rows, cols), m.reshape(rows, cols), v.reshape(rows, cols), parts, "adamw_" + n)
        outs[n] = [r.reshape(w.shape) for r in res]

    result = [loss, dx[None]]
    for k in range(4):
        result += [outs[n][k] for n in NAMES]
    return tuple(result)
```

```python
import jax
import jax.numpy as jnp
from jax import lax
from jax.experimental import pallas as pl
from jax.experimental.pallas import tpu as pltpu

F32 = jnp.float32
BF16 = jnp.bfloat16

GDN_DK = 128
CHUNK = 64
GDN_CONV = 4
CNV_K = 31
RMS_EPS = 1e-6
LN_EPS = 1e-5
L2_EPS = 1e-6
ADAM_LR = 0.001
ADAM_B1 = 0.9
ADAM_B2 = 0.999
ADAM_EPS = 1e-08
ADAM_WD = 0.01
ADAM_STEP = 10

LANES = 128
SUB = 16
VMEM_LIMIT = 56 * 1024 * 1024
N_DEV = 8
MESH = pl.DeviceIdType.MESH


def _cparams(sem=None, **kw):
    if sem is not None:
        kw["dimension_semantics"] = sem
    return pltpu.CompilerParams(vmem_limit_bytes=VMEM_LIMIT, **kw)


def _tile(dim, target):
    best = None
    for t in range(LANES, min(dim, target) + 1, LANES):
        if dim % t == 0:
            best = t
    return best if best is not None else dim


def _sigmoid(x):
    return 1.0 / (1.0 + jnp.exp(-x))


def _silu(x):
    return x * _sigmoid(x)


def _dsilu(x):
    s = _sigmoid(x)
    return s * (1.0 + x * (1.0 - s))


MM_VMEM_BUDGET = 40 * 1024 * 1024
MM_MAX_TILE = 2048


def _mm_tiles(M, N, K, out_bytes):
    def cands(dim):
        c = [t for t in range(LANES, min(dim, MM_MAX_TILE) + 1, LANES) if dim % t == 0]
        return c or [dim]
    best = None
    for tm in cands(M):
        for tn in cands(N):
            vm = 2 * (2 * K * (tm + tn) + tm * tn * out_bytes)
            if vm <= MM_VMEM_BUDGET and (best is None or tm * tn > best[0] * best[1]):
                best = (tm, tn)
    return best if best is not None else (cands(M)[0], cands(N)[0])


def _matmul(a, b, mode, out_dtype, name):
    if mode == "nn":
        (M, K), N = a.shape, b.shape[1]
    elif mode == "nt":
        (M, K), N = a.shape, b.shape[0]
    else:
        (K, M), N = a.shape, b.shape[1]
    tm, tn = _mm_tiles(M, N, K, jnp.dtype(out_dtype).itemsize)
    if mode == "nn":
        a_spec = pl.BlockSpec((tm, K), lambda j, i: (i, 0))
        b_spec = pl.BlockSpec((K, tn), lambda j, i: (0, j))
        dn = (((1,), (0,)), ((), ()))
    elif mode == "nt":
        a_spec = pl.BlockSpec((tm, K), lambda j, i: (i, 0))
        b_spec = pl.BlockSpec((tn, K), lambda j, i: (j, 0))
        dn = (((1,), (1,)), ((), ()))
    else:
        a_spec = pl.BlockSpec((K, tm), lambda j, i: (0, i))
        b_spec = pl.BlockSpec((K, tn), lambda j, i: (0, j))
        dn = (((0,), (0,)), ((), ()))

    def body(a_ref, b_ref, o_ref):
        o_ref[...] = lax.dot_general(a_ref[...], b_ref[...], dn, preferred_element_type=F32).astype(out_dtype)

    return pl.pallas_call(
        body, name=name, grid=(N // tn, M // tm), in_specs=[a_spec, b_spec],
        out_specs=pl.BlockSpec((tm, tn), lambda j, i: (i, j)),
        out_shape=jax.ShapeDtypeStruct((M, N), out_dtype),
        compiler_params=_cparams(("parallel", "parallel")),
    )(a, b)


def _rowcall(name, body, T, tb, row_ins, par_ins, row_outs, acc_outs):
    n_ri, n_pi, n_ro = len(row_ins), len(par_ins), len(row_outs)

    def kern(*refs):
        ri, pi = refs[:n_ri], refs[n_ri:n_ri + n_pi]
        ro, ao = refs[n_ri + n_pi:n_ri + n_pi + n_ro], refs[n_ri + n_pi + n_ro:]
        if ao:
            @pl.when(pl.program_id(0) == 0)
            def _():
                for r in ao:
                    r[...] = jnp.zeros_like(r)
        body(ri, pi, ro, ao)

    in_specs = [pl.BlockSpec((tb, w), lambda i, cb=cb: (i, cb)) for (_, w, cb) in row_ins]
    in_specs += [pl.BlockSpec(p.shape, lambda i: (0, 0)) for p in par_ins]
    out_specs = [pl.BlockSpec((tb, w), lambda i: (i, 0)) for (w, _) in row_outs]
    out_specs += [pl.BlockSpec((1, w), lambda i: (0, 0)) for w in acc_outs]
    out_shape = [jax.ShapeDtypeStruct((T, w), dt) for (w, dt) in row_outs]
    out_shape += [jax.ShapeDtypeStruct((1, w), F32) for w in acc_outs]
    return pl.pallas_call(
        kern, name=name, grid=(T // tb,), in_specs=in_specs, out_specs=out_specs, out_shape=out_shape,
        compiler_params=_cparams(("arbitrary",)),
    )(*[a for (a, _, _) in row_ins], *par_ins)


def _rsum(x):
    return jnp.sum(x, axis=0, keepdims=True)


def _rms_rstd(x):
    return lax.rsqrt(jnp.mean(x * x, axis=-1, keepdims=True) + RMS_EPS)


def _rms_fwd(x, w, name):
    T, D = x.shape

    def body(ri, pi, ro, ao):
        xv = ri[0][...]
        ro[0][...] = (xv * _rms_rstd(xv) * pi[0][...]).astype(BF16)

    return _rowcall(name, body, T, 256, [(x, D, 0)], [w], [(D, BF16)], [])[0]


def _rms_bwd_core(dy, x, w):
    rs = _rms_rstd(x)
    xh = x * rs
    gw = dy * w
    dx = rs * (gw - xh * jnp.mean(gw * xh, axis=-1, keepdims=True))
    return dx, dy * xh


def _pre_bwd(dh, x, w, dres, name):
    T, D = x.shape

    def body(ri, pi, ro, ao):
        dx, dwc = _rms_bwd_core(ri[0][...], ri[1][...], pi[0][...])
        ro[0][...] = ri[2][...] + dx
        ao[0][...] += _rsum(dwc)

    return _rowcall(name, body, T, 256, [(dh, D, 0), (x, D, 0), (dres, D, 0)], [w], [(D, F32)], [D])


def _post_fwd(x, f, w, r, name):
    T, D = x.shape

    def body(ri, pi, ro, ao):
        fv = ri[1][...]
        ro[0][...] = ri[0][...] + r * (fv * _rms_rstd(fv) * pi[0][...])

    return _rowcall(name, body, T, 256, [(x, D, 0), (f, D, 0)], [w], [(D, F32)], [])[0]


def _post_bwd(dxn, f, w, r, name):
    T, D = f.shape

    def body(ri, pi, ro, ao):
        df, dwc = _rms_bwd_core(r * ri[0][...], ri[1][...], pi[0][...])
        ro[0][...] = df.astype(BF16)
        ao[0][...] += _rsum(dwc)

    return _rowcall(name, body, T, 256, [(dxn, D, 0), (f, D, 0)], [w], [(D, BF16)], [D])


def _swiglu_fwd(a, name):
    T, F2 = a.shape
    F = F2 // 2

    def body(ri, pi, ro, ao):
        ro[0][...] = (_silu(ri[0][...]) * ri[1][...]).astype(BF16)

    return _rowcall(name, body, T, 256, [(a, F, 0), (a, F, 1)], [], [(F, BF16)], [])[0]


def _swiglu_bwd(ds, a, name):
    T, F2 = a.shape
    F = F2 // 2

    def body(ri, pi, ro, ao):
        dsv, g, u = ri[0][...], ri[1][...], ri[2][...]
        ro[0][:, :F] = (dsv * u * _dsilu(g)).astype(BF16)
        ro[0][:, F:] = (dsv * _silu(g)).astype(BF16)

    return _rowcall(name, body, T, 256, [(ds, F, 0), (a, F, 0), (a, F, 1)], [], [(F2, BF16)], [])[0]


def _loss_fwd_bwd(y, tgt, name):
    T, D = y.shape

    def body(ri, pi, ro, ao):
        e = ri[0][...] - ri[1][...]
        ro[0][...] = e * (1.0 / D)
        tot = jnp.sum(_rsum(e * e), axis=1, keepdims=True) * (0.5 / D)
        ao[0][...] += jnp.broadcast_to(tot, (1, LANES))

    return _rowcall(name, body, T, 256, [(y, D, 0), (tgt, D, 0)], [], [(D, F32)], [LANES])


def _gdn_gate_fwd(o, p, nw, name):
    T, D = o.shape
    H = D // GDN_DK

    def body(ri, pi, ro, ao):
        for h in range(H):
            sl = slice(h * GDN_DK, (h + 1) * GDN_DK)
            oh = ri[0][:, sl]
            ro[0][:, sl] = (oh * _rms_rstd(oh) * pi[0][...] * _silu(ri[1][:, sl])).astype(BF16)

    return _rowcall(name, body, T, 256, [(o, D, 0), (p, D, 3)], [nw], [(D, BF16)], [])[0]


def _gdn_gate_bwd(dog, o, p, nw, name):
    T, D = o.shape
    H = D // GDN_DK

    def body(ri, pi, ro, ao):
        acc = jnp.zeros((1, GDN_DK), F32)
        for h in range(H):
            sl = slice(h * GDN_DK, (h + 1) * GDN_DK)
            dy, oh, z = ri[0][:, sl], ri[1][:, sl], ri[2][:, sl]
            sz = _silu(z)
            do, dwc = _rms_bwd_core(dy * sz, oh, pi[0][...])
            ro[0][:, sl] = do
            ro[1][:, sl] = (dy * oh * _rms_rstd(oh) * pi[0][...] * _dsilu(z)).astype(BF16)
            acc = acc + _rsum(dwc)
        ao[0][...] += acc

    return _rowcall(name, body, T, 256, [(dog, D, 0), (o, D, 0), (p, D, 3)], [nw], [(D, F32), (D, BF16)], [GDN_DK])


def _glu_fwd(p, b, name):
    T = p.shape[0]
    D = b.shape[1] // 2

    def body(ri, pi, ro, ao):
        ro[0][...] = (ri[0][...] + pi[0][:, :D]) * _sigmoid(ri[1][...] + pi[0][:, D:])

    return _rowcall(name, body, T, 256, [(p, D, 4), (p, D, 5)], [b], [(D, F32)], [])[0]


def _glu_bwd(dhc, p, b, name):
    T = p.shape[0]
    D = b.shape[1] // 2

    def body(ri, pi, ro, ao):
        d, a, g = ri[0][...], ri[1][...] + pi[0][:, :D], ri[2][...] + pi[0][:, D:]
        sg = _sigmoid(g)
        da, dg = d * sg, d * a * sg * (1.0 - sg)
        ro[0][:, :D] = da.astype(BF16)
        ro[0][:, D:] = dg.astype(BF16)
        ao[0][:, :D] += _rsum(da)
        ao[0][:, D:] += _rsum(dg)

    return _rowcall(name, body, T, 256, [(dhc, D, 0), (p, D, 4), (p, D, 5)], [b], [(2 * D, BF16)], [2 * D])


def _ln_stats(x):
    mu = jnp.mean(x, axis=-1, keepdims=True)
    xc = x - mu
    rstd = lax.rsqrt(jnp.mean(xc * xc, axis=-1, keepdims=True) + LN_EPS)
    return xc * rstd, rstd


def _ln_silu_fwd(hcv, g, b, name):
    T, D = hcv.shape

    def body(ri, pi, ro, ao):
        xh, _ = _ln_stats(ri[0][...])
        ro[0][...] = _silu(xh * pi[0][...] + pi[1][...]).astype(BF16)

    return _rowcall(name, body, T, 256, [(hcv, D, 0)], [g, b], [(D, BF16)], [])[0]


def _ln_silu_bwd(dhl, hcv, g, b, name):
    T, D = hcv.shape

    def body(ri, pi, ro, ao):
        xh, rstd = _ln_stats(ri[1][...])
        dyl = ri[0][...] * _dsilu(xh * pi[0][...] + pi[1][...])
        dxh = dyl * pi[0][...]
        dx = rstd * (dxh - jnp.mean(dxh, axis=-1, keepdims=True) - xh * jnp.mean(dxh * xh, axis=-1, keepdims=True))
        ro[0][...] = dx
        ao[0][...] += _rsum(dyl * xh)
        ao[1][...] += _rsum(dyl)
        ao[2][...] += _rsum(dx)

    return _rowcall(name, body, T, 256, [(dhl, D, 0), (hcv, D, 0)], [g, b], [(D, F32)], [D, D, D])


def _merge_fwd(p, ya, yb, bo, name):
    T, D = ya.shape

    def body(ri, pi, ro, ao):
        ro[0][...] = (_sigmoid(ri[0][...]) * ri[2][...] + _sigmoid(ri[1][...]) * (ri[3][...] + pi[0][...])).astype(BF16)

    return _rowcall(name, body, T, 256, [(p, D, 6), (p, D, 7), (ya, D, 0), (yb, D, 0)], [bo], [(D, BF16)], [])[0]


def _merge_bwd(dym, p, ya, yb, bo, name):
    T, D = ya.shape

    def body(ri, pi, ro, ao):
        d = ri[0][...]
        ga, gb = _sigmoid(ri[1][...]), _sigmoid(ri[2][...])
        ybv = ri[4][...] + pi[0][...]
        dyb = d * gb
        ro[0][...] = (d * ga).astype(BF16)
        ro[1][...] = dyb.astype(BF16)
        ro[2][:, :D] = (d * ri[3][...] * ga * (1.0 - ga)).astype(BF16)
        ro[2][:, D:] = (d * ybv * gb * (1.0 - gb)).astype(BF16)
        ao[0][...] += _rsum(dyb)

    return _rowcall(name, body, T, 256, [(dym, D, 0), (p, D, 6), (p, D, 7), (ya, D, 0), (yb, D, 0)], [bo],
                    [(D, BF16), (D, BF16), (2 * D, BF16)], [D])


PAD = 32
RC = 256


def _causal_taps(xp_ref, w, K, c0):
    acc = None
    for j in range(K):
        term = w[j:j + 1, :] * xp_ref[pl.ds(PAD - (K - 1) + j + c0, RC), :]
        acc = term if acc is None else acc + term
    return acc


def _anticausal_taps(dp_ref, w, K, c0):
    acc = None
    for j in range(K):
        term = w[j:j + 1, :] * dp_ref[pl.ds((K - 1) - j + c0, RC), :]
        acc = term if acc is None else acc + term
    return acc


def _tap_grads(dw_ref, dc_ref, xp_ref, K, T):
    for j in range(K):
        acc = jnp.zeros((1, LANES), F32)
        for c in range(T // RC):
            acc = acc + _rsum(dc_ref[pl.ds(c * RC, RC), :] * xp_ref[pl.ds(PAD - (K - 1) + j + c * RC, RC), :])
        dw_ref[j:j + 1, :] = acc


def _qkv_conv_fwd(p, cw, H, name):
    T = p.shape[0]
    K = cw.shape[0]

    def body(x_ref, w_ref, o_ref, xp_ref):
        j = pl.program_id(0)
        xp_ref[pl.ds(0, PAD), :] = jnp.zeros((PAD, LANES), F32)
        xp_ref[pl.ds(PAD, T), :] = x_ref[...]
        w = w_ref[...]
        scale = jnp.where(j < H, GDN_DK ** -0.5, 1.0).astype(F32)
        for c in range(T // RC):
            act = _silu(_causal_taps(xp_ref, w, K, c * RC))
            nrm = act * lax.rsqrt(jnp.sum(act * act, axis=-1, keepdims=True) + L2_EPS) * scale
            o_ref[pl.ds(c * RC, RC), :] = jnp.where(j < 2 * H, nrm, act)

    return pl.pallas_call(
        body, name=name, grid=(3 * H,),
        in_specs=[pl.BlockSpec((T, LANES), lambda j: (0, j)), pl.BlockSpec((K, LANES), lambda j: (0, j))],
        out_specs=pl.BlockSpec((T, LANES), lambda j: (0, j)),
        out_shape=jax.ShapeDtypeStruct((T, 3 * H * GDN_DK), F32),
        scratch_shapes=[pltpu.VMEM((T + PAD, LANES), F32)],
        compiler_params=_cparams(("arbitrary",)),
    )(p, cw)


def _qkv_conv_bwd(dn, p, cw, H, name):
    T = p.shape[0]
    K = cw.shape[0]

    def body(dn_ref, x_ref, w_ref, dx_ref, dw_ref, xp_ref, dc_ref):
        j = pl.program_id(0)
        xp_ref[pl.ds(0, PAD), :] = jnp.zeros((PAD, LANES), F32)
        xp_ref[pl.ds(PAD, T), :] = x_ref[...]
        dc_ref[pl.ds(T, PAD), :] = jnp.zeros((PAD, LANES), F32)
        w = w_ref[...]
        scale = jnp.where(j < H, GDN_DK ** -0.5, 1.0).astype(F32)
        for c in range(T // RC):
            pre = _causal_taps(xp_ref, w, K, c * RC)
            act = _silu(pre)
            d = dn_ref[pl.ds(c * RC, RC), :]
            rs = lax.rsqrt(jnp.sum(act * act, axis=-1, keepdims=True) + L2_EPS)
            nh = act * rs
            dact_n = scale * rs * (d - nh * jnp.sum(d * nh, axis=-1, keepdims=True))
            dact = jnp.where(j < 2 * H, dact_n, d)
            dc_ref[pl.ds(c * RC, RC), :] = dact * _dsilu(pre)
        for c in range(T // RC):
            dx_ref[pl.ds(c * RC, RC), :] = _anticausal_taps(dc_ref, w, K, c * RC).astype(BF16)
        _tap_grads(dw_ref, dc_ref, xp_ref, K, T)

    return pl.pallas_call(
        body, name=name, grid=(3 * H,),
        in_specs=[pl.BlockSpec((T, LANES), lambda j: (0, j)), pl.BlockSpec((T, LANES), lambda j: (0, j)),
                  pl.BlockSpec((K, LANES), lambda j: (0, j))],
        out_specs=[pl.BlockSpec((T, LANES), lambda j: (0, j)), pl.BlockSpec((K, LANES), lambda j: (0, j))],
        out_shape=[jax.ShapeDtypeStruct((T, 3 * H * GDN_DK), BF16), jax.ShapeDtypeStruct(cw.shape, F32)],
        scratch_shapes=[pltpu.VMEM((T + PAD, LANES), F32), pltpu.VMEM((T + PAD, LANES), F32)],
        compiler_params=_cparams(("arbitrary",)),
    )(dn, p, cw)


def _dw_conv_fwd(hc, w, b, name):
    T, D = hc.shape
    K = w.shape[0]

    def body(x_ref, w_ref, b_ref, o_ref, xp_ref):
        xp_ref[pl.ds(0, PAD), :] = jnp.zeros((PAD, LANES), F32)
        xp_ref[pl.ds(PAD, T), :] = x_ref[...]
        wv = w_ref[...]
        for c in range(T // RC):
            o_ref[pl.ds(c * RC, RC), :] = _causal_taps(xp_ref, wv, K, c * RC) + b_ref[...]

    return pl.pallas_call(
        body, name=name, grid=(D // LANES,),
        in_specs=[pl.BlockSpec((T, LANES), lambda j: (0, j)), pl.BlockSpec((K, LANES), lambda j: (0, j)),
                  pl.BlockSpec((1, LANES), lambda j: (0, j))],
        out_specs=pl.BlockSpec((T, LANES), lambda j: (0, j)),
        out_shape=jax.ShapeDtypeStruct((T, D), F32),
        scratch_shapes=[pltpu.VMEM((T + PAD, LANES), F32)],
        compiler_params=_cparams(("arbitrary",)),
    )(hc, w, b)


def _dw_conv_bwd(dy, hc, w, name):
    T, D = hc.shape
    K = w.shape[0]

    def body(dy_ref, x_ref, w_ref, dx_ref, dw_ref, xp_ref, dc_ref):
        xp_ref[pl.ds(0, PAD), :] = jnp.zeros((PAD, LANES), F32)
        xp_ref[pl.ds(PAD, T), :] = x_ref[...]
        dc_ref[pl.ds(T, PAD), :] = jnp.zeros((PAD, LANES), F32)
        dc_ref[pl.ds(0, T), :] = dy_ref[...]
        wv = w_ref[...]
        for c in range(T // RC):
            dx_ref[pl.ds(c * RC, RC), :] = _anticausal_taps(dc_ref, wv, K, c * RC)
        _tap_grads(dw_ref, dc_ref, xp_ref, K, T)

    return pl.pallas_call(
        body, name=name, grid=(D // LANES,),
        in_specs=[pl.BlockSpec((T, LANES), lambda j: (0, j)), pl.BlockSpec((T, LANES), lambda j: (0, j)),
                  pl.BlockSpec((K, LANES), lambda j: (0, j))],
        out_specs=[pl.BlockSpec((T, LANES), lambda j: (0, j)), pl.BlockSpec((K, LANES), lambda j: (0, j))],
        out_shape=[jax.ShapeDtypeStruct((T, D), F32), jax.ShapeDtypeStruct(w.shape, F32)],
        scratch_shapes=[pltpu.VMEM((T + PAD, LANES), F32), pltpu.VMEM((T + PAD, LANES), F32)],
        compiler_params=_cparams(("arbitrary",)),
    )(dy, hc, w)


NN = (((1,), (0,)), ((), ()))
NT = (((1,), (1,)), ((), ()))
TN = (((0,), (0,)), ((), ()))


def _dotb(a, b, dn=NN):
    return lax.dot_general(a.astype(BF16), b.astype(BF16), dn, preferred_element_type=F32)


def _split_bf16(x, n):
    parts, r = [], x
    for _ in range(n):
        p = r.astype(BF16)
        parts.append(p)
        r = r - p.astype(F32)
    return parts


def _dot3(a, b, dn=NN):
    (ah, al), (bh, bl) = _split_bf16(a, 2), _split_bf16(b, 2)
    d = lambda u, v: lax.dot_general(u, v, dn, preferred_element_type=F32)
    return d(ah, bh) + (d(al, bh) + d(ah, bl))


def _dot_sel(sel, x, pieces, sel_left=True):
    sb = sel.astype(BF16)
    acc = None
    for p in _split_bf16(x, pieces):
        t = (lax.dot_general(sb, p, NN, preferred_element_type=F32) if sel_left
             else lax.dot_general(p, sb, NN, preferred_element_type=F32))
        acc = t if acc is None else acc + t
    return acc


def _iota2(shape, axis):
    return lax.broadcasted_iota(jnp.int32, shape, axis)


def _to_row(col, eye):
    return jnp.sum(jnp.where(eye, col, 0.0), axis=0, keepdims=True)


def _to_col(row, eye):
    return jnp.sum(jnp.where(eye, row, 0.0), axis=1, keepdims=True)


def _gdn_gates(bl, al, alog, dtb):
    beta = _sigmoid(bl)
    x = al + dtb
    sp = jnp.maximum(x, 0.0) + jnp.log(1.0 + jnp.exp(-jnp.abs(x)))
    g = -jnp.exp(alog) * sp
    r, c = _iota2((CHUNK, CHUNK), 0), _iota2((CHUNK, CHUNK), 1)
    G = _dot_sel(r >= c, g, 3)
    return beta, g, G, x


def _head_prep(k, Gc, bc):
    r, c = _iota2((CHUNK, CHUNK), 0), _iota2((CHUNK, CHUNK), 1)
    eye = r == c
    Gr, br = _to_row(Gc, eye), _to_row(bc, eye)
    low, up = r >= c, r <= c
    Dm = jnp.where(low, jnp.exp(jnp.where(low, Gc - Gr, 0.0)), 0.0)
    Dt = jnp.where(up, jnp.exp(jnp.where(up, Gr - Gc, 0.0)), 0.0)
    KK = _dotb(k, k, NT)
    M = jnp.where(r > c, KK * Dm, 0.0)
    At = jnp.where(r < c, KK * Dt, 0.0) * br
    return Dm, KK, M, M * bc, At


def _unit_lower_inverses(As, Ats):
    n = len(As)
    nb = CHUNK // SUB
    lane = _iota2((SUB, CHUNK), 1)
    row = _iota2((SUB, CHUNK), 0)
    Atp = []
    for At in Ats:
        acc = jnp.zeros((SUB, CHUNK), F32)
        for b in range(nb):
            acc = jnp.where(lane // SUB == b, At[b * SUB:(b + 1) * SUB, :], acc)
        Atp.append(acc)
    gr, gc = _iota2((CHUNK, CHUNK), 0), _iota2((CHUNK, CHUNK), 1)
    ones_bd = gr // SUB == gc // SUB
    stack = jnp.concatenate(
        [jnp.where(lane % SUB == i, Atp[m], 0.0) for i in range(1, SUB) for m in range(n)], axis=0)
    Cm = _dot_sel(ones_bd, stack, 2, sel_left=False)
    Z = [(row == lane % SUB).astype(F32) for _ in range(n)]
    for i in range(1, SUB):
        for m in range(n):
            cm = Cm[((i - 1) * n + m) * SUB:((i - 1) * n + m + 1) * SUB, :]
            new = -jnp.sum(cm * Z[m], axis=0, keepdims=True)
            Z[m] = Z[m] + jnp.where(row == i, new, 0.0)
    out = []
    bd = gr // SUB == gc // SUB
    for m in range(n):
        X = jnp.where(bd, jnp.concatenate([Z[m]] * nb, axis=0), 0.0)
        blk = SUB
        while blk < CHUNK:
            N = jnp.where((gr // (2 * blk) == gc // (2 * blk)) & (gr // blk != gc // blk), As[m], 0.0)
            X = X - _dot3(_dot3(X, N), X)
            blk *= 2
        out.append(X)
    return out


def _gdn_fwd(qkvn, p, alog, dtb, H, name):
    T = qkvn.shape[0]
    D = H * GDN_DK
    N = T // CHUNK
    bblk = 8 * D // LANES

    def body(q_ref, k_ref, v_ref, b_ref, a_ref, alog_ref, dtb_ref, o_ref, t_ref, s_ref, S_scr):
        @pl.when(pl.program_id(0) == 0)
        def _():
            S_scr[...] = jnp.zeros_like(S_scr)

        beta, _, G, _ = _gdn_gates(b_ref[...], a_ref[...], alog_ref[...], dtb_ref[...])
        prep = []
        for h in range(H):
            sl = slice(h * GDN_DK, (h + 1) * GDN_DK)
            prep.append(_head_prep(k_ref[:, sl], G[:, h:h + 1], beta[:, h:h + 1]))
        Ts = _unit_lower_inverses([pr[3] for pr in prep], [pr[4] for pr in prep])
        for h in range(H):
            sl = slice(h * GDN_DK, (h + 1) * GDN_DK)
            q, k, v = q_ref[:, sl], k_ref[:, sl], v_ref[:, sl]
            Gc, bc = G[:, h:h + 1], beta[:, h:h + 1]
            Dm = prep[h][0]
            Tm = Ts[h]
            t_ref[0, h] = Tm
            eG = jnp.exp(Gc)
            Gl = Gc[CHUNK - 1:CHUNK, :]
            u = _dotb(Tm, v * bc)
            w = _dotb(Tm, k * (bc * eG))
            QK = _dotb(q, k, NT) * Dm
            S = S_scr[h]
            s_ref[0, h] = S
            vn = u - _dotb(w, S)
            o_ref[:, sl] = _dotb(q * eG, S) + _dotb(QK, vn)
            S_scr[h] = S * jnp.exp(Gl) + _dotb(k * jnp.exp(Gl - Gc), vn, TN)

    qkv_spec = [pl.BlockSpec((CHUNK, D), lambda n, cb=cb: (n, cb)) for cb in range(3)]
    return pl.pallas_call(
        body, name=name, grid=(N,),
        in_specs=qkv_spec + [pl.BlockSpec((CHUNK, LANES), lambda n: (n, bblk)),
                             pl.BlockSpec((CHUNK, LANES), lambda n: (n, bblk + 1)),
                             pl.BlockSpec((1, LANES), lambda n: (0, 0)), pl.BlockSpec((1, LANES), lambda n: (0, 0))],
        out_specs=[pl.BlockSpec((CHUNK, D), lambda n: (n, 0)),
                   pl.BlockSpec((1, H, CHUNK, CHUNK), lambda n: (n, 0, 0, 0)),
                   pl.BlockSpec((1, H, GDN_DK, GDN_DK), lambda n: (n, 0, 0, 0))],
        out_shape=[jax.ShapeDtypeStruct((T, D), F32), jax.ShapeDtypeStruct((N, H, CHUNK, CHUNK), F32),
                   jax.ShapeDtypeStruct((N, H, GDN_DK, GDN_DK), F32)],
        scratch_shapes=[pltpu.VMEM((H, GDN_DK, GDN_DK), F32)],
        compiler_params=_cparams(("arbitrary",)),
    )(qkvn, qkvn, qkvn, p, p, alog, dtb)


def _gdn_bwd(do, qkvn, p, alog, dtb, Tinv, Sin, H, name):
    T = qkvn.shape[0]
    D = H * GDN_DK
    N = T // CHUNK
    bblk = 8 * D // LANES

    def body(do_ref, q_ref, k_ref, v_ref, b_ref, a_ref, alog_ref, dtb_ref, t_ref, s_ref,
             dqkv_ref, dba_ref, dalog_ref, ddtb_ref, dS_scr):
        @pl.when(pl.program_id(0) == 0)
        def _():
            dS_scr[...] = jnp.zeros_like(dS_scr)
            dalog_ref[...] = jnp.zeros_like(dalog_ref)
            ddtb_ref[...] = jnp.zeros_like(ddtb_ref)

        beta, g, G, x = _gdn_gates(b_ref[...], a_ref[...], alog_ref[...], dtb_ref[...])
        r, c = _iota2((CHUNK, CHUNK), 0), _iota2((CHUNK, CHUNK), 1)
        eye, low, strict = r == c, r >= c, r > c
        lane = _iota2((CHUNK, LANES), 1)
        dG_all = jnp.zeros((CHUNK, LANES), F32)
        dbeta_all = jnp.zeros((CHUNK, LANES), F32)
        for h in range(H):
            sl = slice(h * GDN_DK, (h + 1) * GDN_DK)
            q, k, v, dov = q_ref[:, sl], k_ref[:, sl], v_ref[:, sl], do_ref[:, sl]
            Gc, bc = G[:, h:h + 1], beta[:, h:h + 1]
            Dm, KK, M, _, _ = _head_prep(k, Gc, bc)
            Tm, S, dSo = t_ref[0, h], s_ref[0, h], dS_scr[h]
            eG = jnp.exp(Gc)
            Gl = Gc[CHUNK - 1:CHUNK, :]
            eR, dch = jnp.exp(Gl - Gc), jnp.exp(Gl)
            vb, kb = v * bc, k * (bc * eG)
            u, w = _dotb(Tm, vb), _dotb(Tm, kb)
            QKr = _dotb(q, k, NT)
            QK = QKr * Dm
            qd, kd = q * eG, k * eR
            vn = u - _dotb(w, S)
            dqd = _dotb(dov, S, NT)
            dS = _dotb(qd, dov, TN)
            dQK = jnp.where(low, _dotb(dov, vn, NT), 0.0)
            dvn = _dotb(QK, dov, TN)
            ddch = jnp.sum(jnp.sum(dSo * S, axis=1, keepdims=True), axis=0, keepdims=True)
            dS = dS + dch * dSo
            dkd = _dotb(vn, dSo, NT)
            dvn = dvn + _dotb(kd, dSo)
            dw = -_dotb(dvn, S, NT)
            dS = dS - _dotb(w, dvn, TN)
            dS_scr[h] = dS
            dvb = _dotb(Tm, dvn, TN)
            dkb = _dotb(Tm, dw, TN)
            dA = -jnp.where(strict, _dotb(dvb, u, NT) + _dotb(dkb, w, NT), 0.0)
            rk = jnp.sum(dkb * k, axis=1, keepdims=True)
            dbeta = jnp.sum(dvb * v, axis=1, keepdims=True) + rk * eG + jnp.sum(dA * M, axis=1, keepdims=True)
            deG = rk * bc
            dM = dA * bc
            dKK = dM * Dm
            dQKr = dQK * Dm
            E = dM * M + dQK * QK
            dq = _dotb(dQKr, k) + dqd * eG
            dk = (dkb * (bc * eG) + _dotb(dKK, k) + _dotb(dKK, k, TN) + _dotb(dQKr, q, TN) + dkd * eR)
            deG = deG + jnp.sum(dqd * q, axis=1, keepdims=True)
            deR = jnp.sum(dkd * k, axis=1, keepdims=True)
            dGl = jnp.sum(deR * eR, axis=0, keepdims=True) + ddch * dch
            dGc = (jnp.sum(E, axis=1, keepdims=True) - _to_col(jnp.sum(E, axis=0, keepdims=True), eye)
                   + deG * eG - deR * eR + jnp.where(r[:, :1] == CHUNK - 1, dGl, 0.0))
            dqkv_ref[:, sl] = dq
            dqkv_ref[:, D + h * GDN_DK:D + (h + 1) * GDN_DK] = dk
            dqkv_ref[:, 2 * D + h * GDN_DK:2 * D + (h + 1) * GDN_DK] = dvb * bc
            dG_all = jnp.where(lane == h, dGc, dG_all)
            dbeta_all = jnp.where(lane == h, dbeta, dbeta_all)
        dg = _dot_sel(r <= c, dG_all, 3)
        da = dg * (-jnp.exp(alog_ref[...])) * _sigmoid(x)
        dba_ref[:, :LANES] = (dbeta_all * beta * (1.0 - beta)).astype(BF16)
        dba_ref[:, LANES:] = da.astype(BF16)
        dalog_ref[...] += _rsum(dg * g)
        ddtb_ref[...] += _rsum(da)

    rev = lambda n: N - 1 - n
    qkv_spec = [pl.BlockSpec((CHUNK, D), lambda n, cb=cb: (rev(n), cb)) for cb in range(3)]
    return pl.pallas_call(
        body, name=name, grid=(N,),
        in_specs=[pl.BlockSpec((CHUNK, D), lambda n: (rev(n), 0))] + qkv_spec + [
            pl.BlockSpec((CHUNK, LANES), lambda n: (rev(n), bblk)),
            pl.BlockSpec((CHUNK, LANES), lambda n: (rev(n), bblk + 1)),
            pl.BlockSpec((1, LANES), lambda n: (0, 0)), pl.BlockSpec((1, LANES), lambda n: (0, 0)),
            pl.BlockSpec((1, H, CHUNK, CHUNK), lambda n: (rev(n), 0, 0, 0)),
            pl.BlockSpec((1, H, GDN_DK, GDN_DK), lambda n: (rev(n), 0, 0, 0))],
        out_specs=[pl.BlockSpec((CHUNK, 3 * D), lambda n: (rev(n), 0)),
                   pl.BlockSpec((CHUNK, 2 * LANES), lambda n: (rev(n), 0)),
                   pl.BlockSpec((1, LANES), lambda n: (0, 0)), pl.BlockSpec((1, LANES), lambda n: (0, 0))],
        out_shape=[jax.ShapeDtypeStruct((T, 3 * D), F32), jax.ShapeDtypeStruct((T, 2 * LANES), BF16),
                   jax.ShapeDtypeStruct((1, LANES), F32), jax.ShapeDtypeStruct((1, LANES), F32)],
        scratch_shapes=[pltpu.VMEM((H, GDN_DK, GDN_DK), F32)],
        compiler_params=_cparams(("arbitrary",)),
    )(do, qkvn, qkvn, qkvn, p, p, alog, dtb, Tinv, Sin)


def _mix_in_reorder(w, D, H):
    o1 = 4 * D
    o2, o3 = o1 + H, o1 + 2 * H
    z = jnp.zeros((w.shape[0], LANES - H), w.dtype)
    return jnp.concatenate([w[:, :o1], w[:, o3:], w[:, o1:o2], z, w[:, o2:o3], z], axis=1)


def _mix_in_restore(dw, D, H):
    b0 = 8 * D
    return jnp.concatenate([dw[:, :4 * D], dw[:, b0:b0 + H], dw[:, b0 + LANES:b0 + LANES + H], dw[:, 4 * D:b0]], axis=1)


def _ffn_fwd(x, W, pre, tag):
    h = _rms_fwd(x, W[pre + "_norm_pre"], tag + "_pre")
    a = _matmul(h, W[pre + "_w_in"], "nn", F32, tag + "_in")
    s = _swiglu_fwd(a, tag + "_act")
    f = _matmul(s, W[pre + "_w_out"], "nn", F32, tag + "_out")
    return _post_fwd(x, f, W[pre + "_norm_post"], 0.5, tag + "_post"), (x, h, a, s, f)


def _ffn_bwd(dxn, saved, W, pre, tag):
    x, h, a, s, f = saved
    df, dpost = _post_bwd(dxn, f, W[pre + "_norm_post"], 0.5, tag + "_dpost")
    ds = _matmul(df, W[pre + "_w_out"], "nt", F32, tag + "_ds")
    dw_out = _matmul(s, df, "tn", F32, tag + "_dwout")
    da = _swiglu_bwd(ds, a, tag + "_dact")
    dh = _matmul(da, W[pre + "_w_in"], "nt", F32, tag + "_dh")
    dw_in = _matmul(h, da, "tn", F32, tag + "_dwin")
    dx, dpre = _pre_bwd(dh, x, W[pre + "_norm_pre"], dxn, tag + "_dpre")
    return dx, {pre + "_norm_pre": dpre, pre + "_norm_post": dpost, pre + "_w_in": dw_in, pre + "_w_out": dw_out}


def _mix_fwd(x, W, H, tag):
    h = _rms_fwd(x, W["mix_norm_pre"], tag + "_pre")
    p = _matmul(h, W["mix_w_in"], "nn", F32, tag + "_in")
    qkvn = _qkv_conv_fwd(p, W["gdn_conv_w"], H, tag + "_qkvconv")
    o, Tinv, Sin = _gdn_fwd(qkvn, p, W["gdn_a_log"], W["gdn_dt_bias"], H, tag + "_gdn")
    og = _gdn_gate_fwd(o, p, W["gdn_norm_w"], tag + "_gdngate")
    ya = _matmul(og, W["gdn_w_o"], "nn", F32, tag + "_gdno")
    hc = _glu_fwd(p, W["cnv_pw1_b"], tag + "_glu")
    hcv = _dw_conv_fwd(hc, W["cnv_dw_w"], W["cnv_dw_b"], tag + "_dwconv")
    hl = _ln_silu_fwd(hcv, W["cnv_ln_g"], W["cnv_ln_b"], tag + "_ln")
    yb = _matmul(hl, W["cnv_w_o"], "nn", F32, tag + "_cnvo")
    ym = _merge_fwd(p, ya, yb, W["cnv_b_o"], tag + "_merge")
    y = _matmul(ym, W["mix_w_out"], "nn", F32, tag + "_out")
    xn = _post_fwd(x, y, W["mix_norm_post"], 1.0, tag + "_post")
    return xn, (x, h, p, qkvn, o, Tinv, Sin, og, ya, hc, hcv, hl, yb, ym, y)


def _mix_bwd(dxn, saved, W, H, tag):
    x, h, p, qkvn, o, Tinv, Sin, og, ya, hc, hcv, hl, yb, ym, y = saved
    g = {}
    dy, g["mix_norm_post"] = _post_bwd(dxn, y, W["mix_norm_post"], 1.0, tag + "_dpost")
    dym = _matmul(dy, W["mix_w_out"], "nt", F32, tag + "_dym")
    g["mix_w_out"] = _matmul(ym, dy, "tn", F32, tag + "_dwout")
    dya, dyb, dgates, g["cnv_b_o"] = _merge_bwd(dym, p, ya, yb, W["cnv_b_o"], tag + "_dmerge")
    dhl = _matmul(dyb, W["cnv_w_o"], "nt", F32, tag + "_dhl")
    g["cnv_w_o"] = _matmul(hl, dyb, "tn", F32, tag + "_dwcnvo")
    dhcv, g["cnv_ln_g"], g["cnv_ln_b"], g["cnv_dw_b"] = _ln_silu_bwd(dhl, hcv, W["cnv_ln_g"], W["cnv_ln_b"], tag + "_dln")
    dhc, g["cnv_dw_w"] = _dw_conv_bwd(dhcv, hc, W["cnv_dw_w"], tag + "_ddwconv")
    dglu, g["cnv_pw1_b"] = _glu_bwd(dhc, p, W["cnv_pw1_b"], tag + "_dglu")
    dog = _matmul(dya, W["gdn_w_o"], "nt", F32, tag + "_dog")
    g["gdn_w_o"] = _matmul(og, dya, "tn", F32, tag + "_dwgdno")
    do, dz, g["gdn_norm_w"] = _gdn_gate_bwd(dog, o, p, W["gdn_norm_w"], tag + "_dgdngate")
    dqkvn, dba, g["gdn_a_log"], g["gdn_dt_bias"] = _gdn_bwd(
        do, qkvn, p, W["gdn_a_log"], W["gdn_dt_bias"], Tinv, Sin, H, tag + "_dgdn")
    dqkv, g["gdn_conv_w"] = _qkv_conv_bwd(dqkvn, p, W["gdn_conv_w"], H, tag + "_dqkvconv")
    dp = jnp.concatenate([dqkv, dz, dglu, dgates, dba], axis=1)
    dh = _matmul(dp, W["mix_w_in"], "nt", F32, tag + "_dh")
    g["mix_w_in"] = _matmul(h, dp, "tn", F32, tag + "_dwin")
    dx, g["mix_norm_pre"] = _pre_bwd(dh, x, W["mix_norm_pre"], dxn, tag + "_dpre")
    return dx, g


def _trunk_fwd_bwd(x, tgt, Ws, H):
    saved = []
    for i, W in enumerate(Ws):
        x, s1 = _ffn_fwd(x, W, "ffn1", f"l{i}_ffn1")
        x, s2 = _mix_fwd(x, W, H, f"l{i}_mix")
        x, s3 = _ffn_fwd(x, W, "ffn2", f"l{i}_ffn2")
        saved.append((s1, s2, s3))
    dx, loss = _loss_fwd_bwd(x, tgt, "loss")
    grads = [None] * len(Ws)
    for i in reversed(range(len(Ws))):
        s1, s2, s3 = saved[i]
        dx, g3 = _ffn_bwd(dx, s3, Ws[i], "ffn2", f"l{i}_ffn2")
        dx, g2 = _mix_bwd(dx, s2, Ws[i], H, f"l{i}_mix")
        dx, g1 = _ffn_bwd(dx, s1, Ws[i], "ffn1", f"l{i}_ffn1")
        grads[i] = {**g1, **g2, **g3}
    return loss, dx, grads


HBM_SPEC = pl.BlockSpec(memory_space=pltpu.HBM)


def _coords():
    return lax.axis_index("x"), lax.axis_index("y"), lax.axis_index("c")


def _all_gather(shards, name):
    n = len(shards)

    def body(*refs):
        ins, outs = refs[:n], refs[n:2 * n]
        send_sems, recv_sems, local_sems = refs[2 * n:]
        x, y, c = _coords()
        me, sibling = (x, y, c), (x, y, 1 - c)
        chips = [(1 - x, y), (x, 1 - y), (1 - x, 1 - y)]

        def copy(w, k, block, to, src=None):
            dst = outs[w].at[4 * block[0] + 2 * block[1] + block[2]]
            return pltpu.make_async_remote_copy(
                src_ref=dst if src is None else src, dst_ref=dst, send_sem=send_sems.at[w, k],
                recv_sem=recv_sems.at[w, k], device_id=to, device_id_type=MESH)

        mine = [pltpu.make_async_copy(ins[w], outs[w].at[4 * x + 2 * y + c], local_sems.at[w]) for w in range(n)]
        first = []
        for w in range(n):
            mine[w].start()
            first.append(copy(w, 0, me, sibling, src=ins[w]))
            first += [copy(w, 1 + j, me, (*chip, c), src=ins[w]) for j, chip in enumerate(chips)]
        for cp in first:
            cp.start()
        passed = []
        for j, chip in enumerate(chips):
            for w in range(n):
                copy(w, 1 + j, (*chip, c), me).wait_recv()
                fwd = copy(w, 4 + j, (*chip, c), sibling)
                fwd.start()
                passed.append(fwd)
        for w in range(n):
            copy(w, 0, sibling, me).wait_recv()
            for j, chip in enumerate(chips):
                copy(w, 4 + j, (*chip, 1 - c), me).wait_recv()
        for cp in first + passed:
            cp.wait_send()
        for cp in mine:
            cp.wait()

    return pl.pallas_call(
        body, name=name,
        out_shape=[jax.ShapeDtypeStruct((N_DEV,) + s.shape, s.dtype) for s in shards],
        in_specs=[HBM_SPEC] * n, out_specs=[HBM_SPEC] * n,
        scratch_shapes=[pltpu.SemaphoreType.DMA((n, 7)), pltpu.SemaphoreType.DMA((n, 7)), pltpu.SemaphoreType.DMA((n,))],
    )(*shards)


def _rs_sibling(Gs, name):
    n = len(Gs)

    def body(*refs):
        ins, outs = refs[:n], refs[n:2 * n]
        send_sems, recv_sems = refs[2 * n:]
        x, y, c = _coords()
        cps = []
        for w in range(n):
            for q in range(4):
                cp = pltpu.make_async_remote_copy(
                    src_ref=ins[w].at[2 * q + (1 - c)], dst_ref=outs[w].at[q], send_sem=send_sems.at[w, q],
                    recv_sem=recv_sems.at[w, q], device_id=(x, y, 1 - c), device_id_type=MESH)
                cp.start()
                cps.append(cp)
        for cp in cps:
            cp.wait()

    return pl.pallas_call(
        body, name=name,
        out_shape=[jax.ShapeDtypeStruct((4,) + g.shape[1:], g.dtype) for g in Gs],
        in_specs=[HBM_SPEC] * n, out_specs=[HBM_SPEC] * n,
        scratch_shapes=[pltpu.SemaphoreType.DMA((n, 4)), pltpu.SemaphoreType.DMA((n, 4))],
    )(*Gs)


def _rs_chips(Ps, name):
    n = len(Ps)

    def body(*refs):
        ins, outs = refs[:n], refs[n:2 * n]
        send_sems, recv_sems, local_sems = refs[2 * n:]
        x, y, c = _coords()
        me_q = 2 * x + y
        chips = [(1 - x, y), (x, 1 - y), (1 - x, 1 - y)]
        cps = []
        for w in range(n):
            loc = pltpu.make_async_copy(ins[w].at[me_q], outs[w].at[me_q], local_sems.at[w])
            loc.start()
            cps.append(loc)
            for j, (px, py) in enumerate(chips):
                cp = pltpu.make_async_remote_copy(
                    src_ref=ins[w].at[2 * px + py], dst_ref=outs[w].at[me_q], send_sem=send_sems.at[w, j],
                    recv_sem=recv_sems.at[w, j], device_id=(px, py, c), device_id_type=MESH)
                cp.start()
                cps.append(cp)
        for cp in cps:
            cp.wait()

    return pl.pallas_call(
        body, name=name,
        out_shape=[jax.ShapeDtypeStruct(p.shape, p.dtype) for p in Ps],
        in_specs=[HBM_SPEC] * n, out_specs=[HBM_SPEC] * n,
        scratch_shapes=[pltpu.SemaphoreType.DMA((n, 3)), pltpu.SemaphoreType.DMA((n, 3)), pltpu.SemaphoreType.DMA((n,))],
    )(*Ps)


def _row_tile(R, target=256):
    best = None
    for t in range(8, min(R, target) + 1, 8):
        if R % t == 0:
            best = t
    return best if best is not None else R


def _pair_add(G, R1, cidx, name):
    _, _, R, C = G.shape
    tb = _row_tile(R)

    def body(c_ref, g_ref, r_ref, o_ref):
        o_ref[...] = (g_ref[...] + r_ref[...]).astype(BF16)

    return pl.pallas_call(
        body, name=name,
        grid_spec=pltpu.PrefetchScalarGridSpec(
            num_scalar_prefetch=1, grid=(4, R // tb),
            in_specs=[pl.BlockSpec((None, None, tb, C), lambda q, i, cr: (q, cr[0], i, 0)),
                      pl.BlockSpec((None, tb, C), lambda q, i, cr: (q, i, 0))],
            out_specs=pl.BlockSpec((None, tb, C), lambda q, i, cr: (q, i, 0))),
        out_shape=jax.ShapeDtypeStruct((4, R, C), BF16),
        compiler_params=_cparams(("arbitrary", "arbitrary")),
    )(cidx, G, R1)


def _sum_parts(parts, name):
    P, R, C = parts.shape

    def body(p_ref, o_ref):
        acc = p_ref[0]
        for j in range(1, P):
            acc = acc + p_ref[j]
        o_ref[...] = acc

    return pl.pallas_call(
        body, name=name, out_shape=jax.ShapeDtypeStruct((R, C), F32),
        in_specs=[pl.BlockSpec(memory_space=pltpu.VMEM)], out_specs=pl.BlockSpec(memory_space=pltpu.VMEM),
        compiler_params=_cparams(),
    )(parts)


def _adamw(w, m, v, parts, name):
    R, C = w.shape
    P = parts.shape[0]
    tb = _row_tile(R)
    c1 = 1.0 / (1.0 - ADAM_B1 ** ADAM_STEP)
    c2 = 1.0 / (1.0 - ADAM_B2 ** ADAM_STEP)

    def body(w_ref, m_ref, v_ref, p_ref, g_ref, d_ref, nm_ref, nv_ref):
        g = p_ref[0].astype(F32)
        for j in range(1, P):
            g = g + p_ref[j].astype(F32)
        nm = ADAM_B1 * m_ref[...] + (1.0 - ADAM_B1) * g
        nv = ADAM_B2 * v_ref[...] + (1.0 - ADAM_B2) * (g * g)
        g_ref[...] = g
        nm_ref[...] = nm
        nv_ref[...] = nv
        d_ref[...] = -ADAM_LR * ((nm * c1) / (jnp.sqrt(nv * c2) + ADAM_EPS) + ADAM_WD * w_ref[...])

    blk = pl.BlockSpec((tb, C), lambda i: (i, 0))
    return pl.pallas_call(
        body, name=name, grid=(R // tb,),
        in_specs=[blk, blk, blk, pl.BlockSpec((P, tb, C), lambda i: (0, i, 0))],
        out_specs=[blk] * 4, out_shape=[jax.ShapeDtypeStruct((R, C), F32)] * 4,
        compiler_params=_cparams(("parallel",)),
    )(w, m, v, parts)


BIG = ("ffn1_w_in", "ffn1_w_out", "mix_w_in", "gdn_w_o", "cnv_w_o", "mix_w_out", "ffn2_w_in", "ffn2_w_out")
COL_SHARDED = ("ffn1_w_in", "mix_w_in", "ffn2_w_in")
SMALL_SHARDED = ("gdn_conv_w", "cnv_dw_w")
NAMES = ("ffn1_norm_pre", "ffn1_norm_post", "ffn1_w_in", "ffn1_w_out", "mix_norm_pre", "mix_norm_post", "mix_w_in",
         "gdn_conv_w", "gdn_a_log", "gdn_dt_bias", "gdn_norm_w", "gdn_w_o", "cnv_pw1_b", "cnv_dw_w", "cnv_dw_b",
         "cnv_ln_g", "cnv_ln_b", "cnv_w_o", "cnv_b_o", "mix_w_out", "ffn2_norm_pre", "ffn2_norm_post", "ffn2_w_in",
         "ffn2_w_out")
SMALL = tuple(n for n in NAMES if n not in BIG)


def _gathered_layer_weights(gath, params, i, D, H):
    W = {}
    for n in BIG:
        g = gath[n][:, i]
        if n in COL_SHARDED:
            g = jnp.transpose(g, (1, 0, 2)).reshape(g.shape[1], -1)
            if n == "mix_w_in":
                g = _mix_in_reorder(g, D, H)
        else:
            g = g.reshape(-1, g.shape[-1])
        W[n] = g
    for n in SMALL_SHARDED:
        g = gath[n][:, i]
        W[n] = jnp.transpose(g, (1, 0, 2)).reshape(g.shape[1], -1)
    for n in SMALL:
        if n in SMALL_SHARDED:
            continue
        v = params[n][i]
        if n in ("gdn_a_log", "gdn_dt_bias"):
            v = jnp.pad(v, (0, LANES - H))
        W[n] = v.reshape(1, -1)
    return W


def kernel(x, ffn1_norm_pre, ffn1_norm_post, ffn1_w_in, ffn1_w_out, mix_norm_pre, mix_norm_post, mix_w_in, gdn_conv_w, gdn_a_log, gdn_dt_bias, gdn_norm_w, gdn_w_o, cnv_pw1_b, cnv_dw_w, cnv_dw_b, cnv_ln_g, cnv_ln_b, cnv_w_o, cnv_b_o, mix_w_out, ffn2_norm_pre, ffn2_norm_post, ffn2_w_in, ffn2_w_out, loss_target, m_ffn1_norm_pre, m_ffn1_norm_post, m_ffn1_w_in, m_ffn1_w_out, m_mix_norm_pre, m_mix_norm_post, m_mix_w_in, m_gdn_conv_w, m_gdn_a_log, m_gdn_dt_bias, m_gdn_norm_w, m_gdn_w_o, m_cnv_pw1_b, m_cnv_dw_w, m_cnv_dw_b, m_cnv_ln_g, m_cnv_ln_b, m_cnv_w_o, m_cnv_b_o, m_mix_w_out, m_ffn2_norm_pre, m_ffn2_norm_post, m_ffn2_w_in, m_ffn2_w_out, v_ffn1_norm_pre, v_ffn1_norm_post, v_ffn1_w_in, v_ffn1_w_out, v_mix_norm_pre, v_mix_norm_post, v_mix_w_in, v_gdn_conv_w, v_gdn_a_log, v_gdn_dt_bias, v_gdn_norm_w, v_gdn_w_o, v_cnv_pw1_b, v_cnv_dw_w, v_cnv_dw_b, v_cnv_ln_g, v_cnv_ln_b, v_cnv_w_o, v_cnv_b_o, v_mix_w_out, v_ffn2_norm_pre, v_ffn2_norm_post, v_ffn2_w_in, v_ffn2_w_out):
    params = dict(zip(NAMES, (ffn1_norm_pre, ffn1_norm_post, ffn1_w_in, ffn1_w_out, mix_norm_pre, mix_norm_post, mix_w_in, gdn_conv_w, gdn_a_log, gdn_dt_bias, gdn_norm_w, gdn_w_o, cnv_pw1_b, cnv_dw_w, cnv_dw_b, cnv_ln_g, cnv_ln_b, cnv_w_o, cnv_b_o, mix_w_out, ffn2_norm_pre, ffn2_norm_post, ffn2_w_in, ffn2_w_out)))
    mom1 = dict(zip(NAMES, (m_ffn1_norm_pre, m_ffn1_norm_post, m_ffn1_w_in, m_ffn1_w_out, m_mix_norm_pre, m_mix_norm_post, m_mix_w_in, m_gdn_conv_w, m_gdn_a_log, m_gdn_dt_bias, m_gdn_norm_w, m_gdn_w_o, m_cnv_pw1_b, m_cnv_dw_w, m_cnv_dw_b, m_cnv_ln_g, m_cnv_ln_b, m_cnv_w_o, m_cnv_b_o, m_mix_w_out, m_ffn2_norm_pre, m_ffn2_norm_post, m_ffn2_w_in, m_ffn2_w_out)))
    mom2 = dict(zip(NAMES, (v_ffn1_norm_pre, v_ffn1_norm_post, v_ffn1_w_in, v_ffn1_w_out, v_mix_norm_pre, v_mix_norm_post, v_mix_w_in, v_gdn_conv_w, v_gdn_a_log, v_gdn_dt_bias, v_gdn_norm_w, v_gdn_w_o, v_cnv_pw1_b, v_cnv_dw_w, v_cnv_dw_b, v_cnv_ln_g, v_cnv_ln_b, v_cnv_w_o, v_cnv_b_o, v_mix_w_out, v_ffn2_norm_pre, v_ffn2_norm_post, v_ffn2_w_in, v_ffn2_w_out)))
    T, D = x.shape[1], x.shape[2]
    H = D // GDN_DK
    L = ffn1_norm_pre.shape[0]
    xi, yi, ci = _coords()
    dev = 4 * xi + 2 * yi + ci

    ag_names = BIG + SMALL_SHARDED
    gathered = _all_gather([params[n].astype(BF16) if n in BIG else params[n] for n in ag_names], "ag_weights")
    gath = dict(zip(ag_names, gathered))
    Ws = [_gathered_layer_weights(gath, params, i, D, H) for i in range(L)]

    loss_row, dx, grads = _trunk_fwd_bwd(x[0], loss_target[0], Ws, H)
    loss = lax.psum(loss_row[0, 0], ("x", "y", "c"))

    Gs = []
    for n in BIG:
        per_layer = []
        for i in range(L):
            g = grads[i][n]
            if n == "mix_w_in":
                g = _mix_in_restore(g, D, H)
            if n in COL_SHARDED:
                g = jnp.transpose(g.reshape(g.shape[0], N_DEV, -1), (1, 0, 2))
            else:
                g = g.reshape(N_DEV, -1, g.shape[-1])
            per_layer.append(g)
        Gs.append(jnp.stack(per_layer, axis=1))
    R1s = _rs_sibling(Gs, "rs_sibling")
    cidx = jnp.reshape(ci, (1,)).astype(jnp.int32)
    Ps = []
    for n, G, R1 in zip(BIG, Gs, R1s):
        rows, cols = G.shape[1] * G.shape[2], G.shape[3]
        Ps.append(_pair_add(G.reshape(4, 2, rows, cols), R1.reshape(4, rows, cols), cidx, "pair_add_" + n))
    R2s = dict(zip(BIG, _rs_chips(Ps, "rs_chips")))

    pieces = []
    for i in range(L):
        for n in SMALL:
            pieces.append(grads[i][n].reshape(-1, LANES))
    rows = sum(p.shape[0] for p in pieces)
    pad = (-rows) % 8
    if pad:
        pieces.append(jnp.zeros((pad, LANES), F32))
    small_all = _all_gather([jnp.concatenate(pieces, axis=0)], "ag_small_grads")[0]
    small_sum = _sum_parts(small_all, "sum_small_grads")
    small_g = {n: [None] * L for n in SMALL}
    off = 0
    for i in range(L):
        for n in SMALL:
            shape = grads[i][n].shape
            cnt = shape[0] * shape[1] // LANES
            g = small_sum[off:off + cnt].reshape(shape)
            off += cnt
            if n in ("gdn_a_log", "gdn_dt_bias"):
                g = g[:, :H]
            if n in SMALL_SHARDED:
                wloc = params[n].shape[-1]
                g = lax.dynamic_slice_in_dim(g, dev * wloc, wloc, axis=1)
            small_g[n][i] = g

    outs = {}
    for n in NAMES:
        w, m, v = params[n], mom1[n], mom2[n]
        if n in BIG:
            rows, cols = w.shape[0] * w.shape[1], w.shape[2]
            parts = R2s[n]
        else:
            rows, cols = (w.shape[0] * w.shape[1], w.shape[2]) if w.ndim == 3 else w.shape
            parts = jnp.stack(small_g[n], axis=0).reshape(1, rows, cols)
        res = _adamw(w.reshape(rows, cols), m.reshape(rows, cols), v.reshape(rows, cols), parts, "adamw_" + n)
        outs[n] = [r.reshape(w.shape) for r in res]

    result = [loss, dx[None]]
    for k in range(4):
        result += [outs[n][k] for n in NAMES]
    return tuple(result)
```

```python
import jax
import jax.numpy as jnp
from jax import lax
from jax.experimental import pallas as pl
from jax.experimental.pallas import tpu as pltpu

F32 = jnp.float32
BF16 = jnp.bfloat16

GDN_DK = 128
CHUNK = 64
GDN_CONV = 4
CNV_K = 31
RMS_EPS = 1e-6
LN_EPS = 1e-5
L2_EPS = 1e-6
ADAM_LR = 0.001
ADAM_B1 = 0.9
ADAM_B2 = 0.999
ADAM_EPS = 1e-08
ADAM_WD = 0.01
ADAM_STEP = 10

LANES = 128
SUB = 16
VMEM_LIMIT = 56 * 1024 * 1024
N_DEV = 8
MESH = pl.DeviceIdType.MESH


def _cparams(sem=None, **kw):
    if sem is not None:
        kw["dimension_semantics"] = sem
    return pltpu.CompilerParams(vmem_limit_bytes=VMEM_LIMIT, **kw)


def _tile(dim, target):
    best = None
    for t in range(LANES, min(dim, target) + 1, LANES):
        if dim % t == 0:
            best = t
    return best if best is not None else dim


def _sigmoid(x):
    return 1.0 / (1.0 + jnp.exp(-x))


def _silu(x):
    return x * _sigmoid(x)


def _dsilu(x):
    s = _sigmoid(x)
    return s * (1.0 + x * (1.0 - s))


MM_VMEM_BUDGET = 40 * 1024 * 1024
MM_MAX_TILE = 2048


def _mm_tiles(M, N, K, out_bytes):
    def cands(dim):
        c = [t for t in range(LANES, min(dim, MM_MAX_TILE) + 1, LANES) if dim % t == 0]
        return c or [dim]
    best = None
    for tm in cands(M):
        for tn in cands(N):
            vm = 2 * (2 * K * (tm + tn) + tm * tn * out_bytes)
            if vm <= MM_VMEM_BUDGET and (best is None or tm * tn > best[0] * best[1]):
                best = (tm, tn)
    return best if best is not None else (cands(M)[0], cands(N)[0])


def _matmul(a, b, mode, out_dtype, name):
    if mode == "nn":
        (M, K), N = a.shape, b.shape[1]
    elif mode == "nt":
        (M, K), N = a.shape, b.shape[0]
    else:
        (K, M), N = a.shape, b.shape[1]
    tm, tn = _mm_tiles(M, N, K, jnp.dtype(out_dtype).itemsize)
    if mode == "nn":
        a_spec = pl.BlockSpec((tm, K), lambda j, i: (i, 0))
        b_spec = pl.BlockSpec((K, tn), lambda j, i: (0, j))
        dn = (((1,), (0,)), ((), ()))
    elif mode == "nt":
        a_spec = pl.BlockSpec((tm, K), lambda j, i: (i, 0))
        b_spec = pl.BlockSpec((tn, K), lambda j, i: (j, 0))
        dn = (((1,), (1,)), ((), ()))
    else:
        a_spec = pl.BlockSpec((K, tm), lambda j, i: (0, i))
        b_spec = pl.BlockSpec((K, tn), lambda j, i: (0, j))
        dn = (((0,), (0,)), ((), ()))

    def body(a_ref, b_ref, o_ref):
        o_ref[...] = lax.dot_general(a_ref[...], b_ref[...], dn, preferred_element_type=F32).astype(out_dtype)

    return pl.pallas_call(
        body, name=name, grid=(N // tn, M // tm), in_specs=[a_spec, b_spec],
        out_specs=pl.BlockSpec((tm, tn), lambda j, i: (i, j)),
        out_shape=jax.ShapeDtypeStruct((M, N), out_dtype),
        compiler_params=_cparams(("parallel", "parallel")),
    )(a, b)


def _rowcall(name, body, T, tb, row_ins, par_ins, row_outs, acc_outs):
    n_ri, n_pi, n_ro = len(row_ins), len(par_ins), len(row_outs)

    def kern(*refs):
        ri, pi = refs[:n_ri], refs[n_ri:n_ri + n_pi]
        ro, ao = refs[n_ri + n_pi:n_ri + n_pi + n_ro], refs[n_ri + n_pi + n_ro:]
        if ao:
            @pl.when(pl.program_id(0) == 0)
            def _():
                for r in ao:
                    r[...] = jnp.zeros_like(r)
        body(ri, pi, ro, ao)

    in_specs = [pl.BlockSpec((tb, w), lambda i, cb=cb: (i, cb)) for (_, w, cb) in row_ins]
    in_specs += [pl.BlockSpec(p.shape, lambda i: (0, 0)) for p in par_ins]
    out_specs = [pl.BlockSpec((tb, w), lambda i: (i, 0)) for (w, _) in row_outs]
    out_specs += [pl.BlockSpec((1, w), lambda i: (0, 0)) for w in acc_outs]
    out_shape = [jax.ShapeDtypeStruct((T, w), dt) for (w, dt) in row_outs]
    out_shape += [jax.ShapeDtypeStruct((1, w), F32) for w in acc_outs]
    return pl.pallas_call(
        kern, name=name, grid=(T // tb,), in_specs=in_specs, out_specs=out_specs, out_shape=out_shape,
        compiler_params=_cparams(("arbitrary",)),
    )(*[a for (a, _, _) in row_ins], *par_ins)


def _rsum(x):
    return jnp.sum(x, axis=0, keepdims=True)


def _rms_rstd(x):
    return lax.rsqrt(jnp.mean(x * x, axis=-1, keepdims=True) + RMS_EPS)


def _rms_fwd(x, w, name):
    T, D = x.shape

    def body(ri, pi, ro, ao):
        xv = ri[0][...]
        ro[0][...] = (xv * _rms_rstd(xv) * pi[0][...]).astype(BF16)

    return _rowcall(name, body, T, 256, [(x, D, 0)], [w], [(D, BF16)], [])[0]


def _rms_bwd_core(dy, x, w):
    rs = _rms_rstd(x)
    xh = x * rs
    gw = dy * w
    dx = rs * (gw - xh * jnp.mean(gw * xh, axis=-1, keepdims=True))
    return dx, dy * xh


def _pre_bwd(dh, x, w, dres, name):
    T, D = x.shape

    def body(ri, pi, ro, ao):
        dx, dwc = _rms_bwd_core(ri[0][...], ri[1][...], pi[0][...])
        ro[0][...] = ri[2][...] + dx
        ao[0][...] += _rsum(dwc)

    return _rowcall(name, body, T, 256, [(dh, D, 0), (x, D, 0), (dres, D, 0)], [w], [(D, F32)], [D])


def _post_fwd(x, f, w, r, name):
    T, D = x.shape

    def body(ri, pi, ro, ao):
        fv = ri[1][...]
        ro[0][...] = ri[0][...] + r * (fv * _rms_rstd(fv) * pi[0][...])

    return _rowcall(name, body, T, 256, [(x, D, 0), (f, D, 0)], [w], [(D, F32)], [])[0]


def _post_bwd(dxn, f, w, r, name):
    T, D = f.shape

    def body(ri, pi, ro, ao):
        df, dwc = _rms_bwd_core(r * ri[0][...], ri[1][...], pi[0][...])
        ro[0][...] = df.astype(BF16)
        ao[0][...] += _rsum(dwc)

    return _rowcall(name, body, T, 256, [(dxn, D, 0), (f, D, 0)], [w], [(D, BF16)], [D])


def _swiglu_fwd(a, name):
    T, F2 = a.shape
    F = F2 // 2

    def body(ri, pi, ro, ao):
        ro[0][...] = (_silu(ri[0][...]) * ri[1][...]).astype(BF16)

    return _rowcall(name, body, T, 256, [(a, F, 0), (a, F, 1)], [], [(F, BF16)], [])[0]


def _swiglu_bwd(ds, a, name):
    T, F2 = a.shape
    F = F2 // 2

    def body(ri, pi, ro, ao):
        dsv, g, u = ri[0][...], ri[1][...], ri[2][...]
        ro[0][:, :F] = (dsv * u * _dsilu(g)).astype(BF16)
        ro[0][:, F:] = (dsv * _silu(g)).astype(BF16)

    return _rowcall(name, body, T, 256, [(ds, F, 0), (a, F, 0), (a, F, 1)], [], [(F2, BF16)], [])[0]


def _loss_fwd_bwd(y, tgt, name):
    T, D = y.shape

    def body(ri, pi, ro, ao):
        e = ri[0][...] - ri[1][...]
        ro[0][...] = e * (1.0 / D)
        tot = jnp.sum(_rsum(e * e), axis=1, keepdims=True) * (0.5 / D)
        ao[0][...] += jnp.broadcast_to(tot, (1, LANES))

    return _rowcall(name, body, T, 256, [(y, D, 0), (tgt, D, 0)], [], [(D, F32)], [LANES])


def _gdn_gate_fwd(o, p, nw, name):
    T, D = o.shape
    H = D // GDN_DK

    def body(ri, pi, ro, ao):
        for h in range(H):
            sl = slice(h * GDN_DK, (h + 1) * GDN_DK)
            oh = ri[0][:, sl]
            ro[0][:, sl] = (oh * _rms_rstd(oh) * pi[0][...] * _silu(ri[1][:, sl])).astype(BF16)

    return _rowcall(name, body, T, 256, [(o, D, 0), (p, D, 3)], [nw], [(D, BF16)], [])[0]


def _gdn_gate_bwd(dog, o, p, nw, name):
    T, D = o.shape
    H = D // GDN_DK

    def body(ri, pi, ro, ao):
        acc = jnp.zeros((1, GDN_DK), F32)
        for h in range(H):
            sl = slice(h * GDN_DK, (h + 1) * GDN_DK)
            dy, oh, z = ri[0][:, sl], ri[1][:, sl], ri[2][:, sl]
            sz = _silu(z)
            do, dwc = _rms_bwd_core(dy * sz, oh, pi[0][...])
            ro[0][:, sl] = do
            ro[1][:, sl] = (dy * oh * _rms_rstd(oh) * pi[0][...] * _dsilu(z)).astype(BF16)
            acc = acc + _rsum(dwc)
        ao[0][...] += acc

    return _rowcall(name, body, T, 256, [(dog, D, 0), (o, D, 0), (p, D, 3)], [nw], [(D, F32), (D, BF16)], [GDN_DK])


def _glu_fwd(p, b, name):
    T = p.shape[0]
    D = b.shape[1] // 2

    def body(ri, pi, ro, ao):
        ro[0][...] = (ri[0][...] + pi[0][:, :D]) * _sigmoid(ri[1][...] + pi[0][:, D:])

    return _rowcall(name, body, T, 256, [(p, D, 4), (p, D, 5)], [b], [(D, F32)], [])[0]


def _glu_bwd(dhc, p, b, name):
    T = p.shape[0]
    D = b.shape[1] // 2

    def body(ri, pi, ro, ao):
        d, a, g = ri[0][...], ri[1][...] + pi[0][:, :D], ri[2][...] + pi[0][:, D:]
        sg = _sigmoid(g)
        da, dg = d * sg, d * a * sg * (1.0 - sg)
        ro[0][:, :D] = da.astype(BF16)
        ro[0][:, D:] = dg.astype(BF16)
        ao[0][:, :D] += _rsum(da)
        ao[0][:, D:] += _rsum(dg)

    return _rowcall(name, body, T, 256, [(dhc, D, 0), (p, D, 4), (p, D, 5)], [b], [(2 * D, BF16)], [2 * D])


def _ln_stats(x):
    mu = jnp.mean(x, axis=-1, keepdims=True)
    xc = x - mu
    rstd = lax.rsqrt(jnp.mean(xc * xc, axis=-1, keepdims=True) + LN_EPS)
    return xc * rstd, rstd


def _ln_silu_fwd(hcv, g, b, name):
    T, D = hcv.shape

    def body(ri, pi, ro, ao):
        xh, _ = _ln_stats(ri[0][...])
        ro[0][...] = _silu(xh * pi[0][...] + pi[1][...]).astype(BF16)

    return _rowcall(name, body, T, 256, [(hcv, D, 0)], [g, b], [(D, BF16)], [])[0]


def _ln_silu_bwd(dhl, hcv, g, b, name):
    T, D = hcv.shape

    def body(ri, pi, ro, ao):
        xh, rstd = _ln_stats(ri[1][...])
        dyl = ri[0][...] * _dsilu(xh * pi[0][...] + pi[1][...])
        dxh = dyl * pi[0][...]
        dx = rstd * (dxh - jnp.mean(dxh, axis=-1, keepdims=True) - xh * jnp.mean(dxh * xh, axis=-1, keepdims=True))
        ro[0][...] = dx
        ao[0][...] += _rsum(dyl * xh)
        ao[1][...] += _rsum(dyl)
        ao[2][...] += _rsum(dx)

    return _rowcall(name, body, T, 256, [(dhl, D, 0), (hcv, D, 0)], [g, b], [(D, F32)], [D, D, D])


def _merge_fwd(p, ya, yb, bo, name):
    T, D = ya.shape

    def body(ri, pi, ro, ao):
        ro[0][...] = (_sigmoid(ri[0][...]) * ri[2][...] + _sigmoid(ri[1][...]) * (ri[3][...] + pi[0][...])).astype(BF16)

    return _rowcall(name, body, T, 256, [(p, D, 6), (p, D, 7), (ya, D, 0), (yb, D, 0)], [bo], [(D, BF16)], [])[0]


def _merge_bwd(dym, p, ya, yb, bo, name):
    T, D = ya.shape

    def body(ri, pi, ro, ao):
        d = ri[0][...]
        ga, gb = _sigmoid(ri[1][...]), _sigmoid(ri[2][...])
        ybv = ri[4][...] + pi[0][...]
        dyb = d * gb
        ro[0][...] = (d * ga).astype(BF16)
        ro[1][...] = dyb.astype(BF16)
        ro[2][:, :D] = (d * ri[3][...] * ga * (1.0 - ga)).astype(BF16)
        ro[2][:, D:] = (d * ybv * gb * (1.0 - gb)).astype(BF16)
        ao[0][...] += _rsum(dyb)

    return _rowcall(name, body, T, 256, [(dym, D, 0), (p, D, 6), (p, D, 7), (ya, D, 0), (yb, D, 0)], [bo],
                    [(D, BF16), (D, BF16), (2 * D, BF16)], [D])


PAD = 32
RC = 256


def _causal_taps(xp_ref, w, K, c0):
    acc = None
    for j in range(K):
        term = w[j:j + 1, :] * xp_ref[pl.ds(PAD - (K - 1) + j + c0, RC), :]
        acc = term if acc is None else acc + term
    return acc


def _anticausal_taps(dp_ref, w, K, c0):
    acc = None
    for j in range(K):
        term = w[j:j + 1, :] * dp_ref[pl.ds((K - 1) - j + c0, RC), :]
        acc = term if acc is None else acc + term
    return acc


def _tap_grads(dw_ref, dc_ref, xp_ref, K, T):
    for j in range(K):
        acc = jnp.zeros((1, LANES), F32)
        for c in range(T // RC):
            acc = acc + _rsum(dc_ref[pl.ds(c * RC, RC), :] * xp_ref[pl.ds(PAD - (K - 1) + j + c * RC, RC), :])
        dw_ref[j:j + 1, :] = acc


def _qkv_conv_fwd(p, cw, H, name):
    T = p.shape[0]
    K = cw.shape[0]

    def body(x_ref, w_ref, o_ref, xp_ref):
        j = pl.program_id(0)
        xp_ref[pl.ds(0, PAD), :] = jnp.zeros((PAD, LANES), F32)
        xp_ref[pl.ds(PAD, T), :] = x_ref[...]
        w = w_ref[...]
        scale = jnp.where(j < H, GDN_DK ** -0.5, 1.0).astype(F32)
        for c in range(T // RC):
            act = _silu(_causal_taps(xp_ref, w, K, c * RC))
            nrm = act * lax.rsqrt(jnp.sum(act * act, axis=-1, keepdims=True) + L2_EPS) * scale
            o_ref[pl.ds(c * RC, RC), :] = jnp.where(j < 2 * H, nrm, act)

    return pl.pallas_call(
        body, name=name, grid=(3 * H,),
        in_specs=[pl.BlockSpec((T, LANES), lambda j: (0, j)), pl.BlockSpec((K, LANES), lambda j: (0, j))],
        out_specs=pl.BlockSpec((T, LANES), lambda j: (0, j)),
        out_shape=jax.ShapeDtypeStruct((T, 3 * H * GDN_DK), F32),
        scratch_shapes=[pltpu.VMEM((T + PAD, LANES), F32)],
        compiler_params=_cparams(("arbitrary",)),
    )(p, cw)


def _qkv_conv_bwd(dn, p, cw, H, name):
    T = p.shape[0]
    K = cw.shape[0]

    def body(dn_ref, x_ref, w_ref, dx_ref, dw_ref, xp_ref, dc_ref):
        j = pl.program_id(0)
        xp_ref[pl.ds(0, PAD), :] = jnp.zeros((PAD, LANES), F32)
        xp_ref[pl.ds(PAD, T), :] = x_ref[...]
        dc_ref[pl.ds(T, PAD), :] = jnp.zeros((PAD, LANES), F32)
        w = w_ref[...]
        scale = jnp.where(j < H, GDN_DK ** -0.5, 1.0).astype(F32)
        for c in range(T // RC):
            pre = _causal_taps(xp_ref, w, K, c * RC)
            act = _silu(pre)
            d = dn_ref[pl.ds(c * RC, RC), :]
            rs = lax.rsqrt(jnp.sum(act * act, axis=-1, keepdims=True) + L2_EPS)
            nh = act * rs
            dact_n = scale * rs * (d - nh * jnp.sum(d * nh, axis=-1, keepdims=True))
            dact = jnp.where(j < 2 * H, dact_n, d)
            dc_ref[pl.ds(c * RC, RC), :] = dact * _dsilu(pre)
        for c in range(T // RC):
            dx_ref[pl.ds(c * RC, RC), :] = _anticausal_taps(dc_ref, w, K, c * RC).astype(BF16)
        _tap_grads(dw_ref, dc_ref, xp_ref, K, T)

    return pl.pallas_call(
        body, name=name, grid=(3 * H,),
        in_specs=[pl.BlockSpec((T, LANES), lambda j: (0, j)), pl.BlockSpec((T, LANES), lambda j: (0, j)),
                  pl.BlockSpec((K, LANES), lambda j: (0, j))],
        out_specs=[pl.BlockSpec((T, LANES), lambda j: (0, j)), pl.BlockSpec((K, LANES), lambda j: (0, j))],
        out_shape=[jax.ShapeDtypeStruct((T, 3 * H * GDN_DK), BF16), jax.ShapeDtypeStruct(cw.shape, F32)],
        scratch_shapes=[pltpu.VMEM((T + PAD, LANES), F32), pltpu.VMEM((T + PAD, LANES), F32)],
        compiler_params=_cparams(("arbitrary",)),
    )(dn, p, cw)


def _dw_conv_fwd(hc, w, b, name):
    T, D = hc.shape
    K = w.shape[0]

    def body(x_ref, w_ref, b_ref, o_ref, xp_ref):
        xp_ref[pl.ds(0, PAD), :] = jnp.zeros((PAD, LANES), F32)
        xp_ref[pl.ds(PAD, T), :] = x_ref[...]
        wv = w_ref[...]
        for c in range(T // RC):
            o_ref[pl.ds(c * RC, RC), :] = _causal_taps(xp_ref, wv, K, c * RC) + b_ref[...]

    return pl.pallas_call(
        body, name=name, grid=(D // LANES,),
        in_specs=[pl.BlockSpec((T, LANES), lambda j: (0, j)), pl.BlockSpec((K, LANES), lambda j: (0, j)),
                  pl.BlockSpec((1, LANES), lambda j: (0, j))],
        out_specs=pl.BlockSpec((T, LANES), lambda j: (0, j)),
        out_shape=jax.ShapeDtypeStruct((T, D), F32),
        scratch_shapes=[pltpu.VMEM((T + PAD, LANES), F32)],
        compiler_params=_cparams(("arbitrary",)),
    )(hc, w, b)


def _dw_conv_bwd(dy, hc, w, name):
    T, D = hc.shape
    K = w.shape[0]

    def body(dy_ref, x_ref, w_ref, dx_ref, dw_ref, xp_ref, dc_ref):
        xp_ref[pl.ds(0, PAD), :] = jnp.zeros((PAD, LANES), F32)
        xp_ref[pl.ds(PAD, T), :] = x_ref[...]
        dc_ref[pl.ds(T, PAD), :] = jnp.zeros((PAD, LANES), F32)
        dc_ref[pl.ds(0, T), :] = dy_ref[...]
        wv = w_ref[...]
        for c in range(T // RC):
            dx_ref[pl.ds(c * RC, RC), :] = _anticausal_taps(dc_ref, wv, K, c * RC)
        _tap_grads(dw_ref, dc_ref, xp_ref, K, T)

    return pl.pallas_call(
        body, name=name, grid=(D // LANES,),
        in_specs=[pl.BlockSpec((T, LANES), lambda j: (0, j)), pl.BlockSpec((T, LANES), lambda j: (0, j)),
                  pl.BlockSpec((K, LANES), lambda j: (0, j))],
        out_specs=[pl.BlockSpec((T, LANES), lambda j: (0, j)), pl.BlockSpec((K, LANES), lambda j: (0, j))],
        out_shape=[jax.ShapeDtypeStruct((T, D), F32), jax.ShapeDtypeStruct(w.shape, F32)],
        scratch_shapes=[pltpu.VMEM((T + PAD, LANES), F32), pltpu.VMEM((T + PAD, LANES), F32)],
        compiler_params=_cparams(("arbitrary",)),
    )(dy, hc, w)


NN = (((1,), (0,)), ((), ()))
NT = (((1,), (1,)), ((), ()))
TN = (((0,), (0,)), ((), ()))


def _dotb(a, b, dn=NN):
    return lax.dot_general(a.astype(BF16), b.astype(BF16), dn, preferred_element_type=F32)


def _split_bf16(x, n):
    parts, r = [], x
    for _ in range(n):
        p = r.astype(BF16)
        parts.append(p)
        r = r - p.astype(F32)
    return parts


def _dot3(a, b, dn=NN):
    (ah, al), (bh, bl) = _split_bf16(a, 2), _split_bf16(b, 2)
    d = lambda u, v: lax.dot_general(u, v, dn, preferred_element_type=F32)
    return d(ah, bh) + (d(al, bh) + d(ah, bl))


def _dot_sel(sel, x, pieces, sel_left=True):
    sb = sel.astype(BF16)
    acc = None
    for p in _split_bf16(x, pieces):
        t = (lax.dot_general(sb, p, NN, preferred_element_type=F32) if sel_left
             else lax.dot_general(p, sb, NN, preferred_element_type=F32))
        acc = t if acc is None else acc + t
    return acc


def _iota2(shape, axis):
    return lax.broadcasted_iota(jnp.int32, shape, axis)


def _to_row(col, eye):
    return jnp.sum(jnp.where(eye, col, 0.0), axis=0, keepdims=True)


def _to_col(row, eye):
    return jnp.sum(jnp.where(eye, row, 0.0), axis=1, keepdims=True)


def _gdn_gates(bl, al, alog, dtb):
    beta = _sigmoid(bl)
    x = al + dtb
    sp = jnp.maximum(x, 0.0) + jnp.log(1.0 + jnp.exp(-jnp.abs(x)))
    g = -jnp.exp(alog) * sp
    r, c = _iota2((CHUNK, CHUNK), 0), _iota2((CHUNK, CHUNK), 1)
    G = _dot_sel(r >= c, g, 3)
    return beta, g, G, x


def _head_prep(k, Gc, bc):
    r, c = _iota2((CHUNK, CHUNK), 0), _iota2((CHUNK, CHUNK), 1)
    eye = r == c
    Gr, br = _to_row(Gc, eye), _to_row(bc, eye)
    low, up = r >= c, r <= c
    Dm = jnp.where(low, jnp.exp(jnp.where(low, Gc - Gr, 0.0)), 0.0)
    Dt = jnp.where(up, jnp.exp(jnp.where(up, Gr - Gc, 0.0)), 0.0)
    KK = _dotb(k, k, NT)
    M = jnp.where(r > c, KK * Dm, 0.0)
    At = jnp.where(r < c, KK * Dt, 0.0) * br
    return Dm, KK, M, M * bc, At


def _unit_lower_inverses(As, Ats):
    n = len(As)
    nb = CHUNK // SUB
    lane = _iota2((SUB, CHUNK), 1)
    row = _iota2((SUB, CHUNK), 0)
    Atp = []
    for At in Ats:
        acc = jnp.zeros((SUB, CHUNK), F32)
        for b in range(nb):
            acc = jnp.where(lane // SUB == b, At[b * SUB:(b + 1) * SUB, :], acc)
        Atp.append(acc)
    gr, gc = _iota2((CHUNK, CHUNK), 0), _iota2((CHUNK, CHUNK), 1)
    ones_bd = gr // SUB == gc // SUB
    stack = jnp.concatenate(
        [jnp.where(lane % SUB == i, Atp[m], 0.0) for i in range(1, SUB) for m in range(n)], axis=0)
    Cm = _dot_sel(ones_bd, stack, 2, sel_left=False)
    Z = [(row == lane % SUB).astype(F32) for _ in range(n)]
    for i in range(1, SUB):
        for m in range(n):
            cm = Cm[((i - 1) * n + m) * SUB:((i - 1) * n + m + 1) * SUB, :]
            new = -jnp.sum(cm * Z[m], axis=0, keepdims=True)
            Z[m] = Z[m] + jnp.where(row == i, new, 0.0)
    out = []
    bd = gr // SUB == gc // SUB
    for m in range(n):
        X = jnp.where(bd, jnp.concatenate([Z[m]] * nb, axis=0), 0.0)
        blk = SUB
        while blk < CHUNK:
            N = jnp.where((gr // (2 * blk) == gc // (2 * blk)) & (gr // blk != gc // blk), As[m], 0.0)
            X = X - _dot3(_dot3(X, N), X)
            blk *= 2
        out.append(X)
    return out


def _gdn_fwd(qkvn, p, alog, dtb, H, name):
    T = qkvn.shape[0]
    D = H * GDN_DK
    N = T // CHUNK
    bblk = 8 * D // LANES

    def body(q_ref, k_ref, v_ref, b_ref, a_ref, alog_ref, dtb_ref, o_ref, t_ref, s_ref, S_scr):
        @pl.when(pl.program_id(0) == 0)
        def _():
            S_scr[...] = jnp.zeros_like(S_scr)

        beta, _, G, _ = _gdn_gates(b_ref[...], a_ref[...], alog_ref[...], dtb_ref[...])
        prep = []
        for h in range(H):
            sl = slice(h * GDN_DK, (h + 1) * GDN_DK)
            prep.append(_head_prep(k_ref[:, sl], G[:, h:h + 1], beta[:, h:h + 1]))
        Ts = _unit_lower_inverses([pr[3] for pr in prep], [pr[4] for pr in prep])
        for h in range(H):
            sl = slice(h * GDN_DK, (h + 1) * GDN_DK)
            q, k, v = q_ref[:, sl], k_ref[:, sl], v_ref[:, sl]
            Gc, bc = G[:, h:h + 1], beta[:, h:h + 1]
            Dm = prep[h][0]
            Tm = Ts[h]
            t_ref[0, h] = Tm
            eG = jnp.exp(Gc)
            Gl = Gc[CHUNK - 1:CHUNK, :]
            u = _dotb(Tm, v * bc)
            w = _dotb(Tm, k * (bc * eG))
            QK = _dotb(q, k, NT) * Dm
            S = S_scr[h]
            s_ref[0, h] = S
            vn = u - _dotb(w, S)
            o_ref[:, sl] = _dotb(q * eG, S) + _dotb(QK, vn)
            S_scr[h] = S * jnp.exp(Gl) + _dotb(k * jnp.exp(Gl - Gc), vn, TN)

    qkv_spec = [pl.BlockSpec((CHUNK, D), lambda n, cb=cb: (n, cb)) for cb in range(3)]
    return pl.pallas_call(
        body, name=name, grid=(N,),
        in_specs=qkv_spec + [pl.BlockSpec((CHUNK, LANES), lambda n: (n, bblk)),
                             pl.BlockSpec((CHUNK, LANES), lambda n: (n, bblk + 1)),
                             pl.BlockSpec((1, LANES), lambda n: (0, 0)), pl.BlockSpec((1, LANES), lambda n: (0, 0))],
        out_specs=[pl.BlockSpec((CHUNK, D), lambda n: (n, 0)),
                   pl.BlockSpec((1, H, CHUNK, CHUNK), lambda n: (n, 0, 0, 0)),
                   pl.BlockSpec((1, H, GDN_DK, GDN_DK), lambda n: (n, 0, 0, 0))],
        out_shape=[jax.ShapeDtypeStruct((T, D), F32), jax.ShapeDtypeStruct((N, H, CHUNK, CHUNK), F32),
                   jax.ShapeDtypeStruct((N, H, GDN_DK, GDN_DK), F32)],
        scratch_shapes=[pltpu.VMEM((H, GDN_DK, GDN_DK), F32)],
        compiler_params=_cparams(("arbitrary",)),
    )(qkvn, qkvn, qkvn, p, p, alog, dtb)


def _gdn_bwd(do, qkvn, p, alog, dtb, Tinv, Sin, H, name):
    T = qkvn.shape[0]
    D = H * GDN_DK
    N = T // CHUNK
    bblk = 8 * D // LANES

    def body(do_ref, q_ref, k_ref, v_ref, b_ref, a_ref, alog_ref, dtb_ref, t_ref, s_ref,
             dqkv_ref, dba_ref, dalog_ref, ddtb_ref, dS_scr):
        @pl.when(pl.program_id(0) == 0)
        def _():
            dS_scr[...] = jnp.zeros_like(dS_scr)
            dalog_ref[...] = jnp.zeros_like(dalog_ref)
            ddtb_ref[...] = jnp.zeros_like(ddtb_ref)

        beta, g, G, x = _gdn_gates(b_ref[...], a_ref[...], alog_ref[...], dtb_ref[...])
        r, c = _iota2((CHUNK, CHUNK), 0), _iota2((CHUNK, CHUNK), 1)
        eye, low, strict = r == c, r >= c, r > c
        lane = _iota2((CHUNK, LANES), 1)
        dG_all = jnp.zeros((CHUNK, LANES), F32)
        dbeta_all = jnp.zeros((CHUNK, LANES), F32)
        for h in range(H):
            sl = slice(h * GDN_DK, (h + 1) * GDN_DK)
            q, k, v, dov = q_ref[:, sl], k_ref[:, sl], v_ref[:, sl], do_ref[:, sl]
            Gc, bc = G[:, h:h + 1], beta[:, h:h + 1]
            Dm, KK, M, _, _ = _head_prep(k, Gc, bc)
            Tm, S, dSo = t_ref[0, h], s_ref[0, h], dS_scr[h]
            eG = jnp.exp(Gc)
            Gl = Gc[CHUNK - 1:CHUNK, :]
            eR, dch = jnp.exp(Gl - Gc), jnp.exp(Gl)
            vb, kb = v * bc, k * (bc * eG)
            u, w = _dotb(Tm, vb), _dotb(Tm, kb)
            QKr = _dotb(q, k, NT)
            QK = QKr * Dm
            qd, kd = q * eG, k * eR
            vn = u - _dotb(w, S)
            dqd = _dotb(dov, S, NT)
            dS = _dotb(qd, dov, TN)
            dQK = jnp.where(low, _dotb(dov, vn, NT), 0.0)
            dvn = _dotb(QK, dov, TN)
            ddch = jnp.sum(jnp.sum(dSo * S, axis=1, keepdims=True), axis=0, keepdims=True)
            dS = dS + dch * dSo
            dkd = _dotb(vn, dSo, NT)
            dvn = dvn + _dotb(kd, dSo)
            dw = -_dotb(dvn, S, NT)
            dS = dS - _dotb(w, dvn, TN)
            dS_scr[h] = dS
            dvb = _dotb(Tm, dvn, TN)
            dkb = _dotb(Tm, dw, TN)
            dA = -jnp.where(strict, _dotb(dvb, u, NT) + _dotb(dkb, w, NT), 0.0)
            rk = jnp.sum(dkb * k, axis=1, keepdims=True)
            dbeta = jnp.sum(dvb * v, axis=1, keepdims=True) + rk * eG + jnp.sum(dA * M, axis=1, keepdims=True)
            deG = rk * bc
            dM = dA * bc
            dKK = dM * Dm
            dQKr = dQK * Dm
            E = dM * M + dQK * QK
            dq = _dotb(dQKr, k) + dqd * eG
            dk = (dkb * (bc * eG) + _dotb(dKK, k) + _dotb(dKK, k, TN) + _dotb(dQKr, q, TN) + dkd * eR)
            deG = deG + jnp.sum(dqd * q, axis=1, keepdims=True)
            deR = jnp.sum(dkd * k, axis=1, keepdims=True)
            dGl = jnp.sum(deR * eR, axis=0, keepdims=True) + ddch * dch
            dGc = (jnp.sum(E, axis=1, keepdims=True) - _to_col(jnp.sum(E, axis=0, keepdims=True), eye)
                   + deG * eG - deR * eR + jnp.where(r[:, :1] == CHUNK - 1, dGl, 0.0))
            dqkv_ref[:, sl] = dq
            dqkv_ref[:, D + h * GDN_DK:D + (h + 1) * GDN_DK] = dk
            dqkv_ref[:, 2 * D + h * GDN_DK:2 * D + (h + 1) * GDN_DK] = dvb * bc
            dG_all = jnp.where(lane == h, dGc, dG_all)
            dbeta_all = jnp.where(lane == h, dbeta, dbeta_all)
        dg = _dot_sel(r <= c, dG_all, 3)
        da = dg * (-jnp.exp(alog_ref[...])) * _sigmoid(x)
        dba_ref[:, :LANES] = (dbeta_all * beta * (1.0 - beta)).astype(BF16)
        dba_ref[:, LANES:] = da.astype(BF16)
        dalog_ref[...] += _rsum(dg * g)
        ddtb_ref[...] += _rsum(da)

    rev = lambda n: N - 1 - n
    qkv_spec = [pl.BlockSpec((CHUNK, D), lambda n, cb=cb: (rev(n), cb)) for cb in range(3)]
    return pl.pallas_call(
        body, name=name, grid=(N,),
        in_specs=[pl.BlockSpec((CHUNK, D), lambda n: (rev(n), 0))] + qkv_spec + [
            pl.BlockSpec((CHUNK, LANES), lambda n: (rev(n), bblk)),
            pl.BlockSpec((CHUNK, LANES), lambda n: (rev(n), bblk + 1)),
            pl.BlockSpec((1, LANES), lambda n: (0, 0)), pl.BlockSpec((1, LANES), lambda n: (0, 0)),
            pl.BlockSpec((1, H, CHUNK, CHUNK), lambda n: (rev(n), 0, 0, 0)),
            pl.BlockSpec((1, H, GDN_DK, GDN_DK), lambda n: (rev(n), 0, 0, 0))],
        out_specs=[pl.BlockSpec((CHUNK, 3 * D), lambda n: (rev(n), 0)),
                   pl.BlockSpec((CHUNK, 2 * LANES), lambda n: (rev(n), 0)),
                   pl.BlockSpec((1, LANES), lambda n: (0, 0)), pl.BlockSpec((1, LANES), lambda n: (0, 0))],
        out_shape=[jax.ShapeDtypeStruct((T, 3 * D), F32), jax.ShapeDtypeStruct((T, 2 * LANES), BF16),
                   jax.ShapeDtypeStruct((1, LANES), F32), jax.ShapeDtypeStruct((1, LANES), F32)],
        scratch_shapes=[pltpu.VMEM((H, GDN_DK, GDN_DK), F32)],
        compiler_params=_cparams(("arbitrary",)),
    )(do, qkvn, qkvn, qkvn, p, p, alog, dtb, Tinv, Sin)


def _mix_in_reorder(wt, D, H):
    o1 = 4 * D
    o2, o3 = o1 + H, o1 + 2 * H
    z = jnp.zeros((LANES - H, wt.shape[1]), wt.dtype)
    return jnp.concatenate([wt[:o1], wt[o3:], wt[o1:o2], z, wt[o2:o3], z], axis=0)


def _mix_in_restore(dwt, D, H):
    b0 = 8 * D
    return jnp.concatenate([dwt[:4 * D], dwt[b0:b0 + H], dwt[b0 + LANES:b0 + LANES + H], dwt[4 * D:b0]], axis=0)


def _ffn_fwd(x, W, pre, tag):
    h = _rms_fwd(x, W[pre + "_norm_pre"], tag + "_pre")
    a = _matmul(h, W[pre + "_w_in"], "nt", F32, tag + "_in")
    s = _swiglu_fwd(a, tag + "_act")
    f = _matmul(s, W[pre + "_w_out"], "nn", F32, tag + "_out")
    return _post_fwd(x, f, W[pre + "_norm_post"], 0.5, tag + "_post"), (x, h, a, s, f)


def _ffn_bwd(dxn, saved, W, pre, tag):
    x, h, a, s, f = saved
    df, dpost = _post_bwd(dxn, f, W[pre + "_norm_post"], 0.5, tag + "_dpost")
    ds = _matmul(df, W[pre + "_w_out"], "nt", F32, tag + "_ds")
    dw_out = _matmul(s, df, "tn", F32, tag + "_dwout")
    da = _swiglu_bwd(ds, a, tag + "_dact")
    dh = _matmul(da, W[pre + "_w_in"], "nn", F32, tag + "_dh")
    dw_in = _matmul(da, h, "tn", F32, tag + "_dwin")
    dx, dpre = _pre_bwd(dh, x, W[pre + "_norm_pre"], dxn, tag + "_dpre")
    return dx, {pre + "_norm_pre": dpre, pre + "_norm_post": dpost, pre + "_w_in": dw_in, pre + "_w_out": dw_out}


def _mix_fwd(x, W, H, tag):
    h = _rms_fwd(x, W["mix_norm_pre"], tag + "_pre")
    p = _matmul(h, W["mix_w_in"], "nt", F32, tag + "_in")
    qkvn = _qkv_conv_fwd(p, W["gdn_conv_w"], H, tag + "_qkvconv")
    o, Tinv, Sin = _gdn_fwd(qkvn, p, W["gdn_a_log"], W["gdn_dt_bias"], H, tag + "_gdn")
    og = _gdn_gate_fwd(o, p, W["gdn_norm_w"], tag + "_gdngate")
    ya = _matmul(og, W["gdn_w_o"], "nn", F32, tag + "_gdno")
    hc = _glu_fwd(p, W["cnv_pw1_b"], tag + "_glu")
    hcv = _dw_conv_fwd(hc, W["cnv_dw_w"], W["cnv_dw_b"], tag + "_dwconv")
    hl = _ln_silu_fwd(hcv, W["cnv_ln_g"], W["cnv_ln_b"], tag + "_ln")
    yb = _matmul(hl, W["cnv_w_o"], "nn", F32, tag + "_cnvo")
    ym = _merge_fwd(p, ya, yb, W["cnv_b_o"], tag + "_merge")
    y = _matmul(ym, W["mix_w_out"], "nn", F32, tag + "_out")
    xn = _post_fwd(x, y, W["mix_norm_post"], 1.0, tag + "_post")
    return xn, (x, h, p, qkvn, o, Tinv, Sin, og, ya, hc, hcv, hl, yb, ym, y)


def _mix_bwd(dxn, saved, W, H, tag):
    x, h, p, qkvn, o, Tinv, Sin, og, ya, hc, hcv, hl, yb, ym, y = saved
    g = {}
    dy, g["mix_norm_post"] = _post_bwd(dxn, y, W["mix_norm_post"], 1.0, tag + "_dpost")
    dym = _matmul(dy, W["mix_w_out"], "nt", F32, tag + "_dym")
    g["mix_w_out"] = _matmul(ym, dy, "tn", F32, tag + "_dwout")
    dya, dyb, dgates, g["cnv_b_o"] = _merge_bwd(dym, p, ya, yb, W["cnv_b_o"], tag + "_dmerge")
    dhl = _matmul(dyb, W["cnv_w_o"], "nt", F32, tag + "_dhl")
    g["cnv_w_o"] = _matmul(hl, dyb, "tn", F32, tag + "_dwcnvo")
    dhcv, g["cnv_ln_g"], g["cnv_ln_b"], g["cnv_dw_b"] = _ln_silu_bwd(dhl, hcv, W["cnv_ln_g"], W["cnv_ln_b"], tag + "_dln")
    dhc, g["cnv_dw_w"] = _dw_conv_bwd(dhcv, hc, W["cnv_dw_w"], tag + "_ddwconv")
    dglu, g["cnv_pw1_b"] = _glu_bwd(dhc, p, W["cnv_pw1_b"], tag + "_dglu")
    dog = _matmul(dya, W["gdn_w_o"], "nt", F32, tag + "_dog")
    g["gdn_w_o"] = _matmul(og, dya, "tn", F32, tag + "_dwgdno")
    do, dz, g["gdn_norm_w"] = _gdn_gate_bwd(dog, o, p, W["gdn_norm_w"], tag + "_dgdngate")
    dqkvn, dba, g["gdn_a_log"], g["gdn_dt_bias"] = _gdn_bwd(
        do, qkvn, p, W["gdn_a_log"], W["gdn_dt_bias"], Tinv, Sin, H, tag + "_dgdn")
    dqkv, g["gdn_conv_w"] = _qkv_conv_bwd(dqkvn, p, W["gdn_conv_w"], H, tag + "_dqkvconv")
    dp = jnp.concatenate([dqkv, dz, dglu, dgates, dba], axis=1)
    dh = _matmul(dp, W["mix_w_in"], "nn", F32, tag + "_dh")
    g["mix_w_in"] = _matmul(dp, h, "tn", F32, tag + "_dwin")
    dx, g["mix_norm_pre"] = _pre_bwd(dh, x, W["mix_norm_pre"], dxn, tag + "_dpre")
    return dx, g


def _trunk_fwd_bwd(x, tgt, Ws, H):
    saved = []
    for i, W in enumerate(Ws):
        x, s1 = _ffn_fwd(x, W, "ffn1", f"l{i}_ffn1")
        x, s2 = _mix_fwd(x, W, H, f"l{i}_mix")
        x, s3 = _ffn_fwd(x, W, "ffn2", f"l{i}_ffn2")
        saved.append((s1, s2, s3))
    dx, loss = _loss_fwd_bwd(x, tgt, "loss")
    grads = [None] * len(Ws)
    for i in reversed(range(len(Ws))):
        s1, s2, s3 = saved[i]
        dx, g3 = _ffn_bwd(dx, s3, Ws[i], "ffn2", f"l{i}_ffn2")
        dx, g2 = _mix_bwd(dx, s2, Ws[i], H, f"l{i}_mix")
        dx, g1 = _ffn_bwd(dx, s1, Ws[i], "ffn1", f"l{i}_ffn1")
        grads[i] = {**g1, **g2, **g3}
    return loss, dx, grads


HBM_SPEC = pl.BlockSpec(memory_space=pltpu.HBM)


def _coords():
    return lax.axis_index("x"), lax.axis_index("y"), lax.axis_index("c")


def _all_gather(shards, name, layered):
    n = len(shards)

    def body(*refs):
        ins, outs = refs[:n], refs[n:2 * n]
        send_sems, recv_sems, local_sems = refs[2 * n:]
        x, y, c = _coords()
        me, sibling = (x, y, c), (x, y, 1 - c)
        chips = [(1 - x, y), (x, 1 - y), (1 - x, 1 - y)]

        def slot_ref(w, block):
            s = 4 * block[0] + 2 * block[1] + block[2]
            return outs[w].at[:, s] if layered else outs[w].at[s]

        def copy(w, k, block, to, src=None):
            dst = slot_ref(w, block)
            return pltpu.make_async_remote_copy(
                src_ref=dst if src is None else src, dst_ref=dst, send_sem=send_sems.at[w, k],
                recv_sem=recv_sems.at[w, k], device_id=to, device_id_type=MESH)

        mine = [pltpu.make_async_copy(ins[w], slot_ref(w, me), local_sems.at[w]) for w in range(n)]
        first = []
        for w in range(n):
            mine[w].start()
            first.append(copy(w, 0, me, sibling, src=ins[w]))
            first += [copy(w, 1 + j, me, (*chip, c), src=ins[w]) for j, chip in enumerate(chips)]
        for cp in first:
            cp.start()
        passed = []
        for j, chip in enumerate(chips):
            for w in range(n):
                copy(w, 1 + j, (*chip, c), me).wait_recv()
                fwd = copy(w, 4 + j, (*chip, c), sibling)
                fwd.start()
                passed.append(fwd)
        for w in range(n):
            copy(w, 0, sibling, me).wait_recv()
            for j, chip in enumerate(chips):
                copy(w, 4 + j, (*chip, 1 - c), me).wait_recv()
        for cp in first + passed:
            cp.wait_send()
        for cp in mine:
            cp.wait()

    return pl.pallas_call(
        body, name=name,
        out_shape=[jax.ShapeDtypeStruct((s.shape[0], N_DEV) + s.shape[1:] if layered else (N_DEV,) + s.shape, s.dtype)
                   for s in shards],
        in_specs=[HBM_SPEC] * n, out_specs=[HBM_SPEC] * n,
        scratch_shapes=[pltpu.SemaphoreType.DMA((n, 7)), pltpu.SemaphoreType.DMA((n, 7)), pltpu.SemaphoreType.DMA((n,))],
    )(*shards)


def _rs_sibling(Gs, name):
    n = len(Gs)

    def body(*refs):
        ins, outs = refs[:n], refs[n:2 * n]
        send_sems, recv_sems = refs[2 * n:]
        x, y, c = _coords()
        cps = []
        for w in range(n):
            for q in range(4):
                cp = pltpu.make_async_remote_copy(
                    src_ref=ins[w].at[2 * q + (1 - c)], dst_ref=outs[w].at[q], send_sem=send_sems.at[w, q],
                    recv_sem=recv_sems.at[w, q], device_id=(x, y, 1 - c), device_id_type=MESH)
                cp.start()
                cps.append(cp)
        for cp in cps:
            cp.wait()

    return pl.pallas_call(
        body, name=name,
        out_shape=[jax.ShapeDtypeStruct((4,) + g.shape[1:], g.dtype) for g in Gs],
        in_specs=[HBM_SPEC] * n, out_specs=[HBM_SPEC] * n,
        scratch_shapes=[pltpu.SemaphoreType.DMA((n, 4)), pltpu.SemaphoreType.DMA((n, 4))],
    )(*Gs)


def _rs_chips(Ps, name):
    n = len(Ps)

    def body(*refs):
        ins, outs = refs[:n], refs[n:2 * n]
        send_sems, recv_sems, local_sems = refs[2 * n:]
        x, y, c = _coords()
        me_q = 2 * x + y
        chips = [(1 - x, y), (x, 1 - y), (1 - x, 1 - y)]
        cps = []
        for w in range(n):
            loc = pltpu.make_async_copy(ins[w].at[me_q], outs[w].at[me_q], local_sems.at[w])
            loc.start()
            cps.append(loc)
            for j, (px, py) in enumerate(chips):
                cp = pltpu.make_async_remote_copy(
                    src_ref=ins[w].at[2 * px + py], dst_ref=outs[w].at[me_q], send_sem=send_sems.at[w, j],
                    recv_sem=recv_sems.at[w, j], device_id=(px, py, c), device_id_type=MESH)
                cp.start()
                cps.append(cp)
        for cp in cps:
            cp.wait()

    return pl.pallas_call(
        body, name=name,
        out_shape=[jax.ShapeDtypeStruct(p.shape, p.dtype) for p in Ps],
        in_specs=[HBM_SPEC] * n, out_specs=[HBM_SPEC] * n,
        scratch_shapes=[pltpu.SemaphoreType.DMA((n, 3)), pltpu.SemaphoreType.DMA((n, 3)), pltpu.SemaphoreType.DMA((n,))],
    )(*Ps)


def _row_tile(R, target=256):
    best = None
    for t in range(8, min(R, target) + 1, 8):
        if R % t == 0:
            best = t
    return best if best is not None else R


def _pair_add(G, R1, cidx, name):
    _, _, R, C = G.shape
    tb = _row_tile(R)

    def body(c_ref, g_ref, r_ref, o_ref):
        o_ref[...] = (g_ref[...] + r_ref[...]).astype(BF16)

    return pl.pallas_call(
        body, name=name,
        grid_spec=pltpu.PrefetchScalarGridSpec(
            num_scalar_prefetch=1, grid=(4, R // tb),
            in_specs=[pl.BlockSpec((None, None, tb, C), lambda q, i, cr: (q, cr[0], i, 0)),
                      pl.BlockSpec((None, tb, C), lambda q, i, cr: (q, i, 0))],
            out_specs=pl.BlockSpec((None, tb, C), lambda q, i, cr: (q, i, 0))),
        out_shape=jax.ShapeDtypeStruct((4, R, C), BF16),
        compiler_params=_cparams(("arbitrary", "arbitrary")),
    )(cidx, G, R1)


def _sum_parts(parts, name):
    P, R, C = parts.shape

    def body(p_ref, o_ref):
        acc = p_ref[0]
        for j in range(1, P):
            acc = acc + p_ref[j]
        o_ref[...] = acc

    return pl.pallas_call(
        body, name=name, out_shape=jax.ShapeDtypeStruct((R, C), F32),
        in_specs=[pl.BlockSpec(memory_space=pltpu.VMEM)], out_specs=pl.BlockSpec(memory_space=pltpu.VMEM),
        compiler_params=_cparams(),
    )(parts)


def _adamw(w, m, v, parts, name):
    G, R, C = w.shape
    P = parts[0].shape[0]
    tb = _row_tile(R)
    nb = R // tb
    c1 = 1.0 / (1.0 - ADAM_B1 ** ADAM_STEP)
    c2 = 1.0 / (1.0 - ADAM_B2 ** ADAM_STEP)

    def body(w_ref, m_ref, v_ref, *rest):
        p_refs, (g_ref, d_ref, nm_ref, nv_ref) = rest[:G], rest[G:]
        l = pl.program_id(0)
        g = None
        for k in range(G):
            gk = p_refs[k][0].astype(F32)
            for j in range(1, P):
                gk = gk + p_refs[k][j].astype(F32)
            g = gk if g is None else jnp.where(l == k, gk, g)
        nm = ADAM_B1 * m_ref[...] + (1.0 - ADAM_B1) * g
        nv = ADAM_B2 * v_ref[...] + (1.0 - ADAM_B2) * (g * g)
        g_ref[...] = g
        nm_ref[...] = nm
        nv_ref[...] = nv
        d_ref[...] = -ADAM_LR * ((nm * c1) / (jnp.sqrt(nv * c2) + ADAM_EPS) + ADAM_WD * w_ref[...])

    blk = pl.BlockSpec((None, tb, C), lambda l, i: (l, i, 0))

    def part_spec(k):
        return pl.BlockSpec((P, tb, C), lambda l, i: (0, jnp.where(l < k, 0, jnp.where(l > k, nb - 1, i)), 0))

    return pl.pallas_call(
        body, name=name, grid=(G, nb),
        in_specs=[blk, blk, blk] + [part_spec(k) for k in range(G)],
        out_specs=[blk] * 4, out_shape=[jax.ShapeDtypeStruct((G, R, C), F32)] * 4,
        compiler_params=_cparams(("arbitrary", "arbitrary")),
    )(w, m, v, *parts)


BIG = ("ffn1_w_in", "ffn1_w_out", "mix_w_in", "gdn_w_o", "cnv_w_o", "mix_w_out", "ffn2_w_in", "ffn2_w_out")
COL_SHARDED = ("ffn1_w_in", "mix_w_in", "ffn2_w_in")
SMALL_SHARDED = ("gdn_conv_w", "cnv_dw_w")
NAMES = ("ffn1_norm_pre", "ffn1_norm_post", "ffn1_w_in", "ffn1_w_out", "mix_norm_pre", "mix_norm_post", "mix_w_in",
         "gdn_conv_w", "gdn_a_log", "gdn_dt_bias", "gdn_norm_w", "gdn_w_o", "cnv_pw1_b", "cnv_dw_w", "cnv_dw_b",
         "cnv_ln_g", "cnv_ln_b", "cnv_w_o", "cnv_b_o", "mix_w_out", "ffn2_norm_pre", "ffn2_norm_post", "ffn2_w_in",
         "ffn2_w_out")
SMALL = tuple(n for n in NAMES if n not in BIG)


def _gathered_layer_weights(gath, params, i, D, H):
    W = {}
    for n in BIG:
        g = gath[n][i]
        g = g.reshape(-1, g.shape[-1])
        W[n] = _mix_in_reorder(g, D, H) if n == "mix_w_in" else g
    for n in SMALL_SHARDED:
        g = gath[n][i]
        W[n] = jnp.transpose(g, (1, 0, 2)).reshape(g.shape[1], -1)
    for n in SMALL:
        if n in SMALL_SHARDED:
            continue
        v = params[n][i]
        if n in ("gdn_a_log", "gdn_dt_bias"):
            v = jnp.pad(v, (0, LANES - H))
        W[n] = v.reshape(1, -1)
    return W


def kernel(x, ffn1_norm_pre, ffn1_norm_post, ffn1_w_in, ffn1_w_out, mix_norm_pre, mix_norm_post, mix_w_in, gdn_conv_w, gdn_a_log, gdn_dt_bias, gdn_norm_w, gdn_w_o, cnv_pw1_b, cnv_dw_w, cnv_dw_b, cnv_ln_g, cnv_ln_b, cnv_w_o, cnv_b_o, mix_w_out, ffn2_norm_pre, ffn2_norm_post, ffn2_w_in, ffn2_w_out, loss_target, m_ffn1_norm_pre, m_ffn1_norm_post, m_ffn1_w_in, m_ffn1_w_out, m_mix_norm_pre, m_mix_norm_post, m_mix_w_in, m_gdn_conv_w, m_gdn_a_log, m_gdn_dt_bias, m_gdn_norm_w, m_gdn_w_o, m_cnv_pw1_b, m_cnv_dw_w, m_cnv_dw_b, m_cnv_ln_g, m_cnv_ln_b, m_cnv_w_o, m_cnv_b_o, m_mix_w_out, m_ffn2_norm_pre, m_ffn2_norm_post, m_ffn2_w_in, m_ffn2_w_out, v_ffn1_norm_pre, v_ffn1_norm_post, v_ffn1_w_in, v_ffn1_w_out, v_mix_norm_pre, v_mix_norm_post, v_mix_w_in, v_gdn_conv_w, v_gdn_a_log, v_gdn_dt_bias, v_gdn_norm_w, v_gdn_w_o, v_cnv_pw1_b, v_cnv_dw_w, v_cnv_dw_b, v_cnv_ln_g, v_cnv_ln_b, v_cnv_w_o, v_cnv_b_o, v_mix_w_out, v_ffn2_norm_pre, v_ffn2_norm_post, v_ffn2_w_in, v_ffn2_w_out):
    params = dict(zip(NAMES, (ffn1_norm_pre, ffn1_norm_post, ffn1_w_in, ffn1_w_out, mix_norm_pre, mix_norm_post, mix_w_in, gdn_conv_w, gdn_a_log, gdn_dt_bias, gdn_norm_w, gdn_w_o, cnv_pw1_b, cnv_dw_w, cnv_dw_b, cnv_ln_g, cnv_ln_b, cnv_w_o, cnv_b_o, mix_w_out, ffn2_norm_pre, ffn2_norm_post, ffn2_w_in, ffn2_w_out)))
    mom1 = dict(zip(NAMES, (m_ffn1_norm_pre, m_ffn1_norm_post, m_ffn1_w_in, m_ffn1_w_out, m_mix_norm_pre, m_mix_norm_post, m_mix_w_in, m_gdn_conv_w, m_gdn_a_log, m_gdn_dt_bias, m_gdn_norm_w, m_gdn_w_o, m_cnv_pw1_b, m_cnv_dw_w, m_cnv_dw_b, m_cnv_ln_g, m_cnv_ln_b, m_cnv_w_o, m_cnv_b_o, m_mix_w_out, m_ffn2_norm_pre, m_ffn2_norm_post, m_ffn2_w_in, m_ffn2_w_out)))
    mom2 = dict(zip(NAMES, (v_ffn1_norm_pre, v_ffn1_norm_post, v_ffn1_w_in, v_ffn1_w_out, v_mix_norm_pre, v_mix_norm_post, v_mix_w_in, v_gdn_conv_w, v_gdn_a_log, v_gdn_dt_bias, v_gdn_norm_w, v_gdn_w_o, v_cnv_pw1_b, v_cnv_dw_w, v_cnv_dw_b, v_cnv_ln_g, v_cnv_ln_b, v_cnv_w_o, v_cnv_b_o, v_mix_w_out, v_ffn2_norm_pre, v_ffn2_norm_post, v_ffn2_w_in, v_ffn2_w_out)))
    T, D = x.shape[1], x.shape[2]
    H = D // GDN_DK
    L = ffn1_norm_pre.shape[0]
    xi, yi, ci = _coords()
    dev = 4 * xi + 2 * yi + ci

    ag_names = BIG + SMALL_SHARDED

    def shard_to_send(n):
        if n in COL_SHARDED:
            return jnp.swapaxes(params[n], 1, 2).astype(BF16)
        return params[n].astype(BF16) if n in BIG else params[n]

    gathered = _all_gather([shard_to_send(n) for n in ag_names], "ag_weights", layered=True)
    gath = dict(zip(ag_names, gathered))
    Ws = [_gathered_layer_weights(gath, params, i, D, H) for i in range(L)]

    loss_row, dx, grads = _trunk_fwd_bwd(x[0], loss_target[0], Ws, H)
    loss = lax.psum(loss_row[0, 0], ("x", "y", "c"))

    cidx = jnp.reshape(ci, (1,)).astype(jnp.int32)
    R2s = {n: [None] * L for n in BIG}
    for i in reversed(range(L)):
        Gs = []
        for n in BIG:
            g = grads[i][n]
            if n == "mix_w_in":
                g = _mix_in_restore(g, D, H)
            Gs.append(g.reshape(N_DEV, -1, g.shape[-1]))
        R1s = _rs_sibling(Gs, f"rs_sibling_l{i}")
        Ps = [_pair_add(G.reshape(4, 2, G.shape[1], G.shape[2]), R1, cidx, f"pair_add_l{i}_" + n)
              for n, G, R1 in zip(BIG, Gs, R1s)]
        for n, r2 in zip(BIG, _rs_chips(Ps, f"rs_chips_l{i}")):
            R2s[n][i] = jnp.swapaxes(r2, 1, 2) if n in COL_SHARDED else r2

    pieces = []
    for i in range(L):
        for n in SMALL:
            pieces.append(grads[i][n].reshape(-1, LANES))
    rows = sum(p.shape[0] for p in pieces)
    pad = (-rows) % 8
    if pad:
        pieces.append(jnp.zeros((pad, LANES), F32))
    small_all = _all_gather([jnp.concatenate(pieces, axis=0)], "ag_small_grads", layered=False)[0]
    small_sum = _sum_parts(small_all, "sum_small_grads")
    small_g = {n: [None] * L for n in SMALL}
    off = 0
    for i in range(L):
        for n in SMALL:
            shape = grads[i][n].shape
            cnt = shape[0] * shape[1] // LANES
            g = small_sum[off:off + cnt].reshape(shape)
            off += cnt
            if n in ("gdn_a_log", "gdn_dt_bias"):
                g = g[:, :H]
            if n in SMALL_SHARDED:
                wloc = params[n].shape[-1]
                g = lax.dynamic_slice_in_dim(g, dev * wloc, wloc, axis=1)
            small_g[n][i] = g

    outs = {}
    for n in NAMES:
        w, m, v = params[n], mom1[n], mom2[n]
        if n in BIG:
            shape3, parts = w.shape, R2s[n]
        else:
            rows, cols = (w.shape[0] * w.shape[1], w.shape[2]) if w.ndim == 3 else w.shape
            shape3, parts = (1, rows, cols), [jnp.stack(small_g[n], axis=0).reshape(1, rows, cols)]
        res = _adamw(w.reshape(shape3), m.reshape(shape3), v.reshape(shape3), parts, "adamw_" + n)
        outs[n] = [r.reshape(w.shape) for r in res]

    result = [loss, dx[None]]
    for k in range(4):
        result += [outs[n][k] for n in NAMES]
    return tuple(result)
```

```python
import jax
import jax.numpy as jnp
from jax import lax
from jax.experimental import pallas as pl
from jax.experimental.pallas import tpu as pltpu

F32 = jnp.float32
BF16 = jnp.bfloat16

GDN_DK = 128
CHUNK = 64
GDN_CONV = 4
CNV_K = 31
RMS_EPS = 1e-6
LN_EPS = 1e-5
L2_EPS = 1e-6
ADAM_LR = 0.001
ADAM_B1 = 0.9
ADAM_B2 = 0.999
ADAM_EPS = 1e-08
ADAM_WD = 0.01
ADAM_STEP = 10

LANES = 128
SUB = 16
VMEM_LIMIT = 56 * 1024 * 1024
N_DEV = 8
MESH = pl.DeviceIdType.MESH


def _cparams(sem=None, **kw):
    if sem is not None:
        kw["dimension_semantics"] = sem
    return pltpu.CompilerParams(vmem_limit_bytes=VMEM_LIMIT, **kw)


def _tile(dim, target):
    best = None
    for t in range(LANES, min(dim, target) + 1, LANES):
        if dim % t == 0:
            best = t
    return best if best is not None else dim


def _sigmoid(x):
    return 1.0 / (1.0 + jnp.exp(-x))


def _silu(x):
    return x * _sigmoid(x)


def _dsilu(x):
    s = _sigmoid(x)
    return s * (1.0 + x * (1.0 - s))


MM_VMEM_BUDGET = 40 * 1024 * 1024
MM_MAX_TILE = 2048


def _mm_tiles(M, N, K, out_bytes):
    def cands(dim):
        c = [t for t in range(LANES, min(dim, MM_MAX_TILE) + 1, LANES) if dim % t == 0]
        return c or [dim]
    best = None
    for tm in cands(M):
        for tn in cands(N):
            vm = 2 * (2 * K * (tm + tn) + tm * tn * out_bytes)
            if vm <= MM_VMEM_BUDGET and (best is None or tm * tn > best[0] * best[1]):
                best = (tm, tn)
    return best if best is not None else (cands(M)[0], cands(N)[0])


def _matmul(a, b, mode, out_dtype, name):
    if mode == "nn":
        (M, K), N = a.shape, b.shape[1]
    elif mode == "nt":
        (M, K), N = a.shape, b.shape[0]
    else:
        (K, M), N = a.shape, b.shape[1]
    tm, tn = _mm_tiles(M, N, K, jnp.dtype(out_dtype).itemsize)
    if mode == "nn":
        a_spec = pl.BlockSpec((tm, K), lambda j, i: (i, 0))
        b_spec = pl.BlockSpec((K, tn), lambda j, i: (0, j))
        dn = (((1,), (0,)), ((), ()))
    elif mode == "nt":
        a_spec = pl.BlockSpec((tm, K), lambda j, i: (i, 0))
        b_spec = pl.BlockSpec((tn, K), lambda j, i: (j, 0))
        dn = (((1,), (1,)), ((), ()))
    else:
        a_spec = pl.BlockSpec((K, tm), lambda j, i: (0, i))
        b_spec = pl.BlockSpec((K, tn), lambda j, i: (0, j))
        dn = (((0,), (0,)), ((), ()))

    def body(a_ref, b_ref, o_ref):
        o_ref[...] = lax.dot_general(a_ref[...], b_ref[...], dn, preferred_element_type=F32).astype(out_dtype)

    return pl.pallas_call(
        body, name=name, grid=(N // tn, M // tm), in_specs=[a_spec, b_spec],
        out_specs=pl.BlockSpec((tm, tn), lambda j, i: (i, j)),
        out_shape=jax.ShapeDtypeStruct((M, N), out_dtype),
        compiler_params=_cparams(("parallel", "parallel")),
    )(a, b)


def _rowcall(name, body, T, tb, row_ins, par_ins, row_outs, acc_outs):
    n_ri, n_pi, n_ro = len(row_ins), len(par_ins), len(row_outs)

    def kern(*refs):
        ri, pi = refs[:n_ri], refs[n_ri:n_ri + n_pi]
        ro, ao = refs[n_ri + n_pi:n_ri + n_pi + n_ro], refs[n_ri + n_pi + n_ro:]
        if ao:
            @pl.when(pl.program_id(0) == 0)
            def _():
                for r in ao:
                    r[...] = jnp.zeros_like(r)
        body(ri, pi, ro, ao)

    in_specs = [pl.BlockSpec((tb, w), lambda i, cb=cb: (i, cb)) for (_, w, cb) in row_ins]
    in_specs += [pl.BlockSpec(p.shape, lambda i: (0, 0)) for p in par_ins]
    out_specs = [pl.BlockSpec((tb, w), lambda i: (i, 0)) for (w, _) in row_outs]
    out_specs += [pl.BlockSpec((1, w), lambda i: (0, 0)) for w in acc_outs]
    out_shape = [jax.ShapeDtypeStruct((T, w), dt) for (w, dt) in row_outs]
    out_shape += [jax.ShapeDtypeStruct((1, w), F32) for w in acc_outs]
    return pl.pallas_call(
        kern, name=name, grid=(T // tb,), in_specs=in_specs, out_specs=out_specs, out_shape=out_shape,
        compiler_params=_cparams(("arbitrary",)),
    )(*[a for (a, _, _) in row_ins], *par_ins)


def _rsum(x):
    return jnp.sum(x, axis=0, keepdims=True)


def _rms_rstd(x):
    return lax.rsqrt(jnp.mean(x * x, axis=-1, keepdims=True) + RMS_EPS)


def _rms_fwd(x, w, name):
    T, D = x.shape

    def body(ri, pi, ro, ao):
        xv = ri[0][...]
        ro[0][...] = (xv * _rms_rstd(xv) * pi[0][...]).astype(BF16)

    return _rowcall(name, body, T, 256, [(x, D, 0)], [w], [(D, BF16)], [])[0]


def _rms_bwd_core(dy, x, w):
    rs = _rms_rstd(x)
    xh = x * rs
    gw = dy * w
    dx = rs * (gw - xh * jnp.mean(gw * xh, axis=-1, keepdims=True))
    return dx, dy * xh


def _pre_bwd(dh, x, w, dres, name):
    T, D = x.shape

    def body(ri, pi, ro, ao):
        dx, dwc = _rms_bwd_core(ri[0][...], ri[1][...], pi[0][...])
        ro[0][...] = ri[2][...] + dx
        ao[0][...] += _rsum(dwc)

    return _rowcall(name, body, T, 256, [(dh, D, 0), (x, D, 0), (dres, D, 0)], [w], [(D, F32)], [D])


def _post_fwd(x, f, w, r, name):
    T, D = x.shape

    def body(ri, pi, ro, ao):
        fv = ri[1][...]
        ro[0][...] = ri[0][...] + r * (fv * _rms_rstd(fv) * pi[0][...])

    return _rowcall(name, body, T, 256, [(x, D, 0), (f, D, 0)], [w], [(D, F32)], [])[0]


def _post_bwd(dxn, f, w, r, name):
    T, D = f.shape

    def body(ri, pi, ro, ao):
        df, dwc = _rms_bwd_core(r * ri[0][...], ri[1][...], pi[0][...])
        ro[0][...] = df.astype(BF16)
        ao[0][...] += _rsum(dwc)

    return _rowcall(name, body, T, 256, [(dxn, D, 0), (f, D, 0)], [w], [(D, BF16)], [D])


def _swiglu_fwd(a, name):
    T, F2 = a.shape
    F = F2 // 2

    def body(ri, pi, ro, ao):
        ro[0][...] = (_silu(ri[0][...]) * ri[1][...]).astype(BF16)

    return _rowcall(name, body, T, 256, [(a, F, 0), (a, F, 1)], [], [(F, BF16)], [])[0]


def _swiglu_bwd(ds, a, name):
    T, F2 = a.shape
    F = F2 // 2

    def body(ri, pi, ro, ao):
        dsv, g, u = ri[0][...], ri[1][...], ri[2][...]
        ro[0][:, :F] = (dsv * u * _dsilu(g)).astype(BF16)
        ro[0][:, F:] = (dsv * _silu(g)).astype(BF16)

    return _rowcall(name, body, T, 256, [(ds, F, 0), (a, F, 0), (a, F, 1)], [], [(F2, BF16)], [])[0]


def _loss_fwd_bwd(y, tgt, name):
    T, D = y.shape

    def body(ri, pi, ro, ao):
        e = ri[0][...] - ri[1][...]
        ro[0][...] = e * (1.0 / D)
        tot = jnp.sum(_rsum(e * e), axis=1, keepdims=True) * (0.5 / D)
        ao[0][...] += jnp.broadcast_to(tot, (1, LANES))

    return _rowcall(name, body, T, 256, [(y, D, 0), (tgt, D, 0)], [], [(D, F32)], [LANES])


def _gdn_gate_fwd(o, p, nw, name):
    T, D = o.shape
    H = D // GDN_DK

    def body(ri, pi, ro, ao):
        for h in range(H):
            sl = slice(h * GDN_DK, (h + 1) * GDN_DK)
            oh = ri[0][:, sl]
            ro[0][:, sl] = (oh * _rms_rstd(oh) * pi[0][...] * _silu(ri[1][:, sl])).astype(BF16)

    return _rowcall(name, body, T, 256, [(o, D, 0), (p, D, 3)], [nw], [(D, BF16)], [])[0]


def _gdn_gate_bwd(dog, o, p, nw, name):
    T, D = o.shape
    H = D // GDN_DK

    def body(ri, pi, ro, ao):
        acc = jnp.zeros((1, GDN_DK), F32)
        for h in range(H):
            sl = slice(h * GDN_DK, (h + 1) * GDN_DK)
            dy, oh, z = ri[0][:, sl], ri[1][:, sl], ri[2][:, sl]
            sz = _silu(z)
            do, dwc = _rms_bwd_core(dy * sz, oh, pi[0][...])
            ro[0][:, sl] = do
            ro[1][:, sl] = (dy * oh * _rms_rstd(oh) * pi[0][...] * _dsilu(z)).astype(BF16)
            acc = acc + _rsum(dwc)
        ao[0][...] += acc

    return _rowcall(name, body, T, 256, [(dog, D, 0), (o, D, 0), (p, D, 3)], [nw], [(D, F32), (D, BF16)], [GDN_DK])


def _glu_fwd(p, b, name):
    T = p.shape[0]
    D = b.shape[1] // 2

    def body(ri, pi, ro, ao):
        ro[0][...] = (ri[0][...] + pi[0][:, :D]) * _sigmoid(ri[1][...] + pi[0][:, D:])

    return _rowcall(name, body, T, 256, [(p, D, 4), (p, D, 5)], [b], [(D, F32)], [])[0]


def _glu_bwd(dhc, p, b, name):
    T = p.shape[0]
    D = b.shape[1] // 2

    def body(ri, pi, ro, ao):
        d, a, g = ri[0][...], ri[1][...] + pi[0][:, :D], ri[2][...] + pi[0][:, D:]
        sg = _sigmoid(g)
        da, dg = d * sg, d * a * sg * (1.0 - sg)
        ro[0][:, :D] = da.astype(BF16)
        ro[0][:, D:] = dg.astype(BF16)
        ao[0][:, :D] += _rsum(da)
        ao[0][:, D:] += _rsum(dg)

    return _rowcall(name, body, T, 256, [(dhc, D, 0), (p, D, 4), (p, D, 5)], [b], [(2 * D, BF16)], [2 * D])


def _ln_stats(x):
    mu = jnp.mean(x, axis=-1, keepdims=True)
    xc = x - mu
    rstd = lax.rsqrt(jnp.mean(xc * xc, axis=-1, keepdims=True) + LN_EPS)
    return xc * rstd, rstd


def _ln_silu_fwd(hcv, g, b, name):
    T, D = hcv.shape

    def body(ri, pi, ro, ao):
        xh, _ = _ln_stats(ri[0][...])
        ro[0][...] = _silu(xh * pi[0][...] + pi[1][...]).astype(BF16)

    return _rowcall(name, body, T, 256, [(hcv, D, 0)], [g, b], [(D, BF16)], [])[0]


def _ln_silu_bwd(dhl, hcv, g, b, name):
    T, D = hcv.shape

    def body(ri, pi, ro, ao):
        xh, rstd = _ln_stats(ri[1][...])
        dyl = ri[0][...] * _dsilu(xh * pi[0][...] + pi[1][...])
        dxh = dyl * pi[0][...]
        dx = rstd * (dxh - jnp.mean(dxh, axis=-1, keepdims=True) - xh * jnp.mean(dxh * xh, axis=-1, keepdims=True))
        ro[0][...] = dx
        ao[0][...] += _rsum(dyl * xh)
        ao[1][...] += _rsum(dyl)
        ao[2][...] += _rsum(dx)

    return _rowcall(name, body, T, 256, [(dhl, D, 0), (hcv, D, 0)], [g, b], [(D, F32)], [D, D, D])


def _merge_fwd(p, ya, yb, bo, name):
    T, D = ya.shape

    def body(ri, pi, ro, ao):
        ro[0][...] = (_sigmoid(ri[0][...]) * ri[2][...] + _sigmoid(ri[1][...]) * (ri[3][...] + pi[0][...])).astype(BF16)

    return _rowcall(name, body, T, 256, [(p, D, 6), (p, D, 7), (ya, D, 0), (yb, D, 0)], [bo], [(D, BF16)], [])[0]


def _merge_bwd(dym, p, ya, yb, bo, name):
    T, D = ya.shape

    def body(ri, pi, ro, ao):
        d = ri[0][...]
        ga, gb = _sigmoid(ri[1][...]), _sigmoid(ri[2][...])
        ybv = ri[4][...] + pi[0][...]
        dyb = d * gb
        ro[0][...] = (d * ga).astype(BF16)
        ro[1][...] = dyb.astype(BF16)
        ro[2][:, :D] = (d * ri[3][...] * ga * (1.0 - ga)).astype(BF16)
        ro[2][:, D:] = (d * ybv * gb * (1.0 - gb)).astype(BF16)
        ao[0][...] += _rsum(dyb)

    return _rowcall(name, body, T, 256, [(dym, D, 0), (p, D, 6), (p, D, 7), (ya, D, 0), (yb, D, 0)], [bo],
                    [(D, BF16), (D, BF16), (2 * D, BF16)], [D])


PAD = 32
RC = 256


def _causal_taps(xp_ref, w, K, c0):
    acc = None
    for j in range(K):
        term = w[j:j + 1, :] * xp_ref[pl.ds(PAD - (K - 1) + j + c0, RC), :]
        acc = term if acc is None else acc + term
    return acc


def _anticausal_taps(dp_ref, w, K, c0):
    acc = None
    for j in range(K):
        term = w[j:j + 1, :] * dp_ref[pl.ds((K - 1) - j + c0, RC), :]
        acc = term if acc is None else acc + term
    return acc


def _tap_grads(dw_ref, dc_ref, xp_ref, K, T):
    for j in range(K):
        acc = jnp.zeros((1, LANES), F32)
        for c in range(T // RC):
            acc = acc + _rsum(dc_ref[pl.ds(c * RC, RC), :] * xp_ref[pl.ds(PAD - (K - 1) + j + c * RC, RC), :])
        dw_ref[j:j + 1, :] = acc


def _qkv_conv_fwd(p, cw, H, name):
    T = p.shape[0]
    K = cw.shape[0]

    def body(x_ref, w_ref, o_ref, xp_ref):
        j = pl.program_id(0)
        xp_ref[pl.ds(0, PAD), :] = jnp.zeros((PAD, LANES), F32)
        xp_ref[pl.ds(PAD, T), :] = x_ref[...]
        w = w_ref[...]
        scale = jnp.where(j < H, GDN_DK ** -0.5, 1.0).astype(F32)
        for c in range(T // RC):
            act = _silu(_causal_taps(xp_ref, w, K, c * RC))
            nrm = act * lax.rsqrt(jnp.sum(act * act, axis=-1, keepdims=True) + L2_EPS) * scale
            o_ref[pl.ds(c * RC, RC), :] = jnp.where(j < 2 * H, nrm, act)

    return pl.pallas_call(
        body, name=name, grid=(3 * H,),
        in_specs=[pl.BlockSpec((T, LANES), lambda j: (0, j)), pl.BlockSpec((K, LANES), lambda j: (0, j))],
        out_specs=pl.BlockSpec((T, LANES), lambda j: (0, j)),
        out_shape=jax.ShapeDtypeStruct((T, 3 * H * GDN_DK), F32),
        scratch_shapes=[pltpu.VMEM((T + PAD, LANES), F32)],
        compiler_params=_cparams(("arbitrary",)),
    )(p, cw)


def _qkv_conv_bwd(dn, p, cw, H, name):
    T = p.shape[0]
    K = cw.shape[0]

    def body(dn_ref, x_ref, w_ref, dx_ref, dw_ref, xp_ref, dc_ref):
        j = pl.program_id(0)
        xp_ref[pl.ds(0, PAD), :] = jnp.zeros((PAD, LANES), F32)
        xp_ref[pl.ds(PAD, T), :] = x_ref[...]
        dc_ref[pl.ds(T, PAD), :] = jnp.zeros((PAD, LANES), F32)
        w = w_ref[...]
        scale = jnp.where(j < H, GDN_DK ** -0.5, 1.0).astype(F32)
        for c in range(T // RC):
            pre = _causal_taps(xp_ref, w, K, c * RC)
            act = _silu(pre)
            d = dn_ref[pl.ds(c * RC, RC), :]
            rs = lax.rsqrt(jnp.sum(act * act, axis=-1, keepdims=True) + L2_EPS)
            nh = act * rs
            dact_n = scale * rs * (d - nh * jnp.sum(d * nh, axis=-1, keepdims=True))
            dact = jnp.where(j < 2 * H, dact_n, d)
            dc_ref[pl.ds(c * RC, RC), :] = dact * _dsilu(pre)
        for c in range(T // RC):
            dx_ref[pl.ds(c * RC, RC), :] = _anticausal_taps(dc_ref, w, K, c * RC).astype(BF16)
        _tap_grads(dw_ref, dc_ref, xp_ref, K, T)

    return pl.pallas_call(
        body, name=name, grid=(3 * H,),
        in_specs=[pl.BlockSpec((T, LANES), lambda j: (0, j)), pl.BlockSpec((T, LANES), lambda j: (0, j)),
                  pl.BlockSpec((K, LANES), lambda j: (0, j))],
        out_specs=[pl.BlockSpec((T, LANES), lambda j: (0, j)), pl.BlockSpec((K, LANES), lambda j: (0, j))],
        out_shape=[jax.ShapeDtypeStruct((T, 3 * H * GDN_DK), BF16), jax.ShapeDtypeStruct(cw.shape, F32)],
        scratch_shapes=[pltpu.VMEM((T + PAD, LANES), F32), pltpu.VMEM((T + PAD, LANES), F32)],
        compiler_params=_cparams(("arbitrary",)),
    )(dn, p, cw)


def _dw_conv_fwd(hc, w, b, name):
    T, D = hc.shape
    K = w.shape[0]

    def body(x_ref, w_ref, b_ref, o_ref, xp_ref):
        xp_ref[pl.ds(0, PAD), :] = jnp.zeros((PAD, LANES), F32)
        xp_ref[pl.ds(PAD, T), :] = x_ref[...]
        wv = w_ref[...]
        for c in range(T // RC):
            o_ref[pl.ds(c * RC, RC), :] = _causal_taps(xp_ref, wv, K, c * RC) + b_ref[...]

    return pl.pallas_call(
        body, name=name, grid=(D // LANES,),
        in_specs=[pl.BlockSpec((T, LANES), lambda j: (0, j)), pl.BlockSpec((K, LANES), lambda j: (0, j)),
                  pl.BlockSpec((1, LANES), lambda j: (0, j))],
        out_specs=pl.BlockSpec((T, LANES), lambda j: (0, j)),
        out_shape=jax.ShapeDtypeStruct((T, D), F32),
        scratch_shapes=[pltpu.VMEM((T + PAD, LANES), F32)],
        compiler_params=_cparams(("arbitrary",)),
    )(hc, w, b)


def _dw_conv_bwd(dy, hc, w, name, comm=None):
    T, D = hc.shape
    K = w.shape[0]

    def body(dy_ref, x_ref, w_ref, dx_ref, dw_ref, xp_ref, dc_ref):
        xp_ref[pl.ds(0, PAD), :] = jnp.zeros((PAD, LANES), F32)
        xp_ref[pl.ds(PAD, T), :] = x_ref[...]
        dc_ref[pl.ds(T, PAD), :] = jnp.zeros((PAD, LANES), F32)
        dc_ref[pl.ds(0, T), :] = dy_ref[...]
        wv = w_ref[...]
        for c in range(T // RC):
            dx_ref[pl.ds(c * RC, RC), :] = _anticausal_taps(dc_ref, wv, K, c * RC)
        _tap_grads(dw_ref, dc_ref, xp_ref, K, T)

    return _comm_call(
        body, comm, name=name, grid=(D // LANES,),
        in_specs=[pl.BlockSpec((T, LANES), lambda j: (0, j)), pl.BlockSpec((T, LANES), lambda j: (0, j)),
                  pl.BlockSpec((K, LANES), lambda j: (0, j))],
        out_specs=[pl.BlockSpec((T, LANES), lambda j: (0, j)), pl.BlockSpec((K, LANES), lambda j: (0, j))],
        out_shape=[jax.ShapeDtypeStruct((T, D), F32), jax.ShapeDtypeStruct(w.shape, F32)],
        scratch_shapes=[pltpu.VMEM((T + PAD, LANES), F32), pltpu.VMEM((T + PAD, LANES), F32)],
        args=(dy, hc, w))


NN = (((1,), (0,)), ((), ()))
NT = (((1,), (1,)), ((), ()))
TN = (((0,), (0,)), ((), ()))


def _dotb(a, b, dn=NN):
    return lax.dot_general(a.astype(BF16), b.astype(BF16), dn, preferred_element_type=F32)


def _split_bf16(x, n):
    parts, r = [], x
    for _ in range(n):
        p = r.astype(BF16)
        parts.append(p)
        r = r - p.astype(F32)
    return parts


def _dot3(a, b, dn=NN):
    (ah, al), (bh, bl) = _split_bf16(a, 2), _split_bf16(b, 2)
    d = lambda u, v: lax.dot_general(u, v, dn, preferred_element_type=F32)
    return d(ah, bh) + (d(al, bh) + d(ah, bl))


def _dot_sel(sel, x, pieces, sel_left=True):
    sb = sel.astype(BF16)
    acc = None
    for p in _split_bf16(x, pieces):
        t = (lax.dot_general(sb, p, NN, preferred_element_type=F32) if sel_left
             else lax.dot_general(p, sb, NN, preferred_element_type=F32))
        acc = t if acc is None else acc + t
    return acc


def _iota2(shape, axis):
    return lax.broadcasted_iota(jnp.int32, shape, axis)


def _to_row(col, eye):
    return jnp.sum(jnp.where(eye, col, 0.0), axis=0, keepdims=True)


def _to_col(row, eye):
    return jnp.sum(jnp.where(eye, row, 0.0), axis=1, keepdims=True)


def _gdn_gates(bl, al, alog, dtb):
    beta = _sigmoid(bl)
    x = al + dtb
    sp = jnp.maximum(x, 0.0) + jnp.log(1.0 + jnp.exp(-jnp.abs(x)))
    g = -jnp.exp(alog) * sp
    r, c = _iota2((CHUNK, CHUNK), 0), _iota2((CHUNK, CHUNK), 1)
    G = _dot_sel(r >= c, g, 3)
    return beta, g, G, x


def _head_prep(k, Gc, bc):
    r, c = _iota2((CHUNK, CHUNK), 0), _iota2((CHUNK, CHUNK), 1)
    eye = r == c
    Gr, br = _to_row(Gc, eye), _to_row(bc, eye)
    low, up = r >= c, r <= c
    Dm = jnp.where(low, jnp.exp(jnp.where(low, Gc - Gr, 0.0)), 0.0)
    Dt = jnp.where(up, jnp.exp(jnp.where(up, Gr - Gc, 0.0)), 0.0)
    KK = _dotb(k, k, NT)
    M = jnp.where(r > c, KK * Dm, 0.0)
    At = jnp.where(r < c, KK * Dt, 0.0) * br
    return Dm, KK, M, M * bc, At


def _unit_lower_inverses(As, Ats):
    n = len(As)
    nb = CHUNK // SUB
    lane = _iota2((SUB, CHUNK), 1)
    row = _iota2((SUB, CHUNK), 0)
    Atp = []
    for At in Ats:
        acc = jnp.zeros((SUB, CHUNK), F32)
        for b in range(nb):
            acc = jnp.where(lane // SUB == b, At[b * SUB:(b + 1) * SUB, :], acc)
        Atp.append(acc)
    gr, gc = _iota2((CHUNK, CHUNK), 0), _iota2((CHUNK, CHUNK), 1)
    ones_bd = gr // SUB == gc // SUB
    stack = jnp.concatenate(
        [jnp.where(lane % SUB == i, Atp[m], 0.0) for i in range(1, SUB) for m in range(n)], axis=0)
    Cm = _dot_sel(ones_bd, stack, 2, sel_left=False)
    Z = [(row == lane % SUB).astype(F32) for _ in range(n)]
    for i in range(1, SUB):
        for m in range(n):
            cm = Cm[((i - 1) * n + m) * SUB:((i - 1) * n + m + 1) * SUB, :]
            new = -jnp.sum(cm * Z[m], axis=0, keepdims=True)
            Z[m] = Z[m] + jnp.where(row == i, new, 0.0)
    out = []
    bd = gr // SUB == gc // SUB
    for m in range(n):
        X = jnp.where(bd, jnp.concatenate([Z[m]] * nb, axis=0), 0.0)
        blk = SUB
        while blk < CHUNK:
            N = jnp.where((gr // (2 * blk) == gc // (2 * blk)) & (gr // blk != gc // blk), As[m], 0.0)
            X = X - _dot3(_dot3(X, N), X)
            blk *= 2
        out.append(X)
    return out


def _gdn_fwd(qkvn, p, alog, dtb, H, name, comm=None):
    T = qkvn.shape[0]
    D = H * GDN_DK
    N = T // CHUNK
    bblk = 8 * D // LANES

    def body(q_ref, k_ref, v_ref, b_ref, a_ref, alog_ref, dtb_ref, o_ref, t_ref, s_ref, S_scr):
        @pl.when(pl.program_id(0) == 0)
        def _():
            S_scr[...] = jnp.zeros_like(S_scr)

        beta, _, G, _ = _gdn_gates(b_ref[...], a_ref[...], alog_ref[...], dtb_ref[...])
        prep = []
        for h in range(H):
            sl = slice(h * GDN_DK, (h + 1) * GDN_DK)
            prep.append(_head_prep(k_ref[:, sl], G[:, h:h + 1], beta[:, h:h + 1]))
        Ts = _unit_lower_inverses([pr[3] for pr in prep], [pr[4] for pr in prep])
        for h in range(H):
            sl = slice(h * GDN_DK, (h + 1) * GDN_DK)
            q, k, v = q_ref[:, sl], k_ref[:, sl], v_ref[:, sl]
            Gc, bc = G[:, h:h + 1], beta[:, h:h + 1]
            Dm = prep[h][0]
            Tm = Ts[h]
            t_ref[0, h] = Tm
            eG = jnp.exp(Gc)
            Gl = Gc[CHUNK - 1:CHUNK, :]
            u = _dotb(Tm, v * bc)
            w = _dotb(Tm, k * (bc * eG))
            QK = _dotb(q, k, NT) * Dm
            S = S_scr[h]
            s_ref[0, h] = S
            vn = u - _dotb(w, S)
            o_ref[:, sl] = _dotb(q * eG, S) + _dotb(QK, vn)
            S_scr[h] = S * jnp.exp(Gl) + _dotb(k * jnp.exp(Gl - Gc), vn, TN)

    qkv_spec = [pl.BlockSpec((CHUNK, D), lambda n, cb=cb: (n, cb)) for cb in range(3)]
    return _comm_call(
        body, comm, name=name, grid=(N,),
        in_specs=qkv_spec + [pl.BlockSpec((CHUNK, LANES), lambda n: (n, bblk)),
                             pl.BlockSpec((CHUNK, LANES), lambda n: (n, bblk + 1)),
                             pl.BlockSpec((1, LANES), lambda n: (0, 0)), pl.BlockSpec((1, LANES), lambda n: (0, 0))],
        out_specs=[pl.BlockSpec((CHUNK, D), lambda n: (n, 0)),
                   pl.BlockSpec((1, H, CHUNK, CHUNK), lambda n: (n, 0, 0, 0)),
                   pl.BlockSpec((1, H, GDN_DK, GDN_DK), lambda n: (n, 0, 0, 0))],
        out_shape=[jax.ShapeDtypeStruct((T, D), F32), jax.ShapeDtypeStruct((N, H, CHUNK, CHUNK), F32),
                   jax.ShapeDtypeStruct((N, H, GDN_DK, GDN_DK), F32)],
        scratch_shapes=[pltpu.VMEM((H, GDN_DK, GDN_DK), F32)],
        args=(qkvn, qkvn, qkvn, p, p, alog, dtb))


def _gdn_bwd(do, qkvn, p, alog, dtb, Tinv, Sin, H, name, comm=None):
    T = qkvn.shape[0]
    D = H * GDN_DK
    N = T // CHUNK
    bblk = 8 * D // LANES

    def body(do_ref, q_ref, k_ref, v_ref, b_ref, a_ref, alog_ref, dtb_ref, t_ref, s_ref,
             dqkv_ref, dba_ref, dalog_ref, ddtb_ref, dS_scr):
        @pl.when(pl.program_id(0) == 0)
        def _():
            dS_scr[...] = jnp.zeros_like(dS_scr)
            dalog_ref[...] = jnp.zeros_like(dalog_ref)
            ddtb_ref[...] = jnp.zeros_like(ddtb_ref)

        beta, g, G, x = _gdn_gates(b_ref[...], a_ref[...], alog_ref[...], dtb_ref[...])
        r, c = _iota2((CHUNK, CHUNK), 0), _iota2((CHUNK, CHUNK), 1)
        eye, low, strict = r == c, r >= c, r > c
        lane = _iota2((CHUNK, LANES), 1)
        dG_all = jnp.zeros((CHUNK, LANES), F32)
        dbeta_all = jnp.zeros((CHUNK, LANES), F32)
        for h in range(H):
            sl = slice(h * GDN_DK, (h + 1) * GDN_DK)
            q, k, v, dov = q_ref[:, sl], k_ref[:, sl], v_ref[:, sl], do_ref[:, sl]
            Gc, bc = G[:, h:h + 1], beta[:, h:h + 1]
            Dm, KK, M, _, _ = _head_prep(k, Gc, bc)
            Tm, S, dSo = t_ref[0, h], s_ref[0, h], dS_scr[h]
            eG = jnp.exp(Gc)
            Gl = Gc[CHUNK - 1:CHUNK, :]
            eR, dch = jnp.exp(Gl - Gc), jnp.exp(Gl)
            vb, kb = v * bc, k * (bc * eG)
            u, w = _dotb(Tm, vb), _dotb(Tm, kb)
            QKr = _dotb(q, k, NT)
            QK = QKr * Dm
            qd, kd = q * eG, k * eR
            vn = u - _dotb(w, S)
            dqd = _dotb(dov, S, NT)
            dS = _dotb(qd, dov, TN)
            dQK = jnp.where(low, _dotb(dov, vn, NT), 0.0)
            dvn = _dotb(QK, dov, TN)
            ddch = jnp.sum(jnp.sum(dSo * S, axis=1, keepdims=True), axis=0, keepdims=True)
            dS = dS + dch * dSo
            dkd = _dotb(vn, dSo, NT)
            dvn = dvn + _dotb(kd, dSo)
            dw = -_dotb(dvn, S, NT)
            dS = dS - _dotb(w, dvn, TN)
            dS_scr[h] = dS
            dvb = _dotb(Tm, dvn, TN)
            dkb = _dotb(Tm, dw, TN)
            dA = -jnp.where(strict, _dotb(dvb, u, NT) + _dotb(dkb, w, NT), 0.0)
            rk = jnp.sum(dkb * k, axis=1, keepdims=True)
            dbeta = jnp.sum(dvb * v, axis=1, keepdims=True) + rk * eG + jnp.sum(dA * M, axis=1, keepdims=True)
            deG = rk * bc
            dM = dA * bc
            dKK = dM * Dm
            dQKr = dQK * Dm
            E = dM * M + dQK * QK
            dq = _dotb(dQKr, k) + dqd * eG
            dk = (dkb * (bc * eG) + _dotb(dKK, k) + _dotb(dKK, k, TN) + _dotb(dQKr, q, TN) + dkd * eR)
            deG = deG + jnp.sum(dqd * q, axis=1, keepdims=True)
            deR = jnp.sum(dkd * k, axis=1, keepdims=True)
            dGl = jnp.sum(deR * eR, axis=0, keepdims=True) + ddch * dch
            dGc = (jnp.sum(E, axis=1, keepdims=True) - _to_col(jnp.sum(E, axis=0, keepdims=True), eye)
                   + deG * eG - deR * eR + jnp.where(r[:, :1] == CHUNK - 1, dGl, 0.0))
            dqkv_ref[:, sl] = dq
            dqkv_ref[:, D + h * GDN_DK:D + (h + 1) * GDN_DK] = dk
            dqkv_ref[:, 2 * D + h * GDN_DK:2 * D + (h + 1) * GDN_DK] = dvb * bc
            dG_all = jnp.where(lane == h, dGc, dG_all)
            dbeta_all = jnp.where(lane == h, dbeta, dbeta_all)
        dg = _dot_sel(r <= c, dG_all, 3)
        da = dg * (-jnp.exp(alog_ref[...])) * _sigmoid(x)
        dba_ref[:, :LANES] = (dbeta_all * beta * (1.0 - beta)).astype(BF16)
        dba_ref[:, LANES:] = da.astype(BF16)
        dalog_ref[...] += _rsum(dg * g)
        ddtb_ref[...] += _rsum(da)

    rev = lambda n: N - 1 - n
    qkv_spec = [pl.BlockSpec((CHUNK, D), lambda n, cb=cb: (rev(n), cb)) for cb in range(3)]
    return _comm_call(
        body, comm, name=name, grid=(N,),
        in_specs=[pl.BlockSpec((CHUNK, D), lambda n: (rev(n), 0))] + qkv_spec + [
            pl.BlockSpec((CHUNK, LANES), lambda n: (rev(n), bblk)),
            pl.BlockSpec((CHUNK, LANES), lambda n: (rev(n), bblk + 1)),
            pl.BlockSpec((1, LANES), lambda n: (0, 0)), pl.BlockSpec((1, LANES), lambda n: (0, 0)),
            pl.BlockSpec((1, H, CHUNK, CHUNK), lambda n: (rev(n), 0, 0, 0)),
            pl.BlockSpec((1, H, GDN_DK, GDN_DK), lambda n: (rev(n), 0, 0, 0))],
        out_specs=[pl.BlockSpec((CHUNK, 3 * D), lambda n: (rev(n), 0)),
                   pl.BlockSpec((CHUNK, 2 * LANES), lambda n: (rev(n), 0)),
                   pl.BlockSpec((1, LANES), lambda n: (0, 0)), pl.BlockSpec((1, LANES), lambda n: (0, 0))],
        out_shape=[jax.ShapeDtypeStruct((T, 3 * D), F32), jax.ShapeDtypeStruct((T, 2 * LANES), BF16),
                   jax.ShapeDtypeStruct((1, LANES), F32), jax.ShapeDtypeStruct((1, LANES), F32)],
        scratch_shapes=[pltpu.VMEM((H, GDN_DK, GDN_DK), F32)],
        args=(do, qkvn, qkvn, qkvn, p, p, alog, dtb, Tinv, Sin))


def _mix_in_reorder(wt, D, H):
    o1 = 4 * D
    o2, o3 = o1 + H, o1 + 2 * H
    z = jnp.zeros((LANES - H, wt.shape[1]), wt.dtype)
    return jnp.concatenate([wt[:o1], wt[o3:], wt[o1:o2], z, wt[o2:o3], z], axis=0)


def _mix_in_restore(dwt, D, H):
    b0 = 8 * D
    return jnp.concatenate([dwt[:4 * D], dwt[b0:b0 + H], dwt[b0 + LANES:b0 + LANES + H], dwt[4 * D:b0]], axis=0)


def _ffn_fwd(x, W, pre, tag):
    h = _rms_fwd(x, W[pre + "_norm_pre"], tag + "_pre")
    a = _matmul(h, W[pre + "_w_in"], "nt", F32, tag + "_in")
    s = _swiglu_fwd(a, tag + "_act")
    f = _matmul(s, W[pre + "_w_out"], "nn", F32, tag + "_out")
    return _post_fwd(x, f, W[pre + "_norm_post"], 0.5, tag + "_post"), (x, h, a, s, f)


def _ffn_bwd(dxn, saved, W, pre, tag):
    x, h, a, s, f = saved
    df, dpost = _post_bwd(dxn, f, W[pre + "_norm_post"], 0.5, tag + "_dpost")
    ds = _matmul(df, W[pre + "_w_out"], "nt", F32, tag + "_ds")
    dw_out = _matmul(s, df, "tn", F32, tag + "_dwout")
    da = _swiglu_bwd(ds, a, tag + "_dact")
    dh = _matmul(da, W[pre + "_w_in"], "nn", F32, tag + "_dh")
    dw_in = _matmul(da, h, "tn", F32, tag + "_dwin")
    dx, dpre = _pre_bwd(dh, x, W[pre + "_norm_pre"], dxn, tag + "_dpre")
    return dx, {pre + "_norm_pre": dpre, pre + "_norm_post": dpost, pre + "_w_in": dw_in, pre + "_w_out": dw_out}


def _mix_fwd(x, W, H, tag, gather=None):
    h = _rms_fwd(x, W["mix_norm_pre"], tag + "_pre")
    p = _matmul(h, W["mix_w_in"], "nt", F32, tag + "_in")
    qkvn = _qkv_conv_fwd(p, W["gdn_conv_w"], H, tag + "_qkvconv")
    (o, Tinv, Sin), gathered = _gdn_fwd(qkvn, p, W["gdn_a_log"], W["gdn_dt_bias"], H, tag + "_gdn",
                                        comm=None if gather is None else (_GatherPlan(gather), gather))
    og = _gdn_gate_fwd(o, p, W["gdn_norm_w"], tag + "_gdngate")
    ya = _matmul(og, W["gdn_w_o"], "nn", F32, tag + "_gdno")
    hc = _glu_fwd(p, W["cnv_pw1_b"], tag + "_glu")
    hcv = _dw_conv_fwd(hc, W["cnv_dw_w"], W["cnv_dw_b"], tag + "_dwconv")
    hl = _ln_silu_fwd(hcv, W["cnv_ln_g"], W["cnv_ln_b"], tag + "_ln")
    yb = _matmul(hl, W["cnv_w_o"], "nn", F32, tag + "_cnvo")
    ym = _merge_fwd(p, ya, yb, W["cnv_b_o"], tag + "_merge")
    y = _matmul(ym, W["mix_w_out"], "nn", F32, tag + "_out")
    xn = _post_fwd(x, y, W["mix_norm_post"], 1.0, tag + "_post")
    return xn, (x, h, p, qkvn, o, Tinv, Sin, og, ya, hc, hcv, hl, yb, ym, y), gathered


def _pair_adds(Gs, R1s, cidx, tag):
    return [_pair_add(G.reshape(4, 2, G.shape[1], G.shape[2]), R1, cidx, f"{tag}_pair_add{k}")
            for k, (G, R1) in enumerate(zip(Gs, R1s))]


def _mix_bwd(dxn, saved, W, H, tag, reduce=None):
    x, h, p, qkvn, o, Tinv, Sin, og, ya, hc, hcv, hl, yb, ym, y = saved
    g = {}
    dy, g["mix_norm_post"] = _post_bwd(dxn, y, W["mix_norm_post"], 1.0, tag + "_dpost")
    dym = _matmul(dy, W["mix_w_out"], "nt", F32, tag + "_dym")
    g["mix_w_out"] = _matmul(ym, dy, "tn", F32, tag + "_dwout")
    dya, dyb, dgates, g["cnv_b_o"] = _merge_bwd(dym, p, ya, yb, W["cnv_b_o"], tag + "_dmerge")
    dhl = _matmul(dyb, W["cnv_w_o"], "nt", F32, tag + "_dhl")
    g["cnv_w_o"] = _matmul(hl, dyb, "tn", F32, tag + "_dwcnvo")
    dhcv, g["cnv_ln_g"], g["cnv_ln_b"], g["cnv_dw_b"] = _ln_silu_bwd(dhl, hcv, W["cnv_ln_g"], W["cnv_ln_b"], tag + "_dln")
    chips = None
    if reduce is None:
        (dhc, g["cnv_dw_w"]), _ = _dw_conv_bwd(dhcv, hc, W["cnv_dw_w"], tag + "_ddwconv")
    else:
        Gs, cidx, rtag = reduce
        (dhc, g["cnv_dw_w"]), R1s = _dw_conv_bwd(dhcv, hc, W["cnv_dw_w"], tag + "_ddwconv", comm=(_SiblingPlan(Gs), Gs))
        Ps = _pair_adds(Gs, R1s, cidx, rtag)
        chips = (_ChipsPlan(Ps), Ps)
    dglu, g["cnv_pw1_b"] = _glu_bwd(dhc, p, W["cnv_pw1_b"], tag + "_dglu")
    dog = _matmul(dya, W["gdn_w_o"], "nt", F32, tag + "_dog")
    g["gdn_w_o"] = _matmul(og, dya, "tn", F32, tag + "_dwgdno")
    do, dz, g["gdn_norm_w"] = _gdn_gate_bwd(dog, o, p, W["gdn_norm_w"], tag + "_dgdngate")
    (dqkvn, dba, g["gdn_a_log"], g["gdn_dt_bias"]), reduced = _gdn_bwd(
        do, qkvn, p, W["gdn_a_log"], W["gdn_dt_bias"], Tinv, Sin, H, tag + "_dgdn", comm=chips)
    dqkv, g["gdn_conv_w"] = _qkv_conv_bwd(dqkvn, p, W["gdn_conv_w"], H, tag + "_dqkvconv")
    dp = jnp.concatenate([dqkv, dz, dglu, dgates, dba], axis=1)
    dh = _matmul(dp, W["mix_w_in"], "nn", F32, tag + "_dh")
    g["mix_w_in"] = _matmul(dp, h, "tn", F32, tag + "_dwin")
    dx, g["mix_norm_pre"] = _pre_bwd(dh, x, W["mix_norm_pre"], dxn, tag + "_dpre")
    return dx, g, reduced


def _trunk_fwd_bwd(x, tgt, H, L, weights_of, blocks_of=None, to_reduce=None):
    saved, Ws = [], []
    W = weights_of(0, None)
    for i in range(L):
        Ws.append(W)
        x, s1 = _ffn_fwd(x, W, "ffn1", f"l{i}_ffn1")
        nxt = blocks_of(i + 1) if (blocks_of is not None and i + 1 < L) else None
        x, s2, gathered = _mix_fwd(x, W, H, f"l{i}_mix", gather=nxt)
        x, s3 = _ffn_fwd(x, W, "ffn2", f"l{i}_ffn2")
        saved.append((s1, s2, s3))
        if i + 1 < L:
            W = weights_of(i + 1, gathered)
    dx, loss = _loss_fwd_bwd(x, tgt, "loss")
    grads, reduced, pending = [None] * L, [None] * L, None
    for i in reversed(range(L)):
        s1, s2, s3 = saved[i]
        dx, g3 = _ffn_bwd(dx, s3, Ws[i], "ffn2", f"l{i}_ffn2")
        dx, g2, red = _mix_bwd(dx, s2, Ws[i], H, f"l{i}_mix", reduce=pending)
        if pending is not None:
            reduced[i + 1] = red
        dx, g1 = _ffn_bwd(dx, s1, Ws[i], "ffn1", f"l{i}_ffn1")
        grads[i] = {**g1, **g2, **g3}
        pending = to_reduce(i, grads[i]) if (to_reduce is not None and i > 0) else None
    return loss, dx, grads, reduced


HBM_SPEC = pl.BlockSpec(memory_space=pltpu.HBM)


def _coords():
    return lax.axis_index("x"), lax.axis_index("y"), lax.axis_index("c")


class _GatherPlan:
    has_middle = True

    def __init__(self, shards):
        self.n = len(shards)
        self.out_shape = [jax.ShapeDtypeStruct((N_DEV,) + s.shape, s.dtype) for s in shards]
        self.sems = [pltpu.SemaphoreType.DMA((self.n, 7)), pltpu.SemaphoreType.DMA((self.n, 7)),
                     pltpu.SemaphoreType.DMA((self.n,))]

    def _parts(self, ins, outs, sems):
        send_sems, recv_sems, local_sems = sems
        x, y, c = _coords()
        me, sibling = (x, y, c), (x, y, 1 - c)
        chips = [(1 - x, y), (x, 1 - y), (1 - x, 1 - y)]

        def copy(w, k, block, to, src=None):
            dst = outs[w].at[4 * block[0] + 2 * block[1] + block[2]]
            return pltpu.make_async_remote_copy(
                src_ref=dst if src is None else src, dst_ref=dst, send_sem=send_sems.at[w, k],
                recv_sem=recv_sems.at[w, k], device_id=to, device_id_type=MESH)

        mine = [pltpu.make_async_copy(ins[w], outs[w].at[4 * x + 2 * y + c], local_sems.at[w]) for w in range(self.n)]
        first = []
        for w in range(self.n):
            first.append(copy(w, 0, me, sibling, src=ins[w]))
            first += [copy(w, 1 + j, me, (*chip, c), src=ins[w]) for j, chip in enumerate(chips)]
        passed = [copy(w, 4 + j, (*chip, c), sibling) for j, chip in enumerate(chips) for w in range(self.n)]
        return copy, mine, first, passed, chips, me, sibling, c

    def begin(self, ins, outs, sems):
        _, mine, first, _, _, _, _, _ = self._parts(ins, outs, sems)
        for cp in mine + first:
            cp.start()

    def middle(self, ins, outs, sems):
        copy, _, _, passed, chips, me, _, c = self._parts(ins, outs, sems)
        for j, chip in enumerate(chips):
            for w in range(self.n):
                copy(w, 1 + j, (*chip, c), me).wait_recv()
                passed[j * self.n + w].start()

    def finish(self, ins, outs, sems):
        copy, mine, first, passed, chips, me, sibling, c = self._parts(ins, outs, sems)
        for w in range(self.n):
            copy(w, 0, sibling, me).wait_recv()
            for j, chip in enumerate(chips):
                copy(w, 4 + j, (*chip, 1 - c), me).wait_recv()
        for cp in first + passed:
            cp.wait_send()
        for cp in mine:
            cp.wait()


class _SiblingPlan:
    has_middle = False

    def __init__(self, Gs):
        self.n = len(Gs)
        self.out_shape = [jax.ShapeDtypeStruct((4,) + g.shape[1:], g.dtype) for g in Gs]
        self.sems = [pltpu.SemaphoreType.DMA((self.n, 4)), pltpu.SemaphoreType.DMA((self.n, 4))]

    def _copies(self, ins, outs, sems):
        send_sems, recv_sems = sems
        x, y, c = _coords()
        return [pltpu.make_async_remote_copy(
            src_ref=ins[w].at[2 * q + (1 - c)], dst_ref=outs[w].at[q], send_sem=send_sems.at[w, q],
            recv_sem=recv_sems.at[w, q], device_id=(x, y, 1 - c), device_id_type=MESH)
            for w in range(self.n) for q in range(4)]

    def begin(self, ins, outs, sems):
        for cp in self._copies(ins, outs, sems):
            cp.start()

    def finish(self, ins, outs, sems):
        for cp in self._copies(ins, outs, sems):
            cp.wait()


class _ChipsPlan:
    has_middle = False

    def __init__(self, Ps):
        self.n = len(Ps)
        self.out_shape = [jax.ShapeDtypeStruct(p.shape, p.dtype) for p in Ps]
        self.sems = [pltpu.SemaphoreType.DMA((self.n, 3)), pltpu.SemaphoreType.DMA((self.n, 3)),
                     pltpu.SemaphoreType.DMA((self.n,))]

    def _copies(self, ins, outs, sems):
        send_sems, recv_sems, local_sems = sems
        x, y, c = _coords()
        me_q = 2 * x + y
        cps = []
        for w in range(self.n):
            cps.append(pltpu.make_async_copy(ins[w].at[me_q], outs[w].at[me_q], local_sems.at[w]))
            for j, (px, py) in enumerate([(1 - x, y), (x, 1 - y), (1 - x, 1 - y)]):
                cps.append(pltpu.make_async_remote_copy(
                    src_ref=ins[w].at[2 * px + py], dst_ref=outs[w].at[me_q], send_sem=send_sems.at[w, j],
                    recv_sem=recv_sems.at[w, j], device_id=(px, py, c), device_id_type=MESH))
        return cps

    def begin(self, ins, outs, sems):
        for cp in self._copies(ins, outs, sems):
            cp.start()

    def finish(self, ins, outs, sems):
        for cp in self._copies(ins, outs, sems):
            cp.wait()


def _comm_only(plan, arrays, name):
    n = plan.n

    def body(*refs):
        ins, outs, sems = refs[:n], refs[n:2 * n], refs[2 * n:]
        plan.begin(ins, outs, sems)
        if plan.has_middle:
            plan.middle(ins, outs, sems)
        plan.finish(ins, outs, sems)

    return pl.pallas_call(
        body, name=name, out_shape=plan.out_shape, in_specs=[HBM_SPEC] * n, out_specs=[HBM_SPEC] * n,
        scratch_shapes=plan.sems,
    )(*arrays)


def _comm_call(body, comm, *, name, grid, in_specs, out_specs, out_shape, scratch_shapes, args):
    if comm is None:
        res = pl.pallas_call(body, name=name, grid=grid, in_specs=in_specs, out_specs=out_specs, out_shape=out_shape,
                             scratch_shapes=scratch_shapes, compiler_params=_cparams(("arbitrary",)))(*args)
        return res, None
    plan, arrays = comm
    n_in, n_out, n_scr, n = len(in_specs), len(out_specs), len(scratch_shapes), plan.n
    steps = grid[0]

    def kern(*refs):
        ins, cins = refs[:n_in], refs[n_in:n_in + n]
        outs, couts = refs[n_in + n:n_in + n + n_out], refs[n_in + n + n_out:n_in + 2 * n + n_out]
        scr, csems = refs[n_in + 2 * n + n_out:n_in + 2 * n + n_out + n_scr], refs[n_in + 2 * n + n_out + n_scr:]
        step = pl.program_id(0)

        @pl.when(step == 0)
        def _():
            plan.begin(cins, couts, csems)

        body(*ins, *outs, *scr)
        if plan.has_middle:
            @pl.when(step == (3 * steps) // 4)
            def _():
                plan.middle(cins, couts, csems)

        @pl.when(step == steps - 1)
        def _():
            plan.finish(cins, couts, csems)

    res = pl.pallas_call(
        kern, name=name, grid=grid, in_specs=list(in_specs) + [HBM_SPEC] * n,
        out_specs=list(out_specs) + [HBM_SPEC] * n, out_shape=list(out_shape) + plan.out_shape,
        scratch_shapes=list(scratch_shapes) + plan.sems, compiler_params=_cparams(("arbitrary",)),
    )(*args, *arrays)
    return res[:n_out], res[n_out:]


def _row_tile(R, target=256):
    best = None
    for t in range(8, min(R, target) + 1, 8):
        if R % t == 0:
            best = t
    return best if best is not None else R


def _pair_add(G, R1, cidx, name):
    _, _, R, C = G.shape
    tb = _row_tile(R)

    def body(c_ref, g_ref, r_ref, o_ref):
        o_ref[...] = (g_ref[...] + r_ref[...]).astype(BF16)

    return pl.pallas_call(
        body, name=name,
        grid_spec=pltpu.PrefetchScalarGridSpec(
            num_scalar_prefetch=1, grid=(4, R // tb),
            in_specs=[pl.BlockSpec((None, None, tb, C), lambda q, i, cr: (q, cr[0], i, 0)),
                      pl.BlockSpec((None, tb, C), lambda q, i, cr: (q, i, 0))],
            out_specs=pl.BlockSpec((None, tb, C), lambda q, i, cr: (q, i, 0))),
        out_shape=jax.ShapeDtypeStruct((4, R, C), BF16),
        compiler_params=_cparams(("arbitrary", "arbitrary")),
    )(cidx, G, R1)


def _sum_parts(parts, name):
    P, R, C = parts.shape

    def body(p_ref, o_ref):
        acc = p_ref[0]
        for j in range(1, P):
            acc = acc + p_ref[j]
        o_ref[...] = acc

    return pl.pallas_call(
        body, name=name, out_shape=jax.ShapeDtypeStruct((R, C), F32),
        in_specs=[pl.BlockSpec(memory_space=pltpu.VMEM)], out_specs=pl.BlockSpec(memory_space=pltpu.VMEM),
        compiler_params=_cparams(),
    )(parts)


def _adamw(w, m, v, parts, name):
    G, R, C = w.shape
    P = parts[0].shape[0]
    tb = _row_tile(R)
    nb = R // tb
    c1 = 1.0 / (1.0 - ADAM_B1 ** ADAM_STEP)
    c2 = 1.0 / (1.0 - ADAM_B2 ** ADAM_STEP)

    def body(w_ref, m_ref, v_ref, *rest):
        p_refs, (g_ref, d_ref, nm_ref, nv_ref) = rest[:G], rest[G:]
        l = pl.program_id(0)
        g = None
        for k in range(G):
            gk = p_refs[k][0].astype(F32)
            for j in range(1, P):
                gk = gk + p_refs[k][j].astype(F32)
            g = gk if g is None else jnp.where(l == k, gk, g)
        nm = ADAM_B1 * m_ref[...] + (1.0 - ADAM_B1) * g
        nv = ADAM_B2 * v_ref[...] + (1.0 - ADAM_B2) * (g * g)
        g_ref[...] = g
        nm_ref[...] = nm
        nv_ref[...] = nv
        d_ref[...] = -ADAM_LR * ((nm * c1) / (jnp.sqrt(nv * c2) + ADAM_EPS) + ADAM_WD * w_ref[...])

    blk = pl.BlockSpec((None, tb, C), lambda l, i: (l, i, 0))

    def part_spec(k):
        return pl.BlockSpec((P, tb, C), lambda l, i: (0, jnp.where(l < k, 0, jnp.where(l > k, nb - 1, i)), 0))

    return pl.pallas_call(
        body, name=name, grid=(G, nb),
        in_specs=[blk, blk, blk] + [part_spec(k) for k in range(G)],
        out_specs=[blk] * 4, out_shape=[jax.ShapeDtypeStruct((G, R, C), F32)] * 4,
        compiler_params=_cparams(("arbitrary", "arbitrary")),
    )(w, m, v, *parts)


BIG = ("ffn1_w_in", "ffn1_w_out", "mix_w_in", "gdn_w_o", "cnv_w_o", "mix_w_out", "ffn2_w_in", "ffn2_w_out")
COL_SHARDED = ("ffn1_w_in", "mix_w_in", "ffn2_w_in")
SMALL_SHARDED = ("gdn_conv_w", "cnv_dw_w")
NAMES = ("ffn1_norm_pre", "ffn1_norm_post", "ffn1_w_in", "ffn1_w_out", "mix_norm_pre", "mix_norm_post", "mix_w_in",
         "gdn_conv_w", "gdn_a_log", "gdn_dt_bias", "gdn_norm_w", "gdn_w_o", "cnv_pw1_b", "cnv_dw_w", "cnv_dw_b",
         "cnv_ln_g", "cnv_ln_b", "cnv_w_o", "cnv_b_o", "mix_w_out", "ffn2_norm_pre", "ffn2_norm_post", "ffn2_w_in",
         "ffn2_w_out")
SMALL = tuple(n for n in NAMES if n not in BIG)


def _gathered_layer_weights(gath, params, i, D, H):
    W = {}
    for n in BIG:
        g = gath[n]
        g = g.reshape(-1, g.shape[-1])
        W[n] = _mix_in_reorder(g, D, H) if n == "mix_w_in" else g
    for n in SMALL_SHARDED:
        g = gath[n]
        W[n] = jnp.transpose(g, (1, 0, 2)).reshape(g.shape[1], -1)
    for n in SMALL:
        if n in SMALL_SHARDED:
            continue
        v = params[n][i]
        if n in ("gdn_a_log", "gdn_dt_bias"):
            v = jnp.pad(v, (0, LANES - H))
        W[n] = v.reshape(1, -1)
    return W


def kernel(x, ffn1_norm_pre, ffn1_norm_post, ffn1_w_in, ffn1_w_out, mix_norm_pre, mix_norm_post, mix_w_in, gdn_conv_w, gdn_a_log, gdn_dt_bias, gdn_norm_w, gdn_w_o, cnv_pw1_b, cnv_dw_w, cnv_dw_b, cnv_ln_g, cnv_ln_b, cnv_w_o, cnv_b_o, mix_w_out, ffn2_norm_pre, ffn2_norm_post, ffn2_w_in, ffn2_w_out, loss_target, m_ffn1_norm_pre, m_ffn1_norm_post, m_ffn1_w_in, m_ffn1_w_out, m_mix_norm_pre, m_mix_norm_post, m_mix_w_in, m_gdn_conv_w, m_gdn_a_log, m_gdn_dt_bias, m_gdn_norm_w, m_gdn_w_o, m_cnv_pw1_b, m_cnv_dw_w, m_cnv_dw_b, m_cnv_ln_g, m_cnv_ln_b, m_cnv_w_o, m_cnv_b_o, m_mix_w_out, m_ffn2_norm_pre, m_ffn2_norm_post, m_ffn2_w_in, m_ffn2_w_out, v_ffn1_norm_pre, v_ffn1_norm_post, v_ffn1_w_in, v_ffn1_w_out, v_mix_norm_pre, v_mix_norm_post, v_mix_w_in, v_gdn_conv_w, v_gdn_a_log, v_gdn_dt_bias, v_gdn_norm_w, v_gdn_w_o, v_cnv_pw1_b, v_cnv_dw_w, v_cnv_dw_b, v_cnv_ln_g, v_cnv_ln_b, v_cnv_w_o, v_cnv_b_o, v_mix_w_out, v_ffn2_norm_pre, v_ffn2_norm_post, v_ffn2_w_in, v_ffn2_w_out):
    params = dict(zip(NAMES, (ffn1_norm_pre, ffn1_norm_post, ffn1_w_in, ffn1_w_out, mix_norm_pre, mix_norm_post, mix_w_in, gdn_conv_w, gdn_a_log, gdn_dt_bias, gdn_norm_w, gdn_w_o, cnv_pw1_b, cnv_dw_w, cnv_dw_b, cnv_ln_g, cnv_ln_b, cnv_w_o, cnv_b_o, mix_w_out, ffn2_norm_pre, ffn2_norm_post, ffn2_w_in, ffn2_w_out)))
    mom1 = dict(zip(NAMES, (m_ffn1_norm_pre, m_ffn1_norm_post, m_ffn1_w_in, m_ffn1_w_out, m_mix_norm_pre, m_mix_norm_post, m_mix_w_in, m_gdn_conv_w, m_gdn_a_log, m_gdn_dt_bias, m_gdn_norm_w, m_gdn_w_o, m_cnv_pw1_b, m_cnv_dw_w, m_cnv_dw_b, m_cnv_ln_g, m_cnv_ln_b, m_cnv_w_o, m_cnv_b_o, m_mix_w_out, m_ffn2_norm_pre, m_ffn2_norm_post, m_ffn2_w_in, m_ffn2_w_out)))
    mom2 = dict(zip(NAMES, (v_ffn1_norm_pre, v_ffn1_norm_post, v_ffn1_w_in, v_ffn1_w_out, v_mix_norm_pre, v_mix_norm_post, v_mix_w_in, v_gdn_conv_w, v_gdn_a_log, v_gdn_dt_bias, v_gdn_norm_w, v_gdn_w_o, v_cnv_pw1_b, v_cnv_dw_w, v_cnv_dw_b, v_cnv_ln_g, v_cnv_ln_b, v_cnv_w_o, v_cnv_b_o, v_mix_w_out, v_ffn2_norm_pre, v_ffn2_norm_post, v_ffn2_w_in, v_ffn2_w_out)))
    T, D = x.shape[1], x.shape[2]
    H = D // GDN_DK
    L = ffn1_norm_pre.shape[0]
    xi, yi, ci = _coords()
    dev = 4 * xi + 2 * yi + ci

    ag_names = BIG + SMALL_SHARDED

    def shard_to_send(n):
        if n in COL_SHARDED:
            return jnp.swapaxes(params[n], 1, 2).astype(BF16)
        return params[n].astype(BF16) if n in BIG else params[n]

    send = {n: shard_to_send(n) for n in ag_names}

    def blocks_of(i):
        return [send[n][i] for n in ag_names]

    first = blocks_of(0)
    gathered0 = _comm_only(_GatherPlan(first), first, "ag_weights_l0")

    def weights_of(i, gathered):
        return _gathered_layer_weights(dict(zip(ag_names, gathered0 if i == 0 else gathered)), params, i, D, H)

    cidx = jnp.reshape(ci, (1,)).astype(jnp.int32)

    def to_reduce(i, grads_i):
        Gs = []
        for n in BIG:
            g = grads_i[n]
            if n == "mix_w_in":
                g = _mix_in_restore(g, D, H)
            Gs.append(g.reshape(N_DEV, -1, g.shape[-1]))
        return Gs, cidx, f"l{i}"

    loss_row, dx, grads, reduced = _trunk_fwd_bwd(x[0], loss_target[0], H, L, weights_of, blocks_of, to_reduce)
    loss = lax.psum(loss_row[0, 0], ("x", "y", "c"))

    Gs, _, _ = to_reduce(0, grads[0])
    R1s = _comm_only(_SiblingPlan(Gs), Gs, "rs_sibling_l0")
    Ps = _pair_adds(Gs, R1s, cidx, "l0")
    reduced[0] = _comm_only(_ChipsPlan(Ps), Ps, "rs_chips_l0")
    R2s = {n: [jnp.swapaxes(reduced[i][k], 1, 2) if n in COL_SHARDED else reduced[i][k] for i in range(L)]
           for k, n in enumerate(BIG)}

    pieces = []
    for i in range(L):
        for n in SMALL:
            piece = grads[i][n].reshape(-1, LANES)
            pieces.append(jnp.pad(piece, ((0, (-piece.shape[0]) % 8), (0, 0))))
    packed = jnp.concatenate(pieces, axis=0)
    small_all = _comm_only(_GatherPlan([packed]), [packed], "ag_small_grads")[0]
    small_sum = _sum_parts(small_all, "sum_small_grads")
    small_g = {n: [None] * L for n in SMALL}
    off = 0
    for i in range(L):
        for n in SMALL:
            shape = grads[i][n].shape
            cnt = shape[0] * shape[1] // LANES
            g = small_sum[off:off + cnt].reshape(shape)
            off += cnt + (-cnt) % 8
            if n in ("gdn_a_log", "gdn_dt_bias"):
                g = g[:, :H]
            if n in SMALL_SHARDED:
                wloc = params[n].shape[-1]
                g = lax.dynamic_slice_in_dim(g, dev * wloc, wloc, axis=1)
            small_g[n][i] = g

    outs = {}
    for n in NAMES:
        w, m, v = params[n], mom1[n], mom2[n]
        if n in BIG:
            shape3, parts = w.shape, R2s[n]
        else:
            rows, cols = (w.shape[0] * w.shape[1], w.shape[2]) if w.ndim == 3 else w.shape
            shape3, parts = (1, rows, cols), [jnp.stack(small_g[n], axis=0).reshape(1, rows, cols)]
        res = _adamw(w.reshape(shape3), m.reshape(shape3), v.reshape(shape3), parts, "adamw_" + n)
        outs[n] = [r.reshape(w.shape) for r in res]

    result = [loss, dx[None]]
    for k in range(4):
        result += [outs[n][k] for n in NAMES]
    return tuple(result)
```

```python
import jax
import jax.numpy as jnp
from jax import lax
from jax.experimental import pallas as pl
from jax.experimental.pallas import tpu as pltpu

F32 = jnp.float32
BF16 = jnp.bfloat16

GDN_DK = 128
CHUNK = 64
GDN_CONV = 4
CNV_K = 31
RMS_EPS = 1e-6
LN_EPS = 1e-5
L2_EPS = 1e-6
ADAM_LR = 0.001
ADAM_B1 = 0.9
ADAM_B2 = 0.999
ADAM_EPS = 1e-08
ADAM_WD = 0.01
ADAM_STEP = 10

LANES = 128
SUB = 16
VMEM_LIMIT = 56 * 1024 * 1024
N_DEV = 8
MESH = pl.DeviceIdType.MESH


def _cparams(sem=None, **kw):
    if sem is not None:
        kw["dimension_semantics"] = sem
    return pltpu.CompilerParams(vmem_limit_bytes=VMEM_LIMIT, **kw)


def _tile(dim, target):
    best = None
    for t in range(LANES, min(dim, target) + 1, LANES):
        if dim % t == 0:
            best = t
    return best if best is not None else dim


def _sigmoid(x):
    return 1.0 / (1.0 + jnp.exp(-x))


def _silu(x):
    return x * _sigmoid(x)


def _dsilu(x):
    s = _sigmoid(x)
    return s * (1.0 + x * (1.0 - s))


MM_VMEM_BUDGET = 40 * 1024 * 1024
MM_MAX_TILE = 2048


def _mm_tiles(M, N, K, out_bytes):
    def cands(dim):
        c = [t for t in range(LANES, min(dim, MM_MAX_TILE) + 1, LANES) if dim % t == 0]
        return c or [dim]
    best = None
    for tm in cands(M):
        for tn in cands(N):
            vm = 2 * (2 * K * (tm + tn) + tm * tn * out_bytes)
            if vm <= MM_VMEM_BUDGET and (best is None or tm * tn > best[0] * best[1]):
                best = (tm, tn)
    return best if best is not None else (cands(M)[0], cands(N)[0])


def _matmul(a, b, mode, out_dtype, name):
    if mode == "nn":
        (M, K), N = a.shape, b.shape[1]
    elif mode == "nt":
        (M, K), N = a.shape, b.shape[0]
    else:
        (K, M), N = a.shape, b.shape[1]
    tm, tn = _mm_tiles(M, N, K, jnp.dtype(out_dtype).itemsize)
    if mode == "nn":
        a_spec = pl.BlockSpec((tm, K), lambda j, i: (i, 0))
        b_spec = pl.BlockSpec((K, tn), lambda j, i: (0, j))
        dn = (((1,), (0,)), ((), ()))
    elif mode == "nt":
        a_spec = pl.BlockSpec((tm, K), lambda j, i: (i, 0))
        b_spec = pl.BlockSpec((tn, K), lambda j, i: (j, 0))
        dn = (((1,), (1,)), ((), ()))
    else:
        a_spec = pl.BlockSpec((K, tm), lambda j, i: (0, i))
        b_spec = pl.BlockSpec((K, tn), lambda j, i: (0, j))
        dn = (((0,), (0,)), ((), ()))

    def body(a_ref, b_ref, o_ref):
        o_ref[...] = lax.dot_general(a_ref[...], b_ref[...], dn, preferred_element_type=F32).astype(out_dtype)

    return pl.pallas_call(
        body, name=name, grid=(N // tn, M // tm), in_specs=[a_spec, b_spec],
        out_specs=pl.BlockSpec((tm, tn), lambda j, i: (i, j)),
        out_shape=jax.ShapeDtypeStruct((M, N), out_dtype),
        compiler_params=_cparams(("parallel", "parallel")),
    )(a, b)


def _rowcall(name, body, T, tb, row_ins, par_ins, row_outs, acc_outs):
    n_ri, n_pi, n_ro = len(row_ins), len(par_ins), len(row_outs)

    def kern(*refs):
        ri, pi = refs[:n_ri], refs[n_ri:n_ri + n_pi]
        ro, ao = refs[n_ri + n_pi:n_ri + n_pi + n_ro], refs[n_ri + n_pi + n_ro:]
        if ao:
            @pl.when(pl.program_id(0) == 0)
            def _():
                for r in ao:
                    r[...] = jnp.zeros_like(r)
        body(ri, pi, ro, ao)

    in_specs = [pl.BlockSpec((tb, w), lambda i, cb=cb: (i, cb)) for (_, w, cb) in row_ins]
    in_specs += [pl.BlockSpec(p.shape, lambda i: (0, 0)) for p in par_ins]
    out_specs = [pl.BlockSpec((tb, w), lambda i: (i, 0)) for (w, _) in row_outs]
    out_specs += [pl.BlockSpec((1, w), lambda i: (0, 0)) for w in acc_outs]
    out_shape = [jax.ShapeDtypeStruct((T, w), dt) for (w, dt) in row_outs]
    out_shape += [jax.ShapeDtypeStruct((1, w), F32) for w in acc_outs]
    return pl.pallas_call(
        kern, name=name, grid=(T // tb,), in_specs=in_specs, out_specs=out_specs, out_shape=out_shape,
        compiler_params=_cparams(("arbitrary",)),
    )(*[a for (a, _, _) in row_ins], *par_ins)


def _rsum(x):
    return jnp.sum(x, axis=0, keepdims=True)


def _rms_rstd(x):
    return lax.rsqrt(jnp.mean(x * x, axis=-1, keepdims=True) + RMS_EPS)


def _rms_fwd(x, w, name):
    T, D = x.shape

    def body(ri, pi, ro, ao):
        xv = ri[0][...]
        ro[0][...] = (xv * _rms_rstd(xv) * pi[0][...]).astype(BF16)

    return _rowcall(name, body, T, 256, [(x, D, 0)], [w], [(D, BF16)], [])[0]


def _rms_bwd_core(dy, x, w):
    rs = _rms_rstd(x)
    xh = x * rs
    gw = dy * w
    dx = rs * (gw - xh * jnp.mean(gw * xh, axis=-1, keepdims=True))
    return dx, dy * xh


def _pre_bwd(dh, x, w, dres, name):
    T, D = x.shape

    def body(ri, pi, ro, ao):
        dx, dwc = _rms_bwd_core(ri[0][...], ri[1][...], pi[0][...])
        ro[0][...] = ri[2][...] + dx
        ao[0][...] += _rsum(dwc)

    return _rowcall(name, body, T, 256, [(dh, D, 0), (x, D, 0), (dres, D, 0)], [w], [(D, F32)], [D])


def _post_fwd(x, f, w, r, name):
    T, D = x.shape

    def body(ri, pi, ro, ao):
        fv = ri[1][...]
        ro[0][...] = ri[0][...] + r * (fv * _rms_rstd(fv) * pi[0][...])

    return _rowcall(name, body, T, 256, [(x, D, 0), (f, D, 0)], [w], [(D, F32)], [])[0]


def _post_bwd(dxn, f, w, r, name):
    T, D = f.shape

    def body(ri, pi, ro, ao):
        df, dwc = _rms_bwd_core(r * ri[0][...], ri[1][...], pi[0][...])
        ro[0][...] = df.astype(BF16)
        ao[0][...] += _rsum(dwc)

    return _rowcall(name, body, T, 256, [(dxn, D, 0), (f, D, 0)], [w], [(D, BF16)], [D])


def _swiglu_fwd(a, name):
    T, F2 = a.shape
    F = F2 // 2

    def body(ri, pi, ro, ao):
        ro[0][...] = (_silu(ri[0][...]) * ri[1][...]).astype(BF16)

    return _rowcall(name, body, T, 256, [(a, F, 0), (a, F, 1)], [], [(F, BF16)], [])[0]


def _swiglu_bwd(ds, a, name):
    T, F2 = a.shape
    F = F2 // 2

    def body(ri, pi, ro, ao):
        dsv, g, u = ri[0][...], ri[1][...], ri[2][...]
        ro[0][:, :F] = (dsv * u * _dsilu(g)).astype(BF16)
        ro[0][:, F:] = (dsv * _silu(g)).astype(BF16)

    return _rowcall(name, body, T, 256, [(ds, F, 0), (a, F, 0), (a, F, 1)], [], [(F2, BF16)], [])[0]


def _loss_fwd_bwd(y, tgt, name):
    T, D = y.shape

    def body(ri, pi, ro, ao):
        e = ri[0][...] - ri[1][...]
        ro[0][...] = e * (1.0 / D)
        tot = jnp.sum(_rsum(e * e), axis=1, keepdims=True) * (0.5 / D)
        ao[0][...] += jnp.broadcast_to(tot, (1, LANES))

    return _rowcall(name, body, T, 256, [(y, D, 0), (tgt, D, 0)], [], [(D, F32)], [LANES])


def _gdn_gate_fwd(o, p, nw, name):
    T, D = o.shape
    H = D // GDN_DK

    def body(ri, pi, ro, ao):
        for h in range(H):
            sl = slice(h * GDN_DK, (h + 1) * GDN_DK)
            oh = ri[0][:, sl]
            ro[0][:, sl] = (oh * _rms_rstd(oh) * pi[0][...] * _silu(ri[1][:, sl])).astype(BF16)

    return _rowcall(name, body, T, 256, [(o, D, 0), (p, D, 3)], [nw], [(D, BF16)], [])[0]


def _gdn_gate_bwd(dog, o, p, nw, name):
    T, D = o.shape
    H = D // GDN_DK

    def body(ri, pi, ro, ao):
        acc = jnp.zeros((1, GDN_DK), F32)
        for h in range(H):
            sl = slice(h * GDN_DK, (h + 1) * GDN_DK)
            dy, oh, z = ri[0][:, sl], ri[1][:, sl], ri[2][:, sl]
            sz = _silu(z)
            do, dwc = _rms_bwd_core(dy * sz, oh, pi[0][...])
            ro[0][:, sl] = do
            ro[1][:, sl] = (dy * oh * _rms_rstd(oh) * pi[0][...] * _dsilu(z)).astype(BF16)
            acc = acc + _rsum(dwc)
        ao[0][...] += acc

    return _rowcall(name, body, T, 256, [(dog, D, 0), (o, D, 0), (p, D, 3)], [nw], [(D, F32), (D, BF16)], [GDN_DK])


def _glu_fwd(p, b, name):
    T = p.shape[0]
    D = b.shape[1] // 2

    def body(ri, pi, ro, ao):
        ro[0][...] = (ri[0][...] + pi[0][:, :D]) * _sigmoid(ri[1][...] + pi[0][:, D:])

    return _rowcall(name, body, T, 256, [(p, D, 4), (p, D, 5)], [b], [(D, F32)], [])[0]


def _glu_bwd(dhc, p, b, name):
    T = p.shape[0]
    D = b.shape[1] // 2

    def body(ri, pi, ro, ao):
        d, a, g = ri[0][...], ri[1][...] + pi[0][:, :D], ri[2][...] + pi[0][:, D:]
        sg = _sigmoid(g)
        da, dg = d * sg, d * a * sg * (1.0 - sg)
        ro[0][:, :D] = da.astype(BF16)
        ro[0][:, D:] = dg.astype(BF16)
        ao[0][:, :D] += _rsum(da)
        ao[0][:, D:] += _rsum(dg)

    return _rowcall(name, body, T, 256, [(dhc, D, 0), (p, D, 4), (p, D, 5)], [b], [(2 * D, BF16)], [2 * D])


def _ln_stats(x):
    mu = jnp.mean(x, axis=-1, keepdims=True)
    xc = x - mu
    rstd = lax.rsqrt(jnp.mean(xc * xc, axis=-1, keepdims=True) + LN_EPS)
    return xc * rstd, rstd


def _ln_silu_fwd(hcv, g, b, name):
    T, D = hcv.shape

    def body(ri, pi, ro, ao):
        xh, _ = _ln_stats(ri[0][...])
        ro[0][...] = _silu(xh * pi[0][...] + pi[1][...]).astype(BF16)

    return _rowcall(name, body, T, 256, [(hcv, D, 0)], [g, b], [(D, BF16)], [])[0]


def _ln_silu_bwd(dhl, hcv, g, b, name):
    T, D = hcv.shape

    def body(ri, pi, ro, ao):
        xh, rstd = _ln_stats(ri[1][...])
        dyl = ri[0][...] * _dsilu(xh * pi[0][...] + pi[1][...])
        dxh = dyl * pi[0][...]
        dx = rstd * (dxh - jnp.mean(dxh, axis=-1, keepdims=True) - xh * jnp.mean(dxh * xh, axis=-1, keepdims=True))
        ro[0][...] = dx
        ao[0][...] += _rsum(dyl * xh)
        ao[1][...] += _rsum(dyl)
        ao[2][...] += _rsum(dx)

    return _rowcall(name, body, T, 256, [(dhl, D, 0), (hcv, D, 0)], [g, b], [(D, F32)], [D, D, D])


def _merge_fwd(p, ya, yb, bo, name):
    T, D = ya.shape

    def body(ri, pi, ro, ao):
        ro[0][...] = (_sigmoid(ri[0][...]) * ri[2][...] + _sigmoid(ri[1][...]) * (ri[3][...] + pi[0][...])).astype(BF16)

    return _rowcall(name, body, T, 256, [(p, D, 6), (p, D, 7), (ya, D, 0), (yb, D, 0)], [bo], [(D, BF16)], [])[0]


def _merge_bwd(dym, p, ya, yb, bo, name):
    T, D = ya.shape

    def body(ri, pi, ro, ao):
        d = ri[0][...]
        ga, gb = _sigmoid(ri[1][...]), _sigmoid(ri[2][...])
        ybv = ri[4][...] + pi[0][...]
        dyb = d * gb
        ro[0][...] = (d * ga).astype(BF16)
        ro[1][...] = dyb.astype(BF16)
        ro[2][:, :D] = (d * ri[3][...] * ga * (1.0 - ga)).astype(BF16)
        ro[2][:, D:] = (d * ybv * gb * (1.0 - gb)).astype(BF16)
        ao[0][...] += _rsum(dyb)

    return _rowcall(name, body, T, 256, [(dym, D, 0), (p, D, 6), (p, D, 7), (ya, D, 0), (yb, D, 0)], [bo],
                    [(D, BF16), (D, BF16), (2 * D, BF16)], [D])


PAD = 32
RC = 256


def _causal_taps(xp_ref, w, K, c0):
    acc = None
    for j in range(K):
        term = w[j:j + 1, :] * xp_ref[pl.ds(PAD - (K - 1) + j + c0, RC), :]
        acc = term if acc is None else acc + term
    return acc


def _anticausal_taps(dp_ref, w, K, c0):
    acc = None
    for j in range(K):
        term = w[j:j + 1, :] * dp_ref[pl.ds((K - 1) - j + c0, RC), :]
        acc = term if acc is None else acc + term
    return acc


def _tap_grads(dw_ref, dc_ref, xp_ref, K, T):
    for j in range(K):
        acc = jnp.zeros((1, LANES), F32)
        for c in range(T // RC):
            acc = acc + _rsum(dc_ref[pl.ds(c * RC, RC), :] * xp_ref[pl.ds(PAD - (K - 1) + j + c * RC, RC), :])
        dw_ref[j:j + 1, :] = acc


def _qkv_conv_fwd(p, cw, H, name):
    T = p.shape[0]
    K = cw.shape[0]

    def body(x_ref, w_ref, o_ref, xp_ref):
        j = pl.program_id(0)
        xp_ref[pl.ds(0, PAD), :] = jnp.zeros((PAD, LANES), F32)
        xp_ref[pl.ds(PAD, T), :] = x_ref[...]
        w = w_ref[...]
        scale = jnp.where(j < H, GDN_DK ** -0.5, 1.0).astype(F32)
        for c in range(T // RC):
            act = _silu(_causal_taps(xp_ref, w, K, c * RC))
            nrm = act * lax.rsqrt(jnp.sum(act * act, axis=-1, keepdims=True) + L2_EPS) * scale
            o_ref[pl.ds(c * RC, RC), :] = jnp.where(j < 2 * H, nrm, act)

    return pl.pallas_call(
        body, name=name, grid=(3 * H,),
        in_specs=[pl.BlockSpec((T, LANES), lambda j: (0, j)), pl.BlockSpec((K, LANES), lambda j: (0, j))],
        out_specs=pl.BlockSpec((T, LANES), lambda j: (0, j)),
        out_shape=jax.ShapeDtypeStruct((T, 3 * H * GDN_DK), F32),
        scratch_shapes=[pltpu.VMEM((T + PAD, LANES), F32)],
        compiler_params=_cparams(("arbitrary",)),
    )(p, cw)


def _qkv_conv_bwd(dn, p, cw, H, name):
    T = p.shape[0]
    K = cw.shape[0]

    def body(dn_ref, x_ref, w_ref, dx_ref, dw_ref, xp_ref, dc_ref):
        j = pl.program_id(0)
        xp_ref[pl.ds(0, PAD), :] = jnp.zeros((PAD, LANES), F32)
        xp_ref[pl.ds(PAD, T), :] = x_ref[...]
        dc_ref[pl.ds(T, PAD), :] = jnp.zeros((PAD, LANES), F32)
        w = w_ref[...]
        scale = jnp.where(j < H, GDN_DK ** -0.5, 1.0).astype(F32)
        for c in range(T // RC):
            pre = _causal_taps(xp_ref, w, K, c * RC)
            act = _silu(pre)
            d = dn_ref[pl.ds(c * RC, RC), :]
            rs = lax.rsqrt(jnp.sum(act * act, axis=-1, keepdims=True) + L2_EPS)
            nh = act * rs
            dact_n = scale * rs * (d - nh * jnp.sum(d * nh, axis=-1, keepdims=True))
            dact = jnp.where(j < 2 * H, dact_n, d)
            dc_ref[pl.ds(c * RC, RC), :] = dact * _dsilu(pre)
        for c in range(T // RC):
            dx_ref[pl.ds(c * RC, RC), :] = _anticausal_taps(dc_ref, w, K, c * RC).astype(BF16)
        _tap_grads(dw_ref, dc_ref, xp_ref, K, T)

    return pl.pallas_call(
        body, name=name, grid=(3 * H,),
        in_specs=[pl.BlockSpec((T, LANES), lambda j: (0, j)), pl.BlockSpec((T, LANES), lambda j: (0, j)),
                  pl.BlockSpec((K, LANES), lambda j: (0, j))],
        out_specs=[pl.BlockSpec((T, LANES), lambda j: (0, j)), pl.BlockSpec((K, LANES), lambda j: (0, j))],
        out_shape=[jax.ShapeDtypeStruct((T, 3 * H * GDN_DK), BF16), jax.ShapeDtypeStruct(cw.shape, F32)],
        scratch_shapes=[pltpu.VMEM((T + PAD, LANES), F32), pltpu.VMEM((T + PAD, LANES), F32)],
        compiler_params=_cparams(("arbitrary",)),
    )(dn, p, cw)


def _dw_conv_fwd(hc, w, b, name):
    T, D = hc.shape
    K = w.shape[0]

    def body(x_ref, w_ref, b_ref, o_ref, xp_ref):
        xp_ref[pl.ds(0, PAD), :] = jnp.zeros((PAD, LANES), F32)
        xp_ref[pl.ds(PAD, T), :] = x_ref[...]
        wv = w_ref[...]
        for c in range(T // RC):
            o_ref[pl.ds(c * RC, RC), :] = _causal_taps(xp_ref, wv, K, c * RC) + b_ref[...]

    return pl.pallas_call(
        body, name=name, grid=(D // LANES,),
        in_specs=[pl.BlockSpec((T, LANES), lambda j: (0, j)), pl.BlockSpec((K, LANES), lambda j: (0, j)),
                  pl.BlockSpec((1, LANES), lambda j: (0, j))],
        out_specs=pl.BlockSpec((T, LANES), lambda j: (0, j)),
        out_shape=jax.ShapeDtypeStruct((T, D), F32),
        scratch_shapes=[pltpu.VMEM((T + PAD, LANES), F32)],
        compiler_params=_cparams(("arbitrary",)),
    )(hc, w, b)


def _dw_conv_bwd(dy, hc, w, name, comm=None):
    T, D = hc.shape
    K = w.shape[0]

    def body(dy_ref, x_ref, w_ref, dx_ref, dw_ref, xp_ref, dc_ref):
        xp_ref[pl.ds(0, PAD), :] = jnp.zeros((PAD, LANES), F32)
        xp_ref[pl.ds(PAD, T), :] = x_ref[...]
        dc_ref[pl.ds(T, PAD), :] = jnp.zeros((PAD, LANES), F32)
        dc_ref[pl.ds(0, T), :] = dy_ref[...]
        wv = w_ref[...]
        for c in range(T // RC):
            dx_ref[pl.ds(c * RC, RC), :] = _anticausal_taps(dc_ref, wv, K, c * RC)
        _tap_grads(dw_ref, dc_ref, xp_ref, K, T)

    return _comm_call(
        body, comm, name=name, grid=(D // LANES,),
        in_specs=[pl.BlockSpec((T, LANES), lambda j: (0, j)), pl.BlockSpec((T, LANES), lambda j: (0, j)),
                  pl.BlockSpec((K, LANES), lambda j: (0, j))],
        out_specs=[pl.BlockSpec((T, LANES), lambda j: (0, j)), pl.BlockSpec((K, LANES), lambda j: (0, j))],
        out_shape=[jax.ShapeDtypeStruct((T, D), F32), jax.ShapeDtypeStruct(w.shape, F32)],
        scratch_shapes=[pltpu.VMEM((T + PAD, LANES), F32), pltpu.VMEM((T + PAD, LANES), F32)],
        args=(dy, hc, w))


NN = (((1,), (0,)), ((), ()))
NT = (((1,), (1,)), ((), ()))
TN = (((0,), (0,)), ((), ()))


def _dotb(a, b, dn=NN):
    return lax.dot_general(a.astype(BF16), b.astype(BF16), dn, preferred_element_type=F32)


def _split_bf16(x, n):
    parts, r = [], x
    for _ in range(n):
        p = r.astype(BF16)
        parts.append(p)
        r = r - p.astype(F32)
    return parts


def _dot_sel(sel, x, pieces, sel_left=True):
    sb = sel.astype(BF16)
    acc = None
    for p in _split_bf16(x, pieces):
        t = (lax.dot_general(sb, p, NN, preferred_element_type=F32) if sel_left
             else lax.dot_general(p, sb, NN, preferred_element_type=F32))
        acc = t if acc is None else acc + t
    return acc


def _iota2(shape, axis):
    return lax.broadcasted_iota(jnp.int32, shape, axis)


def _to_row(col, eye):
    return jnp.sum(jnp.where(eye, col, 0.0), axis=0, keepdims=True)


def _to_col(row, eye):
    return jnp.sum(jnp.where(eye, row, 0.0), axis=1, keepdims=True)


def _gdn_gates(bl, al, alog, dtb):
    beta = _sigmoid(bl)
    x = al + dtb
    sp = jnp.maximum(x, 0.0) + jnp.log(1.0 + jnp.exp(-jnp.abs(x)))
    g = -jnp.exp(alog) * sp
    r, c = _iota2((CHUNK, CHUNK), 0), _iota2((CHUNK, CHUNK), 1)
    G = _dot_sel(r >= c, g, 3)
    return beta, g, G, x


def _unit_lower_inverses(As, Ats):
    n = len(As)
    nb = CHUNK // SUB
    lane = _iota2((SUB, CHUNK), 1)
    row = _iota2((SUB, CHUNK), 0)
    Atp = []
    for At in Ats:
        acc = jnp.zeros((SUB, CHUNK), F32)
        for b in range(nb):
            acc = jnp.where(lane // SUB == b, At[b * SUB:(b + 1) * SUB, :], acc)
        Atp.append(acc)
    gr, gc = _iota2((CHUNK, CHUNK), 0), _iota2((CHUNK, CHUNK), 1)
    ones_bd = gr // SUB == gc // SUB
    stack = jnp.concatenate(
        [jnp.where(lane % SUB == i, Atp[m], 0.0) for i in range(1, SUB) for m in range(n)], axis=0)
    Cm = _dot_sel(ones_bd, stack, 2, sel_left=False)
    Z = [(row == lane % SUB).astype(F32) for _ in range(n)]
    for i in range(1, SUB):
        for m in range(n):
            cm = Cm[((i - 1) * n + m) * SUB:((i - 1) * n + m + 1) * SUB, :]
            new = -jnp.sum(cm * Z[m], axis=0, keepdims=True)
            Z[m] = Z[m] + jnp.where(row == i, new, 0.0)
    bd = gr // SUB == gc // SUB
    Xs = [jnp.where(bd, jnp.concatenate([Z[m]] * nb, axis=0), 0.0) for m in range(n)]
    blk = SUB
    while blk < CHUNK:
        off = (gr // (2 * blk) == gc // (2 * blk)) & (gr // blk != gc // blk)
        Ys = [_dotb(Xs[m], jnp.where(off, As[m], 0.0)) for m in range(n)]
        Xs = [Xs[m] - _dotb(Ys[m], Xs[m]) for m in range(n)]
        blk *= 2
    return Xs


def _gdn_fwd(qkvn, p, alog, dtb, H, name, comm=None):
    T = qkvn.shape[0]
    D = H * GDN_DK
    N = T // CHUNK
    bblk = 8 * D // LANES

    def body(q_ref, k_ref, v_ref, b_ref, a_ref, alog_ref, dtb_ref, o_ref, t_ref, s_ref, S_scr):
        @pl.when(pl.program_id(0) == 0)
        def _():
            S_scr[...] = jnp.zeros_like(S_scr)

        beta, _, G, _ = _gdn_gates(b_ref[...], a_ref[...], alog_ref[...], dtb_ref[...])
        hs = range(H)
        sl = [slice(h * GDN_DK, (h + 1) * GDN_DK) for h in hs]
        q, k, v = [q_ref[:, s] for s in sl], [k_ref[:, s] for s in sl], [v_ref[:, s] for s in sl]
        Gc, bc = [G[:, h:h + 1] for h in hs], [beta[:, h:h + 1] for h in hs]
        r, c = _iota2((CHUNK, CHUNK), 0), _iota2((CHUNK, CHUNK), 1)
        eye, low, up = r == c, r >= c, r <= c
        Gr, br = [_to_row(Gc[h], eye) for h in hs], [_to_row(bc[h], eye) for h in hs]
        Dm = [jnp.where(low, jnp.exp(jnp.where(low, Gc[h] - Gr[h], 0.0)), 0.0) for h in hs]
        Dt = [jnp.where(up, jnp.exp(jnp.where(up, Gr[h] - Gc[h], 0.0)), 0.0) for h in hs]
        qk = [_dotb(jnp.concatenate([q[h], k[h]], axis=0), k[h], NT) for h in hs]
        QK = [qk[h][:CHUNK] * Dm[h] for h in hs]
        KK = [qk[h][CHUNK:] for h in hs]
        As = [jnp.where(r > c, KK[h] * Dm[h], 0.0) * bc[h] for h in hs]
        Ats = [jnp.where(r < c, KK[h] * Dt[h], 0.0) * br[h] for h in hs]
        Ts = _unit_lower_inverses(As, Ats)
        eG = [jnp.exp(Gc[h]) for h in hs]
        Gl = [Gc[h][CHUNK - 1:CHUNK, :] for h in hs]
        uw = [_dotb(Ts[h], jnp.concatenate([v[h] * bc[h], k[h] * (bc[h] * eG[h])], axis=1)) for h in hs]
        S = [S_scr[h] for h in hs]
        qw = [_dotb(jnp.concatenate([q[h] * eG[h], uw[h][:, GDN_DK:]], axis=0), S[h]) for h in hs]
        vn = [uw[h][:, :GDN_DK] - qw[h][CHUNK:] for h in hs]
        o = [qw[h][:CHUNK] + _dotb(QK[h], vn[h]) for h in hs]
        Sn = [S[h] * jnp.exp(Gl[h]) + _dotb(k[h] * jnp.exp(Gl[h] - Gc[h]), vn[h], TN) for h in hs]
        for h in hs:
            t_ref[0, h] = Ts[h]
            s_ref[0, h] = S[h]
            o_ref[:, sl[h]] = o[h]
            S_scr[h] = Sn[h]

    qkv_spec = [pl.BlockSpec((CHUNK, D), lambda n, cb=cb: (n, cb)) for cb in range(3)]
    return _comm_call(
        body, comm, name=name, grid=(N,),
        in_specs=qkv_spec + [pl.BlockSpec((CHUNK, LANES), lambda n: (n, bblk)),
                             pl.BlockSpec((CHUNK, LANES), lambda n: (n, bblk + 1)),
                             pl.BlockSpec((1, LANES), lambda n: (0, 0)), pl.BlockSpec((1, LANES), lambda n: (0, 0))],
        out_specs=[pl.BlockSpec((CHUNK, D), lambda n: (n, 0)),
                   pl.BlockSpec((1, H, CHUNK, CHUNK), lambda n: (n, 0, 0, 0)),
                   pl.BlockSpec((1, H, GDN_DK, GDN_DK), lambda n: (n, 0, 0, 0))],
        out_shape=[jax.ShapeDtypeStruct((T, D), F32), jax.ShapeDtypeStruct((N, H, CHUNK, CHUNK), F32),
                   jax.ShapeDtypeStruct((N, H, GDN_DK, GDN_DK), F32)],
        scratch_shapes=[pltpu.VMEM((H, GDN_DK, GDN_DK), F32)],
        args=(qkvn, qkvn, qkvn, p, p, alog, dtb))


def _gdn_bwd(do, qkvn, p, alog, dtb, Tinv, Sin, H, name, comm=None):
    T = qkvn.shape[0]
    D = H * GDN_DK
    N = T // CHUNK
    bblk = 8 * D // LANES

    def body(do_ref, q_ref, k_ref, v_ref, b_ref, a_ref, alog_ref, dtb_ref, t_ref, s_ref,
             dqkv_ref, dba_ref, dalog_ref, ddtb_ref, dS_scr):
        @pl.when(pl.program_id(0) == 0)
        def _():
            dS_scr[...] = jnp.zeros_like(dS_scr)
            dalog_ref[...] = jnp.zeros_like(dalog_ref)
            ddtb_ref[...] = jnp.zeros_like(ddtb_ref)

        beta, g, G, x = _gdn_gates(b_ref[...], a_ref[...], alog_ref[...], dtb_ref[...])
        r, c = _iota2((CHUNK, CHUNK), 0), _iota2((CHUNK, CHUNK), 1)
        eye, low, strict = r == c, r >= c, r > c
        lane = _iota2((CHUNK, LANES), 1)
        rsum1 = lambda a: jnp.sum(a, axis=1, keepdims=True)
        hs = range(H)
        sl = [slice(h * GDN_DK, (h + 1) * GDN_DK) for h in hs]
        q, k, v = [q_ref[:, s] for s in sl], [k_ref[:, s] for s in sl], [v_ref[:, s] for s in sl]
        dov = [do_ref[:, s] for s in sl]
        Gc, bc = [G[:, h:h + 1] for h in hs], [beta[:, h:h + 1] for h in hs]
        Tm, S, dSo = [t_ref[0, h] for h in hs], [s_ref[0, h] for h in hs], [dS_scr[h] for h in hs]
        Gr = [_to_row(Gc[h], eye) for h in hs]
        Dm = [jnp.where(low, jnp.exp(jnp.where(low, Gc[h] - Gr[h], 0.0)), 0.0) for h in hs]
        eG = [jnp.exp(Gc[h]) for h in hs]
        Gl = [Gc[h][CHUNK - 1:CHUNK, :] for h in hs]
        eR, dch = [jnp.exp(Gl[h] - Gc[h]) for h in hs], [jnp.exp(Gl[h]) for h in hs]
        qk = [_dotb(jnp.concatenate([q[h], k[h]], axis=0), k[h], NT) for h in hs]
        QKr, KK = [qk[h][:CHUNK] for h in hs], [qk[h][CHUNK:] for h in hs]
        QK = [QKr[h] * Dm[h] for h in hs]
        M = [jnp.where(strict, KK[h] * Dm[h], 0.0) for h in hs]
        uw = [_dotb(Tm[h], jnp.concatenate([v[h] * bc[h], k[h] * (bc[h] * eG[h])], axis=1)) for h in hs]
        u, w = [uw[h][:, :GDN_DK] for h in hs], [uw[h][:, GDN_DK:] for h in hs]
        qd, kd = [q[h] * eG[h] for h in hs], [k[h] * eR[h] for h in hs]
        vn = [u[h] - _dotb(w[h], S[h]) for h in hs]
        dvn = [_dotb(QK[h], dov[h], TN) + _dotb(kd[h], dSo[h]) for h in hs]
        dQK = [jnp.where(low, _dotb(dov[h], vn[h], NT), 0.0) for h in hs]
        dkd = [_dotb(vn[h], dSo[h], NT) for h in hs]
        ddch = [jnp.sum(rsum1(dSo[h] * S[h]), axis=0, keepdims=True) for h in hs]
        dd = [jnp.concatenate([dov[h], dvn[h]], axis=0) for h in hs]
        xs = [_dotb(dd[h], S[h], NT) for h in hs]
        dqd, dw = [xs[h][:CHUNK] for h in hs], [-xs[h][CHUNK:] for h in hs]
        dS = [_dotb(jnp.concatenate([qd[h], -w[h]], axis=0), dd[h], TN) + dch[h] * dSo[h] for h in hs]
        yb = [_dotb(Tm[h], jnp.concatenate([dvn[h], dw[h]], axis=1), TN) for h in hs]
        dvb, dkb = [yb[h][:, :GDN_DK] for h in hs], [yb[h][:, GDN_DK:] for h in hs]
        dA = [-jnp.where(strict, _dotb(yb[h], uw[h], NT), 0.0) for h in hs]
        rk = [rsum1(dkb[h] * k[h]) for h in hs]
        dbeta = [rsum1(dvb[h] * v[h]) + rk[h] * eG[h] + rsum1(dA[h] * M[h]) for h in hs]
        dM = [dA[h] * bc[h] for h in hs]
        dKK = [dM[h] * Dm[h] for h in hs]
        dQKr = [dQK[h] * Dm[h] for h in hs]
        E = [dM[h] * M[h] + dQK[h] * QK[h] for h in hs]
        zk = [_dotb(jnp.concatenate([dQKr[h], dKK[h]], axis=0), k[h]) for h in hs]
        dq = [zk[h][:CHUNK] + dqd[h] * eG[h] for h in hs]
        dk = [dkb[h] * (bc[h] * eG[h]) + zk[h][CHUNK:] + _dotb(dKK[h], k[h], TN) + _dotb(dQKr[h], q[h], TN)
              + dkd[h] * eR[h] for h in hs]
        deG = [rk[h] * bc[h] + rsum1(dqd[h] * q[h]) for h in hs]
        deR = [rsum1(dkd[h] * k[h]) for h in hs]
        dGl = [jnp.sum(deR[h] * eR[h], axis=0, keepdims=True) + ddch[h] * dch[h] for h in hs]
        dGc = [rsum1(E[h]) - _to_col(jnp.sum(E[h], axis=0, keepdims=True), eye) + deG[h] * eG[h] - deR[h] * eR[h]
               + jnp.where(r[:, :1] == CHUNK - 1, dGl[h], 0.0) for h in hs]
        dG_all = jnp.zeros((CHUNK, LANES), F32)
        dbeta_all = jnp.zeros((CHUNK, LANES), F32)
        for h in hs:
            dS_scr[h] = dS[h]
            dqkv_ref[:, sl[h]] = dq[h]
            dqkv_ref[:, D + h * GDN_DK:D + (h + 1) * GDN_DK] = dk[h]
            dqkv_ref[:, 2 * D + h * GDN_DK:2 * D + (h + 1) * GDN_DK] = dvb[h] * bc[h]
            dG_all = jnp.where(lane == h, dGc[h], dG_all)
            dbeta_all = jnp.where(lane == h, dbeta[h], dbeta_all)
        dg = _dot_sel(r <= c, dG_all, 3)
        da = dg * (-jnp.exp(alog_ref[...])) * _sigmoid(x)
        dba_ref[:, :LANES] = (dbeta_all * beta * (1.0 - beta)).astype(BF16)
        dba_ref[:, LANES:] = da.astype(BF16)
        dalog_ref[...] += _rsum(dg * g)
        ddtb_ref[...] += _rsum(da)

    rev = lambda n: N - 1 - n
    qkv_spec = [pl.BlockSpec((CHUNK, D), lambda n, cb=cb: (rev(n), cb)) for cb in range(3)]
    return _comm_call(
        body, comm, name=name, grid=(N,),
        in_specs=[pl.BlockSpec((CHUNK, D), lambda n: (rev(n), 0))] + qkv_spec + [
            pl.BlockSpec((CHUNK, LANES), lambda n: (rev(n), bblk)),
            pl.BlockSpec((CHUNK, LANES), lambda n: (rev(n), bblk + 1)),
            pl.BlockSpec((1, LANES), lambda n: (0, 0)), pl.BlockSpec((1, LANES), lambda n: (0, 0)),
            pl.BlockSpec((1, H, CHUNK, CHUNK), lambda n: (rev(n), 0, 0, 0)),
            pl.BlockSpec((1, H, GDN_DK, GDN_DK), lambda n: (rev(n), 0, 0, 0))],
        out_specs=[pl.BlockSpec((CHUNK, 3 * D), lambda n: (rev(n), 0)),
                   pl.BlockSpec((CHUNK, 2 * LANES), lambda n: (rev(n), 0)),
                   pl.BlockSpec((1, LANES), lambda n: (0, 0)), pl.BlockSpec((1, LANES), lambda n: (0, 0))],
        out_shape=[jax.ShapeDtypeStruct((T, 3 * D), F32), jax.ShapeDtypeStruct((T, 2 * LANES), BF16),
                   jax.ShapeDtypeStruct((1, LANES), F32), jax.ShapeDtypeStruct((1, LANES), F32)],
        scratch_shapes=[pltpu.VMEM((H, GDN_DK, GDN_DK), F32)],
        args=(do, qkvn, qkvn, qkvn, p, p, alog, dtb, Tinv, Sin))


def _mix_in_reorder(wt, D, H):
    o1 = 4 * D
    o2, o3 = o1 + H, o1 + 2 * H
    z = jnp.zeros((LANES - H, wt.shape[1]), wt.dtype)
    return jnp.concatenate([wt[:o1], wt[o3:], wt[o1:o2], z, wt[o2:o3], z], axis=0)


def _mix_in_restore(dwt, D, H):
    b0 = 8 * D
    return jnp.concatenate([dwt[:4 * D], dwt[b0:b0 + H], dwt[b0 + LANES:b0 + LANES + H], dwt[4 * D:b0]], axis=0)


def _ffn_fwd(x, W, pre, tag):
    h = _rms_fwd(x, W[pre + "_norm_pre"], tag + "_pre")
    a = _matmul(h, W[pre + "_w_in"], "nt", F32, tag + "_in")
    s = _swiglu_fwd(a, tag + "_act")
    f = _matmul(s, W[pre + "_w_out"], "nn", F32, tag + "_out")
    return _post_fwd(x, f, W[pre + "_norm_post"], 0.5, tag + "_post"), (x, h, a, s, f)


def _ffn_bwd(dxn, saved, W, pre, tag):
    x, h, a, s, f = saved
    df, dpost = _post_bwd(dxn, f, W[pre + "_norm_post"], 0.5, tag + "_dpost")
    ds = _matmul(df, W[pre + "_w_out"], "nt", F32, tag + "_ds")
    dw_out = _matmul(s, df, "tn", F32, tag + "_dwout")
    da = _swiglu_bwd(ds, a, tag + "_dact")
    dh = _matmul(da, W[pre + "_w_in"], "nn", F32, tag + "_dh")
    dw_in = _matmul(da, h, "tn", F32, tag + "_dwin")
    dx, dpre = _pre_bwd(dh, x, W[pre + "_norm_pre"], dxn, tag + "_dpre")
    return dx, {pre + "_norm_pre": dpre, pre + "_norm_post": dpost, pre + "_w_in": dw_in, pre + "_w_out": dw_out}


def _mix_fwd(x, W, H, tag, gather=None):
    h = _rms_fwd(x, W["mix_norm_pre"], tag + "_pre")
    p = _matmul(h, W["mix_w_in"], "nt", F32, tag + "_in")
    qkvn = _qkv_conv_fwd(p, W["gdn_conv_w"], H, tag + "_qkvconv")
    (o, Tinv, Sin), gathered = _gdn_fwd(qkvn, p, W["gdn_a_log"], W["gdn_dt_bias"], H, tag + "_gdn",
                                        comm=None if gather is None else (_GatherPlan(gather), gather))
    og = _gdn_gate_fwd(o, p, W["gdn_norm_w"], tag + "_gdngate")
    ya = _matmul(og, W["gdn_w_o"], "nn", F32, tag + "_gdno")
    hc = _glu_fwd(p, W["cnv_pw1_b"], tag + "_glu")
    hcv = _dw_conv_fwd(hc, W["cnv_dw_w"], W["cnv_dw_b"], tag + "_dwconv")
    hl = _ln_silu_fwd(hcv, W["cnv_ln_g"], W["cnv_ln_b"], tag + "_ln")
    yb = _matmul(hl, W["cnv_w_o"], "nn", F32, tag + "_cnvo")
    ym = _merge_fwd(p, ya, yb, W["cnv_b_o"], tag + "_merge")
    y = _matmul(ym, W["mix_w_out"], "nn", F32, tag + "_out")
    xn = _post_fwd(x, y, W["mix_norm_post"], 1.0, tag + "_post")
    return xn, (x, h, p, qkvn, o, Tinv, Sin, og, ya, hc, hcv, hl, yb, ym, y), gathered


def _pair_adds(Gs, R1s, cidx, tag):
    return [_pair_add(G, R1, cidx, f"{tag}_pair_add{k}") for k, (G, R1) in enumerate(zip(Gs, R1s))]


def _mix_bwd(dxn, saved, W, H, tag, reduce=None):
    x, h, p, qkvn, o, Tinv, Sin, og, ya, hc, hcv, hl, yb, ym, y = saved
    g = {}
    dy, g["mix_norm_post"] = _post_bwd(dxn, y, W["mix_norm_post"], 1.0, tag + "_dpost")
    dym = _matmul(dy, W["mix_w_out"], "nt", F32, tag + "_dym")
    g["mix_w_out"] = _matmul(ym, dy, "tn", F32, tag + "_dwout")
    dya, dyb, dgates, g["cnv_b_o"] = _merge_bwd(dym, p, ya, yb, W["cnv_b_o"], tag + "_dmerge")
    dhl = _matmul(dyb, W["cnv_w_o"], "nt", F32, tag + "_dhl")
    g["cnv_w_o"] = _matmul(hl, dyb, "tn", F32, tag + "_dwcnvo")
    dhcv, g["cnv_ln_g"], g["cnv_ln_b"], g["cnv_dw_b"] = _ln_silu_bwd(dhl, hcv, W["cnv_ln_g"], W["cnv_ln_b"], tag + "_dln")
    chips = None
    if reduce is None:
        (dhc, g["cnv_dw_w"]), _ = _dw_conv_bwd(dhcv, hc, W["cnv_dw_w"], tag + "_ddwconv")
    else:
        Gs, cidx, rtag = reduce
        (dhc, g["cnv_dw_w"]), R1s = _dw_conv_bwd(dhcv, hc, W["cnv_dw_w"], tag + "_ddwconv", comm=(_SiblingPlan(Gs), Gs))
        Ps = _pair_adds(Gs, R1s, cidx, rtag)
        chips = (_ChipsPlan(Ps), Ps)
    dglu, g["cnv_pw1_b"] = _glu_bwd(dhc, p, W["cnv_pw1_b"], tag + "_dglu")
    dog = _matmul(dya, W["gdn_w_o"], "nt", F32, tag + "_dog")
    g["gdn_w_o"] = _matmul(og, dya, "tn", F32, tag + "_dwgdno")
    do, dz, g["gdn_norm_w"] = _gdn_gate_bwd(dog, o, p, W["gdn_norm_w"], tag + "_dgdngate")
    (dqkvn, dba, g["gdn_a_log"], g["gdn_dt_bias"]), reduced = _gdn_bwd(
        do, qkvn, p, W["gdn_a_log"], W["gdn_dt_bias"], Tinv, Sin, H, tag + "_dgdn", comm=chips)
    dqkv, g["gdn_conv_w"] = _qkv_conv_bwd(dqkvn, p, W["gdn_conv_w"], H, tag + "_dqkvconv")
    dp = jnp.concatenate([dqkv, dz, dglu, dgates, dba], axis=1)
    dh = _matmul(dp, W["mix_w_in"], "nn", F32, tag + "_dh")
    g["mix_w_in"] = _matmul(dp, h, "tn", F32, tag + "_dwin")
    dx, g["mix_norm_pre"] = _pre_bwd(dh, x, W["mix_norm_pre"], dxn, tag + "_dpre")
    return dx, g, reduced


def _trunk_fwd_bwd(x, tgt, H, L, weights_of, blocks_of=None, to_reduce=None):
    saved, Ws = [], []
    W = weights_of(0, None)
    for i in range(L):
        Ws.append(W)
        x, s1 = _ffn_fwd(x, W, "ffn1", f"l{i}_ffn1")
        nxt = blocks_of(i + 1) if (blocks_of is not None and i + 1 < L) else None
        x, s2, gathered = _mix_fwd(x, W, H, f"l{i}_mix", gather=nxt)
        x, s3 = _ffn_fwd(x, W, "ffn2", f"l{i}_ffn2")
        saved.append((s1, s2, s3))
        if i + 1 < L:
            W = weights_of(i + 1, gathered)
    dx, loss = _loss_fwd_bwd(x, tgt, "loss")
    grads, reduced, pending = [None] * L, [None] * L, None
    for i in reversed(range(L)):
        s1, s2, s3 = saved[i]
        dx, g3 = _ffn_bwd(dx, s3, Ws[i], "ffn2", f"l{i}_ffn2")
        dx, g2, red = _mix_bwd(dx, s2, Ws[i], H, f"l{i}_mix", reduce=pending)
        if pending is not None:
            reduced[i + 1] = red
        dx, g1 = _ffn_bwd(dx, s1, Ws[i], "ffn1", f"l{i}_ffn1")
        grads[i] = {**g1, **g2, **g3}
        pending = to_reduce(i, grads[i]) if (to_reduce is not None and i > 0) else None
    return loss, dx, grads, reduced


HBM_SPEC = pl.BlockSpec(memory_space=pltpu.HBM)


def _coords():
    return lax.axis_index("x"), lax.axis_index("y"), lax.axis_index("c")


class _GatherPlan:
    has_middle = True

    def __init__(self, shards):
        self.n = len(shards)
        self.out_shape = [jax.ShapeDtypeStruct((N_DEV,) + s.shape, s.dtype) for s in shards]
        self.sems = [pltpu.SemaphoreType.DMA((self.n, 7)), pltpu.SemaphoreType.DMA((self.n, 7)),
                     pltpu.SemaphoreType.DMA((self.n,))]

    def _parts(self, ins, outs, sems):
        send_sems, recv_sems, local_sems = sems
        x, y, c = _coords()
        me, sibling = (x, y, c), (x, y, 1 - c)
        chips = [(1 - x, y), (x, 1 - y), (1 - x, 1 - y)]

        def copy(w, k, block, to, src=None):
            dst = outs[w].at[4 * block[0] + 2 * block[1] + block[2]]
            return pltpu.make_async_remote_copy(
                src_ref=dst if src is None else src, dst_ref=dst, send_sem=send_sems.at[w, k],
                recv_sem=recv_sems.at[w, k], device_id=to, device_id_type=MESH)

        mine = [pltpu.make_async_copy(ins[w], outs[w].at[4 * x + 2 * y + c], local_sems.at[w]) for w in range(self.n)]
        first = []
        for w in range(self.n):
            first.append(copy(w, 0, me, sibling, src=ins[w]))
            first += [copy(w, 1 + j, me, (*chip, c), src=ins[w]) for j, chip in enumerate(chips)]
        passed = [copy(w, 4 + j, (*chip, c), sibling) for j, chip in enumerate(chips) for w in range(self.n)]
        return copy, mine, first, passed, chips, me, sibling, c

    def begin(self, ins, outs, sems):
        _, mine, first, _, _, _, _, _ = self._parts(ins, outs, sems)
        for cp in mine + first:
            cp.start()

    def middle(self, ins, outs, sems):
        copy, _, _, passed, chips, me, _, c = self._parts(ins, outs, sems)
        for j, chip in enumerate(chips):
            for w in range(self.n):
                copy(w, 1 + j, (*chip, c), me).wait_recv()
                passed[j * self.n + w].start()

    def finish(self, ins, outs, sems):
        copy, mine, first, passed, chips, me, sibling, c = self._parts(ins, outs, sems)
        for w in range(self.n):
            copy(w, 0, sibling, me).wait_recv()
            for j, chip in enumerate(chips):
                copy(w, 4 + j, (*chip, 1 - c), me).wait_recv()
        for cp in first + passed:
            cp.wait_send()
        for cp in mine:
            cp.wait()


class _SiblingPlan:
    has_middle = False

    def __init__(self, Gs):
        self.n = len(Gs)
        self.out_shape = [jax.ShapeDtypeStruct((4,) + g.shape[1:], g.dtype) for g in Gs]
        self.sems = [pltpu.SemaphoreType.DMA((self.n, 4)), pltpu.SemaphoreType.DMA((self.n, 4))]

    def _copies(self, ins, outs, sems):
        send_sems, recv_sems = sems
        x, y, c = _coords()
        return [pltpu.make_async_remote_copy(
            src_ref=ins[w].at[2 * q + (1 - c)], dst_ref=outs[w].at[q], send_sem=send_sems.at[w, q],
            recv_sem=recv_sems.at[w, q], device_id=(x, y, 1 - c), device_id_type=MESH)
            for w in range(self.n) for q in range(4)]

    def begin(self, ins, outs, sems):
        for cp in self._copies(ins, outs, sems):
            cp.start()

    def finish(self, ins, outs, sems):
        for cp in self._copies(ins, outs, sems):
            cp.wait()


class _ChipsPlan:
    has_middle = False

    def __init__(self, Ps):
        self.n = len(Ps)
        self.out_shape = [jax.ShapeDtypeStruct(p.shape, p.dtype) for p in Ps]
        self.sems = [pltpu.SemaphoreType.DMA((self.n, 3)), pltpu.SemaphoreType.DMA((self.n, 3)),
                     pltpu.SemaphoreType.DMA((self.n,))]

    def _copies(self, ins, outs, sems):
        send_sems, recv_sems, local_sems = sems
        x, y, c = _coords()
        me_q = 2 * x + y
        cps = []
        for w in range(self.n):
            cps.append(pltpu.make_async_copy(ins[w].at[me_q], outs[w].at[me_q], local_sems.at[w]))
            for j, (px, py) in enumerate([(1 - x, y), (x, 1 - y), (1 - x, 1 - y)]):
                cps.append(pltpu.make_async_remote_copy(
                    src_ref=ins[w].at[2 * px + py], dst_ref=outs[w].at[me_q], send_sem=send_sems.at[w, j],
                    recv_sem=recv_sems.at[w, j], device_id=(px, py, c), device_id_type=MESH))
        return cps

    def begin(self, ins, outs, sems):
        for cp in self._copies(ins, outs, sems):
            cp.start()

    def finish(self, ins, outs, sems):
        for cp in self._copies(ins, outs, sems):
            cp.wait()


def _comm_only(plan, arrays, name):
    n = plan.n

    def body(*refs):
        ins, outs, sems = refs[:n], refs[n:2 * n], refs[2 * n:]
        plan.begin(ins, outs, sems)
        if plan.has_middle:
            plan.middle(ins, outs, sems)
        plan.finish(ins, outs, sems)

    return pl.pallas_call(
        body, name=name, out_shape=plan.out_shape, in_specs=[HBM_SPEC] * n, out_specs=[HBM_SPEC] * n,
        scratch_shapes=plan.sems,
    )(*arrays)


def _comm_call(body, comm, *, name, grid, in_specs, out_specs, out_shape, scratch_shapes, args):
    if comm is None:
        res = pl.pallas_call(body, name=name, grid=grid, in_specs=in_specs, out_specs=out_specs, out_shape=out_shape,
                             scratch_shapes=scratch_shapes, compiler_params=_cparams(("arbitrary",)))(*args)
        return res, None
    plan, arrays = comm
    n_in, n_out, n_scr, n = len(in_specs), len(out_specs), len(scratch_shapes), plan.n
    steps = grid[0]

    def kern(*refs):
        ins, cins = refs[:n_in], refs[n_in:n_in + n]
        outs, couts = refs[n_in + n:n_in + n + n_out], refs[n_in + n + n_out:n_in + 2 * n + n_out]
        scr, csems = refs[n_in + 2 * n + n_out:n_in + 2 * n + n_out + n_scr], refs[n_in + 2 * n + n_out + n_scr:]
        step = pl.program_id(0)

        @pl.when(step == 0)
        def _():
            plan.begin(cins, couts, csems)

        body(*ins, *outs, *scr)
        if plan.has_middle:
            @pl.when(step == (3 * steps) // 4)
            def _():
                plan.middle(cins, couts, csems)

        @pl.when(step == steps - 1)
        def _():
            plan.finish(cins, couts, csems)

    res = pl.pallas_call(
        kern, name=name, grid=grid, in_specs=list(in_specs) + [HBM_SPEC] * n,
        out_specs=list(out_specs) + [HBM_SPEC] * n, out_shape=list(out_shape) + plan.out_shape,
        scratch_shapes=list(scratch_shapes) + plan.sems, compiler_params=_cparams(("arbitrary",)),
    )(*args, *arrays)
    return res[:n_out], res[n_out:]


def _row_tile(R, target=256):
    best = None
    for t in range(8, min(R, target) + 1, 8):
        if R % t == 0:
            best = t
    return best if best is not None else R


def _pair_add(G, R1, cidx, name):
    _, R, C = G.shape
    tb = _row_tile(R)

    def body(c_ref, g_ref, r_ref, o_ref):
        o_ref[...] = (g_ref[...] + r_ref[...]).astype(BF16)

    return pl.pallas_call(
        body, name=name,
        grid_spec=pltpu.PrefetchScalarGridSpec(
            num_scalar_prefetch=1, grid=(4, R // tb),
            in_specs=[pl.BlockSpec((None, tb, C), lambda q, i, cr: (2 * q + cr[0], i, 0)),
                      pl.BlockSpec((None, tb, C), lambda q, i, cr: (q, i, 0))],
            out_specs=pl.BlockSpec((None, tb, C), lambda q, i, cr: (q, i, 0))),
        out_shape=jax.ShapeDtypeStruct((4, R, C), BF16),
        compiler_params=_cparams(("arbitrary", "arbitrary")),
    )(cidx, G, R1)


def _sum_parts(parts, name):
    P, R, C = parts.shape

    def body(p_ref, o_ref):
        acc = p_ref[0]
        for j in range(1, P):
            acc = acc + p_ref[j]
        o_ref[...] = acc

    return pl.pallas_call(
        body, name=name, out_shape=jax.ShapeDtypeStruct((R, C), F32),
        in_specs=[pl.BlockSpec(memory_space=pltpu.VMEM)], out_specs=pl.BlockSpec(memory_space=pltpu.VMEM),
        compiler_params=_cparams(),
    )(parts)


def _adamw(w, m, v, parts, name):
    G, R, C = w.shape
    P = parts[0].shape[0]
    tb = _row_tile(R)
    nb = R // tb
    c1 = 1.0 / (1.0 - ADAM_B1 ** ADAM_STEP)
    c2 = 1.0 / (1.0 - ADAM_B2 ** ADAM_STEP)

    def body(w_ref, m_ref, v_ref, *rest):
        p_refs, (g_ref, d_ref, nm_ref, nv_ref) = rest[:G], rest[G:]
        l = pl.program_id(0)
        g = None
        for k in range(G):
            gk = p_refs[k][0].astype(F32)
            for j in range(1, P):
                gk = gk + p_refs[k][j].astype(F32)
            g = gk if g is None else jnp.where(l == k, gk, g)
        nm = ADAM_B1 * m_ref[...] + (1.0 - ADAM_B1) * g
        nv = ADAM_B2 * v_ref[...] + (1.0 - ADAM_B2) * (g * g)
        g_ref[...] = g
        nm_ref[...] = nm
        nv_ref[...] = nv
        d_ref[...] = -ADAM_LR * ((nm * c1) / (jnp.sqrt(nv * c2) + ADAM_EPS) + ADAM_WD * w_ref[...])

    blk = pl.BlockSpec((None, tb, C), lambda l, i: (l, i, 0))

    def part_spec(k):
        return pl.BlockSpec((P, tb, C), lambda l, i: (0, jnp.where(l < k, 0, jnp.where(l > k, nb - 1, i)), 0))

    return pl.pallas_call(
        body, name=name, grid=(G, nb),
        in_specs=[blk, blk, blk] + [part_spec(k) for k in range(G)],
        out_specs=[blk] * 4, out_shape=[jax.ShapeDtypeStruct((G, R, C), F32)] * 4,
        compiler_params=_cparams(("arbitrary", "arbitrary")),
    )(w, m, v, *parts)


BIG = ("ffn1_w_in", "ffn1_w_out", "mix_w_in", "gdn_w_o", "cnv_w_o", "mix_w_out", "ffn2_w_in", "ffn2_w_out")
COL_SHARDED = ("ffn1_w_in", "mix_w_in", "ffn2_w_in")
SMALL_SHARDED = ("gdn_conv_w", "cnv_dw_w")
NAMES = ("ffn1_norm_pre", "ffn1_norm_post", "ffn1_w_in", "ffn1_w_out", "mix_norm_pre", "mix_norm_post", "mix_w_in",
         "gdn_conv_w", "gdn_a_log", "gdn_dt_bias", "gdn_norm_w", "gdn_w_o", "cnv_pw1_b", "cnv_dw_w", "cnv_dw_b",
         "cnv_ln_g", "cnv_ln_b", "cnv_w_o", "cnv_b_o", "mix_w_out", "ffn2_norm_pre", "ffn2_norm_post", "ffn2_w_in",
         "ffn2_w_out")
SMALL = tuple(n for n in NAMES if n not in BIG)


def _gathered_layer_weights(gath, params, i, D, H):
    W = {}
    for n in BIG:
        g = gath[n]
        g = g.reshape(-1, g.shape[-1])
        W[n] = _mix_in_reorder(g, D, H) if n == "mix_w_in" else g
    for n in SMALL_SHARDED:
        g = gath[n]
        W[n] = jnp.transpose(g, (1, 0, 2)).reshape(g.shape[1], -1)
    for n in SMALL:
        if n in SMALL_SHARDED:
            continue
        v = params[n][i]
        if n in ("gdn_a_log", "gdn_dt_bias"):
            v = jnp.pad(v, (0, LANES - H))
        W[n] = v.reshape(1, -1)
    return W


def kernel(x, ffn1_norm_pre, ffn1_norm_post, ffn1_w_in, ffn1_w_out, mix_norm_pre, mix_norm_post, mix_w_in, gdn_conv_w, gdn_a_log, gdn_dt_bias, gdn_norm_w, gdn_w_o, cnv_pw1_b, cnv_dw_w, cnv_dw_b, cnv_ln_g, cnv_ln_b, cnv_w_o, cnv_b_o, mix_w_out, ffn2_norm_pre, ffn2_norm_post, ffn2_w_in, ffn2_w_out, loss_target, m_ffn1_norm_pre, m_ffn1_norm_post, m_ffn1_w_in, m_ffn1_w_out, m_mix_norm_pre, m_mix_norm_post, m_mix_w_in, m_gdn_conv_w, m_gdn_a_log, m_gdn_dt_bias, m_gdn_norm_w, m_gdn_w_o, m_cnv_pw1_b, m_cnv_dw_w, m_cnv_dw_b, m_cnv_ln_g, m_cnv_ln_b, m_cnv_w_o, m_cnv_b_o, m_mix_w_out, m_ffn2_norm_pre, m_ffn2_norm_post, m_ffn2_w_in, m_ffn2_w_out, v_ffn1_norm_pre, v_ffn1_norm_post, v_ffn1_w_in, v_ffn1_w_out, v_mix_norm_pre, v_mix_norm_post, v_mix_w_in, v_gdn_conv_w, v_gdn_a_log, v_gdn_dt_bias, v_gdn_norm_w, v_gdn_w_o, v_cnv_pw1_b, v_cnv_dw_w, v_cnv_dw_b, v_cnv_ln_g, v_cnv_ln_b, v_cnv_w_o, v_cnv_b_o, v_mix_w_out, v_ffn2_norm_pre, v_ffn2_norm_post, v_ffn2_w_in, v_ffn2_w_out):
    params = dict(zip(NAMES, (ffn1_norm_pre, ffn1_norm_post, ffn1_w_in, ffn1_w_out, mix_norm_pre, mix_norm_post, mix_w_in, gdn_conv_w, gdn_a_log, gdn_dt_bias, gdn_norm_w, gdn_w_o, cnv_pw1_b, cnv_dw_w, cnv_dw_b, cnv_ln_g, cnv_ln_b, cnv_w_o, cnv_b_o, mix_w_out, ffn2_norm_pre, ffn2_norm_post, ffn2_w_in, ffn2_w_out)))
    mom1 = dict(zip(NAMES, (m_ffn1_norm_pre, m_ffn1_norm_post, m_ffn1_w_in, m_ffn1_w_out, m_mix_norm_pre, m_mix_norm_post, m_mix_w_in, m_gdn_conv_w, m_gdn_a_log, m_gdn_dt_bias, m_gdn_norm_w, m_gdn_w_o, m_cnv_pw1_b, m_cnv_dw_w, m_cnv_dw_b, m_cnv_ln_g, m_cnv_ln_b, m_cnv_w_o, m_cnv_b_o, m_mix_w_out, m_ffn2_norm_pre, m_ffn2_norm_post, m_ffn2_w_in, m_ffn2_w_out)))
    mom2 = dict(zip(NAMES, (v_ffn1_norm_pre, v_ffn1_norm_post, v_ffn1_w_in, v_ffn1_w_out, v_mix_norm_pre, v_mix_norm_post, v_mix_w_in, v_gdn_conv_w, v_gdn_a_log, v_gdn_dt_bias, v_gdn_norm_w, v_gdn_w_o, v_cnv_pw1_b, v_cnv_dw_w, v_cnv_dw_b, v_cnv_ln_g, v_cnv_ln_b, v_cnv_w_o, v_cnv_b_o, v_mix_w_out, v_ffn2_norm_pre, v_ffn2_norm_post, v_ffn2_w_in, v_ffn2_w_out)))
    T, D = x.shape[1], x.shape[2]
    H = D // GDN_DK
    L = ffn1_norm_pre.shape[0]
    xi, yi, ci = _coords()
    dev = 4 * xi + 2 * yi + ci

    ag_names = BIG + SMALL_SHARDED

    def shard_to_send(n):
        if n in COL_SHARDED:
            return jnp.swapaxes(params[n], 1, 2).astype(BF16)
        return params[n].astype(BF16) if n in BIG else params[n]

    send = {n: shard_to_send(n) for n in ag_names}

    def blocks_of(i):
        return [send[n][i] for n in ag_names]

    first = blocks_of(0)
    gathered0 = _comm_only(_GatherPlan(first), first, "ag_weights_l0")

    def weights_of(i, gathered):
        return _gathered_layer_weights(dict(zip(ag_names, gathered0 if i == 0 else gathered)), params, i, D, H)

    cidx = jnp.reshape(ci, (1,)).astype(jnp.int32)

    def to_reduce(i, grads_i):
        Gs = []
        for n in BIG:
            g = grads_i[n]
            if n == "mix_w_in":
                g = _mix_in_restore(g, D, H)
            Gs.append(g.reshape(N_DEV, -1, g.shape[-1]))
        return Gs, cidx, f"l{i}"

    loss_row, dx, grads, reduced = _trunk_fwd_bwd(x[0], loss_target[0], H, L, weights_of, blocks_of, to_reduce)
    loss = lax.psum(loss_row[0, 0], ("x", "y", "c"))

    Gs, _, _ = to_reduce(0, grads[0])
    R1s = _comm_only(_SiblingPlan(Gs), Gs, "rs_sibling_l0")
    Ps = _pair_adds(Gs, R1s, cidx, "l0")
    reduced[0] = _comm_only(_ChipsPlan(Ps), Ps, "rs_chips_l0")
    R2s = {n: [jnp.swapaxes(reduced[i][k], 1, 2) if n in COL_SHARDED else reduced[i][k] for i in range(L)]
           for k, n in enumerate(BIG)}

    pieces = []
    for i in range(L):
        for n in SMALL:
            piece = grads[i][n].reshape(-1, LANES)
            pieces.append(jnp.pad(piece, ((0, (-piece.shape[0]) % 8), (0, 0))))
    packed = jnp.concatenate(pieces, axis=0)
    small_all = _comm_only(_GatherPlan([packed]), [packed], "ag_small_grads")[0]
    small_sum = _sum_parts(small_all, "sum_small_grads")
    small_g = {n: [None] * L for n in SMALL}
    off = 0
    for i in range(L):
        for n in SMALL:
            shape = grads[i][n].shape
            cnt = shape[0] * shape[1] // LANES
            g = small_sum[off:off + cnt].reshape(shape)
            off += cnt + (-cnt) % 8
            if n in ("gdn_a_log", "gdn_dt_bias"):
                g = g[:, :H]
            if n in SMALL_SHARDED:
                wloc = params[n].shape[-1]
                g = lax.dynamic_slice_in_dim(g, dev * wloc, wloc, axis=1)
            small_g[n][i] = g

    outs = {}
    for n in NAMES:
        w, m, v = params[n], mom1[n], mom2[n]
        if n in BIG:
            shape3, parts = w.shape, R2s[n]
        else:
            rows, cols = (w.shape[0] * w.shape[1], w.shape[2]) if w.ndim == 3 else w.shape
            shape3, parts = (1, rows, cols), [jnp.stack(small_g[n], axis=0).reshape(1, rows, cols)]
        res = _adamw(w.reshape(shape3), m.reshape(shape3), v.reshape(shape3), parts, "adamw_" + n)
        outs[n] = [r.reshape(w.shape) for r in res]

    result = [loss, dx[None]]
    for k in range(4):
        result += [outs[n][k] for n in NAMES]
    return tuple(result)
```

```python
import jax
import jax.numpy as jnp
from jax import lax
from jax.experimental import pallas as pl
from jax.experimental.pallas import tpu as pltpu

F32 = jnp.float32
BF16 = jnp.bfloat16

GDN_DK = 128
CHUNK = 64
GDN_CONV = 4
CNV_K = 31
RMS_EPS = 1e-6
LN_EPS = 1e-5
L2_EPS = 1e-6
ADAM_LR = 0.001
ADAM_B1 = 0.9
ADAM_B2 = 0.999
ADAM_EPS = 1e-08
ADAM_WD = 0.01
ADAM_STEP = 10

LANES = 128
SUB = 16
VMEM_LIMIT = 56 * 1024 * 1024
N_DEV = 8
MESH = pl.DeviceIdType.MESH


def _cparams(sem=None, **kw):
    if sem is not None:
        kw["dimension_semantics"] = sem
    return pltpu.CompilerParams(vmem_limit_bytes=VMEM_LIMIT, **kw)


def _tile(dim, target):
    best = None
    for t in range(LANES, min(dim, target) + 1, LANES):
        if dim % t == 0:
            best = t
    return best if best is not None else dim


def _sigmoid(x):
    return 1.0 / (1.0 + jnp.exp(-x))


def _silu(x):
    return x * _sigmoid(x)


def _dsilu(x):
    s = _sigmoid(x)
    return s * (1.0 + x * (1.0 - s))


MM_VMEM_BUDGET = 40 * 1024 * 1024
MM_MAX_TILE = 2048


def _mm_tiles(M, N, K, out_bytes):
    def cands(dim):
        c = [t for t in range(LANES, min(dim, MM_MAX_TILE) + 1, LANES) if dim % t == 0]
        return c or [dim]
    best = None
    for tm in cands(M):
        for tn in cands(N):
            vm = 2 * (2 * K * (tm + tn) + tm * tn * out_bytes)
            if vm <= MM_VMEM_BUDGET and (best is None or tm * tn > best[0] * best[1]):
                best = (tm, tn)
    return best if best is not None else (cands(M)[0], cands(N)[0])


def _matmul(a, b, mode, out_dtype, name):
    if mode == "nn":
        (M, K), N = a.shape, b.shape[1]
    elif mode == "nt":
        (M, K), N = a.shape, b.shape[0]
    else:
        (K, M), N = a.shape, b.shape[1]
    tm, tn = _mm_tiles(M, N, K, jnp.dtype(out_dtype).itemsize)
    if mode == "nn":
        a_spec = pl.BlockSpec((tm, K), lambda j, i: (i, 0))
        b_spec = pl.BlockSpec((K, tn), lambda j, i: (0, j))
        dn = (((1,), (0,)), ((), ()))
    elif mode == "nt":
        a_spec = pl.BlockSpec((tm, K), lambda j, i: (i, 0))
        b_spec = pl.BlockSpec((tn, K), lambda j, i: (j, 0))
        dn = (((1,), (1,)), ((), ()))
    else:
        a_spec = pl.BlockSpec((K, tm), lambda j, i: (0, i))
        b_spec = pl.BlockSpec((K, tn), lambda j, i: (0, j))
        dn = (((0,), (0,)), ((), ()))

    def body(a_ref, b_ref, o_ref):
        o_ref[...] = lax.dot_general(a_ref[...], b_ref[...], dn, preferred_element_type=F32).astype(out_dtype)

    return pl.pallas_call(
        body, name=name, grid=(N // tn, M // tm), in_specs=[a_spec, b_spec],
        out_specs=pl.BlockSpec((tm, tn), lambda j, i: (i, j)),
        out_shape=jax.ShapeDtypeStruct((M, N), out_dtype),
        compiler_params=_cparams(("parallel", "parallel")),
    )(a, b)


def _rowcall(name, body, T, tb, row_ins, par_ins, row_outs, acc_outs):
    n_ri, n_pi, n_ro = len(row_ins), len(par_ins), len(row_outs)

    def kern(*refs):
        ri, pi = refs[:n_ri], refs[n_ri:n_ri + n_pi]
        ro, ao = refs[n_ri + n_pi:n_ri + n_pi + n_ro], refs[n_ri + n_pi + n_ro:]
        if ao:
            @pl.when(pl.program_id(0) == 0)
            def _():
                for r in ao:
                    r[...] = jnp.zeros_like(r)
        body(ri, pi, ro, ao)

    in_specs = [pl.BlockSpec((tb, w), lambda i, cb=cb: (i, cb)) for (_, w, cb) in row_ins]
    in_specs += [pl.BlockSpec(p.shape, lambda i: (0, 0)) for p in par_ins]
    out_specs = [pl.BlockSpec((tb, w), lambda i: (i, 0)) for (w, _) in row_outs]
    out_specs += [pl.BlockSpec((1, w), lambda i: (0, 0)) for w in acc_outs]
    out_shape = [jax.ShapeDtypeStruct((T, w), dt) for (w, dt) in row_outs]
    out_shape += [jax.ShapeDtypeStruct((1, w), F32) for w in acc_outs]
    return pl.pallas_call(
        kern, name=name, grid=(T // tb,), in_specs=in_specs, out_specs=out_specs, out_shape=out_shape,
        compiler_params=_cparams(("arbitrary",)),
    )(*[a for (a, _, _) in row_ins], *par_ins)


def _rsum(x):
    return jnp.sum(x, axis=0, keepdims=True)


def _rms_rstd(x):
    return lax.rsqrt(jnp.mean(x * x, axis=-1, keepdims=True) + RMS_EPS)


def _rms_fwd(x, w, name):
    T, D = x.shape

    def body(ri, pi, ro, ao):
        xv = ri[0][...]
        ro[0][...] = (xv * _rms_rstd(xv) * pi[0][...]).astype(BF16)

    return _rowcall(name, body, T, 256, [(x, D, 0)], [w], [(D, BF16)], [])[0]


def _rms_bwd_core(dy, x, w):
    rs = _rms_rstd(x)
    xh = x * rs
    gw = dy * w
    dx = rs * (gw - xh * jnp.mean(gw * xh, axis=-1, keepdims=True))
    return dx, dy * xh


def _pre_bwd(dh, x, w, dres, name):
    T, D = x.shape

    def body(ri, pi, ro, ao):
        dx, dwc = _rms_bwd_core(ri[0][...], ri[1][...], pi[0][...])
        ro[0][...] = ri[2][...] + dx
        ao[0][...] += _rsum(dwc)

    return _rowcall(name, body, T, 256, [(dh, D, 0), (x, D, 0), (dres, D, 0)], [w], [(D, F32)], [D])


def _post_fwd(x, f, w, r, name):
    T, D = x.shape

    def body(ri, pi, ro, ao):
        fv = ri[1][...]
        ro[0][...] = ri[0][...] + r * (fv * _rms_rstd(fv) * pi[0][...])

    return _rowcall(name, body, T, 256, [(x, D, 0), (f, D, 0)], [w], [(D, F32)], [])[0]


def _post_bwd(dxn, f, w, r, name):
    T, D = f.shape

    def body(ri, pi, ro, ao):
        df, dwc = _rms_bwd_core(r * ri[0][...], ri[1][...], pi[0][...])
        ro[0][...] = df.astype(BF16)
        ao[0][...] += _rsum(dwc)

    return _rowcall(name, body, T, 256, [(dxn, D, 0), (f, D, 0)], [w], [(D, BF16)], [D])


def _swiglu_fwd(a, name):
    T, F2 = a.shape
    F = F2 // 2

    def body(ri, pi, ro, ao):
        ro[0][...] = (_silu(ri[0][...].astype(F32)) * ri[1][...].astype(F32)).astype(BF16)

    return _rowcall(name, body, T, 256, [(a, F, 0), (a, F, 1)], [], [(F, BF16)], [])[0]


def _swiglu_bwd(ds, a, name):
    T, F2 = a.shape
    F = F2 // 2

    def body(ri, pi, ro, ao):
        dsv, g, u = ri[0][...].astype(F32), ri[1][...].astype(F32), ri[2][...].astype(F32)
        ro[0][:, :F] = (dsv * u * _dsilu(g)).astype(BF16)
        ro[0][:, F:] = (dsv * _silu(g)).astype(BF16)

    return _rowcall(name, body, T, 256, [(ds, F, 0), (a, F, 0), (a, F, 1)], [], [(F2, BF16)], [])[0]


def _loss_fwd_bwd(y, tgt, name):
    T, D = y.shape

    def body(ri, pi, ro, ao):
        e = ri[0][...] - ri[1][...]
        ro[0][...] = e * (1.0 / D)
        tot = jnp.sum(_rsum(e * e), axis=1, keepdims=True) * (0.5 / D)
        ao[0][...] += jnp.broadcast_to(tot, (1, LANES))

    return _rowcall(name, body, T, 256, [(y, D, 0), (tgt, D, 0)], [], [(D, F32)], [LANES])


def _gdn_gate_fwd(o, p, nw, name):
    T, D = o.shape
    H = D // GDN_DK

    def body(ri, pi, ro, ao):
        for h in range(H):
            sl = slice(h * GDN_DK, (h + 1) * GDN_DK)
            oh = ri[0][:, sl]
            ro[0][:, sl] = (oh * _rms_rstd(oh) * pi[0][...] * _silu(ri[1][:, sl])).astype(BF16)

    return _rowcall(name, body, T, 256, [(o, D, 0), (p, D, 3)], [nw], [(D, BF16)], [])[0]


def _gdn_gate_bwd(dog, o, p, nw, name):
    T, D = o.shape
    H = D // GDN_DK

    def body(ri, pi, ro, ao):
        acc = jnp.zeros((1, GDN_DK), F32)
        for h in range(H):
            sl = slice(h * GDN_DK, (h + 1) * GDN_DK)
            dy, oh, z = ri[0][:, sl], ri[1][:, sl], ri[2][:, sl]
            sz = _silu(z)
            do, dwc = _rms_bwd_core(dy * sz, oh, pi[0][...])
            ro[0][:, sl] = do
            ro[1][:, sl] = (dy * oh * _rms_rstd(oh) * pi[0][...] * _dsilu(z)).astype(BF16)
            acc = acc + _rsum(dwc)
        ao[0][...] += acc

    return _rowcall(name, body, T, 256, [(dog, D, 0), (o, D, 0), (p, D, 3)], [nw], [(D, F32), (D, BF16)], [GDN_DK])


def _glu_fwd(p, b, name):
    T = p.shape[0]
    D = b.shape[1] // 2

    def body(ri, pi, ro, ao):
        ro[0][...] = (ri[0][...] + pi[0][:, :D]) * _sigmoid(ri[1][...] + pi[0][:, D:])

    return _rowcall(name, body, T, 256, [(p, D, 4), (p, D, 5)], [b], [(D, F32)], [])[0]


def _glu_bwd(dhc, p, b, name):
    T = p.shape[0]
    D = b.shape[1] // 2

    def body(ri, pi, ro, ao):
        d, a, g = ri[0][...], ri[1][...] + pi[0][:, :D], ri[2][...] + pi[0][:, D:]
        sg = _sigmoid(g)
        da, dg = d * sg, d * a * sg * (1.0 - sg)
        ro[0][:, :D] = da.astype(BF16)
        ro[0][:, D:] = dg.astype(BF16)
        ao[0][:, :D] += _rsum(da)
        ao[0][:, D:] += _rsum(dg)

    return _rowcall(name, body, T, 256, [(dhc, D, 0), (p, D, 4), (p, D, 5)], [b], [(2 * D, BF16)], [2 * D])


def _ln_stats(x):
    mu = jnp.mean(x, axis=-1, keepdims=True)
    xc = x - mu
    rstd = lax.rsqrt(jnp.mean(xc * xc, axis=-1, keepdims=True) + LN_EPS)
    return xc * rstd, rstd


def _ln_silu_fwd(hcv, g, b, name):
    T, D = hcv.shape

    def body(ri, pi, ro, ao):
        xh, _ = _ln_stats(ri[0][...])
        ro[0][...] = _silu(xh * pi[0][...] + pi[1][...]).astype(BF16)

    return _rowcall(name, body, T, 256, [(hcv, D, 0)], [g, b], [(D, BF16)], [])[0]


def _ln_silu_bwd(dhl, hcv, g, b, name):
    T, D = hcv.shape

    def body(ri, pi, ro, ao):
        xh, rstd = _ln_stats(ri[1][...])
        dyl = ri[0][...] * _dsilu(xh * pi[0][...] + pi[1][...])
        dxh = dyl * pi[0][...]
        dx = rstd * (dxh - jnp.mean(dxh, axis=-1, keepdims=True) - xh * jnp.mean(dxh * xh, axis=-1, keepdims=True))
        ro[0][...] = dx
        ao[0][...] += _rsum(dyl * xh)
        ao[1][...] += _rsum(dyl)
        ao[2][...] += _rsum(dx)

    return _rowcall(name, body, T, 256, [(dhl, D, 0), (hcv, D, 0)], [g, b], [(D, F32)], [D, D, D])


def _merge_fwd(p, ya, yb, bo, name):
    T, D = ya.shape

    def body(ri, pi, ro, ao):
        ro[0][...] = (_sigmoid(ri[0][...]) * ri[2][...] + _sigmoid(ri[1][...]) * (ri[3][...] + pi[0][...])).astype(BF16)

    return _rowcall(name, body, T, 256, [(p, D, 6), (p, D, 7), (ya, D, 0), (yb, D, 0)], [bo], [(D, BF16)], [])[0]


def _merge_bwd(dym, p, ya, yb, bo, name):
    T, D = ya.shape

    def body(ri, pi, ro, ao):
        d = ri[0][...]
        ga, gb = _sigmoid(ri[1][...]), _sigmoid(ri[2][...])
        ybv = ri[4][...] + pi[0][...]
        dyb = d * gb
        ro[0][...] = (d * ga).astype(BF16)
        ro[1][...] = dyb.astype(BF16)
        ro[2][:, :D] = (d * ri[3][...] * ga * (1.0 - ga)).astype(BF16)
        ro[2][:, D:] = (d * ybv * gb * (1.0 - gb)).astype(BF16)
        ao[0][...] += _rsum(dyb)

    return _rowcall(name, body, T, 256, [(dym, D, 0), (p, D, 6), (p, D, 7), (ya, D, 0), (yb, D, 0)], [bo],
                    [(D, BF16), (D, BF16), (2 * D, BF16)], [D])


PAD = 32
RC = 256


def _tap_windows(ref, offs):
    groups = {}
    for j, o in enumerate(offs):
        groups.setdefault(o % 8, []).append((j, o))
    for grp in groups.values():
        lo, hi = min(o for _, o in grp), max(o for _, o in grp)
        win = ref[pl.ds(lo, RC + hi - lo), :]
        for j, o in grp:
            yield j, win[o - lo:o - lo + RC]


def _causal_taps(xp_ref, w, K, c0):
    acc = None
    for j, xs in _tap_windows(xp_ref, [PAD - (K - 1) + j + c0 for j in range(K)]):
        term = w[j:j + 1, :] * xs
        acc = term if acc is None else acc + term
    return acc


def _anticausal_taps(dp_ref, w, K, c0):
    acc = None
    for j, ds in _tap_windows(dp_ref, [(K - 1) - j + c0 for j in range(K)]):
        term = w[j:j + 1, :] * ds
        acc = term if acc is None else acc + term
    return acc


def _tap_grads(dw_ref, dc_ref, xp_ref, K, T):
    accs = [jnp.zeros((8, LANES), F32) for _ in range(K)]
    for c in range(T // RC):
        d = dc_ref[pl.ds(c * RC, RC), :]
        for j, xs in _tap_windows(xp_ref, [PAD - (K - 1) + j + c * RC for j in range(K)]):
            accs[j] = accs[j] + jnp.sum((d * xs).reshape(RC // 8, 8, LANES), axis=0)
    for j in range(K):
        dw_ref[j:j + 1, :] = _rsum(accs[j])


def _qkv_conv_fwd(p, cw, H, name):
    T = p.shape[0]
    K = cw.shape[0]

    def body(x_ref, w_ref, o_ref, xp_ref):
        j = pl.program_id(0)
        xp_ref[pl.ds(0, PAD), :] = jnp.zeros((PAD, LANES), F32)
        xp_ref[pl.ds(PAD, T), :] = x_ref[...]
        w = w_ref[...]
        scale = jnp.where(j < H, GDN_DK ** -0.5, 1.0).astype(F32)
        for c in range(T // RC):
            act = _silu(_causal_taps(xp_ref, w, K, c * RC))
            nrm = act * lax.rsqrt(jnp.sum(act * act, axis=-1, keepdims=True) + L2_EPS) * scale
            o_ref[pl.ds(c * RC, RC), :] = jnp.where(j < 2 * H, nrm, act)

    return pl.pallas_call(
        body, name=name, grid=(3 * H,),
        in_specs=[pl.BlockSpec((T, LANES), lambda j: (0, j)), pl.BlockSpec((K, LANES), lambda j: (0, j))],
        out_specs=pl.BlockSpec((T, LANES), lambda j: (0, j)),
        out_shape=jax.ShapeDtypeStruct((T, 3 * H * GDN_DK), F32),
        scratch_shapes=[pltpu.VMEM((T + PAD, LANES), F32)],
        compiler_params=_cparams(("arbitrary",)),
    )(p, cw)


def _qkv_conv_bwd(dn, p, cw, H, name):
    T = p.shape[0]
    K = cw.shape[0]

    def body(dn_ref, x_ref, w_ref, dx_ref, dw_ref, xp_ref, dc_ref):
        j = pl.program_id(0)
        xp_ref[pl.ds(0, PAD), :] = jnp.zeros((PAD, LANES), F32)
        xp_ref[pl.ds(PAD, T), :] = x_ref[...]
        dc_ref[pl.ds(T, PAD), :] = jnp.zeros((PAD, LANES), F32)
        w = w_ref[...]
        scale = jnp.where(j < H, GDN_DK ** -0.5, 1.0).astype(F32)
        for c in range(T // RC):
            pre = _causal_taps(xp_ref, w, K, c * RC)
            act = _silu(pre)
            d = dn_ref[pl.ds(c * RC, RC), :]
            rs = lax.rsqrt(jnp.sum(act * act, axis=-1, keepdims=True) + L2_EPS)
            nh = act * rs
            dact_n = scale * rs * (d - nh * jnp.sum(d * nh, axis=-1, keepdims=True))
            dact = jnp.where(j < 2 * H, dact_n, d)
            dc_ref[pl.ds(c * RC, RC), :] = dact * _dsilu(pre)
        for c in range(T // RC):
            dx_ref[pl.ds(c * RC, RC), :] = _anticausal_taps(dc_ref, w, K, c * RC).astype(BF16)
        _tap_grads(dw_ref, dc_ref, xp_ref, K, T)

    return pl.pallas_call(
        body, name=name, grid=(3 * H,),
        in_specs=[pl.BlockSpec((T, LANES), lambda j: (0, j)), pl.BlockSpec((T, LANES), lambda j: (0, j)),
                  pl.BlockSpec((K, LANES), lambda j: (0, j))],
        out_specs=[pl.BlockSpec((T, LANES), lambda j: (0, j)), pl.BlockSpec((K, LANES), lambda j: (0, j))],
        out_shape=[jax.ShapeDtypeStruct((T, 3 * H * GDN_DK), BF16), jax.ShapeDtypeStruct(cw.shape, F32)],
        scratch_shapes=[pltpu.VMEM((T + PAD, LANES), F32), pltpu.VMEM((T + PAD, LANES), F32)],
        compiler_params=_cparams(("arbitrary",)),
    )(dn, p, cw)


def _dw_conv_fwd(hc, w, b, name):
    T, D = hc.shape
    K = w.shape[0]

    def body(x_ref, w_ref, b_ref, o_ref, xp_ref):
        xp_ref[pl.ds(0, PAD), :] = jnp.zeros((PAD, LANES), F32)
        xp_ref[pl.ds(PAD, T), :] = x_ref[...]
        wv = w_ref[...]
        for c in range(T // RC):
            o_ref[pl.ds(c * RC, RC), :] = _causal_taps(xp_ref, wv, K, c * RC) + b_ref[...]

    return pl.pallas_call(
        body, name=name, grid=(D // LANES,),
        in_specs=[pl.BlockSpec((T, LANES), lambda j: (0, j)), pl.BlockSpec((K, LANES), lambda j: (0, j)),
                  pl.BlockSpec((1, LANES), lambda j: (0, j))],
        out_specs=pl.BlockSpec((T, LANES), lambda j: (0, j)),
        out_shape=jax.ShapeDtypeStruct((T, D), F32),
        scratch_shapes=[pltpu.VMEM((T + PAD, LANES), F32)],
        compiler_params=_cparams(("arbitrary",)),
    )(hc, w, b)


def _dw_conv_bwd(dy, hc, w, name, comm=None):
    T, D = hc.shape
    K = w.shape[0]

    def body(dy_ref, x_ref, w_ref, dx_ref, dw_ref, xp_ref, dc_ref):
        xp_ref[pl.ds(0, PAD), :] = jnp.zeros((PAD, LANES), F32)
        xp_ref[pl.ds(PAD, T), :] = x_ref[...]
        dc_ref[pl.ds(T, PAD), :] = jnp.zeros((PAD, LANES), F32)
        dc_ref[pl.ds(0, T), :] = dy_ref[...]
        wv = w_ref[...]
        for c in range(T // RC):
            dx_ref[pl.ds(c * RC, RC), :] = _anticausal_taps(dc_ref, wv, K, c * RC)
        _tap_grads(dw_ref, dc_ref, xp_ref, K, T)

    return _comm_call(
        body, comm, name=name, grid=(D // LANES,),
        in_specs=[pl.BlockSpec((T, LANES), lambda j: (0, j)), pl.BlockSpec((T, LANES), lambda j: (0, j)),
                  pl.BlockSpec((K, LANES), lambda j: (0, j))],
        out_specs=[pl.BlockSpec((T, LANES), lambda j: (0, j)), pl.BlockSpec((K, LANES), lambda j: (0, j))],
        out_shape=[jax.ShapeDtypeStruct((T, D), F32), jax.ShapeDtypeStruct(w.shape, F32)],
        scratch_shapes=[pltpu.VMEM((T + PAD, LANES), F32), pltpu.VMEM((T + PAD, LANES), F32)],
        args=(dy, hc, w))


NN = (((1,), (0,)), ((), ()))
NT = (((1,), (1,)), ((), ()))
TN = (((0,), (0,)), ((), ()))


def _dotb(a, b, dn=NN):
    return lax.dot_general(a.astype(BF16), b.astype(BF16), dn, preferred_element_type=F32)


def _split_bf16(x, n):
    parts, r = [], x
    for _ in range(n):
        p = r.astype(BF16)
        parts.append(p)
        r = r - p.astype(F32)
    return parts


def _dot_sel(sel, x, pieces, sel_left=True):
    sb = sel.astype(BF16)
    acc = None
    for p in _split_bf16(x, pieces):
        t = (lax.dot_general(sb, p, NN, preferred_element_type=F32) if sel_left
             else lax.dot_general(p, sb, NN, preferred_element_type=F32))
        acc = t if acc is None else acc + t
    return acc


def _iota2(shape, axis):
    return lax.broadcasted_iota(jnp.int32, shape, axis)


def _to_row(col, eye):
    return jnp.sum(jnp.where(eye, col, 0.0), axis=0, keepdims=True)


def _to_col(row, eye):
    return jnp.sum(jnp.where(eye, row, 0.0), axis=1, keepdims=True)


def _gdn_gates(bl, al, alog, dtb):
    beta = _sigmoid(bl)
    x = al + dtb
    sp = jnp.maximum(x, 0.0) + jnp.log(1.0 + jnp.exp(-jnp.abs(x)))
    g = -jnp.exp(alog) * sp
    r, c = _iota2((CHUNK, CHUNK), 0), _iota2((CHUNK, CHUNK), 1)
    G = _dot_sel(r >= c, g, 3)
    return beta, g, G, x


def _unit_lower_inverses(As, Ats):
    n = len(As)
    nb = CHUNK // SUB
    lane = _iota2((SUB, CHUNK), 1)
    row = _iota2((SUB, CHUNK), 0)
    Atp = []
    for At in Ats:
        acc = jnp.zeros((SUB, CHUNK), F32)
        for b in range(nb):
            acc = jnp.where(lane // SUB == b, At[b * SUB:(b + 1) * SUB, :], acc)
        Atp.append(acc)
    gr, gc = _iota2((CHUNK, CHUNK), 0), _iota2((CHUNK, CHUNK), 1)
    ones_bd = gr // SUB == gc // SUB
    stack = jnp.concatenate(
        [jnp.where(lane % SUB == i, Atp[m], 0.0) for i in range(1, SUB) for m in range(n)], axis=0)
    Cm = _dot_sel(ones_bd, stack, 2, sel_left=False)
    Z = [(row == lane % SUB).astype(F32) for _ in range(n)]
    for i in range(1, SUB):
        for m in range(n):
            cm = Cm[((i - 1) * n + m) * SUB:((i - 1) * n + m + 1) * SUB, :]
            new = -jnp.sum(cm * Z[m], axis=0, keepdims=True)
            Z[m] = Z[m] + jnp.where(row == i, new, 0.0)
    bd = gr // SUB == gc // SUB
    Xs = [jnp.where(bd, jnp.concatenate([Z[m]] * nb, axis=0), 0.0) for m in range(n)]
    blk = SUB
    while blk < CHUNK:
        off = (gr // (2 * blk) == gc // (2 * blk)) & (gr // blk != gc // blk)
        Ys = [_dotb(Xs[m], jnp.where(off, As[m], 0.0)) for m in range(n)]
        Xs = [Xs[m] - _dotb(Ys[m], Xs[m]) for m in range(n)]
        blk *= 2
    return Xs


def _gdn_fwd(qkvn, p, alog, dtb, H, name, comm=None):
    T = qkvn.shape[0]
    D = H * GDN_DK
    N = T // CHUNK
    bblk = 8 * D // LANES

    def body(q_ref, k_ref, v_ref, b_ref, a_ref, alog_ref, dtb_ref, o_ref, t_ref, s_ref, S_scr):
        @pl.when(pl.program_id(0) == 0)
        def _():
            S_scr[...] = jnp.zeros_like(S_scr)

        beta, _, G, _ = _gdn_gates(b_ref[...], a_ref[...], alog_ref[...], dtb_ref[...])
        hs = range(H)
        sl = [slice(h * GDN_DK, (h + 1) * GDN_DK) for h in hs]
        q, k, v = [q_ref[:, s] for s in sl], [k_ref[:, s] for s in sl], [v_ref[:, s] for s in sl]
        Gc, bc = [G[:, h:h + 1] for h in hs], [beta[:, h:h + 1] for h in hs]
        r, c = _iota2((CHUNK, CHUNK), 0), _iota2((CHUNK, CHUNK), 1)
        eye, low, up = r == c, r >= c, r <= c
        Gr, br = [_to_row(Gc[h], eye) for h in hs], [_to_row(bc[h], eye) for h in hs]
        Dm = [jnp.where(low, jnp.exp(jnp.where(low, Gc[h] - Gr[h], 0.0)), 0.0) for h in hs]
        Dt = [jnp.where(up, jnp.exp(jnp.where(up, Gr[h] - Gc[h], 0.0)), 0.0) for h in hs]
        qk = [_dotb(jnp.concatenate([q[h], k[h]], axis=0), k[h], NT) for h in hs]
        QK = [qk[h][:CHUNK] * Dm[h] for h in hs]
        KK = [qk[h][CHUNK:] for h in hs]
        As = [jnp.where(r > c, KK[h] * Dm[h], 0.0) * bc[h] for h in hs]
        Ats = [jnp.where(r < c, KK[h] * Dt[h], 0.0) * br[h] for h in hs]
        Ts = _unit_lower_inverses(As, Ats)
        eG = [jnp.exp(Gc[h]) for h in hs]
        Gl = [Gc[h][CHUNK - 1:CHUNK, :] for h in hs]
        uw = [_dotb(Ts[h], jnp.concatenate([v[h] * bc[h], k[h] * (bc[h] * eG[h])], axis=1)) for h in hs]
        S = [S_scr[h] for h in hs]
        qw = [_dotb(jnp.concatenate([q[h] * eG[h], uw[h][:, GDN_DK:]], axis=0), S[h]) for h in hs]
        vn = [uw[h][:, :GDN_DK] - qw[h][CHUNK:] for h in hs]
        o = [qw[h][:CHUNK] + _dotb(QK[h], vn[h]) for h in hs]
        Sn = [S[h] * jnp.exp(Gl[h]) + _dotb(k[h] * jnp.exp(Gl[h] - Gc[h]), vn[h], TN) for h in hs]
        for h in hs:
            t_ref[0, h] = Ts[h]
            s_ref[0, h] = S[h]
            o_ref[:, sl[h]] = o[h]
            S_scr[h] = Sn[h]

    qkv_spec = [pl.BlockSpec((CHUNK, D), lambda n, cb=cb: (n, cb)) for cb in range(3)]
    return _comm_call(
        body, comm, name=name, grid=(N,),
        in_specs=qkv_spec + [pl.BlockSpec((CHUNK, LANES), lambda n: (n, bblk)),
                             pl.BlockSpec((CHUNK, LANES), lambda n: (n, bblk + 1)),
                             pl.BlockSpec((1, LANES), lambda n: (0, 0)), pl.BlockSpec((1, LANES), lambda n: (0, 0))],
        out_specs=[pl.BlockSpec((CHUNK, D), lambda n: (n, 0)),
                   pl.BlockSpec((1, H, CHUNK, CHUNK), lambda n: (n, 0, 0, 0)),
                   pl.BlockSpec((1, H, GDN_DK, GDN_DK), lambda n: (n, 0, 0, 0))],
        out_shape=[jax.ShapeDtypeStruct((T, D), F32), jax.ShapeDtypeStruct((N, H, CHUNK, CHUNK), F32),
                   jax.ShapeDtypeStruct((N, H, GDN_DK, GDN_DK), F32)],
        scratch_shapes=[pltpu.VMEM((H, GDN_DK, GDN_DK), F32)],
        args=(qkvn, qkvn, qkvn, p, p, alog, dtb))


def _gdn_bwd(do, qkvn, p, alog, dtb, Tinv, Sin, H, name, comm=None):
    T = qkvn.shape[0]
    D = H * GDN_DK
    N = T // CHUNK
    bblk = 8 * D // LANES

    def body(do_ref, q_ref, k_ref, v_ref, b_ref, a_ref, alog_ref, dtb_ref, t_ref, s_ref,
             dqkv_ref, dba_ref, dalog_ref, ddtb_ref, dS_scr):
        @pl.when(pl.program_id(0) == 0)
        def _():
            dS_scr[...] = jnp.zeros_like(dS_scr)
            dalog_ref[...] = jnp.zeros_like(dalog_ref)
            ddtb_ref[...] = jnp.zeros_like(ddtb_ref)

        beta, g, G, x = _gdn_gates(b_ref[...], a_ref[...], alog_ref[...], dtb_ref[...])
        r, c = _iota2((CHUNK, CHUNK), 0), _iota2((CHUNK, CHUNK), 1)
        eye, low, strict = r == c, r >= c, r > c
        lane = _iota2((CHUNK, LANES), 1)
        rsum1 = lambda a: jnp.sum(a, axis=1, keepdims=True)
        hs = range(H)
        sl = [slice(h * GDN_DK, (h + 1) * GDN_DK) for h in hs]
        q, k, v = [q_ref[:, s] for s in sl], [k_ref[:, s] for s in sl], [v_ref[:, s] for s in sl]
        dov = [do_ref[:, s] for s in sl]
        Gc, bc = [G[:, h:h + 1] for h in hs], [beta[:, h:h + 1] for h in hs]
        Tm, S, dSo = [t_ref[0, h] for h in hs], [s_ref[0, h] for h in hs], [dS_scr[h] for h in hs]
        Gr = [_to_row(Gc[h], eye) for h in hs]
        Dm = [jnp.where(low, jnp.exp(jnp.where(low, Gc[h] - Gr[h], 0.0)), 0.0) for h in hs]
        eG = [jnp.exp(Gc[h]) for h in hs]
        Gl = [Gc[h][CHUNK - 1:CHUNK, :] for h in hs]
        eR, dch = [jnp.exp(Gl[h] - Gc[h]) for h in hs], [jnp.exp(Gl[h]) for h in hs]
        qk = [_dotb(jnp.concatenate([q[h], k[h]], axis=0), k[h], NT) for h in hs]
        QKr, KK = [qk[h][:CHUNK] for h in hs], [qk[h][CHUNK:] for h in hs]
        QK = [QKr[h] * Dm[h] for h in hs]
        M = [jnp.where(strict, KK[h] * Dm[h], 0.0) for h in hs]
        uw = [_dotb(Tm[h], jnp.concatenate([v[h] * bc[h], k[h] * (bc[h] * eG[h])], axis=1)) for h in hs]
        u, w = [uw[h][:, :GDN_DK] for h in hs], [uw[h][:, GDN_DK:] for h in hs]
        qd, kd = [q[h] * eG[h] for h in hs], [k[h] * eR[h] for h in hs]
        vn = [u[h] - _dotb(w[h], S[h]) for h in hs]
        dvn = [_dotb(QK[h], dov[h], TN) + _dotb(kd[h], dSo[h]) for h in hs]
        dQK = [jnp.where(low, _dotb(dov[h], vn[h], NT), 0.0) for h in hs]
        dkd = [_dotb(vn[h], dSo[h], NT) for h in hs]
        ddch = [jnp.sum(rsum1(dSo[h] * S[h]), axis=0, keepdims=True) for h in hs]
        dd = [jnp.concatenate([dov[h], dvn[h]], axis=0) for h in hs]
        xs = [_dotb(dd[h], S[h], NT) for h in hs]
        dqd, dw = [xs[h][:CHUNK] for h in hs], [-xs[h][CHUNK:] for h in hs]
        dS = [_dotb(jnp.concatenate([qd[h], -w[h]], axis=0), dd[h], TN) + dch[h] * dSo[h] for h in hs]
        yb = [_dotb(Tm[h], jnp.concatenate([dvn[h], dw[h]], axis=1), TN) for h in hs]
        dvb, dkb = [yb[h][:, :GDN_DK] for h in hs], [yb[h][:, GDN_DK:] for h in hs]
        dA = [-jnp.where(strict, _dotb(yb[h], uw[h], NT), 0.0) for h in hs]
        rk = [rsum1(dkb[h] * k[h]) for h in hs]
        dbeta = [rsum1(dvb[h] * v[h]) + rk[h] * eG[h] + rsum1(dA[h] * M[h]) for h in hs]
        dM = [dA[h] * bc[h] for h in hs]
        dKK = [dM[h] * Dm[h] for h in hs]
        dQKr = [dQK[h] * Dm[h] for h in hs]
        E = [dM[h] * M[h] + dQK[h] * QK[h] for h in hs]
        zk = [_dotb(jnp.concatenate([dQKr[h], dKK[h]], axis=0), k[h]) for h in hs]
        dq = [zk[h][:CHUNK] + dqd[h] * eG[h] for h in hs]
        dk = [dkb[h] * (bc[h] * eG[h]) + zk[h][CHUNK:] + _dotb(dKK[h], k[h], TN) + _dotb(dQKr[h], q[h], TN)
              + dkd[h] * eR[h] for h in hs]
        deG = [rk[h] * bc[h] + rsum1(dqd[h] * q[h]) for h in hs]
        deR = [rsum1(dkd[h] * k[h]) for h in hs]
        dGl = [jnp.sum(deR[h] * eR[h], axis=0, keepdims=True) + ddch[h] * dch[h] for h in hs]
        dGc = [rsum1(E[h]) - _to_col(jnp.sum(E[h], axis=0, keepdims=True), eye) + deG[h] * eG[h] - deR[h] * eR[h]
               + jnp.where(r[:, :1] == CHUNK - 1, dGl[h], 0.0) for h in hs]
        dG_all = jnp.zeros((CHUNK, LANES), F32)
        dbeta_all = jnp.zeros((CHUNK, LANES), F32)
        for h in hs:
            dS_scr[h] = dS[h]
            dqkv_ref[:, sl[h]] = dq[h]
            dqkv_ref[:, D + h * GDN_DK:D + (h + 1) * GDN_DK] = dk[h]
            dqkv_ref[:, 2 * D + h * GDN_DK:2 * D + (h + 1) * GDN_DK] = dvb[h] * bc[h]
            dG_all = jnp.where(lane == h, dGc[h], dG_all)
            dbeta_all = jnp.where(lane == h, dbeta[h], dbeta_all)
        dg = _dot_sel(r <= c, dG_all, 3)
        da = dg * (-jnp.exp(alog_ref[...])) * _sigmoid(x)
        dba_ref[:, :LANES] = (dbeta_all * beta * (1.0 - beta)).astype(BF16)
        dba_ref[:, LANES:] = da.astype(BF16)
        dalog_ref[...] += _rsum(dg * g)
        ddtb_ref[...] += _rsum(da)

    rev = lambda n: N - 1 - n
    qkv_spec = [pl.BlockSpec((CHUNK, D), lambda n, cb=cb: (rev(n), cb)) for cb in range(3)]
    return _comm_call(
        body, comm, name=name, grid=(N,),
        in_specs=[pl.BlockSpec((CHUNK, D), lambda n: (rev(n), 0))] + qkv_spec + [
            pl.BlockSpec((CHUNK, LANES), lambda n: (rev(n), bblk)),
            pl.BlockSpec((CHUNK, LANES), lambda n: (rev(n), bblk + 1)),
            pl.BlockSpec((1, LANES), lambda n: (0, 0)), pl.BlockSpec((1, LANES), lambda n: (0, 0)),
            pl.BlockSpec((1, H, CHUNK, CHUNK), lambda n: (rev(n), 0, 0, 0)),
            pl.BlockSpec((1, H, GDN_DK, GDN_DK), lambda n: (rev(n), 0, 0, 0))],
        out_specs=[pl.BlockSpec((CHUNK, 3 * D), lambda n: (rev(n), 0)),
                   pl.BlockSpec((CHUNK, 2 * LANES), lambda n: (rev(n), 0)),
                   pl.BlockSpec((1, LANES), lambda n: (0, 0)), pl.BlockSpec((1, LANES), lambda n: (0, 0))],
        out_shape=[jax.ShapeDtypeStruct((T, 3 * D), F32), jax.ShapeDtypeStruct((T, 2 * LANES), BF16),
                   jax.ShapeDtypeStruct((1, LANES), F32), jax.ShapeDtypeStruct((1, LANES), F32)],
        scratch_shapes=[pltpu.VMEM((H, GDN_DK, GDN_DK), F32)],
        args=(do, qkvn, qkvn, qkvn, p, p, alog, dtb, Tinv, Sin))


def _mix_in_reorder(wt, D, H):
    o1 = 4 * D
    o2, o3 = o1 + H, o1 + 2 * H
    z = jnp.zeros((LANES - H, wt.shape[1]), wt.dtype)
    return jnp.concatenate([wt[:o1], wt[o3:], wt[o1:o2], z, wt[o2:o3], z], axis=0)


def _mix_in_restore(dwt, D, H):
    b0 = 8 * D
    return jnp.concatenate([dwt[:4 * D], dwt[b0:b0 + H], dwt[b0 + LANES:b0 + LANES + H], dwt[4 * D:b0]], axis=0)


def _ffn_fwd(x, W, pre, tag):
    h = _rms_fwd(x, W[pre + "_norm_pre"], tag + "_pre")
    a = _matmul(h, W[pre + "_w_in"], "nt", BF16, tag + "_in")
    s = _swiglu_fwd(a, tag + "_act")
    f = _matmul(s, W[pre + "_w_out"], "nn", F32, tag + "_out")
    return _post_fwd(x, f, W[pre + "_norm_post"], 0.5, tag + "_post"), (x, h, a, s, f)


def _ffn_bwd(dxn, saved, W, pre, tag):
    x, h, a, s, f = saved
    df, dpost = _post_bwd(dxn, f, W[pre + "_norm_post"], 0.5, tag + "_dpost")
    ds = _matmul(df, W[pre + "_w_out"], "nt", BF16, tag + "_ds")
    dw_out = _matmul(s, df, "tn", BF16, tag + "_dwout")
    da = _swiglu_bwd(ds, a, tag + "_dact")
    dh = _matmul(da, W[pre + "_w_in"], "nn", F32, tag + "_dh")
    dw_in = _matmul(da, h, "tn", BF16, tag + "_dwin")
    dx, dpre = _pre_bwd(dh, x, W[pre + "_norm_pre"], dxn, tag + "_dpre")
    return dx, {pre + "_norm_pre": dpre, pre + "_norm_post": dpost, pre + "_w_in": dw_in, pre + "_w_out": dw_out}


def _mix_fwd(x, W, H, tag, gather=None):
    h = _rms_fwd(x, W["mix_norm_pre"], tag + "_pre")
    p = _matmul(h, W["mix_w_in"], "nt", F32, tag + "_in")
    qkvn = _qkv_conv_fwd(p, W["gdn_conv_w"], H, tag + "_qkvconv")
    (o, Tinv, Sin), gathered = _gdn_fwd(qkvn, p, W["gdn_a_log"], W["gdn_dt_bias"], H, tag + "_gdn",
                                        comm=None if gather is None else (_GatherPlan(gather), gather))
    og = _gdn_gate_fwd(o, p, W["gdn_norm_w"], tag + "_gdngate")
    ya = _matmul(og, W["gdn_w_o"], "nn", F32, tag + "_gdno")
    hc = _glu_fwd(p, W["cnv_pw1_b"], tag + "_glu")
    hcv = _dw_conv_fwd(hc, W["cnv_dw_w"], W["cnv_dw_b"], tag + "_dwconv")
    hl = _ln_silu_fwd(hcv, W["cnv_ln_g"], W["cnv_ln_b"], tag + "_ln")
    yb = _matmul(hl, W["cnv_w_o"], "nn", F32, tag + "_cnvo")
    ym = _merge_fwd(p, ya, yb, W["cnv_b_o"], tag + "_merge")
    y = _matmul(ym, W["mix_w_out"], "nn", F32, tag + "_out")
    xn = _post_fwd(x, y, W["mix_norm_post"], 1.0, tag + "_post")
    return xn, (x, h, p, qkvn, o, Tinv, Sin, og, ya, hc, hcv, hl, yb, ym, y), gathered


def _pair_adds(Gs, R1s, cidx, tag):
    return [_pair_add(G, R1, cidx, f"{tag}_pair_add{k}") for k, (G, R1) in enumerate(zip(Gs, R1s))]


def _mix_bwd(dxn, saved, W, H, tag, reduce=None):
    x, h, p, qkvn, o, Tinv, Sin, og, ya, hc, hcv, hl, yb, ym, y = saved
    g = {}
    dy, g["mix_norm_post"] = _post_bwd(dxn, y, W["mix_norm_post"], 1.0, tag + "_dpost")
    dym = _matmul(dy, W["mix_w_out"], "nt", F32, tag + "_dym")
    g["mix_w_out"] = _matmul(ym, dy, "tn", BF16, tag + "_dwout")
    dya, dyb, dgates, g["cnv_b_o"] = _merge_bwd(dym, p, ya, yb, W["cnv_b_o"], tag + "_dmerge")
    dhl = _matmul(dyb, W["cnv_w_o"], "nt", F32, tag + "_dhl")
    g["cnv_w_o"] = _matmul(hl, dyb, "tn", BF16, tag + "_dwcnvo")
    dhcv, g["cnv_ln_g"], g["cnv_ln_b"], g["cnv_dw_b"] = _ln_silu_bwd(dhl, hcv, W["cnv_ln_g"], W["cnv_ln_b"], tag + "_dln")
    chips = None
    if reduce is None:
        (dhc, g["cnv_dw_w"]), _ = _dw_conv_bwd(dhcv, hc, W["cnv_dw_w"], tag + "_ddwconv")
    else:
        Gs, cidx, rtag = reduce
        (dhc, g["cnv_dw_w"]), R1s = _dw_conv_bwd(dhcv, hc, W["cnv_dw_w"], tag + "_ddwconv", comm=(_SiblingPlan(Gs), Gs))
        Ps = _pair_adds(Gs, R1s, cidx, rtag)
        chips = (_ChipsPlan(Ps), Ps)
    dglu, g["cnv_pw1_b"] = _glu_bwd(dhc, p, W["cnv_pw1_b"], tag + "_dglu")
    dog = _matmul(dya, W["gdn_w_o"], "nt", F32, tag + "_dog")
    g["gdn_w_o"] = _matmul(og, dya, "tn", BF16, tag + "_dwgdno")
    do, dz, g["gdn_norm_w"] = _gdn_gate_bwd(dog, o, p, W["gdn_norm_w"], tag + "_dgdngate")
    (dqkvn, dba, g["gdn_a_log"], g["gdn_dt_bias"]), reduced = _gdn_bwd(
        do, qkvn, p, W["gdn_a_log"], W["gdn_dt_bias"], Tinv, Sin, H, tag + "_dgdn", comm=chips)
    dqkv, g["gdn_conv_w"] = _qkv_conv_bwd(dqkvn, p, W["gdn_conv_w"], H, tag + "_dqkvconv")
    dp = jnp.concatenate([dqkv, dz, dglu, dgates, dba], axis=1)
    dh = _matmul(dp, W["mix_w_in"], "nn", F32, tag + "_dh")
    g["mix_w_in"] = _matmul(dp, h, "tn", BF16, tag + "_dwin")
    dx, g["mix_norm_pre"] = _pre_bwd(dh, x, W["mix_norm_pre"], dxn, tag + "_dpre")
    return dx, g, reduced


def _trunk_fwd_bwd(x, tgt, H, L, weights_of, blocks_of=None, to_reduce=None):
    saved, Ws = [], []
    W = weights_of(0, None)
    for i in range(L):
        Ws.append(W)
        x, s1 = _ffn_fwd(x, W, "ffn1", f"l{i}_ffn1")
        nxt = blocks_of(i + 1) if (blocks_of is not None and i + 1 < L) else None
        x, s2, gathered = _mix_fwd(x, W, H, f"l{i}_mix", gather=nxt)
        x, s3 = _ffn_fwd(x, W, "ffn2", f"l{i}_ffn2")
        saved.append((s1, s2, s3))
        if i + 1 < L:
            W = weights_of(i + 1, gathered)
    dx, loss = _loss_fwd_bwd(x, tgt, "loss")
    grads, reduced, pending = [None] * L, [None] * L, None
    for i in reversed(range(L)):
        s1, s2, s3 = saved[i]
        dx, g3 = _ffn_bwd(dx, s3, Ws[i], "ffn2", f"l{i}_ffn2")
        dx, g2, red = _mix_bwd(dx, s2, Ws[i], H, f"l{i}_mix", reduce=pending)
        if pending is not None:
            reduced[i + 1] = red
        dx, g1 = _ffn_bwd(dx, s1, Ws[i], "ffn1", f"l{i}_ffn1")
        grads[i] = {**g1, **g2, **g3}
        pending = to_reduce(i, grads[i]) if (to_reduce is not None and i > 0) else None
    return loss, dx, grads, reduced


HBM_SPEC = pl.BlockSpec(memory_space=pltpu.HBM)


def _coords():
    return lax.axis_index("x"), lax.axis_index("y"), lax.axis_index("c")


class _GatherPlan:
    has_middle = True

    def __init__(self, shards):
        self.n = len(shards)
        self.out_shape = [jax.ShapeDtypeStruct((N_DEV,) + s.shape, s.dtype) for s in shards]
        self.sems = [pltpu.SemaphoreType.DMA((self.n, 7)), pltpu.SemaphoreType.DMA((self.n, 7)),
                     pltpu.SemaphoreType.DMA((self.n,))]

    def _parts(self, ins, outs, sems):
        send_sems, recv_sems, local_sems = sems
        x, y, c = _coords()
        me, sibling = (x, y, c), (x, y, 1 - c)
        chips = [(1 - x, y), (x, 1 - y), (1 - x, 1 - y)]

        def copy(w, k, block, to, src=None):
            dst = outs[w].at[4 * block[0] + 2 * block[1] + block[2]]
            return pltpu.make_async_remote_copy(
                src_ref=dst if src is None else src, dst_ref=dst, send_sem=send_sems.at[w, k],
                recv_sem=recv_sems.at[w, k], device_id=to, device_id_type=MESH)

        mine = [pltpu.make_async_copy(ins[w], outs[w].at[4 * x + 2 * y + c], local_sems.at[w]) for w in range(self.n)]
        first = []
        for w in range(self.n):
            first.append(copy(w, 0, me, sibling, src=ins[w]))
            first += [copy(w, 1 + j, me, (*chip, c), src=ins[w]) for j, chip in enumerate(chips)]
        passed = [copy(w, 4 + j, (*chip, c), sibling) for j, chip in enumerate(chips) for w in range(self.n)]
        return copy, mine, first, passed, chips, me, sibling, c

    def begin(self, ins, outs, sems):
        _, mine, first, _, _, _, _, _ = self._parts(ins, outs, sems)
        for cp in mine + first:
            cp.start()

    def middle(self, ins, outs, sems):
        copy, _, _, passed, chips, me, _, c = self._parts(ins, outs, sems)
        for j, chip in enumerate(chips):
            for w in range(self.n):
                copy(w, 1 + j, (*chip, c), me).wait_recv()
                passed[j * self.n + w].start()

    def finish(self, ins, outs, sems):
        copy, mine, first, passed, chips, me, sibling, c = self._parts(ins, outs, sems)
        for w in range(self.n):
            copy(w, 0, sibling, me).wait_recv()
            for j, chip in enumerate(chips):
                copy(w, 4 + j, (*chip, 1 - c), me).wait_recv()
        for cp in first + passed:
            cp.wait_send()
        for cp in mine:
            cp.wait()


class _SiblingPlan:
    has_middle = False

    def __init__(self, Gs):
        self.n = len(Gs)
        self.out_shape = [jax.ShapeDtypeStruct((4,) + g.shape[1:], g.dtype) for g in Gs]
        self.sems = [pltpu.SemaphoreType.DMA((self.n, 4)), pltpu.SemaphoreType.DMA((self.n, 4))]

    def _copies(self, ins, outs, sems):
        send_sems, recv_sems = sems
        x, y, c = _coords()
        return [pltpu.make_async_remote_copy(
            src_ref=ins[w].at[2 * q + (1 - c)], dst_ref=outs[w].at[q], send_sem=send_sems.at[w, q],
            recv_sem=recv_sems.at[w, q], device_id=(x, y, 1 - c), device_id_type=MESH)
            for w in range(self.n) for q in range(4)]

    def begin(self, ins, outs, sems):
        for cp in self._copies(ins, outs, sems):
            cp.start()

    def finish(self, ins, outs, sems):
        for cp in self._copies(ins, outs, sems):
            cp.wait()


class _ChipsPlan:
    has_middle = False

    def __init__(self, Ps):
        self.n = len(Ps)
        self.out_shape = [jax.ShapeDtypeStruct(p.shape, p.dtype) for p in Ps]
        self.sems = [pltpu.SemaphoreType.DMA((self.n, 3)), pltpu.SemaphoreType.DMA((self.n, 3)),
                     pltpu.SemaphoreType.DMA((self.n,))]

    def _copies(self, ins, outs, sems):
        send_sems, recv_sems, local_sems = sems
        x, y, c = _coords()
        me_q = 2 * x + y
        cps = []
        for w in range(self.n):
            cps.append(pltpu.make_async_copy(ins[w].at[me_q], outs[w].at[me_q], local_sems.at[w]))
            for j, (px, py) in enumerate([(1 - x, y), (x, 1 - y), (1 - x, 1 - y)]):
                cps.append(pltpu.make_async_remote_copy(
                    src_ref=ins[w].at[2 * px + py], dst_ref=outs[w].at[me_q], send_sem=send_sems.at[w, j],
                    recv_sem=recv_sems.at[w, j], device_id=(px, py, c), device_id_type=MESH))
        return cps

    def begin(self, ins, outs, sems):
        for cp in self._copies(ins, outs, sems):
            cp.start()

    def finish(self, ins, outs, sems):
        for cp in self._copies(ins, outs, sems):
            cp.wait()


def _comm_only(plan, arrays, name):
    n = plan.n

    def body(*refs):
        ins, outs, sems = refs[:n], refs[n:2 * n], refs[2 * n:]
        plan.begin(ins, outs, sems)
        if plan.has_middle:
            plan.middle(ins, outs, sems)
        plan.finish(ins, outs, sems)

    return pl.pallas_call(
        body, name=name, out_shape=plan.out_shape, in_specs=[HBM_SPEC] * n, out_specs=[HBM_SPEC] * n,
        scratch_shapes=plan.sems,
    )(*arrays)


def _comm_call(body, comm, *, name, grid, in_specs, out_specs, out_shape, scratch_shapes, args):
    if comm is None:
        res = pl.pallas_call(body, name=name, grid=grid, in_specs=in_specs, out_specs=out_specs, out_shape=out_shape,
                             scratch_shapes=scratch_shapes, compiler_params=_cparams(("arbitrary",)))(*args)
        return res, None
    plan, arrays = comm
    n_in, n_out, n_scr, n = len(in_specs), len(out_specs), len(scratch_shapes), plan.n
    steps = grid[0]

    def kern(*refs):
        ins, cins = refs[:n_in], refs[n_in:n_in + n]
        outs, couts = refs[n_in + n:n_in + n + n_out], refs[n_in + n + n_out:n_in + 2 * n + n_out]
        scr, csems = refs[n_in + 2 * n + n_out:n_in + 2 * n + n_out + n_scr], refs[n_in + 2 * n + n_out + n_scr:]
        step = pl.program_id(0)

        @pl.when(step == 0)
        def _():
            plan.begin(cins, couts, csems)

        body(*ins, *outs, *scr)
        if plan.has_middle:
            @pl.when(step == (3 * steps) // 4)
            def _():
                plan.middle(cins, couts, csems)

        @pl.when(step == steps - 1)
        def _():
            plan.finish(cins, couts, csems)

    res = pl.pallas_call(
        kern, name=name, grid=grid, in_specs=list(in_specs) + [HBM_SPEC] * n,
        out_specs=list(out_specs) + [HBM_SPEC] * n, out_shape=list(out_shape) + plan.out_shape,
        scratch_shapes=list(scratch_shapes) + plan.sems, compiler_params=_cparams(("arbitrary",)),
    )(*args, *arrays)
    return res[:n_out], res[n_out:]


def _row_tile(R, target=256):
    best = None
    for t in range(8, min(R, target) + 1, 8):
        if R % t == 0:
            best = t
    return best if best is not None else R


def _pair_add(G, R1, cidx, name):
    _, R, C = G.shape
    tb = _row_tile(R)

    def body(c_ref, g_ref, r_ref, o_ref):
        o_ref[...] = (g_ref[...].astype(F32) + r_ref[...].astype(F32)).astype(BF16)

    return pl.pallas_call(
        body, name=name,
        grid_spec=pltpu.PrefetchScalarGridSpec(
            num_scalar_prefetch=1, grid=(4, R // tb),
            in_specs=[pl.BlockSpec((None, tb, C), lambda q, i, cr: (2 * q + cr[0], i, 0)),
                      pl.BlockSpec((None, tb, C), lambda q, i, cr: (q, i, 0))],
            out_specs=pl.BlockSpec((None, tb, C), lambda q, i, cr: (q, i, 0))),
        out_shape=jax.ShapeDtypeStruct((4, R, C), BF16),
        compiler_params=_cparams(("arbitrary", "arbitrary")),
    )(cidx, G, R1)


def _sum_parts(parts, name):
    P, R, C = parts.shape

    def body(p_ref, o_ref):
        acc = p_ref[0]
        for j in range(1, P):
            acc = acc + p_ref[j]
        o_ref[...] = acc

    return pl.pallas_call(
        body, name=name, out_shape=jax.ShapeDtypeStruct((R, C), F32),
        in_specs=[pl.BlockSpec(memory_space=pltpu.VMEM)], out_specs=pl.BlockSpec(memory_space=pltpu.VMEM),
        compiler_params=_cparams(),
    )(parts)


def _adamw(w, m, v, parts, name):
    G, R, C = w.shape
    P = parts[0].shape[0]
    tb = _row_tile(R)
    nb = R // tb
    c1 = 1.0 / (1.0 - ADAM_B1 ** ADAM_STEP)
    c2 = 1.0 / (1.0 - ADAM_B2 ** ADAM_STEP)

    def body(w_ref, m_ref, v_ref, *rest):
        p_refs, (g_ref, d_ref, nm_ref, nv_ref) = rest[:G], rest[G:]
        l = pl.program_id(0)
        g = None
        for k in range(G):
            gk = p_refs[k][0].astype(F32)
            for j in range(1, P):
                gk = gk + p_refs[k][j].astype(F32)
            g = gk if g is None else jnp.where(l == k, gk, g)
        nm = ADAM_B1 * m_ref[...] + (1.0 - ADAM_B1) * g
        nv = ADAM_B2 * v_ref[...] + (1.0 - ADAM_B2) * (g * g)
        g_ref[...] = g
        nm_ref[...] = nm
        nv_ref[...] = nv
        d_ref[...] = -ADAM_LR * ((nm * c1) / (jnp.sqrt(nv * c2) + ADAM_EPS) + ADAM_WD * w_ref[...])

    blk = pl.BlockSpec((None, tb, C), lambda l, i: (l, i, 0))

    def part_spec(k):
        return pl.BlockSpec((P, tb, C), lambda l, i: (0, jnp.where(l < k, 0, jnp.where(l > k, nb - 1, i)), 0))

    return pl.pallas_call(
        body, name=name, grid=(G, nb),
        in_specs=[blk, blk, blk] + [part_spec(k) for k in range(G)],
        out_specs=[blk] * 4, out_shape=[jax.ShapeDtypeStruct((G, R, C), F32)] * 4,
        compiler_params=_cparams(("arbitrary", "arbitrary")),
    )(w, m, v, *parts)


BIG = ("ffn1_w_in", "ffn1_w_out", "mix_w_in", "gdn_w_o", "cnv_w_o", "mix_w_out", "ffn2_w_in", "ffn2_w_out")
COL_SHARDED = ("ffn1_w_in", "mix_w_in", "ffn2_w_in")
SMALL_SHARDED = ("gdn_conv_w", "cnv_dw_w")
NAMES = ("ffn1_norm_pre", "ffn1_norm_post", "ffn1_w_in", "ffn1_w_out", "mix_norm_pre", "mix_norm_post", "mix_w_in",
         "gdn_conv_w", "gdn_a_log", "gdn_dt_bias", "gdn_norm_w", "gdn_w_o", "cnv_pw1_b", "cnv_dw_w", "cnv_dw_b",
         "cnv_ln_g", "cnv_ln_b", "cnv_w_o", "cnv_b_o", "mix_w_out", "ffn2_norm_pre", "ffn2_norm_post", "ffn2_w_in",
         "ffn2_w_out")
SMALL = tuple(n for n in NAMES if n not in BIG)


def _gathered_layer_weights(gath, params, i, D, H):
    W = {}
    for n in BIG:
        g = gath[n]
        g = g.reshape(-1, g.shape[-1])
        W[n] = _mix_in_reorder(g, D, H) if n == "mix_w_in" else g
    for n in SMALL_SHARDED:
        g = gath[n]
        W[n] = jnp.transpose(g, (1, 0, 2)).reshape(g.shape[1], -1)
    for n in SMALL:
        if n in SMALL_SHARDED:
            continue
        v = params[n][i]
        if n in ("gdn_a_log", "gdn_dt_bias"):
            v = jnp.pad(v, (0, LANES - H))
        W[n] = v.reshape(1, -1)
    return W


def kernel(x, ffn1_norm_pre, ffn1_norm_post, ffn1_w_in, ffn1_w_out, mix_norm_pre, mix_norm_post, mix_w_in, gdn_conv_w, gdn_a_log, gdn_dt_bias, gdn_norm_w, gdn_w_o, cnv_pw1_b, cnv_dw_w, cnv_dw_b, cnv_ln_g, cnv_ln_b, cnv_w_o, cnv_b_o, mix_w_out, ffn2_norm_pre, ffn2_norm_post, ffn2_w_in, ffn2_w_out, loss_target, m_ffn1_norm_pre, m_ffn1_norm_post, m_ffn1_w_in, m_ffn1_w_out, m_mix_norm_pre, m_mix_norm_post, m_mix_w_in, m_gdn_conv_w, m_gdn_a_log, m_gdn_dt_bias, m_gdn_norm_w, m_gdn_w_o, m_cnv_pw1_b, m_cnv_dw_w, m_cnv_dw_b, m_cnv_ln_g, m_cnv_ln_b, m_cnv_w_o, m_cnv_b_o, m_mix_w_out, m_ffn2_norm_pre, m_ffn2_norm_post, m_ffn2_w_in, m_ffn2_w_out, v_ffn1_norm_pre, v_ffn1_norm_post, v_ffn1_w_in, v_ffn1_w_out, v_mix_norm_pre, v_mix_norm_post, v_mix_w_in, v_gdn_conv_w, v_gdn_a_log, v_gdn_dt_bias, v_gdn_norm_w, v_gdn_w_o, v_cnv_pw1_b, v_cnv_dw_w, v_cnv_dw_b, v_cnv_ln_g, v_cnv_ln_b, v_cnv_w_o, v_cnv_b_o, v_mix_w_out, v_ffn2_norm_pre, v_ffn2_norm_post, v_ffn2_w_in, v_ffn2_w_out):
    params = dict(zip(NAMES, (ffn1_norm_pre, ffn1_norm_post, ffn1_w_in, ffn1_w_out, mix_norm_pre, mix_norm_post, mix_w_in, gdn_conv_w, gdn_a_log, gdn_dt_bias, gdn_norm_w, gdn_w_o, cnv_pw1_b, cnv_dw_w, cnv_dw_b, cnv_ln_g, cnv_ln_b, cnv_w_o, cnv_b_o, mix_w_out, ffn2_norm_pre, ffn2_norm_post, ffn2_w_in, ffn2_w_out)))
    mom1 = dict(zip(NAMES, (m_ffn1_norm_pre, m_ffn1_norm_post, m_ffn1_w_in, m_ffn1_w_out, m_mix_norm_pre, m_mix_norm_post, m_mix_w_in, m_gdn_conv_w, m_gdn_a_log, m_gdn_dt_bias, m_gdn_norm_w, m_gdn_w_o, m_cnv_pw1_b, m_cnv_dw_w, m_cnv_dw_b, m_cnv_ln_g, m_cnv_ln_b, m_cnv_w_o, m_cnv_b_o, m_mix_w_out, m_ffn2_norm_pre, m_ffn2_norm_post, m_ffn2_w_in, m_ffn2_w_out)))
    mom2 = dict(zip(NAMES, (v_ffn1_norm_pre, v_ffn1_norm_post, v_ffn1_w_in, v_ffn1_w_out, v_mix_norm_pre, v_mix_norm_post, v_mix_w_in, v_gdn_conv_w, v_gdn_a_log, v_gdn_dt_bias, v_gdn_norm_w, v_gdn_w_o, v_cnv_pw1_b, v_cnv_dw_w, v_cnv_dw_b, v_cnv_ln_g, v_cnv_ln_b, v_cnv_w_o, v_cnv_b_o, v_mix_w_out, v_ffn2_norm_pre, v_ffn2_norm_post, v_ffn2_w_in, v_ffn2_w_out)))
    T, D = x.shape[1], x.shape[2]
    H = D // GDN_DK
    L = ffn1_norm_pre.shape[0]
    xi, yi, ci = _coords()
    dev = 4 * xi + 2 * yi + ci

    ag_names = BIG + SMALL_SHARDED

    def shard_to_send(n):
        if n in COL_SHARDED:
            return jnp.swapaxes(params[n], 1, 2).astype(BF16)
        return params[n].astype(BF16) if n in BIG else params[n]

    send = {n: shard_to_send(n) for n in ag_names}

    def blocks_of(i):
        return [send[n][i] for n in ag_names]

    first = blocks_of(0)
    gathered0 = _comm_only(_GatherPlan(first), first, "ag_weights_l0")

    def weights_of(i, gathered):
        return _gathered_layer_weights(dict(zip(ag_names, gathered0 if i == 0 else gathered)), params, i, D, H)

    cidx = jnp.reshape(ci, (1,)).astype(jnp.int32)

    def to_reduce(i, grads_i):
        Gs = []
        for n in BIG:
            g = grads_i[n]
            if n == "mix_w_in":
                g = _mix_in_restore(g, D, H)
            Gs.append(g.reshape(N_DEV, -1, g.shape[-1]))
        return Gs, cidx, f"l{i}"

    loss_row, dx, grads, reduced = _trunk_fwd_bwd(x[0], loss_target[0], H, L, weights_of, blocks_of, to_reduce)
    loss = lax.psum(loss_row[0, 0], ("x", "y", "c"))

    Gs, _, _ = to_reduce(0, grads[0])
    R1s = _comm_only(_SiblingPlan(Gs), Gs, "rs_sibling_l0")
    Ps = _pair_adds(Gs, R1s, cidx, "l0")
    reduced[0] = _comm_only(_ChipsPlan(Ps), Ps, "rs_chips_l0")
    R2s = {n: [jnp.swapaxes(reduced[i][k], 1, 2) if n in COL_SHARDED else reduced[i][k] for i in range(L)]
           for k, n in enumerate(BIG)}

    pieces = []
    for i in range(L):
        for n in SMALL:
            piece = grads[i][n].reshape(-1, LANES)
            pieces.append(jnp.pad(piece, ((0, (-piece.shape[0]) % 8), (0, 0))))
    packed = jnp.concatenate(pieces, axis=0)
    small_all = _comm_only(_GatherPlan([packed]), [packed], "ag_small_grads")[0]
    small_sum = _sum_parts(small_all, "sum_small_grads")
    small_g = {n: [None] * L for n in SMALL}
    off = 0
    for i in range(L):
        for n in SMALL:
            shape = grads[i][n].shape
            cnt = shape[0] * shape[1] // LANES
            g = small_sum[off:off + cnt].reshape(shape)
            off += cnt + (-cnt) % 8
            if n in ("gdn_a_log", "gdn_dt_bias"):
                g = g[:, :H]
            if n in SMALL_SHARDED:
                wloc = params[n].shape[-1]
                g = lax.dynamic_slice_in_dim(g, dev * wloc, wloc, axis=1)
            small_g[n][i] = g

    outs = {}
    for n in NAMES:
        w, m, v = params[n], mom1[n], mom2[n]
        if n in BIG:
            shape3, parts = w.shape, R2s[n]
        else:
            rows, cols = (w.shape[0] * w.shape[1], w.shape[2]) if w.ndim == 3 else w.shape
            shape3, parts = (1, rows, cols), [jnp.stack(small_g[n], axis=0).reshape(1, rows, cols)]
        res = _adamw(w.reshape(shape3), m.reshape(shape3), v.reshape(shape3), parts, "adamw_" + n)
        outs[n] = [r.reshape(w.shape) for r in res]

    result = [loss, dx[None]]
    for k in range(4):
        result += [outs[n][k] for n in NAMES]
    return tuple(result)
```

```python
import jax
import jax.numpy as jnp
from jax import lax
from jax.experimental import pallas as pl
from jax.experimental.pallas import tpu as pltpu

F32 = jnp.float32
BF16 = jnp.bfloat16

GDN_DK = 128
CHUNK = 64
GDN_CONV = 4
CNV_K = 31
RMS_EPS = 1e-6
LN_EPS = 1e-5
L2_EPS = 1e-6
ADAM_LR = 0.001
ADAM_B1 = 0.9
ADAM_B2 = 0.999
ADAM_EPS = 1e-08
ADAM_WD = 0.01
ADAM_STEP = 10

LANES = 128
SUB = 16
VMEM_LIMIT = 56 * 1024 * 1024
N_DEV = 8
MESH = pl.DeviceIdType.MESH


def _cparams(sem=None, **kw):
    if sem is not None:
        kw["dimension_semantics"] = sem
    return pltpu.CompilerParams(vmem_limit_bytes=VMEM_LIMIT, **kw)


def _tile(dim, target):
    best = None
    for t in range(LANES, min(dim, target) + 1, LANES):
        if dim % t == 0:
            best = t
    return best if best is not None else dim


def _sigmoid(x):
    return 1.0 / (1.0 + jnp.exp(-x))


def _silu(x):
    return x * _sigmoid(x)


def _dsilu(x):
    s = _sigmoid(x)
    return s * (1.0 + x * (1.0 - s))


MM_VMEM_BUDGET = 40 * 1024 * 1024
MM_MAX_TILE = 2048


def _mm_tiles(M, N, K, out_bytes):
    def cands(dim):
        c = [t for t in range(LANES, min(dim, MM_MAX_TILE) + 1, LANES) if dim % t == 0]
        return c or [dim]
    best = None
    for tm in cands(M):
        for tn in cands(N):
            vm = 2 * (2 * K * (tm + tn) + tm * tn * out_bytes)
            if vm <= MM_VMEM_BUDGET and (best is None or tm * tn > best[0] * best[1]):
                best = (tm, tn)
    return best if best is not None else (cands(M)[0], cands(N)[0])


def _matmul(a, b, mode, out_dtype, name, hosts=None):
    if mode == "nn":
        (M, K), N = a.shape, b.shape[1]
    elif mode == "nt":
        (M, K), N = a.shape, b.shape[0]
    else:
        (K, M), N = a.shape, b.shape[1]
    tm, tn = _mm_tiles(M, N, K, jnp.dtype(out_dtype).itemsize)
    if mode == "nn":
        a_spec = pl.BlockSpec((tm, K), lambda j, i: (i, 0))
        b_spec = pl.BlockSpec((K, tn), lambda j, i: (0, j))
        dn = (((1,), (0,)), ((), ()))
    elif mode == "nt":
        a_spec = pl.BlockSpec((tm, K), lambda j, i: (i, 0))
        b_spec = pl.BlockSpec((tn, K), lambda j, i: (j, 0))
        dn = (((1,), (1,)), ((), ()))
    else:
        a_spec = pl.BlockSpec((K, tm), lambda j, i: (0, i))
        b_spec = pl.BlockSpec((K, tn), lambda j, i: (0, j))
        dn = (((0,), (0,)), ((), ()))

    def body(a_ref, b_ref, o_ref):
        o_ref[...] = lax.dot_general(a_ref[...], b_ref[...], dn, preferred_element_type=F32).astype(out_dtype)

    return _hosted_call(
        body, hosts, name=name, grid=(N // tn, M // tm), in_specs=[a_spec, b_spec],
        out_specs=[pl.BlockSpec((tm, tn), lambda j, i: (i, j))],
        out_shape=[jax.ShapeDtypeStruct((M, N), out_dtype)], scratch_shapes=[],
        args=(a, b), sem=("parallel", "parallel"))[0]


def _rowcall(name, body, T, tb, row_ins, par_ins, row_outs, acc_outs, hosts=None):
    n_ri, n_pi, n_ro = len(row_ins), len(par_ins), len(row_outs)

    def kern(*refs):
        ri, pi = refs[:n_ri], refs[n_ri:n_ri + n_pi]
        ro, ao = refs[n_ri + n_pi:n_ri + n_pi + n_ro], refs[n_ri + n_pi + n_ro:]
        if ao:
            @pl.when(pl.program_id(0) == 0)
            def _():
                for r in ao:
                    r[...] = jnp.zeros_like(r)
        body(ri, pi, ro, ao)

    in_specs = [pl.BlockSpec((tb, w), lambda i, cb=cb: (i, cb)) for (_, w, cb) in row_ins]
    in_specs += [pl.BlockSpec(p.shape, lambda i: (0, 0)) for p in par_ins]
    out_specs = [pl.BlockSpec((tb, w), lambda i: (i, 0)) for (w, _) in row_outs]
    out_specs += [pl.BlockSpec((1, w), lambda i: (0, 0)) for w in acc_outs]
    out_shape = [jax.ShapeDtypeStruct((T, w), dt) for (w, dt) in row_outs]
    out_shape += [jax.ShapeDtypeStruct((1, w), F32) for w in acc_outs]
    return _hosted_call(
        kern, hosts, name=name, grid=(T // tb,), in_specs=in_specs, out_specs=out_specs, out_shape=out_shape,
        scratch_shapes=[], args=(*[a for (a, _, _) in row_ins], *par_ins), sem=("arbitrary",))


def _rsum(x):
    return jnp.sum(x, axis=0, keepdims=True)


def _rms_rstd(x):
    return lax.rsqrt(jnp.mean(x * x, axis=-1, keepdims=True) + RMS_EPS)


def _rms_fwd(x, w, name):
    T, D = x.shape

    def body(ri, pi, ro, ao):
        xv = ri[0][...]
        ro[0][...] = (xv * _rms_rstd(xv) * pi[0][...]).astype(BF16)

    return _rowcall(name, body, T, 256, [(x, D, 0)], [w], [(D, BF16)], [])[0]


def _rms_bwd_core(dy, x, w):
    rs = _rms_rstd(x)
    xh = x * rs
    gw = dy * w
    dx = rs * (gw - xh * jnp.mean(gw * xh, axis=-1, keepdims=True))
    return dx, dy * xh


def _pre_bwd(dh, x, w, dres, name, hosts=None):
    T, D = x.shape

    def body(ri, pi, ro, ao):
        dx, dwc = _rms_bwd_core(ri[0][...], ri[1][...], pi[0][...])
        ro[0][...] = ri[2][...] + dx
        ao[0][...] += _rsum(dwc)

    return _rowcall(name, body, T, 256, [(dh, D, 0), (x, D, 0), (dres, D, 0)], [w], [(D, F32)], [D], hosts=hosts)


def _post_fwd(x, f, w, r, name):
    T, D = x.shape

    def body(ri, pi, ro, ao):
        fv = ri[1][...]
        ro[0][...] = ri[0][...] + r * (fv * _rms_rstd(fv) * pi[0][...])

    return _rowcall(name, body, T, 256, [(x, D, 0), (f, D, 0)], [w], [(D, F32)], [])[0]


def _post_bwd(dxn, f, w, r, name):
    T, D = f.shape

    def body(ri, pi, ro, ao):
        df, dwc = _rms_bwd_core(r * ri[0][...], ri[1][...], pi[0][...])
        ro[0][...] = df.astype(BF16)
        ao[0][...] += _rsum(dwc)

    return _rowcall(name, body, T, 256, [(dxn, D, 0), (f, D, 0)], [w], [(D, BF16)], [D])


def _swiglu_fwd(a, name):
    T, F2 = a.shape
    F = F2 // 2

    def body(ri, pi, ro, ao):
        ro[0][...] = (_silu(ri[0][...].astype(F32)) * ri[1][...].astype(F32)).astype(BF16)

    return _rowcall(name, body, T, 256, [(a, F, 0), (a, F, 1)], [], [(F, BF16)], [])[0]


def _swiglu_bwd(ds, a, name, hosts=None):
    T, F2 = a.shape
    F = F2 // 2

    def body(ri, pi, ro, ao):
        dsv, g, u = ri[0][...].astype(F32), ri[1][...].astype(F32), ri[2][...].astype(F32)
        ro[0][:, :F] = (dsv * u * _dsilu(g)).astype(BF16)
        ro[0][:, F:] = (dsv * _silu(g)).astype(BF16)

    return _rowcall(name, body, T, 256, [(ds, F, 0), (a, F, 0), (a, F, 1)], [], [(F2, BF16)], [], hosts=hosts)[0]


def _loss_fwd_bwd(y, tgt, name):
    T, D = y.shape

    def body(ri, pi, ro, ao):
        e = ri[0][...] - ri[1][...]
        ro[0][...] = e * (1.0 / D)
        tot = jnp.sum(_rsum(e * e), axis=1, keepdims=True) * (0.5 / D)
        ao[0][...] += jnp.broadcast_to(tot, (1, LANES))

    return _rowcall(name, body, T, 256, [(y, D, 0), (tgt, D, 0)], [], [(D, F32)], [LANES])


def _gdn_gate_fwd(o, p, nw, name):
    T, D = o.shape
    H = D // GDN_DK

    def body(ri, pi, ro, ao):
        for h in range(H):
            sl = slice(h * GDN_DK, (h + 1) * GDN_DK)
            oh = ri[0][:, sl]
            ro[0][:, sl] = (oh * _rms_rstd(oh) * pi[0][...] * _silu(ri[1][:, sl])).astype(BF16)

    return _rowcall(name, body, T, 256, [(o, D, 0), (p, D, 3)], [nw], [(D, BF16)], [])[0]


def _gdn_gate_bwd(dog, o, p, nw, name, hosts=None):
    T, D = o.shape
    H = D // GDN_DK

    def body(ri, pi, ro, ao):
        acc = jnp.zeros((1, GDN_DK), F32)
        for h in range(H):
            sl = slice(h * GDN_DK, (h + 1) * GDN_DK)
            dy, oh, z = ri[0][:, sl], ri[1][:, sl], ri[2][:, sl]
            sz = _silu(z)
            do, dwc = _rms_bwd_core(dy * sz, oh, pi[0][...])
            ro[0][:, sl] = do
            ro[1][:, sl] = (dy * oh * _rms_rstd(oh) * pi[0][...] * _dsilu(z)).astype(BF16)
            acc = acc + _rsum(dwc)
        ao[0][...] += acc

    return _rowcall(name, body, T, 256, [(dog, D, 0), (o, D, 0), (p, D, 3)], [nw], [(D, F32), (D, BF16)], [GDN_DK],
                    hosts=hosts)


def _glu_fwd(p, b, name):
    T = p.shape[0]
    D = b.shape[1] // 2

    def body(ri, pi, ro, ao):
        ro[0][...] = (ri[0][...] + pi[0][:, :D]) * _sigmoid(ri[1][...] + pi[0][:, D:])

    return _rowcall(name, body, T, 256, [(p, D, 4), (p, D, 5)], [b], [(D, F32)], [])[0]


def _glu_bwd(dhc, p, b, name):
    T = p.shape[0]
    D = b.shape[1] // 2

    def body(ri, pi, ro, ao):
        d, a, g = ri[0][...], ri[1][...] + pi[0][:, :D], ri[2][...] + pi[0][:, D:]
        sg = _sigmoid(g)
        da, dg = d * sg, d * a * sg * (1.0 - sg)
        ro[0][:, :D] = da.astype(BF16)
        ro[0][:, D:] = dg.astype(BF16)
        ao[0][:, :D] += _rsum(da)
        ao[0][:, D:] += _rsum(dg)

    return _rowcall(name, body, T, 256, [(dhc, D, 0), (p, D, 4), (p, D, 5)], [b], [(2 * D, BF16)], [2 * D])


def _ln_stats(x):
    mu = jnp.mean(x, axis=-1, keepdims=True)
    xc = x - mu
    rstd = lax.rsqrt(jnp.mean(xc * xc, axis=-1, keepdims=True) + LN_EPS)
    return xc * rstd, rstd


def _ln_silu_fwd(hcv, g, b, name):
    T, D = hcv.shape

    def body(ri, pi, ro, ao):
        xh, _ = _ln_stats(ri[0][...])
        ro[0][...] = _silu(xh * pi[0][...] + pi[1][...]).astype(BF16)

    return _rowcall(name, body, T, 256, [(hcv, D, 0)], [g, b], [(D, BF16)], [])[0]


def _ln_silu_bwd(dhl, hcv, g, b, name, hosts=None):
    T, D = hcv.shape

    def body(ri, pi, ro, ao):
        xh, rstd = _ln_stats(ri[1][...])
        dyl = ri[0][...] * _dsilu(xh * pi[0][...] + pi[1][...])
        dxh = dyl * pi[0][...]
        dx = rstd * (dxh - jnp.mean(dxh, axis=-1, keepdims=True) - xh * jnp.mean(dxh * xh, axis=-1, keepdims=True))
        ro[0][...] = dx
        ao[0][...] += _rsum(dyl * xh)
        ao[1][...] += _rsum(dyl)
        ao[2][...] += _rsum(dx)

    return _rowcall(name, body, T, 256, [(dhl, D, 0), (hcv, D, 0)], [g, b], [(D, F32)], [D, D, D], hosts=hosts)


def _merge_fwd(p, ya, yb, bo, name):
    T, D = ya.shape

    def body(ri, pi, ro, ao):
        ro[0][...] = (_sigmoid(ri[0][...]) * ri[2][...] + _sigmoid(ri[1][...]) * (ri[3][...] + pi[0][...])).astype(BF16)

    return _rowcall(name, body, T, 256, [(p, D, 6), (p, D, 7), (ya, D, 0), (yb, D, 0)], [bo], [(D, BF16)], [])[0]


def _merge_bwd(dym, p, ya, yb, bo, name, hosts=None):
    T, D = ya.shape

    def body(ri, pi, ro, ao):
        d = ri[0][...]
        ga, gb = _sigmoid(ri[1][...]), _sigmoid(ri[2][...])
        ybv = ri[4][...] + pi[0][...]
        dyb = d * gb
        ro[0][...] = (d * ga).astype(BF16)
        ro[1][...] = dyb.astype(BF16)
        ro[2][:, :D] = (d * ri[3][...] * ga * (1.0 - ga)).astype(BF16)
        ro[2][:, D:] = (d * ybv * gb * (1.0 - gb)).astype(BF16)
        ao[0][...] += _rsum(dyb)

    return _rowcall(name, body, T, 256, [(dym, D, 0), (p, D, 6), (p, D, 7), (ya, D, 0), (yb, D, 0)], [bo],
                    [(D, BF16), (D, BF16), (2 * D, BF16)], [D], hosts=hosts)


PAD = 32
RC = 256


def _tap_windows(ref, offs):
    groups = {}
    for j, o in enumerate(offs):
        groups.setdefault(o % 8, []).append((j, o))
    for grp in groups.values():
        lo, hi = min(o for _, o in grp), max(o for _, o in grp)
        win = ref[pl.ds(lo, RC + hi - lo), :]
        for j, o in grp:
            yield j, win[o - lo:o - lo + RC]


def _causal_taps(xp_ref, w, K, c0):
    acc = None
    for j, xs in _tap_windows(xp_ref, [PAD - (K - 1) + j + c0 for j in range(K)]):
        term = w[j:j + 1, :] * xs
        acc = term if acc is None else acc + term
    return acc


def _anticausal_taps(dp_ref, w, K, c0):
    acc = None
    for j, ds in _tap_windows(dp_ref, [(K - 1) - j + c0 for j in range(K)]):
        term = w[j:j + 1, :] * ds
        acc = term if acc is None else acc + term
    return acc


def _tap_grads(dw_ref, dc_ref, xp_ref, K, T):
    accs = [jnp.zeros((8, LANES), F32) for _ in range(K)]
    for c in range(T // RC):
        d = dc_ref[pl.ds(c * RC, RC), :]
        for j, xs in _tap_windows(xp_ref, [PAD - (K - 1) + j + c * RC for j in range(K)]):
            accs[j] = accs[j] + jnp.sum((d * xs).reshape(RC // 8, 8, LANES), axis=0)
    for j in range(K):
        dw_ref[j:j + 1, :] = _rsum(accs[j])


def _qkv_conv_fwd(p, cw, H, name, hosts=None):
    T = p.shape[0]
    K = cw.shape[0]

    def body(x_ref, w_ref, o_ref, xp_ref):
        j = pl.program_id(0)
        xp_ref[pl.ds(0, PAD), :] = jnp.zeros((PAD, LANES), F32)
        xp_ref[pl.ds(PAD, T), :] = x_ref[...]
        w = w_ref[...]
        scale = jnp.where(j < H, GDN_DK ** -0.5, 1.0).astype(F32)
        for c in range(T // RC):
            act = _silu(_causal_taps(xp_ref, w, K, c * RC))
            nrm = act * lax.rsqrt(jnp.sum(act * act, axis=-1, keepdims=True) + L2_EPS) * scale
            o_ref[pl.ds(c * RC, RC), :] = jnp.where(j < 2 * H, nrm, act)

    return _hosted_call(
        body, hosts, name=name, grid=(3 * H,),
        in_specs=[pl.BlockSpec((T, LANES), lambda j: (0, j)), pl.BlockSpec((K, LANES), lambda j: (0, j))],
        out_specs=[pl.BlockSpec((T, LANES), lambda j: (0, j))],
        out_shape=[jax.ShapeDtypeStruct((T, 3 * H * GDN_DK), F32)],
        scratch_shapes=[pltpu.VMEM((T + PAD, LANES), F32)], args=(p, cw), sem=("arbitrary",))[0]


def _qkv_conv_bwd(dn, p, cw, H, name, hosts=None):
    T = p.shape[0]
    K = cw.shape[0]

    def body(dn_ref, x_ref, w_ref, dx_ref, dw_ref, xp_ref, dc_ref):
        j = pl.program_id(0)
        xp_ref[pl.ds(0, PAD), :] = jnp.zeros((PAD, LANES), F32)
        xp_ref[pl.ds(PAD, T), :] = x_ref[...]
        dc_ref[pl.ds(T, PAD), :] = jnp.zeros((PAD, LANES), F32)
        w = w_ref[...]
        scale = jnp.where(j < H, GDN_DK ** -0.5, 1.0).astype(F32)
        for c in range(T // RC):
            pre = _causal_taps(xp_ref, w, K, c * RC)
            act = _silu(pre)
            d = dn_ref[pl.ds(c * RC, RC), :]
            rs = lax.rsqrt(jnp.sum(act * act, axis=-1, keepdims=True) + L2_EPS)
            nh = act * rs
            dact_n = scale * rs * (d - nh * jnp.sum(d * nh, axis=-1, keepdims=True))
            dact = jnp.where(j < 2 * H, dact_n, d)
            dc_ref[pl.ds(c * RC, RC), :] = dact * _dsilu(pre)
        for c in range(T // RC):
            dx_ref[pl.ds(c * RC, RC), :] = _anticausal_taps(dc_ref, w, K, c * RC).astype(BF16)
        _tap_grads(dw_ref, dc_ref, xp_ref, K, T)

    return _hosted_call(
        body, hosts, name=name, grid=(3 * H,),
        in_specs=[pl.BlockSpec((T, LANES), lambda j: (0, j)), pl.BlockSpec((T, LANES), lambda j: (0, j)),
                  pl.BlockSpec((K, LANES), lambda j: (0, j))],
        out_specs=[pl.BlockSpec((T, LANES), lambda j: (0, j)), pl.BlockSpec((K, LANES), lambda j: (0, j))],
        out_shape=[jax.ShapeDtypeStruct((T, 3 * H * GDN_DK), BF16), jax.ShapeDtypeStruct(cw.shape, F32)],
        scratch_shapes=[pltpu.VMEM((T + PAD, LANES), F32), pltpu.VMEM((T + PAD, LANES), F32)],
        args=(dn, p, cw), sem=("arbitrary",))


def _dw_conv_fwd(hc, w, b, name):
    T, D = hc.shape
    K = w.shape[0]

    def body(x_ref, w_ref, b_ref, o_ref, xp_ref):
        xp_ref[pl.ds(0, PAD), :] = jnp.zeros((PAD, LANES), F32)
        xp_ref[pl.ds(PAD, T), :] = x_ref[...]
        wv = w_ref[...]
        for c in range(T // RC):
            o_ref[pl.ds(c * RC, RC), :] = _causal_taps(xp_ref, wv, K, c * RC) + b_ref[...]

    return pl.pallas_call(
        body, name=name, grid=(D // LANES,),
        in_specs=[pl.BlockSpec((T, LANES), lambda j: (0, j)), pl.BlockSpec((K, LANES), lambda j: (0, j)),
                  pl.BlockSpec((1, LANES), lambda j: (0, j))],
        out_specs=pl.BlockSpec((T, LANES), lambda j: (0, j)),
        out_shape=jax.ShapeDtypeStruct((T, D), F32),
        scratch_shapes=[pltpu.VMEM((T + PAD, LANES), F32)],
        compiler_params=_cparams(("arbitrary",)),
    )(hc, w, b)


def _dw_conv_bwd(dy, hc, w, name, hosts=None):
    T, D = hc.shape
    K = w.shape[0]

    def body(dy_ref, x_ref, w_ref, dx_ref, dw_ref, xp_ref, dc_ref):
        xp_ref[pl.ds(0, PAD), :] = jnp.zeros((PAD, LANES), F32)
        xp_ref[pl.ds(PAD, T), :] = x_ref[...]
        dc_ref[pl.ds(T, PAD), :] = jnp.zeros((PAD, LANES), F32)
        dc_ref[pl.ds(0, T), :] = dy_ref[...]
        wv = w_ref[...]
        for c in range(T // RC):
            dx_ref[pl.ds(c * RC, RC), :] = _anticausal_taps(dc_ref, wv, K, c * RC)
        _tap_grads(dw_ref, dc_ref, xp_ref, K, T)

    return _hosted_call(
        body, hosts, name=name, grid=(D // LANES,), sem=("arbitrary",),
        in_specs=[pl.BlockSpec((T, LANES), lambda j: (0, j)), pl.BlockSpec((T, LANES), lambda j: (0, j)),
                  pl.BlockSpec((K, LANES), lambda j: (0, j))],
        out_specs=[pl.BlockSpec((T, LANES), lambda j: (0, j)), pl.BlockSpec((K, LANES), lambda j: (0, j))],
        out_shape=[jax.ShapeDtypeStruct((T, D), F32), jax.ShapeDtypeStruct(w.shape, F32)],
        scratch_shapes=[pltpu.VMEM((T + PAD, LANES), F32), pltpu.VMEM((T + PAD, LANES), F32)],
        args=(dy, hc, w))


NN = (((1,), (0,)), ((), ()))
NT = (((1,), (1,)), ((), ()))
TN = (((0,), (0,)), ((), ()))


def _dotb(a, b, dn=NN):
    return lax.dot_general(a.astype(BF16), b.astype(BF16), dn, preferred_element_type=F32)


def _split_bf16(x, n):
    parts, r = [], x
    for _ in range(n):
        p = r.astype(BF16)
        parts.append(p)
        r = r - p.astype(F32)
    return parts


def _dot_sel(sel, x, pieces, sel_left=True):
    sb = sel.astype(BF16)
    acc = None
    for p in _split_bf16(x, pieces):
        t = (lax.dot_general(sb, p, NN, preferred_element_type=F32) if sel_left
             else lax.dot_general(p, sb, NN, preferred_element_type=F32))
        acc = t if acc is None else acc + t
    return acc


def _iota2(shape, axis):
    return lax.broadcasted_iota(jnp.int32, shape, axis)


def _to_row(col, eye):
    return jnp.sum(jnp.where(eye, col, 0.0), axis=0, keepdims=True)


def _to_col(row, eye):
    return jnp.sum(jnp.where(eye, row, 0.0), axis=1, keepdims=True)


def _gdn_gates(bl, al, alog, dtb):
    beta = _sigmoid(bl)
    x = al + dtb
    sp = jnp.maximum(x, 0.0) + jnp.log(1.0 + jnp.exp(-jnp.abs(x)))
    g = -jnp.exp(alog) * sp
    r, c = _iota2((CHUNK, CHUNK), 0), _iota2((CHUNK, CHUNK), 1)
    G = _dot_sel(r >= c, g, 3)
    return beta, g, G, x


def _unit_lower_inverses(As, Ats):
    n = len(As)
    nb = CHUNK // SUB
    lane = _iota2((SUB, CHUNK), 1)
    row = _iota2((SUB, CHUNK), 0)
    Atp = []
    for At in Ats:
        acc = jnp.zeros((SUB, CHUNK), F32)
        for b in range(nb):
            acc = jnp.where(lane // SUB == b, At[b * SUB:(b + 1) * SUB, :], acc)
        Atp.append(acc)
    gr, gc = _iota2((CHUNK, CHUNK), 0), _iota2((CHUNK, CHUNK), 1)
    ones_bd = gr // SUB == gc // SUB
    stack = jnp.concatenate(
        [jnp.where(lane % SUB == i, Atp[m], 0.0) for i in range(1, SUB) for m in range(n)], axis=0)
    Cm = _dot_sel(ones_bd, stack, 2, sel_left=False)
    Z = [(row == lane % SUB).astype(F32) for _ in range(n)]
    for i in range(1, SUB):
        for m in range(n):
            cm = Cm[((i - 1) * n + m) * SUB:((i - 1) * n + m + 1) * SUB, :]
            new = -jnp.sum(cm * Z[m], axis=0, keepdims=True)
            Z[m] = Z[m] + jnp.where(row == i, new, 0.0)
    bd = gr // SUB == gc // SUB
    Xs = [jnp.where(bd, jnp.concatenate([Z[m]] * nb, axis=0), 0.0) for m in range(n)]
    blk = SUB
    while blk < CHUNK:
        off = (gr // (2 * blk) == gc // (2 * blk)) & (gr // blk != gc // blk)
        Ys = [_dotb(Xs[m], jnp.where(off, As[m], 0.0)) for m in range(n)]
        Xs = [Xs[m] - _dotb(Ys[m], Xs[m]) for m in range(n)]
        blk *= 2
    return Xs


def _gdn_fwd(qkvn, p, alog, dtb, H, name, hosts=None):
    T = qkvn.shape[0]
    D = H * GDN_DK
    N = T // CHUNK
    bblk = 8 * D // LANES

    def body(q_ref, k_ref, v_ref, b_ref, a_ref, alog_ref, dtb_ref, o_ref, t_ref, s_ref, S_scr):
        @pl.when(pl.program_id(0) == 0)
        def _():
            S_scr[...] = jnp.zeros_like(S_scr)

        beta, _, G, _ = _gdn_gates(b_ref[...], a_ref[...], alog_ref[...], dtb_ref[...])
        hs = range(H)
        sl = [slice(h * GDN_DK, (h + 1) * GDN_DK) for h in hs]
        q, k, v = [q_ref[:, s] for s in sl], [k_ref[:, s] for s in sl], [v_ref[:, s] for s in sl]
        Gc, bc = [G[:, h:h + 1] for h in hs], [beta[:, h:h + 1] for h in hs]
        r, c = _iota2((CHUNK, CHUNK), 0), _iota2((CHUNK, CHUNK), 1)
        eye, low, up = r == c, r >= c, r <= c
        Gr, br = [_to_row(Gc[h], eye) for h in hs], [_to_row(bc[h], eye) for h in hs]
        Dm = [jnp.where(low, jnp.exp(jnp.where(low, Gc[h] - Gr[h], 0.0)), 0.0) for h in hs]
        Dt = [jnp.where(up, jnp.exp(jnp.where(up, Gr[h] - Gc[h], 0.0)), 0.0) for h in hs]
        qk = [_dotb(jnp.concatenate([q[h], k[h]], axis=0), k[h], NT) for h in hs]
        QK = [qk[h][:CHUNK] * Dm[h] for h in hs]
        KK = [qk[h][CHUNK:] for h in hs]
        As = [jnp.where(r > c, KK[h] * Dm[h], 0.0) * bc[h] for h in hs]
        Ats = [jnp.where(r < c, KK[h] * Dt[h], 0.0) * br[h] for h in hs]
        Ts = _unit_lower_inverses(As, Ats)
        eG = [jnp.exp(Gc[h]) for h in hs]
        Gl = [Gc[h][CHUNK - 1:CHUNK, :] for h in hs]
        uw = [_dotb(Ts[h], jnp.concatenate([v[h] * bc[h], k[h] * (bc[h] * eG[h])], axis=1)) for h in hs]
        S = [S_scr[h] for h in hs]
        qw = [_dotb(jnp.concatenate([q[h] * eG[h], uw[h][:, GDN_DK:]], axis=0), S[h]) for h in hs]
        vn = [uw[h][:, :GDN_DK] - qw[h][CHUNK:] for h in hs]
        o = [qw[h][:CHUNK] + _dotb(QK[h], vn[h]) for h in hs]
        Sn = [S[h] * jnp.exp(Gl[h]) + _dotb(k[h] * jnp.exp(Gl[h] - Gc[h]), vn[h], TN) for h in hs]
        for h in hs:
            t_ref[0, h] = Ts[h]
            s_ref[0, h] = S[h]
            o_ref[:, sl[h]] = o[h]
            S_scr[h] = Sn[h]

    qkv_spec = [pl.BlockSpec((CHUNK, D), lambda n, cb=cb: (n, cb)) for cb in range(3)]
    return _hosted_call(
        body, hosts, name=name, grid=(N,), sem=("arbitrary",),
        in_specs=qkv_spec + [pl.BlockSpec((CHUNK, LANES), lambda n: (n, bblk)),
                             pl.BlockSpec((CHUNK, LANES), lambda n: (n, bblk + 1)),
                             pl.BlockSpec((1, LANES), lambda n: (0, 0)), pl.BlockSpec((1, LANES), lambda n: (0, 0))],
        out_specs=[pl.BlockSpec((CHUNK, D), lambda n: (n, 0)),
                   pl.BlockSpec((1, H, CHUNK, CHUNK), lambda n: (n, 0, 0, 0)),
                   pl.BlockSpec((1, H, GDN_DK, GDN_DK), lambda n: (n, 0, 0, 0))],
        out_shape=[jax.ShapeDtypeStruct((T, D), F32), jax.ShapeDtypeStruct((N, H, CHUNK, CHUNK), F32),
                   jax.ShapeDtypeStruct((N, H, GDN_DK, GDN_DK), F32)],
        scratch_shapes=[pltpu.VMEM((H, GDN_DK, GDN_DK), F32)],
        args=(qkvn, qkvn, qkvn, p, p, alog, dtb))


def _gdn_bwd(do, qkvn, p, alog, dtb, Tinv, Sin, H, name, hosts=None):
    T = qkvn.shape[0]
    D = H * GDN_DK
    N = T // CHUNK
    bblk = 8 * D // LANES

    def body(do_ref, q_ref, k_ref, v_ref, b_ref, a_ref, alog_ref, dtb_ref, t_ref, s_ref,
             dqkv_ref, dba_ref, dalog_ref, ddtb_ref, dS_scr):
        @pl.when(pl.program_id(0) == 0)
        def _():
            dS_scr[...] = jnp.zeros_like(dS_scr)
            dalog_ref[...] = jnp.zeros_like(dalog_ref)
            ddtb_ref[...] = jnp.zeros_like(ddtb_ref)

        beta, g, G, x = _gdn_gates(b_ref[...], a_ref[...], alog_ref[...], dtb_ref[...])
        r, c = _iota2((CHUNK, CHUNK), 0), _iota2((CHUNK, CHUNK), 1)
        eye, low, strict = r == c, r >= c, r > c
        lane = _iota2((CHUNK, LANES), 1)
        rsum1 = lambda a: jnp.sum(a, axis=1, keepdims=True)
        hs = range(H)
        sl = [slice(h * GDN_DK, (h + 1) * GDN_DK) for h in hs]
        q, k, v = [q_ref[:, s] for s in sl], [k_ref[:, s] for s in sl], [v_ref[:, s] for s in sl]
        dov = [do_ref[:, s] for s in sl]
        Gc, bc = [G[:, h:h + 1] for h in hs], [beta[:, h:h + 1] for h in hs]
        Tm, S, dSo = [t_ref[0, h] for h in hs], [s_ref[0, h] for h in hs], [dS_scr[h] for h in hs]
        Gr = [_to_row(Gc[h], eye) for h in hs]
        Dm = [jnp.where(low, jnp.exp(jnp.where(low, Gc[h] - Gr[h], 0.0)), 0.0) for h in hs]
        eG = [jnp.exp(Gc[h]) for h in hs]
        Gl = [Gc[h][CHUNK - 1:CHUNK, :] for h in hs]
        eR, dch = [jnp.exp(Gl[h] - Gc[h]) for h in hs], [jnp.exp(Gl[h]) for h in hs]
        qk = [_dotb(jnp.concatenate([q[h], k[h]], axis=0), k[h], NT) for h in hs]
        QKr, KK = [qk[h][:CHUNK] for h in hs], [qk[h][CHUNK:] for h in hs]
        QK = [QKr[h] * Dm[h] for h in hs]
        M = [jnp.where(strict, KK[h] * Dm[h], 0.0) for h in hs]
        uw = [_dotb(Tm[h], jnp.concatenate([v[h] * bc[h], k[h] * (bc[h] * eG[h])], axis=1)) for h in hs]
        u, w = [uw[h][:, :GDN_DK] for h in hs], [uw[h][:, GDN_DK:] for h in hs]
        qd, kd = [q[h] * eG[h] for h in hs], [k[h] * eR[h] for h in hs]
        vn = [u[h] - _dotb(w[h], S[h]) for h in hs]
        dvn = [_dotb(QK[h], dov[h], TN) + _dotb(kd[h], dSo[h]) for h in hs]
        dQK = [jnp.where(low, _dotb(dov[h], vn[h], NT), 0.0) for h in hs]
        dkd = [_dotb(vn[h], dSo[h], NT) for h in hs]
        ddch = [jnp.sum(rsum1(dSo[h] * S[h]), axis=0, keepdims=True) for h in hs]
        dd = [jnp.concatenate([dov[h], dvn[h]], axis=0) for h in hs]
        xs = [_dotb(dd[h], S[h], NT) for h in hs]
        dqd, dw = [xs[h][:CHUNK] for h in hs], [-xs[h][CHUNK:] for h in hs]
        dS = [_dotb(jnp.concatenate([qd[h], -w[h]], axis=0), dd[h], TN) + dch[h] * dSo[h] for h in hs]
        yb = [_dotb(Tm[h], jnp.concatenate([dvn[h], dw[h]], axis=1), TN) for h in hs]
        dvb, dkb = [yb[h][:, :GDN_DK] for h in hs], [yb[h][:, GDN_DK:] for h in hs]
        dA = [-jnp.where(strict, _dotb(yb[h], uw[h], NT), 0.0) for h in hs]
        rk = [rsum1(dkb[h] * k[h]) for h in hs]
        dbeta = [rsum1(dvb[h] * v[h]) + rk[h] * eG[h] + rsum1(dA[h] * M[h]) for h in hs]
        dM = [dA[h] * bc[h] for h in hs]
        dKK = [dM[h] * Dm[h] for h in hs]
        dQKr = [dQK[h] * Dm[h] for h in hs]
        E = [dM[h] * M[h] + dQK[h] * QK[h] for h in hs]
        zk = [_dotb(jnp.concatenate([dQKr[h], dKK[h]], axis=0), k[h]) for h in hs]
        dq = [zk[h][:CHUNK] + dqd[h] * eG[h] for h in hs]
        dk = [dkb[h] * (bc[h] * eG[h]) + zk[h][CHUNK:] + _dotb(dKK[h], k[h], TN) + _dotb(dQKr[h], q[h], TN)
              + dkd[h] * eR[h] for h in hs]
        deG = [rk[h] * bc[h] + rsum1(dqd[h] * q[h]) for h in hs]
        deR = [rsum1(dkd[h] * k[h]) for h in hs]
        dGl = [jnp.sum(deR[h] * eR[h], axis=0, keepdims=True) + ddch[h] * dch[h] for h in hs]
        dGc = [rsum1(E[h]) - _to_col(jnp.sum(E[h], axis=0, keepdims=True), eye) + deG[h] * eG[h] - deR[h] * eR[h]
               + jnp.where(r[:, :1] == CHUNK - 1, dGl[h], 0.0) for h in hs]
        dG_all = jnp.zeros((CHUNK, LANES), F32)
        dbeta_all = jnp.zeros((CHUNK, LANES), F32)
        for h in hs:
            dS_scr[h] = dS[h]
            dqkv_ref[:, sl[h]] = dq[h]
            dqkv_ref[:, D + h * GDN_DK:D + (h + 1) * GDN_DK] = dk[h]
            dqkv_ref[:, 2 * D + h * GDN_DK:2 * D + (h + 1) * GDN_DK] = dvb[h] * bc[h]
            dG_all = jnp.where(lane == h, dGc[h], dG_all)
            dbeta_all = jnp.where(lane == h, dbeta[h], dbeta_all)
        dg = _dot_sel(r <= c, dG_all, 3)
        da = dg * (-jnp.exp(alog_ref[...])) * _sigmoid(x)
        dba_ref[:, :LANES] = (dbeta_all * beta * (1.0 - beta)).astype(BF16)
        dba_ref[:, LANES:] = da.astype(BF16)
        dalog_ref[...] += _rsum(dg * g)
        ddtb_ref[...] += _rsum(da)

    rev = lambda n: N - 1 - n
    qkv_spec = [pl.BlockSpec((CHUNK, D), lambda n, cb=cb: (rev(n), cb)) for cb in range(3)]
    return _hosted_call(
        body, hosts, name=name, grid=(N,), sem=("arbitrary",),
        in_specs=[pl.BlockSpec((CHUNK, D), lambda n: (rev(n), 0))] + qkv_spec + [
            pl.BlockSpec((CHUNK, LANES), lambda n: (rev(n), bblk)),
            pl.BlockSpec((CHUNK, LANES), lambda n: (rev(n), bblk + 1)),
            pl.BlockSpec((1, LANES), lambda n: (0, 0)), pl.BlockSpec((1, LANES), lambda n: (0, 0)),
            pl.BlockSpec((1, H, CHUNK, CHUNK), lambda n: (rev(n), 0, 0, 0)),
            pl.BlockSpec((1, H, GDN_DK, GDN_DK), lambda n: (rev(n), 0, 0, 0))],
        out_specs=[pl.BlockSpec((CHUNK, 3 * D), lambda n: (rev(n), 0)),
                   pl.BlockSpec((CHUNK, 2 * LANES), lambda n: (rev(n), 0)),
                   pl.BlockSpec((1, LANES), lambda n: (0, 0)), pl.BlockSpec((1, LANES), lambda n: (0, 0))],
        out_shape=[jax.ShapeDtypeStruct((T, 3 * D), F32), jax.ShapeDtypeStruct((T, 2 * LANES), BF16),
                   jax.ShapeDtypeStruct((1, LANES), F32), jax.ShapeDtypeStruct((1, LANES), F32)],
        scratch_shapes=[pltpu.VMEM((H, GDN_DK, GDN_DK), F32)],
        args=(do, qkvn, qkvn, qkvn, p, p, alog, dtb, Tinv, Sin))


def _mix_in_reorder(wt, D, H):
    o1 = 4 * D
    o2, o3 = o1 + H, o1 + 2 * H
    z = jnp.zeros((LANES - H, wt.shape[1]), wt.dtype)
    return jnp.concatenate([wt[:o1], wt[o3:], wt[o1:o2], z, wt[o2:o3], z], axis=0)


def _mix_in_restore(dwt, D, H):
    b0 = 8 * D
    return jnp.concatenate([dwt[:4 * D], dwt[b0:b0 + H], dwt[b0 + LANES:b0 + LANES + H], dwt[4 * D:b0]], axis=0)


def _ffn_fwd(x, W, pre, tag, hosts=None):
    h = _rms_fwd(x, W[pre + "_norm_pre"], tag + "_pre")
    a = _matmul(h, W[pre + "_w_in"], "nt", BF16, tag + "_in", hosts)
    s = _swiglu_fwd(a, tag + "_act")
    f = _matmul(s, W[pre + "_w_out"], "nn", F32, tag + "_out", hosts)
    return _post_fwd(x, f, W[pre + "_norm_post"], 0.5, tag + "_post"), (x, h, a, s, f)


def _ffn_bwd(dxn, saved, W, pre, tag, g, hosts=None):
    x, h, a, s, f = saved
    df, g[pre + "_norm_post"] = _post_bwd(dxn, f, W[pre + "_norm_post"], 0.5, tag + "_dpost")
    ds = _matmul(df, W[pre + "_w_out"], "nt", BF16, tag + "_ds", hosts)
    g[pre + "_w_out"] = _matmul(s, df, "tn", BF16, tag + "_dwout")
    da = _swiglu_bwd(ds, a, tag + "_dact", hosts)
    dh = _matmul(da, W[pre + "_w_in"], "nn", F32, tag + "_dh", hosts)
    g[pre + "_w_in"] = _matmul(da, h, "tn", BF16, tag + "_dwin", hosts)
    dx, g[pre + "_norm_pre"] = _pre_bwd(dh, x, W[pre + "_norm_pre"], dxn, tag + "_dpre")
    return dx


def _mix_fwd(x, W, H, tag, hosts=None):
    h = _rms_fwd(x, W["mix_norm_pre"], tag + "_pre")
    p = _matmul(h, W["mix_w_in"], "nt", F32, tag + "_in", hosts)
    qkvn = _qkv_conv_fwd(p, W["gdn_conv_w"], H, tag + "_qkvconv", hosts)
    o, Tinv, Sin = _gdn_fwd(qkvn, p, W["gdn_a_log"], W["gdn_dt_bias"], H, tag + "_gdn", hosts)
    og = _gdn_gate_fwd(o, p, W["gdn_norm_w"], tag + "_gdngate")
    ya = _matmul(og, W["gdn_w_o"], "nn", F32, tag + "_gdno")
    hc = _glu_fwd(p, W["cnv_pw1_b"], tag + "_glu")
    hcv = _dw_conv_fwd(hc, W["cnv_dw_w"], W["cnv_dw_b"], tag + "_dwconv")
    hl = _ln_silu_fwd(hcv, W["cnv_ln_g"], W["cnv_ln_b"], tag + "_ln")
    yb = _matmul(hl, W["cnv_w_o"], "nn", F32, tag + "_cnvo")
    ym = _merge_fwd(p, ya, yb, W["cnv_b_o"], tag + "_merge")
    y = _matmul(ym, W["mix_w_out"], "nn", F32, tag + "_out")
    xn = _post_fwd(x, y, W["mix_norm_post"], 1.0, tag + "_post")
    return xn, (x, h, p, qkvn, o, Tinv, Sin, og, ya, hc, hcv, hl, yb, ym, y)


def _mix_bwd(dxn, saved, W, H, tag, g, hosts=None):
    x, h, p, qkvn, o, Tinv, Sin, og, ya, hc, hcv, hl, yb, ym, y = saved
    dy, g["mix_norm_post"] = _post_bwd(dxn, y, W["mix_norm_post"], 1.0, tag + "_dpost")
    dym = _matmul(dy, W["mix_w_out"], "nt", F32, tag + "_dym")
    g["mix_w_out"] = _matmul(ym, dy, "tn", BF16, tag + "_dwout")
    dya, dyb, dgates, g["cnv_b_o"] = _merge_bwd(dym, p, ya, yb, W["cnv_b_o"], tag + "_dmerge", hosts)
    dhl = _matmul(dyb, W["cnv_w_o"], "nt", F32, tag + "_dhl")
    g["cnv_w_o"] = _matmul(hl, dyb, "tn", BF16, tag + "_dwcnvo")
    dhcv, g["cnv_ln_g"], g["cnv_ln_b"], g["cnv_dw_b"] = _ln_silu_bwd(
        dhl, hcv, W["cnv_ln_g"], W["cnv_ln_b"], tag + "_dln", hosts)
    dhc, g["cnv_dw_w"] = _dw_conv_bwd(dhcv, hc, W["cnv_dw_w"], tag + "_ddwconv", hosts)
    dglu, g["cnv_pw1_b"] = _glu_bwd(dhc, p, W["cnv_pw1_b"], tag + "_dglu")
    dog = _matmul(dya, W["gdn_w_o"], "nt", F32, tag + "_dog")
    g["gdn_w_o"] = _matmul(og, dya, "tn", BF16, tag + "_dwgdno")
    do, dz, g["gdn_norm_w"] = _gdn_gate_bwd(dog, o, p, W["gdn_norm_w"], tag + "_dgdngate", hosts)
    dqkvn, dba, g["gdn_a_log"], g["gdn_dt_bias"] = _gdn_bwd(
        do, qkvn, p, W["gdn_a_log"], W["gdn_dt_bias"], Tinv, Sin, H, tag + "_dgdn", hosts)
    dqkv, g["gdn_conv_w"] = _qkv_conv_bwd(dqkvn, p, W["gdn_conv_w"], H, tag + "_dqkvconv", hosts)
    dp = jnp.concatenate([dqkv, dz, dglu, dgates, dba], axis=1)
    dh = _matmul(dp, W["mix_w_in"], "nn", F32, tag + "_dh", hosts)
    g["mix_w_in"] = _matmul(dp, h, "tn", F32, tag + "_dwin")
    dx, g["mix_norm_pre"] = _pre_bwd(dh, x, W["mix_norm_pre"], dxn, tag + "_dpre", hosts)
    return dx


def _trunk_fwd_bwd(x, tgt, H, L, weights_of, grads, hosts=None):
    saved, Ws = [], []
    for i in range(L):
        W = weights_of(i)
        Ws.append(W)
        x, s1 = _ffn_fwd(x, W, "ffn1", f"l{i}_ffn1", hosts)
        x, s2 = _mix_fwd(x, W, H, f"l{i}_mix", hosts)
        x, s3 = _ffn_fwd(x, W, "ffn2", f"l{i}_ffn2", hosts)
        saved.append((s1, s2, s3))
    dx, loss = _loss_fwd_bwd(x, tgt, "loss")
    for i in reversed(range(L)):
        s1, s2, s3 = saved[i]
        dx = _ffn_bwd(dx, s3, Ws[i], "ffn2", f"l{i}_ffn2", grads[i], hosts)
        dx = _mix_bwd(dx, s2, Ws[i], H, f"l{i}_mix", grads[i], hosts)
        dx = _ffn_bwd(dx, s1, Ws[i], "ffn1", f"l{i}_ffn1", grads[i], hosts)
    return loss, dx


HBM_SPEC = pl.BlockSpec(memory_space=pltpu.HBM)


def _coords():
    return lax.axis_index("x"), lax.axis_index("y"), lax.axis_index("c")


class _GatherPlan:
    has_middle = True

    def __init__(self, shards):
        self.n = len(shards)
        self.out_shape = [jax.ShapeDtypeStruct((N_DEV,) + s.shape, s.dtype) for s in shards]
        self.sems = [pltpu.SemaphoreType.DMA((self.n, 7)), pltpu.SemaphoreType.DMA((self.n, 7)),
                     pltpu.SemaphoreType.DMA((self.n,))]

    def _parts(self, ins, outs, sems):
        send_sems, recv_sems, local_sems = sems
        x, y, c = _coords()
        me, sibling = (x, y, c), (x, y, 1 - c)
        chips = [(1 - x, y), (x, 1 - y), (1 - x, 1 - y)]

        def copy(w, k, block, to, src=None):
            dst = outs[w].at[4 * block[0] + 2 * block[1] + block[2]]
            return pltpu.make_async_remote_copy(
                src_ref=dst if src is None else src, dst_ref=dst, send_sem=send_sems.at[w, k],
                recv_sem=recv_sems.at[w, k], device_id=to, device_id_type=MESH)

        mine = [pltpu.make_async_copy(ins[w], outs[w].at[4 * x + 2 * y + c], local_sems.at[w]) for w in range(self.n)]
        first = []
        for w in range(self.n):
            first.append(copy(w, 0, me, sibling, src=ins[w]))
            first += [copy(w, 1 + j, me, (*chip, c), src=ins[w]) for j, chip in enumerate(chips)]
        passed = [copy(w, 4 + j, (*chip, c), sibling) for j, chip in enumerate(chips) for w in range(self.n)]
        return copy, mine, first, passed, chips, me, sibling, c

    def begin(self, ins, outs, sems):
        _, mine, first, _, _, _, _, _ = self._parts(ins, outs, sems)
        for cp in mine + first:
            cp.start()

    def middle(self, ins, outs, sems):
        copy, _, _, passed, chips, me, _, c = self._parts(ins, outs, sems)
        for j, chip in enumerate(chips):
            for w in range(self.n):
                copy(w, 1 + j, (*chip, c), me).wait_recv()
                passed[j * self.n + w].start()

    def finish(self, ins, outs, sems):
        copy, mine, first, passed, chips, me, sibling, c = self._parts(ins, outs, sems)
        for w in range(self.n):
            copy(w, 0, sibling, me).wait_recv()
            for j, chip in enumerate(chips):
                copy(w, 4 + j, (*chip, 1 - c), me).wait_recv()
        for cp in first + passed:
            cp.wait_send()
        for cp in mine:
            cp.wait()


class _SiblingPlan:
    has_middle = False

    def __init__(self, Gs):
        self.n = len(Gs)
        self.out_shape = [jax.ShapeDtypeStruct((4,) + g.shape[1:], g.dtype) for g in Gs]
        self.sems = [pltpu.SemaphoreType.DMA((self.n, 4)), pltpu.SemaphoreType.DMA((self.n, 4))]

    def _copies(self, ins, outs, sems):
        send_sems, recv_sems = sems
        x, y, c = _coords()
        return [pltpu.make_async_remote_copy(
            src_ref=ins[w].at[2 * q + (1 - c)], dst_ref=outs[w].at[q], send_sem=send_sems.at[w, q],
            recv_sem=recv_sems.at[w, q], device_id=(x, y, 1 - c), device_id_type=MESH)
            for w in range(self.n) for q in range(4)]

    def begin(self, ins, outs, sems):
        for cp in self._copies(ins, outs, sems):
            cp.start()

    def finish(self, ins, outs, sems):
        for cp in self._copies(ins, outs, sems):
            cp.wait()


class _ChipsPlan:
    has_middle = False

    def __init__(self, Ps):
        self.n = len(Ps)
        self.out_shape = [jax.ShapeDtypeStruct(p.shape, p.dtype) for p in Ps]
        self.sems = [pltpu.SemaphoreType.DMA((self.n, 3)), pltpu.SemaphoreType.DMA((self.n, 3)),
                     pltpu.SemaphoreType.DMA((self.n,))]

    def _copies(self, ins, outs, sems):
        send_sems, recv_sems, local_sems = sems
        x, y, c = _coords()
        me_q = 2 * x + y
        cps = []
        for w in range(self.n):
            cps.append(pltpu.make_async_copy(ins[w].at[me_q], outs[w].at[me_q], local_sems.at[w]))
            for j, (px, py) in enumerate([(1 - x, y), (x, 1 - y), (1 - x, 1 - y)]):
                cps.append(pltpu.make_async_remote_copy(
                    src_ref=ins[w].at[2 * px + py], dst_ref=outs[w].at[me_q], send_sem=send_sems.at[w, j],
                    recv_sem=recv_sems.at[w, j], device_id=(px, py, c), device_id_type=MESH))
        return cps

    def begin(self, ins, outs, sems):
        for cp in self._copies(ins, outs, sems):
            cp.start()

    def finish(self, ins, outs, sems):
        for cp in self._copies(ins, outs, sems):
            cp.wait()


def _comm_only(plan, arrays, name):
    n = plan.n

    def body(*refs):
        ins, outs, sems = refs[:n], refs[n:2 * n], refs[2 * n:]
        plan.begin(ins, outs, sems)
        if plan.has_middle:
            plan.middle(ins, outs, sems)
        plan.finish(ins, outs, sems)

    return pl.pallas_call(
        body, name=name, out_shape=plan.out_shape, in_specs=[HBM_SPEC] * n, out_specs=[HBM_SPEC] * n,
        scratch_shapes=plan.sems,
    )(*arrays)


class _MultiPlan:
    def __init__(self, plans):
        self.plans = plans
        self.n = sum(p.n for p in plans)
        self.out_shape = [s for p in plans for s in p.out_shape]
        self.sems = [s for p in plans for s in p.sems]
        self.has_middle = any(p.has_middle for p in plans)

    def _each(self, phase, ins, outs, sems):
        a = s = 0
        for p in self.plans:
            if phase != "middle" or p.has_middle:
                getattr(p, phase)(ins[a:a + p.n], outs[a:a + p.n], sems[s:s + len(p.sems)])
            a, s = a + p.n, s + len(p.sems)

    def begin(self, ins, outs, sems):
        self._each("begin", ins, outs, sems)

    def middle(self, ins, outs, sems):
        self._each("middle", ins, outs, sems)

    def finish(self, ins, outs, sems):
        self._each("finish", ins, outs, sems)


class _Hosts:
    def __init__(self):
        self.waiting = {}

    def add(self, host, make):
        self.waiting.setdefault(host, []).append(make)

    def take(self, host):
        makes = self.waiting.pop(host, None)
        if not makes:
            return None
        items = [m() for m in makes]
        return _MultiPlan([it[0] for it in items]), [a for it in items for a in it[1]], [it[2] for it in items]


def _hosted_call(body, hosts, *, name, grid, in_specs, out_specs, out_shape, scratch_shapes, args, sem):
    comm = hosts.take(name) if hosts is not None else None
    if comm is None:
        return pl.pallas_call(body, name=name, grid=grid, in_specs=in_specs, out_specs=out_specs, out_shape=out_shape,
                              scratch_shapes=scratch_shapes, compiler_params=_cparams(sem))(*args)
    plan, arrays, deliver = comm
    n_in, n_out, n_scr, n = len(in_specs), len(out_specs), len(scratch_shapes), plan.n
    total = 1
    for g in grid:
        total *= g

    def kern(*refs):
        ins, cins = refs[:n_in], refs[n_in:n_in + n]
        outs, couts = refs[n_in + n:n_in + n + n_out], refs[n_in + n + n_out:n_in + 2 * n + n_out]
        scr, csems = refs[n_in + 2 * n + n_out:n_in + 2 * n + n_out + n_scr], refs[n_in + 2 * n + n_out + n_scr:]
        step = pl.program_id(0)
        for d in range(1, len(grid)):
            step = step * grid[d] + pl.program_id(d)

        @pl.when(step == 0)
        def _():
            plan.begin(cins, couts, csems)

        body(*ins, *outs, *scr)
        if plan.has_middle:
            @pl.when(step == (3 * total) // 4)
            def _():
                plan.middle(cins, couts, csems)

        @pl.when(step == total - 1)
        def _():
            plan.finish(cins, couts, csems)

    res = pl.pallas_call(
        kern, name=name, grid=grid, in_specs=list(in_specs) + [HBM_SPEC] * n,
        out_specs=list(out_specs) + [HBM_SPEC] * n, out_shape=list(out_shape) + plan.out_shape,
        scratch_shapes=list(scratch_shapes) + plan.sems, compiler_params=_cparams(("arbitrary",) * len(grid)),
    )(*args, *arrays)
    k = n_out
    for p, d in zip(plan.plans, deliver):
        d(res[k:k + p.n])
        k += p.n
    return res[:n_out]


def _row_tile(R, target=256):
    best = None
    for t in range(8, min(R, target) + 1, 8):
        if R % t == 0:
            best = t
    return best if best is not None else R


def _pair_add(G, R1, cidx, name):
    _, R, C = G.shape
    tb = _row_tile(R)

    def body(c_ref, g_ref, r_ref, o_ref):
        o_ref[...] = (g_ref[...].astype(F32) + r_ref[...].astype(F32)).astype(BF16)

    return pl.pallas_call(
        body, name=name,
        grid_spec=pltpu.PrefetchScalarGridSpec(
            num_scalar_prefetch=1, grid=(4, R // tb),
            in_specs=[pl.BlockSpec((None, tb, C), lambda q, i, cr: (2 * q + cr[0], i, 0)),
                      pl.BlockSpec((None, tb, C), lambda q, i, cr: (q, i, 0))],
            out_specs=pl.BlockSpec((None, tb, C), lambda q, i, cr: (q, i, 0))),
        out_shape=jax.ShapeDtypeStruct((4, R, C), BF16),
        compiler_params=_cparams(("arbitrary", "arbitrary")),
    )(cidx, G, R1)


def _sum_parts(parts, name):
    P, R, C = parts.shape

    def body(p_ref, o_ref):
        acc = p_ref[0]
        for j in range(1, P):
            acc = acc + p_ref[j]
        o_ref[...] = acc

    return pl.pallas_call(
        body, name=name, out_shape=jax.ShapeDtypeStruct((R, C), F32),
        in_specs=[pl.BlockSpec(memory_space=pltpu.VMEM)], out_specs=pl.BlockSpec(memory_space=pltpu.VMEM),
        compiler_params=_cparams(),
    )(parts)


def _adamw(w, m, v, parts, name):
    G, R, C = w.shape
    P = parts[0].shape[0]
    tb = _row_tile(R)
    nb = R // tb
    c1 = 1.0 / (1.0 - ADAM_B1 ** ADAM_STEP)
    c2 = 1.0 / (1.0 - ADAM_B2 ** ADAM_STEP)

    def body(w_ref, m_ref, v_ref, *rest):
        p_refs, (g_ref, d_ref, nm_ref, nv_ref) = rest[:G], rest[G:]
        l = pl.program_id(0)
        g = None
        for k in range(G):
            gk = p_refs[k][0].astype(F32)
            for j in range(1, P):
                gk = gk + p_refs[k][j].astype(F32)
            g = gk if g is None else jnp.where(l == k, gk, g)
        nm = ADAM_B1 * m_ref[...] + (1.0 - ADAM_B1) * g
        nv = ADAM_B2 * v_ref[...] + (1.0 - ADAM_B2) * (g * g)
        g_ref[...] = g
        nm_ref[...] = nm
        nv_ref[...] = nv
        d_ref[...] = -ADAM_LR * ((nm * c1) / (jnp.sqrt(nv * c2) + ADAM_EPS) + ADAM_WD * w_ref[...])

    blk = pl.BlockSpec((None, tb, C), lambda l, i: (l, i, 0))

    def part_spec(k):
        return pl.BlockSpec((P, tb, C), lambda l, i: (0, jnp.where(l < k, 0, jnp.where(l > k, nb - 1, i)), 0))

    return pl.pallas_call(
        body, name=name, grid=(G, nb),
        in_specs=[blk, blk, blk] + [part_spec(k) for k in range(G)],
        out_specs=[blk] * 4, out_shape=[jax.ShapeDtypeStruct((G, R, C), F32)] * 4,
        compiler_params=_cparams(("arbitrary", "arbitrary")),
    )(w, m, v, *parts)


BIG = ("ffn1_w_in", "ffn1_w_out", "mix_w_in", "gdn_w_o", "cnv_w_o", "mix_w_out", "ffn2_w_in", "ffn2_w_out")
COL_SHARDED = ("ffn1_w_in", "mix_w_in", "ffn2_w_in")
SMALL_SHARDED = ("gdn_conv_w", "cnv_dw_w")
NAMES = ("ffn1_norm_pre", "ffn1_norm_post", "ffn1_w_in", "ffn1_w_out", "mix_norm_pre", "mix_norm_post", "mix_w_in",
         "gdn_conv_w", "gdn_a_log", "gdn_dt_bias", "gdn_norm_w", "gdn_w_o", "cnv_pw1_b", "cnv_dw_w", "cnv_dw_b",
         "cnv_ln_g", "cnv_ln_b", "cnv_w_o", "cnv_b_o", "mix_w_out", "ffn2_norm_pre", "ffn2_norm_post", "ffn2_w_in",
         "ffn2_w_out")
SMALL = tuple(n for n in NAMES if n not in BIG)


class _LayerWeights:
    def __init__(self, got, params, i, D, H):
        self.got, self.params, self.i, self.D, self.H, self.made = got, params, i, D, H, {}

    def __getitem__(self, n):
        if n not in self.made:
            if n in BIG:
                g = self.got[(self.i, n)]
                g = g.reshape(-1, g.shape[-1])
                w = _mix_in_reorder(g, self.D, self.H) if n == "mix_w_in" else g
            elif n in SMALL_SHARDED:
                g = self.got[(self.i, n)]
                w = jnp.transpose(g, (1, 0, 2)).reshape(g.shape[1], -1)
            else:
                v = self.params[n][self.i]
                if n in ("gdn_a_log", "gdn_dt_bias"):
                    v = jnp.pad(v, (0, LANES - self.H))
                w = v.reshape(1, -1)
            self.made[n] = w
        return self.made[n]


GATHER_HOSTS = (("l{j}_ffn1_in", ("ffn1_w_in",)), ("l{j}_ffn1_out", ("ffn1_w_out",)), ("l{j}_mix_in", ("mix_w_in",)),
                ("l{j}_mix_qkvconv", ("gdn_w_o", "cnv_w_o", "mix_w_out", "gdn_conv_w", "cnv_dw_w")),
                ("l{i}_mix_gdn", ("ffn2_w_in", "ffn2_w_out")))
GATHER_HOSTS_FIRST = ((None, ("ffn1_w_in", "ffn1_w_out", "mix_w_in", "gdn_w_o", "cnv_w_o", "mix_w_out", "gdn_conv_w",
                              "cnv_dw_w")), ("l0_mix_gdn", ("ffn2_w_in", "ffn2_w_out")))
REDUCE_HOSTS = ((BIG, "l{j}_ffn2_ds", (("l{j}_ffn2_dh", ("ffn2_w_in",)),
                                       ("l{j}_ffn2_dwin", ("ffn2_w_out", "gdn_w_o", "cnv_w_o", "mix_w_out")),
                                       ("l{j}_mix_ddwconv", ("ffn1_w_in",)),
                                       ("l{j}_mix_dgdn", ("mix_w_in", "ffn1_w_out")))),)
REDUCE_HOSTS_FIRST = (
    (("ffn2_w_in", "ffn2_w_out"), "l0_mix_dmerge", (("l0_mix_dln", ("ffn2_w_out",)), ("l0_mix_dqkvconv", ("ffn2_w_in",)))),
    (("mix_w_out", "cnv_w_o", "gdn_w_o"), "l0_mix_dgdngate", (("l0_mix_dh", ("mix_w_out", "cnv_w_o", "gdn_w_o")),)),
    (("mix_w_in",), "l0_mix_dpre", (("l0_ffn1_dh", ("mix_w_in",)),)),
    (("ffn1_w_out",), "l0_ffn1_dact", (("l0_ffn1_dwin", ("ffn1_w_out",)),)),
    (("ffn1_w_in",), None, ((None, ("ffn1_w_in",)),)))


def kernel(x, ffn1_norm_pre, ffn1_norm_post, ffn1_w_in, ffn1_w_out, mix_norm_pre, mix_norm_post, mix_w_in, gdn_conv_w, gdn_a_log, gdn_dt_bias, gdn_norm_w, gdn_w_o, cnv_pw1_b, cnv_dw_w, cnv_dw_b, cnv_ln_g, cnv_ln_b, cnv_w_o, cnv_b_o, mix_w_out, ffn2_norm_pre, ffn2_norm_post, ffn2_w_in, ffn2_w_out, loss_target, m_ffn1_norm_pre, m_ffn1_norm_post, m_ffn1_w_in, m_ffn1_w_out, m_mix_norm_pre, m_mix_norm_post, m_mix_w_in, m_gdn_conv_w, m_gdn_a_log, m_gdn_dt_bias, m_gdn_norm_w, m_gdn_w_o, m_cnv_pw1_b, m_cnv_dw_w, m_cnv_dw_b, m_cnv_ln_g, m_cnv_ln_b, m_cnv_w_o, m_cnv_b_o, m_mix_w_out, m_ffn2_norm_pre, m_ffn2_norm_post, m_ffn2_w_in, m_ffn2_w_out, v_ffn1_norm_pre, v_ffn1_norm_post, v_ffn1_w_in, v_ffn1_w_out, v_mix_norm_pre, v_mix_norm_post, v_mix_w_in, v_gdn_conv_w, v_gdn_a_log, v_gdn_dt_bias, v_gdn_norm_w, v_gdn_w_o, v_cnv_pw1_b, v_cnv_dw_w, v_cnv_dw_b, v_cnv_ln_g, v_cnv_ln_b, v_cnv_w_o, v_cnv_b_o, v_mix_w_out, v_ffn2_norm_pre, v_ffn2_norm_post, v_ffn2_w_in, v_ffn2_w_out):
    params = dict(zip(NAMES, (ffn1_norm_pre, ffn1_norm_post, ffn1_w_in, ffn1_w_out, mix_norm_pre, mix_norm_post, mix_w_in, gdn_conv_w, gdn_a_log, gdn_dt_bias, gdn_norm_w, gdn_w_o, cnv_pw1_b, cnv_dw_w, cnv_dw_b, cnv_ln_g, cnv_ln_b, cnv_w_o, cnv_b_o, mix_w_out, ffn2_norm_pre, ffn2_norm_post, ffn2_w_in, ffn2_w_out)))
    mom1 = dict(zip(NAMES, (m_ffn1_norm_pre, m_ffn1_norm_post, m_ffn1_w_in, m_ffn1_w_out, m_mix_norm_pre, m_mix_norm_post, m_mix_w_in, m_gdn_conv_w, m_gdn_a_log, m_gdn_dt_bias, m_gdn_norm_w, m_gdn_w_o, m_cnv_pw1_b, m_cnv_dw_w, m_cnv_dw_b, m_cnv_ln_g, m_cnv_ln_b, m_cnv_w_o, m_cnv_b_o, m_mix_w_out, m_ffn2_norm_pre, m_ffn2_norm_post, m_ffn2_w_in, m_ffn2_w_out)))
    mom2 = dict(zip(NAMES, (v_ffn1_norm_pre, v_ffn1_norm_post, v_ffn1_w_in, v_ffn1_w_out, v_mix_norm_pre, v_mix_norm_post, v_mix_w_in, v_gdn_conv_w, v_gdn_a_log, v_gdn_dt_bias, v_gdn_norm_w, v_gdn_w_o, v_cnv_pw1_b, v_cnv_dw_w, v_cnv_dw_b, v_cnv_ln_g, v_cnv_ln_b, v_cnv_w_o, v_cnv_b_o, v_mix_w_out, v_ffn2_norm_pre, v_ffn2_norm_post, v_ffn2_w_in, v_ffn2_w_out)))
    T, D = x.shape[1], x.shape[2]
    H = D // GDN_DK
    L = ffn1_norm_pre.shape[0]
    xi, yi, ci = _coords()
    dev = 4 * xi + 2 * yi + ci

    ag_names = BIG + SMALL_SHARDED

    def shard_to_send(n):
        if n in COL_SHARDED:
            return jnp.swapaxes(params[n], 1, 2).astype(BF16)
        return params[n].astype(BF16) if n in BIG else params[n]

    send = {n: shard_to_send(n) for n in ag_names}

    hosts, got, later = _Hosts(), {}, []
    for i in range(L):
        for host, names in (GATHER_HOSTS_FIRST if i == 0 else GATHER_HOSTS):
            def make(i=i, names=names):
                blocks = [send[n][i] for n in names]

                def deliver(outs):
                    got.update({(i, n): o for n, o in zip(names, outs)})
                return _GatherPlan(blocks), blocks, deliver
            if host is None:
                plan, blocks, deliver = make()
                deliver(_comm_only(plan, blocks, f"ag_weights_l{i}"))
            else:
                hosts.add(host.format(i=i, j=i - 1), make)

    cidx = jnp.reshape(ci, (1,)).astype(jnp.int32)
    grads, reduced = [{} for _ in range(L)], {}
    for i in range(L):
        for names, sib_host, chip_hosts in (REDUCE_HOSTS_FIRST if i == 0 else REDUCE_HOSTS):
            stage = {}

            def make_sib(i=i, names=names, stage=stage):
                Gs = []
                for n in names:
                    g = _mix_in_restore(grads[i][n], D, H) if n == "mix_w_in" else grads[i][n]
                    Gs.append(g.reshape(N_DEV, -1, g.shape[-1]))
                stage["G"] = dict(zip(names, Gs))

                def deliver(outs):
                    stage["R1"] = dict(zip(names, outs))
                return _SiblingPlan(Gs), Gs, deliver

            def make_chips(ns, i=i, stage=stage):
                Ps = [_pair_add(stage["G"][n], stage["R1"][n], cidx, f"l{i}_pair_add_{n}") for n in ns]

                def deliver(outs):
                    reduced.update({(i, n): o for n, o in zip(ns, outs)})
                return _ChipsPlan(Ps), Ps, deliver

            if sib_host is None:
                later.append((f"l{i}_{names[0]}", make_sib, [(lambda ns=ns, mc=make_chips: mc(ns)) for _, ns in chip_hosts]))
                continue
            hosts.add(sib_host.format(i=i, j=i - 1), make_sib)
            for host, ns in chip_hosts:
                hosts.add(host.format(i=i, j=i - 1), lambda ns=ns, mc=make_chips: mc(ns))

    weights = [_LayerWeights(got, params, i, D, H) for i in range(L)]
    loss_row, dx = _trunk_fwd_bwd(x[0], loss_target[0], H, L, lambda i: weights[i], grads, hosts)
    loss = lax.psum(loss_row[0, 0], ("x", "y", "c"))
    assert not hosts.waiting, sorted(hosts.waiting)

    for tag, make_sib, chip_makes in later:
        plan, Gs, deliver = make_sib()
        deliver(_comm_only(plan, Gs, f"rs_sibling_{tag}"))
        for mk in chip_makes:
            plan, Ps, deliver = mk()
            deliver(_comm_only(plan, Ps, f"rs_chips_{tag}"))
    R2s = {n: [jnp.swapaxes(reduced[(i, n)], 1, 2) if n in COL_SHARDED else reduced[(i, n)] for i in range(L)]
           for n in BIG}

    pieces = []
    for i in range(L):
        for n in SMALL:
            piece = grads[i][n].reshape(-1, LANES)
            pieces.append(jnp.pad(piece, ((0, (-piece.shape[0]) % 8), (0, 0))))
    packed = jnp.concatenate(pieces, axis=0)
    small_all = _comm_only(_GatherPlan([packed]), [packed], "ag_small_grads")[0]
    small_sum = _sum_parts(small_all, "sum_small_grads")
    small_g = {n: [None] * L for n in SMALL}
    off = 0
    for i in range(L):
        for n in SMALL:
            shape = grads[i][n].shape
            cnt = shape[0] * shape[1] // LANES
            g = small_sum[off:off + cnt].reshape(shape)
            off += cnt + (-cnt) % 8
            if n in ("gdn_a_log", "gdn_dt_bias"):
                g = g[:, :H]
            if n in SMALL_SHARDED:
                wloc = params[n].shape[-1]
                g = lax.dynamic_slice_in_dim(g, dev * wloc, wloc, axis=1)
            small_g[n][i] = g

    outs = {}
    for n in NAMES:
        w, m, v = params[n], mom1[n], mom2[n]
        if n in BIG:
            shape3, parts = w.shape, R2s[n]
        else:
            rows, cols = (w.shape[0] * w.shape[1], w.shape[2]) if w.ndim == 3 else w.shape
            shape3, parts = (1, rows, cols), [jnp.stack(small_g[n], axis=0).reshape(1, rows, cols)]
        res = _adamw(w.reshape(shape3), m.reshape(shape3), v.reshape(shape3), parts, "adamw_" + n)
        outs[n] = [r.reshape(w.shape) for r in res]

    result = [loss, dx[None]]
    for k in range(4):
        result += [outs[n][k] for n in NAMES]
    return tuple(result)
```

```python
import jax
import jax.numpy as jnp
from jax import lax
from jax.experimental import pallas as pl
from jax.experimental.pallas import tpu as pltpu

F32 = jnp.float32
BF16 = jnp.bfloat16

GDN_DK = 128
CHUNK = 64
GDN_CONV = 4
CNV_K = 31
RMS_EPS = 1e-6
LN_EPS = 1e-5
L2_EPS = 1e-6
ADAM_LR = 0.001
ADAM_B1 = 0.9
ADAM_B2 = 0.999
ADAM_EPS = 1e-08
ADAM_WD = 0.01
ADAM_STEP = 10

LANES = 128
SUB = 16
VMEM_LIMIT = 56 * 1024 * 1024
N_DEV = 8
MESH = pl.DeviceIdType.MESH


def _cparams(sem=None, **kw):
    if sem is not None:
        kw["dimension_semantics"] = sem
    return pltpu.CompilerParams(vmem_limit_bytes=VMEM_LIMIT, **kw)


def _tile(dim, target):
    best = None
    for t in range(LANES, min(dim, target) + 1, LANES):
        if dim % t == 0:
            best = t
    return best if best is not None else dim


def _sigmoid(x):
    return 1.0 / (1.0 + jnp.exp(-x))


def _silu(x):
    return x * _sigmoid(x)


def _dsilu(x):
    s = _sigmoid(x)
    return s * (1.0 + x * (1.0 - s))


MM_VMEM_BUDGET = 40 * 1024 * 1024
MM_MAX_TILE = 2048


def _mm_tiles(M, N, K, out_bytes):
    def cands(dim):
        c = [t for t in range(LANES, min(dim, MM_MAX_TILE) + 1, LANES) if dim % t == 0]
        return c or [dim]
    best = None
    for tm in cands(M):
        for tn in cands(N):
            vm = 2 * (2 * K * (tm + tn) + tm * tn * out_bytes)
            if vm <= MM_VMEM_BUDGET and (best is None or tm * tn > best[0] * best[1]):
                best = (tm, tn)
    return best if best is not None else (cands(M)[0], cands(N)[0])


def _matmul(a, b, mode, out_dtype, name, hosts=None, rows=None):
    if mode == "nn":
        (M, K), N = a.shape, b.shape[1]
    elif mode == "nt":
        (M, K), N = a.shape, b.shape[0]
    else:
        (K, M), N = a.shape, b.shape[1]
    first = 0
    if rows is not None:
        first, N = rows
    tm, tn = _mm_tiles(M, N, K, jnp.dtype(out_dtype).itemsize)
    assert first % tn == 0
    joff = first // tn
    if mode == "nn":
        a_spec = pl.BlockSpec((tm, K), lambda j, i: (i, 0))
        b_spec = pl.BlockSpec((K, tn), lambda j, i: (0, j))
        dn = (((1,), (0,)), ((), ()))
    elif mode == "nt":
        a_spec = pl.BlockSpec((tm, K), lambda j, i: (i, 0))
        b_spec = pl.BlockSpec((tn, K), lambda j, i: (j + joff, 0))
        dn = (((1,), (1,)), ((), ()))
    else:
        a_spec = pl.BlockSpec((K, tm), lambda j, i: (0, i))
        b_spec = pl.BlockSpec((K, tn), lambda j, i: (0, j))
        dn = (((0,), (0,)), ((), ()))

    def body(a_ref, b_ref, o_ref):
        o_ref[...] = lax.dot_general(a_ref[...], b_ref[...], dn, preferred_element_type=F32).astype(out_dtype)

    return _hosted_call(
        body, hosts, name=name, grid=(N // tn, M // tm), in_specs=[a_spec, b_spec],
        out_specs=[pl.BlockSpec((tm, tn), lambda j, i: (i, j))],
        out_shape=[jax.ShapeDtypeStruct((M, N), out_dtype)], scratch_shapes=[],
        args=(a, b), sem=("parallel", "parallel"))[0]


def _rowcall(name, body, T, tb, row_ins, par_ins, row_outs, acc_outs, hosts=None):
    n_ri, n_pi, n_ro = len(row_ins), len(par_ins), len(row_outs)

    def kern(*refs):
        ri, pi = refs[:n_ri], refs[n_ri:n_ri + n_pi]
        ro, ao = refs[n_ri + n_pi:n_ri + n_pi + n_ro], refs[n_ri + n_pi + n_ro:]
        if ao:
            @pl.when(pl.program_id(0) == 0)
            def _():
                for r in ao:
                    r[...] = jnp.zeros_like(r)
        body(ri, pi, ro, ao)

    in_specs = [pl.BlockSpec((tb, w), lambda i, cb=cb: (i, cb)) for (_, w, cb) in row_ins]
    in_specs += [pl.BlockSpec(p.shape, lambda i: (0, 0)) for p in par_ins]
    out_specs = [pl.BlockSpec((tb, w), lambda i: (i, 0)) for (w, _) in row_outs]
    out_specs += [pl.BlockSpec((1, w), lambda i: (0, 0)) for w in acc_outs]
    out_shape = [jax.ShapeDtypeStruct((T, w), dt) for (w, dt) in row_outs]
    out_shape += [jax.ShapeDtypeStruct((1, w), F32) for w in acc_outs]
    return _hosted_call(
        kern, hosts, name=name, grid=(T // tb,), in_specs=in_specs, out_specs=out_specs, out_shape=out_shape,
        scratch_shapes=[], args=(*[a for (a, _, _) in row_ins], *par_ins), sem=("arbitrary",))


def _rsum(x):
    return jnp.sum(x, axis=0, keepdims=True)


def _rms_rstd(x):
    return lax.rsqrt(jnp.mean(x * x, axis=-1, keepdims=True) + RMS_EPS)


def _rms_fwd(x, w, name):
    T, D = x.shape

    def body(ri, pi, ro, ao):
        xv = ri[0][...]
        ro[0][...] = (xv * _rms_rstd(xv) * pi[0][...]).astype(BF16)

    return _rowcall(name, body, T, 256, [(x, D, 0)], [w], [(D, BF16)], [])[0]


def _rms_bwd_core(dy, x, w):
    rs = _rms_rstd(x)
    xh = x * rs
    gw = dy * w
    dx = rs * (gw - xh * jnp.mean(gw * xh, axis=-1, keepdims=True))
    return dx, dy * xh


def _pre_bwd(dh, x, w, dres, name, hosts=None):
    T, D = x.shape

    def body(ri, pi, ro, ao):
        dx, dwc = _rms_bwd_core(ri[0][...], ri[1][...], pi[0][...])
        ro[0][...] = ri[2][...] + dx
        ao[0][...] += _rsum(dwc)

    return _rowcall(name, body, T, 256, [(dh, D, 0), (x, D, 0), (dres, D, 0)], [w], [(D, F32)], [D], hosts=hosts)


def _post_fwd(x, f, w, r, name):
    T, D = x.shape

    def body(ri, pi, ro, ao):
        fv = ri[1][...]
        ro[0][...] = ri[0][...] + r * (fv * _rms_rstd(fv) * pi[0][...])

    return _rowcall(name, body, T, 256, [(x, D, 0), (f, D, 0)], [w], [(D, F32)], [])[0]


def _post_bwd(dxn, f, w, r, name):
    T, D = f.shape

    def body(ri, pi, ro, ao):
        df, dwc = _rms_bwd_core(r * ri[0][...], ri[1][...], pi[0][...])
        ro[0][...] = df.astype(BF16)
        ao[0][...] += _rsum(dwc)

    return _rowcall(name, body, T, 256, [(dxn, D, 0), (f, D, 0)], [w], [(D, BF16)], [D])


def _swiglu_fwd(a, name):
    T, F2 = a.shape
    F = F2 // 2

    def body(ri, pi, ro, ao):
        ro[0][...] = (_silu(ri[0][...].astype(F32)) * ri[1][...].astype(F32)).astype(BF16)

    return _rowcall(name, body, T, 256, [(a, F, 0), (a, F, 1)], [], [(F, BF16)], [])[0]


def _swiglu_bwd(ds, a, name, hosts=None):
    T, F2 = a.shape
    F = F2 // 2

    def body(ri, pi, ro, ao):
        dsv, g, u = ri[0][...].astype(F32), ri[1][...].astype(F32), ri[2][...].astype(F32)
        ro[0][:, :F] = (dsv * u * _dsilu(g)).astype(BF16)
        ro[0][:, F:] = (dsv * _silu(g)).astype(BF16)

    return _rowcall(name, body, T, 256, [(ds, F, 0), (a, F, 0), (a, F, 1)], [], [(F2, BF16)], [], hosts=hosts)[0]


def _loss_fwd_bwd(y, tgt, name):
    T, D = y.shape

    def body(ri, pi, ro, ao):
        e = ri[0][...] - ri[1][...]
        ro[0][...] = e * (1.0 / D)
        tot = jnp.sum(_rsum(e * e), axis=1, keepdims=True) * (0.5 / D)
        ao[0][...] += jnp.broadcast_to(tot, (1, LANES))

    return _rowcall(name, body, T, 256, [(y, D, 0), (tgt, D, 0)], [], [(D, F32)], [LANES])


def _gdn_gate_fwd(o, p, nw, name):
    T, D = o.shape
    H = D // GDN_DK

    def body(ri, pi, ro, ao):
        for h in range(H):
            sl = slice(h * GDN_DK, (h + 1) * GDN_DK)
            oh = ri[0][:, sl]
            ro[0][:, sl] = (oh * _rms_rstd(oh) * pi[0][...] * _silu(ri[1][:, sl].astype(F32))).astype(BF16)

    return _rowcall(name, body, T, 256, [(o, D, 0), (p, D, 3)], [nw], [(D, BF16)], [])[0]


def _gdn_gate_bwd(dog, o, p, nw, name, hosts=None):
    T, D = o.shape
    H = D // GDN_DK

    def body(ri, pi, ro, ao):
        acc = jnp.zeros((1, GDN_DK), F32)
        for h in range(H):
            sl = slice(h * GDN_DK, (h + 1) * GDN_DK)
            dy, oh, z = ri[0][:, sl], ri[1][:, sl], ri[2][:, sl].astype(F32)
            sz = _silu(z)
            do, dwc = _rms_bwd_core(dy * sz, oh, pi[0][...])
            ro[0][:, sl] = do
            ro[1][:, sl] = (dy * oh * _rms_rstd(oh) * pi[0][...] * _dsilu(z)).astype(BF16)
            acc = acc + _rsum(dwc)
        ao[0][...] += acc

    return _rowcall(name, body, T, 256, [(dog, D, 0), (o, D, 0), (p, D, 3)], [nw], [(D, F32), (D, BF16)], [GDN_DK],
                    hosts=hosts)


def _glu_fwd(p, b, name):
    T = p.shape[0]
    D = b.shape[1] // 2

    def body(ri, pi, ro, ao):
        ro[0][...] = (ri[0][...].astype(F32) + pi[0][:, :D]) * _sigmoid(ri[1][...].astype(F32) + pi[0][:, D:])

    return _rowcall(name, body, T, 256, [(p, D, 4), (p, D, 5)], [b], [(D, F32)], [])[0]


def _glu_bwd(dhc, p, b, name):
    T = p.shape[0]
    D = b.shape[1] // 2

    def body(ri, pi, ro, ao):
        d, a, g = ri[0][...], ri[1][...].astype(F32) + pi[0][:, :D], ri[2][...].astype(F32) + pi[0][:, D:]
        sg = _sigmoid(g)
        da, dg = d * sg, d * a * sg * (1.0 - sg)
        ro[0][:, :D] = da.astype(BF16)
        ro[0][:, D:] = dg.astype(BF16)
        ao[0][:, :D] += _rsum(da)
        ao[0][:, D:] += _rsum(dg)

    return _rowcall(name, body, T, 256, [(dhc, D, 0), (p, D, 4), (p, D, 5)], [b], [(2 * D, BF16)], [2 * D])


def _ln_stats(x):
    mu = jnp.mean(x, axis=-1, keepdims=True)
    xc = x - mu
    rstd = lax.rsqrt(jnp.mean(xc * xc, axis=-1, keepdims=True) + LN_EPS)
    return xc * rstd, rstd


def _ln_silu_fwd(hcv, g, b, name):
    T, D = hcv.shape

    def body(ri, pi, ro, ao):
        xh, _ = _ln_stats(ri[0][...])
        ro[0][...] = _silu(xh * pi[0][...] + pi[1][...]).astype(BF16)

    return _rowcall(name, body, T, 256, [(hcv, D, 0)], [g, b], [(D, BF16)], [])[0]


def _ln_silu_bwd(dhl, hcv, g, b, name, hosts=None):
    T, D = hcv.shape

    def body(ri, pi, ro, ao):
        xh, rstd = _ln_stats(ri[1][...])
        dyl = ri[0][...] * _dsilu(xh * pi[0][...] + pi[1][...])
        dxh = dyl * pi[0][...]
        dx = rstd * (dxh - jnp.mean(dxh, axis=-1, keepdims=True) - xh * jnp.mean(dxh * xh, axis=-1, keepdims=True))
        ro[0][...] = dx
        ao[0][...] += _rsum(dyl * xh)
        ao[1][...] += _rsum(dyl)
        ao[2][...] += _rsum(dx)

    return _rowcall(name, body, T, 256, [(dhl, D, 0), (hcv, D, 0)], [g, b], [(D, F32)], [D, D, D], hosts=hosts)


def _merge_fwd(p, ya, yb, bo, name):
    T, D = ya.shape

    def body(ri, pi, ro, ao):
        ga, gb = _sigmoid(ri[0][...].astype(F32)), _sigmoid(ri[1][...].astype(F32))
        ro[0][...] = (ga * ri[2][...] + gb * (ri[3][...] + pi[0][...])).astype(BF16)

    return _rowcall(name, body, T, 256, [(p, D, 6), (p, D, 7), (ya, D, 0), (yb, D, 0)], [bo], [(D, BF16)], [])[0]


def _merge_bwd(dym, p, ya, yb, bo, name, hosts=None):
    T, D = ya.shape

    def body(ri, pi, ro, ao):
        d = ri[0][...]
        ga, gb = _sigmoid(ri[1][...].astype(F32)), _sigmoid(ri[2][...].astype(F32))
        ybv = ri[4][...] + pi[0][...]
        dyb = d * gb
        ro[0][...] = (d * ga).astype(BF16)
        ro[1][...] = dyb.astype(BF16)
        ro[2][:, :D] = (d * ri[3][...] * ga * (1.0 - ga)).astype(BF16)
        ro[2][:, D:] = (d * ybv * gb * (1.0 - gb)).astype(BF16)
        ao[0][...] += _rsum(dyb)

    return _rowcall(name, body, T, 256, [(dym, D, 0), (p, D, 6), (p, D, 7), (ya, D, 0), (yb, D, 0)], [bo],
                    [(D, BF16), (D, BF16), (2 * D, BF16)], [D], hosts=hosts)


PAD = 32
RC = 256


def _tap_windows(ref, offs):
    groups = {}
    for j, o in enumerate(offs):
        groups.setdefault(o % 8, []).append((j, o))
    for grp in groups.values():
        lo, hi = min(o for _, o in grp), max(o for _, o in grp)
        win = ref[pl.ds(lo, RC + hi - lo), :]
        for j, o in grp:
            yield j, win[o - lo:o - lo + RC]


def _causal_taps(xp_ref, w, K, c0):
    acc = None
    for j, xs in _tap_windows(xp_ref, [PAD - (K - 1) + j + c0 for j in range(K)]):
        term = w[j:j + 1, :] * xs
        acc = term if acc is None else acc + term
    return acc


def _anticausal_taps(dp_ref, w, K, c0):
    acc = None
    for j, ds in _tap_windows(dp_ref, [(K - 1) - j + c0 for j in range(K)]):
        term = w[j:j + 1, :] * ds
        acc = term if acc is None else acc + term
    return acc


def _tap_grads(dw_ref, dc_ref, xp_ref, K, T):
    accs = [jnp.zeros((8, LANES), F32) for _ in range(K)]
    for c in range(T // RC):
        d = dc_ref[pl.ds(c * RC, RC), :]
        for j, xs in _tap_windows(xp_ref, [PAD - (K - 1) + j + c * RC for j in range(K)]):
            accs[j] = accs[j] + jnp.sum((d * xs).reshape(RC // 8, 8, LANES), axis=0)
    for j in range(K):
        dw_ref[j:j + 1, :] = _rsum(accs[j])


def _qkv_conv_fwd(p, cw, H, name, hosts=None):
    T = p.shape[0]
    K = cw.shape[0]

    def body(x_ref, w_ref, o_ref, xp_ref):
        j = pl.program_id(0)
        xp_ref[pl.ds(0, PAD), :] = jnp.zeros((PAD, LANES), F32)
        xp_ref[pl.ds(PAD, T), :] = x_ref[...].astype(F32)
        w = w_ref[...]
        scale = jnp.where(j < H, GDN_DK ** -0.5, 1.0).astype(F32)
        for c in range(T // RC):
            act = _silu(_causal_taps(xp_ref, w, K, c * RC))
            nrm = act * lax.rsqrt(jnp.sum(act * act, axis=-1, keepdims=True) + L2_EPS) * scale
            o_ref[pl.ds(c * RC, RC), :] = jnp.where(j < 2 * H, nrm, act)

    return _hosted_call(
        body, hosts, name=name, grid=(3 * H,),
        in_specs=[pl.BlockSpec((T, LANES), lambda j: (0, j)), pl.BlockSpec((K, LANES), lambda j: (0, j))],
        out_specs=[pl.BlockSpec((T, LANES), lambda j: (0, j))],
        out_shape=[jax.ShapeDtypeStruct((T, 3 * H * GDN_DK), F32)],
        scratch_shapes=[pltpu.VMEM((T + PAD, LANES), F32)], args=(p, cw), sem=("arbitrary",))[0]


def _qkv_conv_bwd(dn, p, cw, H, name, hosts=None):
    T = p.shape[0]
    K = cw.shape[0]

    def body(dn_ref, x_ref, w_ref, dx_ref, dw_ref, xp_ref, dc_ref):
        j = pl.program_id(0)
        xp_ref[pl.ds(0, PAD), :] = jnp.zeros((PAD, LANES), F32)
        xp_ref[pl.ds(PAD, T), :] = x_ref[...].astype(F32)
        dc_ref[pl.ds(T, PAD), :] = jnp.zeros((PAD, LANES), F32)
        w = w_ref[...]
        scale = jnp.where(j < H, GDN_DK ** -0.5, 1.0).astype(F32)
        for c in range(T // RC):
            pre = _causal_taps(xp_ref, w, K, c * RC)
            act = _silu(pre)
            d = dn_ref[pl.ds(c * RC, RC), :]
            rs = lax.rsqrt(jnp.sum(act * act, axis=-1, keepdims=True) + L2_EPS)
            nh = act * rs
            dact_n = scale * rs * (d - nh * jnp.sum(d * nh, axis=-1, keepdims=True))
            dact = jnp.where(j < 2 * H, dact_n, d)
            dc_ref[pl.ds(c * RC, RC), :] = dact * _dsilu(pre)
        for c in range(T // RC):
            dx_ref[pl.ds(c * RC, RC), :] = _anticausal_taps(dc_ref, w, K, c * RC).astype(BF16)
        _tap_grads(dw_ref, dc_ref, xp_ref, K, T)

    return _hosted_call(
        body, hosts, name=name, grid=(3 * H,),
        in_specs=[pl.BlockSpec((T, LANES), lambda j: (0, j)), pl.BlockSpec((T, LANES), lambda j: (0, j)),
                  pl.BlockSpec((K, LANES), lambda j: (0, j))],
        out_specs=[pl.BlockSpec((T, LANES), lambda j: (0, j)), pl.BlockSpec((K, LANES), lambda j: (0, j))],
        out_shape=[jax.ShapeDtypeStruct((T, 3 * H * GDN_DK), BF16), jax.ShapeDtypeStruct(cw.shape, F32)],
        scratch_shapes=[pltpu.VMEM((T + PAD, LANES), F32), pltpu.VMEM((T + PAD, LANES), F32)],
        args=(dn, p, cw), sem=("arbitrary",))


def _dw_conv_fwd(hc, w, b, name):
    T, D = hc.shape
    K = w.shape[0]

    def body(x_ref, w_ref, b_ref, o_ref, xp_ref):
        xp_ref[pl.ds(0, PAD), :] = jnp.zeros((PAD, LANES), F32)
        xp_ref[pl.ds(PAD, T), :] = x_ref[...]
        wv = w_ref[...]
        for c in range(T // RC):
            o_ref[pl.ds(c * RC, RC), :] = _causal_taps(xp_ref, wv, K, c * RC) + b_ref[...]

    return pl.pallas_call(
        body, name=name, grid=(D // LANES,),
        in_specs=[pl.BlockSpec((T, LANES), lambda j: (0, j)), pl.BlockSpec((K, LANES), lambda j: (0, j)),
                  pl.BlockSpec((1, LANES), lambda j: (0, j))],
        out_specs=pl.BlockSpec((T, LANES), lambda j: (0, j)),
        out_shape=jax.ShapeDtypeStruct((T, D), F32),
        scratch_shapes=[pltpu.VMEM((T + PAD, LANES), F32)],
        compiler_params=_cparams(("arbitrary",)),
    )(hc, w, b)


def _dw_conv_bwd(dy, hc, w, name, hosts=None):
    T, D = hc.shape
    K = w.shape[0]

    def body(dy_ref, x_ref, w_ref, dx_ref, dw_ref, xp_ref, dc_ref):
        xp_ref[pl.ds(0, PAD), :] = jnp.zeros((PAD, LANES), F32)
        xp_ref[pl.ds(PAD, T), :] = x_ref[...]
        dc_ref[pl.ds(T, PAD), :] = jnp.zeros((PAD, LANES), F32)
        dc_ref[pl.ds(0, T), :] = dy_ref[...]
        wv = w_ref[...]
        for c in range(T // RC):
            dx_ref[pl.ds(c * RC, RC), :] = _anticausal_taps(dc_ref, wv, K, c * RC)
        _tap_grads(dw_ref, dc_ref, xp_ref, K, T)

    return _hosted_call(
        body, hosts, name=name, grid=(D // LANES,), sem=("arbitrary",),
        in_specs=[pl.BlockSpec((T, LANES), lambda j: (0, j)), pl.BlockSpec((T, LANES), lambda j: (0, j)),
                  pl.BlockSpec((K, LANES), lambda j: (0, j))],
        out_specs=[pl.BlockSpec((T, LANES), lambda j: (0, j)), pl.BlockSpec((K, LANES), lambda j: (0, j))],
        out_shape=[jax.ShapeDtypeStruct((T, D), F32), jax.ShapeDtypeStruct(w.shape, F32)],
        scratch_shapes=[pltpu.VMEM((T + PAD, LANES), F32), pltpu.VMEM((T + PAD, LANES), F32)],
        args=(dy, hc, w))


NN = (((1,), (0,)), ((), ()))
NT = (((1,), (1,)), ((), ()))
TN = (((0,), (0,)), ((), ()))


def _dotb(a, b, dn=NN):
    return lax.dot_general(a.astype(BF16), b.astype(BF16), dn, preferred_element_type=F32)


def _split_bf16(x, n):
    parts, r = [], x
    for _ in range(n):
        p = r.astype(BF16)
        parts.append(p)
        r = r - p.astype(F32)
    return parts


def _dot_sel(sel, x, pieces, sel_left=True):
    sb = sel.astype(BF16)
    acc = None
    for p in _split_bf16(x, pieces):
        t = (lax.dot_general(sb, p, NN, preferred_element_type=F32) if sel_left
             else lax.dot_general(p, sb, NN, preferred_element_type=F32))
        acc = t if acc is None else acc + t
    return acc


def _iota2(shape, axis):
    return lax.broadcasted_iota(jnp.int32, shape, axis)


def _to_row(col, eye):
    return jnp.sum(jnp.where(eye, col, 0.0), axis=0, keepdims=True)


def _to_col(row, eye):
    return jnp.sum(jnp.where(eye, row, 0.0), axis=1, keepdims=True)


def _gdn_gates(bl, al, alog, dtb):
    beta = _sigmoid(bl)
    x = al + dtb
    sp = jnp.maximum(x, 0.0) + jnp.log(1.0 + jnp.exp(-jnp.abs(x)))
    g = -jnp.exp(alog) * sp
    r, c = _iota2((CHUNK, CHUNK), 0), _iota2((CHUNK, CHUNK), 1)
    G = _dot_sel(r >= c, g, 3)
    return beta, g, G, x


def _unit_lower_inverses(As, Ats):
    n = len(As)
    nb = CHUNK // SUB
    lane = _iota2((SUB, CHUNK), 1)
    row = _iota2((SUB, CHUNK), 0)
    Atp = []
    for At in Ats:
        acc = jnp.zeros((SUB, CHUNK), F32)
        for b in range(nb):
            acc = jnp.where(lane // SUB == b, At[b * SUB:(b + 1) * SUB, :], acc)
        Atp.append(acc)
    gr, gc = _iota2((CHUNK, CHUNK), 0), _iota2((CHUNK, CHUNK), 1)
    ones_bd = gr // SUB == gc // SUB
    stack = jnp.concatenate(
        [jnp.where(lane % SUB == i, Atp[m], 0.0) for i in range(1, SUB) for m in range(n)], axis=0)
    Cm = _dot_sel(ones_bd, stack, 2, sel_left=False)
    Z = [(row == lane % SUB).astype(F32) for _ in range(n)]
    for i in range(1, SUB):
        for m in range(n):
            cm = Cm[((i - 1) * n + m) * SUB:((i - 1) * n + m + 1) * SUB, :]
            new = -jnp.sum(cm * Z[m], axis=0, keepdims=True)
            Z[m] = Z[m] + jnp.where(row == i, new, 0.0)
    bd = gr // SUB == gc // SUB
    Xs = [jnp.where(bd, jnp.concatenate([Z[m]] * nb, axis=0), 0.0) for m in range(n)]
    blk = SUB
    while blk < CHUNK:
        off = (gr // (2 * blk) == gc // (2 * blk)) & (gr // blk != gc // blk)
        Ys = [_dotb(Xs[m], jnp.where(off, As[m], 0.0)) for m in range(n)]
        Xs = [Xs[m] - _dotb(Ys[m], Xs[m]) for m in range(n)]
        blk *= 2
    return Xs


def _gdn_fwd(qkvn, p, alog, dtb, H, name, hosts=None):
    T = qkvn.shape[0]
    D = H * GDN_DK
    N = T // CHUNK
    bblk = 0

    def body(q_ref, k_ref, v_ref, b_ref, a_ref, alog_ref, dtb_ref, o_ref, t_ref, s_ref, S_scr):
        @pl.when(pl.program_id(0) == 0)
        def _():
            S_scr[...] = jnp.zeros_like(S_scr)

        beta, _, G, _ = _gdn_gates(b_ref[...], a_ref[...], alog_ref[...], dtb_ref[...])
        hs = range(H)
        sl = [slice(h * GDN_DK, (h + 1) * GDN_DK) for h in hs]
        q, k, v = [q_ref[:, s] for s in sl], [k_ref[:, s] for s in sl], [v_ref[:, s] for s in sl]
        Gc, bc = [G[:, h:h + 1] for h in hs], [beta[:, h:h + 1] for h in hs]
        r, c = _iota2((CHUNK, CHUNK), 0), _iota2((CHUNK, CHUNK), 1)
        eye, low, up = r == c, r >= c, r <= c
        Gr, br = [_to_row(Gc[h], eye) for h in hs], [_to_row(bc[h], eye) for h in hs]
        Dm = [jnp.where(low, jnp.exp(jnp.where(low, Gc[h] - Gr[h], 0.0)), 0.0) for h in hs]
        Dt = [jnp.where(up, jnp.exp(jnp.where(up, Gr[h] - Gc[h], 0.0)), 0.0) for h in hs]
        qk = [_dotb(jnp.concatenate([q[h], k[h]], axis=0), k[h], NT) for h in hs]
        QK = [qk[h][:CHUNK] * Dm[h] for h in hs]
        KK = [qk[h][CHUNK:] for h in hs]
        As = [jnp.where(r > c, KK[h] * Dm[h], 0.0) * bc[h] for h in hs]
        Ats = [jnp.where(r < c, KK[h] * Dt[h], 0.0) * br[h] for h in hs]
        Ts = _unit_lower_inverses(As, Ats)
        eG = [jnp.exp(Gc[h]) for h in hs]
        Gl = [Gc[h][CHUNK - 1:CHUNK, :] for h in hs]
        uw = [_dotb(Ts[h], jnp.concatenate([v[h] * bc[h], k[h] * (bc[h] * eG[h])], axis=1)) for h in hs]
        S = [S_scr[h] for h in hs]
        qw = [_dotb(jnp.concatenate([q[h] * eG[h], uw[h][:, GDN_DK:]], axis=0), S[h]) for h in hs]
        vn = [uw[h][:, :GDN_DK] - qw[h][CHUNK:] for h in hs]
        o = [qw[h][:CHUNK] + _dotb(QK[h], vn[h]) for h in hs]
        Sn = [S[h] * jnp.exp(Gl[h]) + _dotb(k[h] * jnp.exp(Gl[h] - Gc[h]), vn[h], TN) for h in hs]
        for h in hs:
            t_ref[0, h] = Ts[h]
            s_ref[0, h] = S[h]
            o_ref[:, sl[h]] = o[h]
            S_scr[h] = Sn[h]

    qkv_spec = [pl.BlockSpec((CHUNK, D), lambda n, cb=cb: (n, cb)) for cb in range(3)]
    return _hosted_call(
        body, hosts, name=name, grid=(N,), sem=("arbitrary",),
        in_specs=qkv_spec + [pl.BlockSpec((CHUNK, LANES), lambda n: (n, bblk)),
                             pl.BlockSpec((CHUNK, LANES), lambda n: (n, bblk + 1)),
                             pl.BlockSpec((1, LANES), lambda n: (0, 0)), pl.BlockSpec((1, LANES), lambda n: (0, 0))],
        out_specs=[pl.BlockSpec((CHUNK, D), lambda n: (n, 0)),
                   pl.BlockSpec((1, H, CHUNK, CHUNK), lambda n: (n, 0, 0, 0)),
                   pl.BlockSpec((1, H, GDN_DK, GDN_DK), lambda n: (n, 0, 0, 0))],
        out_shape=[jax.ShapeDtypeStruct((T, D), F32), jax.ShapeDtypeStruct((N, H, CHUNK, CHUNK), F32),
                   jax.ShapeDtypeStruct((N, H, GDN_DK, GDN_DK), F32)],
        scratch_shapes=[pltpu.VMEM((H, GDN_DK, GDN_DK), F32)],
        args=(qkvn, qkvn, qkvn, p, p, alog, dtb))


def _gdn_bwd(do, qkvn, p, alog, dtb, Tinv, Sin, H, name, hosts=None):
    T = qkvn.shape[0]
    D = H * GDN_DK
    N = T // CHUNK
    bblk = 0

    def body(do_ref, q_ref, k_ref, v_ref, b_ref, a_ref, alog_ref, dtb_ref, t_ref, s_ref,
             dqkv_ref, dba_ref, dalog_ref, ddtb_ref, dS_scr):
        @pl.when(pl.program_id(0) == 0)
        def _():
            dS_scr[...] = jnp.zeros_like(dS_scr)
            dalog_ref[...] = jnp.zeros_like(dalog_ref)
            ddtb_ref[...] = jnp.zeros_like(ddtb_ref)

        beta, g, G, x = _gdn_gates(b_ref[...], a_ref[...], alog_ref[...], dtb_ref[...])
        r, c = _iota2((CHUNK, CHUNK), 0), _iota2((CHUNK, CHUNK), 1)
        eye, low, strict = r == c, r >= c, r > c
        lane = _iota2((CHUNK, LANES), 1)
        rsum1 = lambda a: jnp.sum(a, axis=1, keepdims=True)
        hs = range(H)
        sl = [slice(h * GDN_DK, (h + 1) * GDN_DK) for h in hs]
        q, k, v = [q_ref[:, s] for s in sl], [k_ref[:, s] for s in sl], [v_ref[:, s] for s in sl]
        dov = [do_ref[:, s] for s in sl]
        Gc, bc = [G[:, h:h + 1] for h in hs], [beta[:, h:h + 1] for h in hs]
        Tm, S, dSo = [t_ref[0, h] for h in hs], [s_ref[0, h] for h in hs], [dS_scr[h] for h in hs]
        Gr = [_to_row(Gc[h], eye) for h in hs]
        Dm = [jnp.where(low, jnp.exp(jnp.where(low, Gc[h] - Gr[h], 0.0)), 0.0) for h in hs]
        eG = [jnp.exp(Gc[h]) for h in hs]
        Gl = [Gc[h][CHUNK - 1:CHUNK, :] for h in hs]
        eR, dch = [jnp.exp(Gl[h] - Gc[h]) for h in hs], [jnp.exp(Gl[h]) for h in hs]
        qk = [_dotb(jnp.concatenate([q[h], k[h]], axis=0), k[h], NT) for h in hs]
        QKr, KK = [qk[h][:CHUNK] for h in hs], [qk[h][CHUNK:] for h in hs]
        QK = [QKr[h] * Dm[h] for h in hs]
        M = [jnp.where(strict, KK[h] * Dm[h], 0.0) for h in hs]
        uw = [_dotb(Tm[h], jnp.concatenate([v[h] * bc[h], k[h] * (bc[h] * eG[h])], axis=1)) for h in hs]
        u, w = [uw[h][:, :GDN_DK] for h in hs], [uw[h][:, GDN_DK:] for h in hs]
        qd, kd = [q[h] * eG[h] for h in hs], [k[h] * eR[h] for h in hs]
        vn = [u[h] - _dotb(w[h], S[h]) for h in hs]
        dvn = [_dotb(QK[h], dov[h], TN) + _dotb(kd[h], dSo[h]) for h in hs]
        dQK = [jnp.where(low, _dotb(dov[h], vn[h], NT), 0.0) for h in hs]
        dkd = [_dotb(vn[h], dSo[h], NT) for h in hs]
        ddch = [jnp.sum(rsum1(dSo[h] * S[h]), axis=0, keepdims=True) for h in hs]
        dd = [jnp.concatenate([dov[h], dvn[h]], axis=0) for h in hs]
        xs = [_dotb(dd[h], S[h], NT) for h in hs]
        dqd, dw = [xs[h][:CHUNK] for h in hs], [-xs[h][CHUNK:] for h in hs]
        dS = [_dotb(jnp.concatenate([qd[h], -w[h]], axis=0), dd[h], TN) + dch[h] * dSo[h] for h in hs]
        yb = [_dotb(Tm[h], jnp.concatenate([dvn[h], dw[h]], axis=1), TN) for h in hs]
        dvb, dkb = [yb[h][:, :GDN_DK] for h in hs], [yb[h][:, GDN_DK:] for h in hs]
        dA = [-jnp.where(strict, _dotb(yb[h], uw[h], NT), 0.0) for h in hs]
        rk = [rsum1(dkb[h] * k[h]) for h in hs]
        dbeta = [rsum1(dvb[h] * v[h]) + rk[h] * eG[h] + rsum1(dA[h] * M[h]) for h in hs]
        dM = [dA[h] * bc[h] for h in hs]
        dKK = [dM[h] * Dm[h] for h in hs]
        dQKr = [dQK[h] * Dm[h] for h in hs]
        E = [dM[h] * M[h] + dQK[h] * QK[h] for h in hs]
        zk = [_dotb(jnp.concatenate([dQKr[h], dKK[h]], axis=0), k[h]) for h in hs]
        dq = [zk[h][:CHUNK] + dqd[h] * eG[h] for h in hs]
        dk = [dkb[h] * (bc[h] * eG[h]) + zk[h][CHUNK:] + _dotb(dKK[h], k[h], TN) + _dotb(dQKr[h], q[h], TN)
              + dkd[h] * eR[h] for h in hs]
        deG = [rk[h] * bc[h] + rsum1(dqd[h] * q[h]) for h in hs]
        deR = [rsum1(dkd[h] * k[h]) for h in hs]
        dGl = [jnp.sum(deR[h] * eR[h], axis=0, keepdims=True) + ddch[h] * dch[h] for h in hs]
        dGc = [rsum1(E[h]) - _to_col(jnp.sum(E[h], axis=0, keepdims=True), eye) + deG[h] * eG[h] - deR[h] * eR[h]
               + jnp.where(r[:, :1] == CHUNK - 1, dGl[h], 0.0) for h in hs]
        dG_all = jnp.zeros((CHUNK, LANES), F32)
        dbeta_all = jnp.zeros((CHUNK, LANES), F32)
        for h in hs:
            dS_scr[h] = dS[h]
            dqkv_ref[:, sl[h]] = dq[h]
            dqkv_ref[:, D + h * GDN_DK:D + (h + 1) * GDN_DK] = dk[h]
            dqkv_ref[:, 2 * D + h * GDN_DK:2 * D + (h + 1) * GDN_DK] = dvb[h] * bc[h]
            dG_all = jnp.where(lane == h, dGc[h], dG_all)
            dbeta_all = jnp.where(lane == h, dbeta[h], dbeta_all)
        dg = _dot_sel(r <= c, dG_all, 3)
        da = dg * (-jnp.exp(alog_ref[...])) * _sigmoid(x)
        dba_ref[:, :LANES] = (dbeta_all * beta * (1.0 - beta)).astype(BF16)
        dba_ref[:, LANES:] = da.astype(BF16)
        dalog_ref[...] += _rsum(dg * g)
        ddtb_ref[...] += _rsum(da)

    rev = lambda n: N - 1 - n
    qkv_spec = [pl.BlockSpec((CHUNK, D), lambda n, cb=cb: (rev(n), cb)) for cb in range(3)]
    return _hosted_call(
        body, hosts, name=name, grid=(N,), sem=("arbitrary",),
        in_specs=[pl.BlockSpec((CHUNK, D), lambda n: (rev(n), 0))] + qkv_spec + [
            pl.BlockSpec((CHUNK, LANES), lambda n: (rev(n), bblk)),
            pl.BlockSpec((CHUNK, LANES), lambda n: (rev(n), bblk + 1)),
            pl.BlockSpec((1, LANES), lambda n: (0, 0)), pl.BlockSpec((1, LANES), lambda n: (0, 0)),
            pl.BlockSpec((1, H, CHUNK, CHUNK), lambda n: (rev(n), 0, 0, 0)),
            pl.BlockSpec((1, H, GDN_DK, GDN_DK), lambda n: (rev(n), 0, 0, 0))],
        out_specs=[pl.BlockSpec((CHUNK, 3 * D), lambda n: (rev(n), 0)),
                   pl.BlockSpec((CHUNK, 2 * LANES), lambda n: (rev(n), 0)),
                   pl.BlockSpec((1, LANES), lambda n: (0, 0)), pl.BlockSpec((1, LANES), lambda n: (0, 0))],
        out_shape=[jax.ShapeDtypeStruct((T, 3 * D), F32), jax.ShapeDtypeStruct((T, 2 * LANES), BF16),
                   jax.ShapeDtypeStruct((1, LANES), F32), jax.ShapeDtypeStruct((1, LANES), F32)],
        scratch_shapes=[pltpu.VMEM((H, GDN_DK, GDN_DK), F32)],
        args=(do, qkvn, qkvn, qkvn, p, p, alog, dtb, Tinv, Sin))


def _mix_in_reorder(wt, D, H):
    o1 = 4 * D
    o2, o3 = o1 + H, o1 + 2 * H
    z = jnp.zeros((LANES - H, wt.shape[1]), wt.dtype)
    return jnp.concatenate([wt[:o1], wt[o3:], wt[o1:o2], z, wt[o2:o3], z], axis=0)


def _mix_in_restore(dwt, D, H):
    b0 = 8 * D
    return jnp.concatenate([dwt[:4 * D], dwt[b0:b0 + H], dwt[b0 + LANES:b0 + LANES + H], dwt[4 * D:b0]], axis=0)


def _ffn_fwd(x, W, pre, tag, hosts=None):
    h = _rms_fwd(x, W[pre + "_norm_pre"], tag + "_pre")
    a = _matmul(h, W[pre + "_w_in"], "nt", BF16, tag + "_in", hosts)
    s = _swiglu_fwd(a, tag + "_act")
    f = _matmul(s, W[pre + "_w_out"], "nn", F32, tag + "_out", hosts)
    return _post_fwd(x, f, W[pre + "_norm_post"], 0.5, tag + "_post"), (x, h, a, s, f)


def _ffn_bwd(dxn, saved, W, pre, tag, g, hosts=None):
    x, h, a, s, f = saved
    df, g[pre + "_norm_post"] = _post_bwd(dxn, f, W[pre + "_norm_post"], 0.5, tag + "_dpost")
    ds = _matmul(df, W[pre + "_w_out"], "nt", BF16, tag + "_ds", hosts)
    g[pre + "_w_out"] = _matmul(s, df, "tn", BF16, tag + "_dwout")
    da = _swiglu_bwd(ds, a, tag + "_dact", hosts)
    dh = _matmul(da, W[pre + "_w_in"], "nn", F32, tag + "_dh", hosts)
    g[pre + "_w_in"] = _matmul(da, h, "tn", BF16, tag + "_dwin", hosts)
    dx, g[pre + "_norm_pre"] = _pre_bwd(dh, x, W[pre + "_norm_pre"], dxn, tag + "_dpre")
    return dx


def _mix_fwd(x, W, H, tag, hosts=None):
    h = _rms_fwd(x, W["mix_norm_pre"], tag + "_pre")
    D = x.shape[1]
    p = _matmul(h, W["mix_w_in"], "nt", BF16, tag + "_in", hosts, rows=(0, 8 * D))
    pba = _matmul(h, W["mix_w_in"], "nt", F32, tag + "_inba", rows=(8 * D, 2 * LANES))
    qkvn = _qkv_conv_fwd(p, W["gdn_conv_w"], H, tag + "_qkvconv", hosts)
    o, Tinv, Sin = _gdn_fwd(qkvn, pba, W["gdn_a_log"], W["gdn_dt_bias"], H, tag + "_gdn", hosts)
    og = _gdn_gate_fwd(o, p, W["gdn_norm_w"], tag + "_gdngate")
    ya = _matmul(og, W["gdn_w_o"], "nn", F32, tag + "_gdno")
    hc = _glu_fwd(p, W["cnv_pw1_b"], tag + "_glu")
    hcv = _dw_conv_fwd(hc, W["cnv_dw_w"], W["cnv_dw_b"], tag + "_dwconv")
    hl = _ln_silu_fwd(hcv, W["cnv_ln_g"], W["cnv_ln_b"], tag + "_ln")
    yb = _matmul(hl, W["cnv_w_o"], "nn", F32, tag + "_cnvo")
    ym = _merge_fwd(p, ya, yb, W["cnv_b_o"], tag + "_merge")
    y = _matmul(ym, W["mix_w_out"], "nn", F32, tag + "_out")
    xn = _post_fwd(x, y, W["mix_norm_post"], 1.0, tag + "_post")
    return xn, (x, h, p, pba, qkvn, o, Tinv, Sin, og, ya, hc, hcv, hl, yb, ym, y)


def _mix_bwd(dxn, saved, W, H, tag, g, hosts=None):
    x, h, p, pba, qkvn, o, Tinv, Sin, og, ya, hc, hcv, hl, yb, ym, y = saved
    dy, g["mix_norm_post"] = _post_bwd(dxn, y, W["mix_norm_post"], 1.0, tag + "_dpost")
    dym = _matmul(dy, W["mix_w_out"], "nt", F32, tag + "_dym")
    g["mix_w_out"] = _matmul(ym, dy, "tn", BF16, tag + "_dwout")
    dya, dyb, dgates, g["cnv_b_o"] = _merge_bwd(dym, p, ya, yb, W["cnv_b_o"], tag + "_dmerge", hosts)
    dhl = _matmul(dyb, W["cnv_w_o"], "nt", F32, tag + "_dhl")
    g["cnv_w_o"] = _matmul(hl, dyb, "tn", BF16, tag + "_dwcnvo")
    dhcv, g["cnv_ln_g"], g["cnv_ln_b"], g["cnv_dw_b"] = _ln_silu_bwd(
        dhl, hcv, W["cnv_ln_g"], W["cnv_ln_b"], tag + "_dln", hosts)
    dhc, g["cnv_dw_w"] = _dw_conv_bwd(dhcv, hc, W["cnv_dw_w"], tag + "_ddwconv", hosts)
    dglu, g["cnv_pw1_b"] = _glu_bwd(dhc, p, W["cnv_pw1_b"], tag + "_dglu")
    dog = _matmul(dya, W["gdn_w_o"], "nt", F32, tag + "_dog")
    g["gdn_w_o"] = _matmul(og, dya, "tn", BF16, tag + "_dwgdno")
    do, dz, g["gdn_norm_w"] = _gdn_gate_bwd(dog, o, p, W["gdn_norm_w"], tag + "_dgdngate", hosts)
    dqkvn, dba, g["gdn_a_log"], g["gdn_dt_bias"] = _gdn_bwd(
        do, qkvn, pba, W["gdn_a_log"], W["gdn_dt_bias"], Tinv, Sin, H, tag + "_dgdn", hosts)
    dqkv, g["gdn_conv_w"] = _qkv_conv_bwd(dqkvn, p, W["gdn_conv_w"], H, tag + "_dqkvconv", hosts)
    dp = jnp.concatenate([dqkv, dz, dglu, dgates, dba], axis=1)
    dh = _matmul(dp, W["mix_w_in"], "nn", F32, tag + "_dh", hosts)
    g["mix_w_in"] = _matmul(dp, h, "tn", F32, tag + "_dwin")
    dx, g["mix_norm_pre"] = _pre_bwd(dh, x, W["mix_norm_pre"], dxn, tag + "_dpre", hosts)
    return dx


def _trunk_fwd_bwd(x, tgt, H, L, weights_of, grads, hosts=None):
    saved, Ws = [], []
    for i in range(L):
        W = weights_of(i)
        Ws.append(W)
        x, s1 = _ffn_fwd(x, W, "ffn1", f"l{i}_ffn1", hosts)
        x, s2 = _mix_fwd(x, W, H, f"l{i}_mix", hosts)
        x, s3 = _ffn_fwd(x, W, "ffn2", f"l{i}_ffn2", hosts)
        saved.append((s1, s2, s3))
    dx, loss = _loss_fwd_bwd(x, tgt, "loss")
    for i in reversed(range(L)):
        s1, s2, s3 = saved[i]
        dx = _ffn_bwd(dx, s3, Ws[i], "ffn2", f"l{i}_ffn2", grads[i], hosts)
        dx = _mix_bwd(dx, s2, Ws[i], H, f"l{i}_mix", grads[i], hosts)
        dx = _ffn_bwd(dx, s1, Ws[i], "ffn1", f"l{i}_ffn1", grads[i], hosts)
    return loss, dx


HBM_SPEC = pl.BlockSpec(memory_space=pltpu.HBM)


def _coords():
    return lax.axis_index("x"), lax.axis_index("y"), lax.axis_index("c")


class _GatherPlan:
    has_middle = True

    def __init__(self, shards):
        self.n = len(shards)
        self.out_shape = [jax.ShapeDtypeStruct((N_DEV,) + s.shape, s.dtype) for s in shards]
        self.sems = [pltpu.SemaphoreType.DMA((self.n, 7)), pltpu.SemaphoreType.DMA((self.n, 7)),
                     pltpu.SemaphoreType.DMA((self.n,))]

    def _parts(self, ins, outs, sems):
        send_sems, recv_sems, local_sems = sems
        x, y, c = _coords()
        me, sibling = (x, y, c), (x, y, 1 - c)
        chips = [(1 - x, y), (x, 1 - y), (1 - x, 1 - y)]

        def copy(w, k, block, to, src=None):
            dst = outs[w].at[4 * block[0] + 2 * block[1] + block[2]]
            return pltpu.make_async_remote_copy(
                src_ref=dst if src is None else src, dst_ref=dst, send_sem=send_sems.at[w, k],
                recv_sem=recv_sems.at[w, k], device_id=to, device_id_type=MESH)

        mine = [pltpu.make_async_copy(ins[w], outs[w].at[4 * x + 2 * y + c], local_sems.at[w]) for w in range(self.n)]
        first = []
        for w in range(self.n):
            first.append(copy(w, 0, me, sibling, src=ins[w]))
            first += [copy(w, 1 + j, me, (*chip, c), src=ins[w]) for j, chip in enumerate(chips)]
        passed = [copy(w, 4 + j, (*chip, c), sibling) for j, chip in enumerate(chips) for w in range(self.n)]
        return copy, mine, first, passed, chips, me, sibling, c

    def begin(self, ins, outs, sems):
        _, mine, first, _, _, _, _, _ = self._parts(ins, outs, sems)
        for cp in mine + first:
            cp.start()

    def middle(self, ins, outs, sems):
        copy, _, _, passed, chips, me, _, c = self._parts(ins, outs, sems)
        for j, chip in enumerate(chips):
            for w in range(self.n):
                copy(w, 1 + j, (*chip, c), me).wait_recv()
                passed[j * self.n + w].start()

    def finish(self, ins, outs, sems):
        copy, mine, first, passed, chips, me, sibling, c = self._parts(ins, outs, sems)
        for w in range(self.n):
            copy(w, 0, sibling, me).wait_recv()
            for j, chip in enumerate(chips):
                copy(w, 4 + j, (*chip, 1 - c), me).wait_recv()
        for cp in first + passed:
            cp.wait_send()
        for cp in mine:
            cp.wait()


class _SiblingPlan:
    has_middle = False

    def __init__(self, Gs):
        self.n = len(Gs)
        self.out_shape = [jax.ShapeDtypeStruct((4,) + g.shape[1:], g.dtype) for g in Gs]
        self.sems = [pltpu.SemaphoreType.DMA((self.n, 4)), pltpu.SemaphoreType.DMA((self.n, 4))]

    def _copies(self, ins, outs, sems):
        send_sems, recv_sems = sems
        x, y, c = _coords()
        return [pltpu.make_async_remote_copy(
            src_ref=ins[w].at[2 * q + (1 - c)], dst_ref=outs[w].at[q], send_sem=send_sems.at[w, q],
            recv_sem=recv_sems.at[w, q], device_id=(x, y, 1 - c), device_id_type=MESH)
            for w in range(self.n) for q in range(4)]

    def begin(self, ins, outs, sems):
        for cp in self._copies(ins, outs, sems):
            cp.start()

    def finish(self, ins, outs, sems):
        for cp in self._copies(ins, outs, sems):
            cp.wait()


class _ChipsPlan:
    has_middle = False

    def __init__(self, Ps):
        self.n = len(Ps)
        self.out_shape = [jax.ShapeDtypeStruct(p.shape, p.dtype) for p in Ps]
        self.sems = [pltpu.SemaphoreType.DMA((self.n, 3)), pltpu.SemaphoreType.DMA((self.n, 3)),
                     pltpu.SemaphoreType.DMA((self.n,))]

    def _copies(self, ins, outs, sems):
        send_sems, recv_sems, local_sems = sems
        x, y, c = _coords()
        me_q = 2 * x + y
        cps = []
        for w in range(self.n):
            cps.append(pltpu.make_async_copy(ins[w].at[me_q], outs[w].at[me_q], local_sems.at[w]))
            for j, (px, py) in enumerate([(1 - x, y), (x, 1 - y), (1 - x, 1 - y)]):
                cps.append(pltpu.make_async_remote_copy(
                    src_ref=ins[w].at[2 * px + py], dst_ref=outs[w].at[me_q], send_sem=send_sems.at[w, j],
                    recv_sem=recv_sems.at[w, j], device_id=(px, py, c), device_id_type=MESH))
        return cps

    def begin(self, ins, outs, sems):
        for cp in self._copies(ins, outs, sems):
            cp.start()

    def finish(self, ins, outs, sems):
        for cp in self._copies(ins, outs, sems):
            cp.wait()


def _comm_only(plan, arrays, name):
    n = plan.n

    def body(*refs):
        ins, outs, sems = refs[:n], refs[n:2 * n], refs[2 * n:]
        plan.begin(ins, outs, sems)
        if plan.has_middle:
            plan.middle(ins, outs, sems)
        plan.finish(ins, outs, sems)

    return pl.pallas_call(
        body, name=name, out_shape=plan.out_shape, in_specs=[HBM_SPEC] * n, out_specs=[HBM_SPEC] * n,
        scratch_shapes=plan.sems,
    )(*arrays)


class _MultiPlan:
    def __init__(self, plans):
        self.plans = plans
        self.n = sum(p.n for p in plans)
        self.out_shape = [s for p in plans for s in p.out_shape]
        self.sems = [s for p in plans for s in p.sems]
        self.has_middle = any(p.has_middle for p in plans)

    def _each(self, phase, ins, outs, sems):
        a = s = 0
        for p in self.plans:
            if phase != "middle" or p.has_middle:
                getattr(p, phase)(ins[a:a + p.n], outs[a:a + p.n], sems[s:s + len(p.sems)])
            a, s = a + p.n, s + len(p.sems)

    def begin(self, ins, outs, sems):
        self._each("begin", ins, outs, sems)

    def middle(self, ins, outs, sems):
        self._each("middle", ins, outs, sems)

    def finish(self, ins, outs, sems):
        self._each("finish", ins, outs, sems)


class _Hosts:
    def __init__(self):
        self.waiting = {}

    def add(self, host, make):
        self.waiting.setdefault(host, []).append(make)

    def take(self, host):
        makes = self.waiting.pop(host, None)
        if not makes:
            return None
        items = [m() for m in makes]
        return _MultiPlan([it[0] for it in items]), [a for it in items for a in it[1]], [it[2] for it in items]


def _hosted_call(body, hosts, *, name, grid, in_specs, out_specs, out_shape, scratch_shapes, args, sem):
    comm = hosts.take(name) if hosts is not None else None
    if comm is None:
        return pl.pallas_call(body, name=name, grid=grid, in_specs=in_specs, out_specs=out_specs, out_shape=out_shape,
                              scratch_shapes=scratch_shapes, compiler_params=_cparams(sem))(*args)
    plan, arrays, deliver = comm
    n_in, n_out, n_scr, n = len(in_specs), len(out_specs), len(scratch_shapes), plan.n
    total = 1
    for g in grid:
        total *= g

    def kern(*refs):
        ins, cins = refs[:n_in], refs[n_in:n_in + n]
        outs, couts = refs[n_in + n:n_in + n + n_out], refs[n_in + n + n_out:n_in + 2 * n + n_out]
        scr, csems = refs[n_in + 2 * n + n_out:n_in + 2 * n + n_out + n_scr], refs[n_in + 2 * n + n_out + n_scr:]
        step = pl.program_id(0)
        for d in range(1, len(grid)):
            step = step * grid[d] + pl.program_id(d)

        @pl.when(step == 0)
        def _():
            plan.begin(cins, couts, csems)

        body(*ins, *outs, *scr)
        if plan.has_middle:
            @pl.when(step == (3 * total) // 4)
            def _():
                plan.middle(cins, couts, csems)

        @pl.when(step == total - 1)
        def _():
            plan.finish(cins, couts, csems)

    res = pl.pallas_call(
        kern, name=name, grid=grid, in_specs=list(in_specs) + [HBM_SPEC] * n,
        out_specs=list(out_specs) + [HBM_SPEC] * n, out_shape=list(out_shape) + plan.out_shape,
        scratch_shapes=list(scratch_shapes) + plan.sems, compiler_params=_cparams(("arbitrary",) * len(grid)),
    )(*args, *arrays)
    k = n_out
    for p, d in zip(plan.plans, deliver):
        d(res[k:k + p.n])
        k += p.n
    return res[:n_out]


def _row_tile(R, target=256):
    best = None
    for t in range(8, min(R, target) + 1, 8):
        if R % t == 0:
            best = t
    return best if best is not None else R


def _pair_add(G, R1, cidx, name):
    _, R, C = G.shape
    tb = _row_tile(R)

    def body(c_ref, g_ref, r_ref, o_ref):
        o_ref[...] = (g_ref[...].astype(F32) + r_ref[...].astype(F32)).astype(BF16)

    return pl.pallas_call(
        body, name=name,
        grid_spec=pltpu.PrefetchScalarGridSpec(
            num_scalar_prefetch=1, grid=(4, R // tb),
            in_specs=[pl.BlockSpec((None, tb, C), lambda q, i, cr: (2 * q + cr[0], i, 0)),
                      pl.BlockSpec((None, tb, C), lambda q, i, cr: (q, i, 0))],
            out_specs=pl.BlockSpec((None, tb, C), lambda q, i, cr: (q, i, 0))),
        out_shape=jax.ShapeDtypeStruct((4, R, C), BF16),
        compiler_params=_cparams(("arbitrary", "arbitrary")),
    )(cidx, G, R1)


def _sum_parts(parts, name):
    P, R, C = parts.shape

    def body(p_ref, o_ref):
        acc = p_ref[0]
        for j in range(1, P):
            acc = acc + p_ref[j]
        o_ref[...] = acc

    return pl.pallas_call(
        body, name=name, out_shape=jax.ShapeDtypeStruct((R, C), F32),
        in_specs=[pl.BlockSpec(memory_space=pltpu.VMEM)], out_specs=pl.BlockSpec(memory_space=pltpu.VMEM),
        compiler_params=_cparams(),
    )(parts)


def _adamw(w, m, v, parts, name):
    G, R, C = w.shape
    P = parts[0].shape[0]
    tb = _row_tile(R)
    nb = R // tb
    c1 = 1.0 / (1.0 - ADAM_B1 ** ADAM_STEP)
    c2 = 1.0 / (1.0 - ADAM_B2 ** ADAM_STEP)

    def body(w_ref, m_ref, v_ref, *rest):
        p_refs, (g_ref, d_ref, nm_ref, nv_ref) = rest[:G], rest[G:]
        l = pl.program_id(0)
        g = None
        for k in range(G):
            gk = p_refs[k][0].astype(F32)
            for j in range(1, P):
                gk = gk + p_refs[k][j].astype(F32)
            g = gk if g is None else jnp.where(l == k, gk, g)
        nm = ADAM_B1 * m_ref[...] + (1.0 - ADAM_B1) * g
        nv = ADAM_B2 * v_ref[...] + (1.0 - ADAM_B2) * (g * g)
        g_ref[...] = g
        nm_ref[...] = nm
        nv_ref[...] = nv
        d_ref[...] = -ADAM_LR * ((nm * c1) / (jnp.sqrt(nv * c2) + ADAM_EPS) + ADAM_WD * w_ref[...])

    blk = pl.BlockSpec((None, tb, C), lambda l, i: (l, i, 0))

    def part_spec(k):
        return pl.BlockSpec((P, tb, C), lambda l, i: (0, jnp.where(l < k, 0, jnp.where(l > k, nb - 1, i)), 0))

    return pl.pallas_call(
        body, name=name, grid=(G, nb),
        in_specs=[blk, blk, blk] + [part_spec(k) for k in range(G)],
        out_specs=[blk] * 4, out_shape=[jax.ShapeDtypeStruct((G, R, C), F32)] * 4,
        compiler_params=_cparams(("arbitrary", "arbitrary")),
    )(w, m, v, *parts)


BIG = ("ffn1_w_in", "ffn1_w_out", "mix_w_in", "gdn_w_o", "cnv_w_o", "mix_w_out", "ffn2_w_in", "ffn2_w_out")
COL_SHARDED = ("ffn1_w_in", "mix_w_in", "ffn2_w_in")
SMALL_SHARDED = ("gdn_conv_w", "cnv_dw_w")
NAMES = ("ffn1_norm_pre", "ffn1_norm_post", "ffn1_w_in", "ffn1_w_out", "mix_norm_pre", "mix_norm_post", "mix_w_in",
         "gdn_conv_w", "gdn_a_log", "gdn_dt_bias", "gdn_norm_w", "gdn_w_o", "cnv_pw1_b", "cnv_dw_w", "cnv_dw_b",
         "cnv_ln_g", "cnv_ln_b", "cnv_w_o", "cnv_b_o", "mix_w_out", "ffn2_norm_pre", "ffn2_norm_post", "ffn2_w_in",
         "ffn2_w_out")
SMALL = tuple(n for n in NAMES if n not in BIG)


class _LayerWeights:
    def __init__(self, got, params, i, D, H):
        self.got, self.params, self.i, self.D, self.H, self.made = got, params, i, D, H, {}

    def __getitem__(self, n):
        if n not in self.made:
            if n in BIG:
                g = self.got[(self.i, n)]
                g = g.reshape(-1, g.shape[-1])
                w = _mix_in_reorder(g, self.D, self.H) if n == "mix_w_in" else g
            elif n in SMALL_SHARDED:
                g = self.got[(self.i, n)]
                w = jnp.transpose(g, (1, 0, 2)).reshape(g.shape[1], -1)
            else:
                v = self.params[n][self.i]
                if n in ("gdn_a_log", "gdn_dt_bias"):
                    v = jnp.pad(v, (0, LANES - self.H))
                w = v.reshape(1, -1)
            self.made[n] = w
        return self.made[n]


MIX_SMALL = ("gdn_w_o", "cnv_w_o", "mix_w_out", "gdn_conv_w", "cnv_dw_w")
GATHER_HOSTS = (("l{j}_mix_in", ("ffn1_w_in", "ffn1_w_out")), ("l{j}_mix_qkvconv", MIX_SMALL),
                ("l{j}_mix_gdn", ("mix_w_in",)), ("l{i}_mix_gdn", ("ffn2_w_in", "ffn2_w_out")))
GATHER_HOSTS_FIRST = ((None, ("ffn1_w_in", "ffn1_w_out")), ("l0_ffn1_in", ("mix_w_in",)), ("l0_ffn1_out", MIX_SMALL),
                      ("l0_mix_gdn", ("ffn2_w_in", "ffn2_w_out")))
REDUCE_HOSTS = ((BIG, "l{j}_ffn2_ds", (("l{j}_ffn2_dh", ("ffn2_w_in",)),
                                       ("l{j}_ffn2_dwin", ("ffn2_w_out", "gdn_w_o", "cnv_w_o", "mix_w_out")),
                                       ("l{j}_mix_ddwconv", ("ffn1_w_in",)),
                                       ("l{j}_mix_dgdn", ("mix_w_in", "ffn1_w_out")))),)
REDUCE_HOSTS_FIRST = (
    (("ffn2_w_in", "ffn2_w_out"), "l0_mix_dmerge", (("l0_mix_dln", ("ffn2_w_out",)), ("l0_mix_dqkvconv", ("ffn2_w_in",)))),
    (("mix_w_out", "cnv_w_o", "gdn_w_o"), "l0_mix_dgdngate", (("l0_mix_dh", ("mix_w_out", "cnv_w_o", "gdn_w_o")),)),
    (("mix_w_in",), "l0_mix_dpre", (("l0_ffn1_dh", ("mix_w_in",)),)),
    (("ffn1_w_out",), "l0_ffn1_dact", (("l0_ffn1_dwin", ("ffn1_w_out",)),)),
    (("ffn1_w_in",), None, ((None, ("ffn1_w_in",)),)))


def kernel(x, ffn1_norm_pre, ffn1_norm_post, ffn1_w_in, ffn1_w_out, mix_norm_pre, mix_norm_post, mix_w_in, gdn_conv_w, gdn_a_log, gdn_dt_bias, gdn_norm_w, gdn_w_o, cnv_pw1_b, cnv_dw_w, cnv_dw_b, cnv_ln_g, cnv_ln_b, cnv_w_o, cnv_b_o, mix_w_out, ffn2_norm_pre, ffn2_norm_post, ffn2_w_in, ffn2_w_out, loss_target, m_ffn1_norm_pre, m_ffn1_norm_post, m_ffn1_w_in, m_ffn1_w_out, m_mix_norm_pre, m_mix_norm_post, m_mix_w_in, m_gdn_conv_w, m_gdn_a_log, m_gdn_dt_bias, m_gdn_norm_w, m_gdn_w_o, m_cnv_pw1_b, m_cnv_dw_w, m_cnv_dw_b, m_cnv_ln_g, m_cnv_ln_b, m_cnv_w_o, m_cnv_b_o, m_mix_w_out, m_ffn2_norm_pre, m_ffn2_norm_post, m_ffn2_w_in, m_ffn2_w_out, v_ffn1_norm_pre, v_ffn1_norm_post, v_ffn1_w_in, v_ffn1_w_out, v_mix_norm_pre, v_mix_norm_post, v_mix_w_in, v_gdn_conv_w, v_gdn_a_log, v_gdn_dt_bias, v_gdn_norm_w, v_gdn_w_o, v_cnv_pw1_b, v_cnv_dw_w, v_cnv_dw_b, v_cnv_ln_g, v_cnv_ln_b, v_cnv_w_o, v_cnv_b_o, v_mix_w_out, v_ffn2_norm_pre, v_ffn2_norm_post, v_ffn2_w_in, v_ffn2_w_out):
    params = dict(zip(NAMES, (ffn1_norm_pre, ffn1_norm_post, ffn1_w_in, ffn1_w_out, mix_norm_pre, mix_norm_post, mix_w_in, gdn_conv_w, gdn_a_log, gdn_dt_bias, gdn_norm_w, gdn_w_o, cnv_pw1_b, cnv_dw_w, cnv_dw_b, cnv_ln_g, cnv_ln_b, cnv_w_o, cnv_b_o, mix_w_out, ffn2_norm_pre, ffn2_norm_post, ffn2_w_in, ffn2_w_out)))
    mom1 = dict(zip(NAMES, (m_ffn1_norm_pre, m_ffn1_norm_post, m_ffn1_w_in, m_ffn1_w_out, m_mix_norm_pre, m_mix_norm_post, m_mix_w_in, m_gdn_conv_w, m_gdn_a_log, m_gdn_dt_bias, m_gdn_norm_w, m_gdn_w_o, m_cnv_pw1_b, m_cnv_dw_w, m_cnv_dw_b, m_cnv_ln_g, m_cnv_ln_b, m_cnv_w_o, m_cnv_b_o, m_mix_w_out, m_ffn2_norm_pre, m_ffn2_norm_post, m_ffn2_w_in, m_ffn2_w_out)))
    mom2 = dict(zip(NAMES, (v_ffn1_norm_pre, v_ffn1_norm_post, v_ffn1_w_in, v_ffn1_w_out, v_mix_norm_pre, v_mix_norm_post, v_mix_w_in, v_gdn_conv_w, v_gdn_a_log, v_gdn_dt_bias, v_gdn_norm_w, v_gdn_w_o, v_cnv_pw1_b, v_cnv_dw_w, v_cnv_dw_b, v_cnv_ln_g, v_cnv_ln_b, v_cnv_w_o, v_cnv_b_o, v_mix_w_out, v_ffn2_norm_pre, v_ffn2_norm_post, v_ffn2_w_in, v_ffn2_w_out)))
    T, D = x.shape[1], x.shape[2]
    H = D // GDN_DK
    L = ffn1_norm_pre.shape[0]
    xi, yi, ci = _coords()
    dev = 4 * xi + 2 * yi + ci

    ag_names = BIG + SMALL_SHARDED

    def shard_to_send(n):
        if n in COL_SHARDED:
            return jnp.swapaxes(params[n], 1, 2).astype(BF16)
        return params[n].astype(BF16) if n in BIG else params[n]

    send = {n: shard_to_send(n) for n in ag_names}

    hosts, got, later = _Hosts(), {}, []
    for i in range(L):
        for host, names in (GATHER_HOSTS_FIRST if i == 0 else GATHER_HOSTS):
            def make(i=i, names=names):
                blocks = [send[n][i] for n in names]

                def deliver(outs):
                    got.update({(i, n): o for n, o in zip(names, outs)})
                return _GatherPlan(blocks), blocks, deliver
            if host is None:
                plan, blocks, deliver = make()
                deliver(_comm_only(plan, blocks, f"ag_weights_l{i}"))
            else:
                hosts.add(host.format(i=i, j=i - 1), make)

    cidx = jnp.reshape(ci, (1,)).astype(jnp.int32)
    grads, reduced = [{} for _ in range(L)], {}
    for i in range(L):
        for names, sib_host, chip_hosts in (REDUCE_HOSTS_FIRST if i == 0 else REDUCE_HOSTS):
            stage = {}

            def make_sib(i=i, names=names, stage=stage):
                Gs = []
                for n in names:
                    g = _mix_in_restore(grads[i][n], D, H) if n == "mix_w_in" else grads[i][n]
                    Gs.append(g.reshape(N_DEV, -1, g.shape[-1]))
                stage["G"] = dict(zip(names, Gs))

                def deliver(outs):
                    stage["R1"] = dict(zip(names, outs))
                return _SiblingPlan(Gs), Gs, deliver

            def make_chips(ns, i=i, stage=stage):
                Ps = [_pair_add(stage["G"][n], stage["R1"][n], cidx, f"l{i}_pair_add_{n}") for n in ns]

                def deliver(outs):
                    reduced.update({(i, n): o for n, o in zip(ns, outs)})
                return _ChipsPlan(Ps), Ps, deliver

            if sib_host is None:
                later.append((f"l{i}_{names[0]}", make_sib, [(lambda ns=ns, mc=make_chips: mc(ns)) for _, ns in chip_hosts]))
                continue
            hosts.add(sib_host.format(i=i, j=i - 1), make_sib)
            for host, ns in chip_hosts:
                hosts.add(host.format(i=i, j=i - 1), lambda ns=ns, mc=make_chips: mc(ns))

    weights = [_LayerWeights(got, params, i, D, H) for i in range(L)]
    loss_row, dx = _trunk_fwd_bwd(x[0], loss_target[0], H, L, lambda i: weights[i], grads, hosts)
    loss = lax.psum(loss_row[0, 0], ("x", "y", "c"))
    assert not hosts.waiting, sorted(hosts.waiting)

    for tag, make_sib, chip_makes in later:
        plan, Gs, deliver = make_sib()
        deliver(_comm_only(plan, Gs, f"rs_sibling_{tag}"))
        for mk in chip_makes:
            plan, Ps, deliver = mk()
            deliver(_comm_only(plan, Ps, f"rs_chips_{tag}"))
    R2s = {n: [jnp.swapaxes(reduced[(i, n)], 1, 2) if n in COL_SHARDED else reduced[(i, n)] for i in range(L)]
           for n in BIG}

    pieces = []
    for i in range(L):
        for n in SMALL:
            piece = grads[i][n].reshape(-1, LANES)
            pieces.append(jnp.pad(piece, ((0, (-piece.shape[0]) % 8), (0, 0))))
    packed = jnp.concatenate(pieces, axis=0)
    small_all = _comm_only(_GatherPlan([packed]), [packed], "ag_small_grads")[0]
    small_sum = _sum_parts(small_all, "sum_small_grads")
    small_g = {n: [None] * L for n in SMALL}
    off = 0
    for i in range(L):
        for n in SMALL:
            shape = grads[i][n].shape
            cnt = shape[0] * shape[1] // LANES
            g = small_sum[off:off + cnt].reshape(shape)
            off += cnt + (-cnt) % 8
            if n in ("gdn_a_log", "gdn_dt_bias"):
                g = g[:, :H]
            if n in SMALL_SHARDED:
                wloc = params[n].shape[-1]
                g = lax.dynamic_slice_in_dim(g, dev * wloc, wloc, axis=1)
            small_g[n][i] = g

    outs = {}
    for n in NAMES:
        w, m, v = params[n], mom1[n], mom2[n]
        if n in BIG:
            shape3, parts = w.shape, R2s[n]
        else:
            rows, cols = (w.shape[0] * w.shape[1], w.shape[2]) if w.ndim == 3 else w.shape
            shape3, parts = (1, rows, cols), [jnp.stack(small_g[n], axis=0).reshape(1, rows, cols)]
        res = _adamw(w.reshape(shape3), m.reshape(shape3), v.reshape(shape3), parts, "adamw_" + n)
        outs[n] = [r.reshape(w.shape) for r in res]

    result = [loss, dx[None]]
    for k in range(4):
        result += [outs[n][k] for n in NAMES]
    return tuple(result)
```

```python
import jax
import jax.numpy as jnp
from jax import lax
from jax.experimental import pallas as pl
from jax.experimental.pallas import tpu as pltpu

F32 = jnp.float32
BF16 = jnp.bfloat16

GDN_DK = 128
CHUNK = 64
GDN_CONV = 4
CNV_K = 31
RMS_EPS = 1e-6
LN_EPS = 1e-5
L2_EPS = 1e-6
ADAM_LR = 0.001
ADAM_B1 = 0.9
ADAM_B2 = 0.999
ADAM_EPS = 1e-08
ADAM_WD = 0.01
ADAM_STEP = 10

LANES = 128
SUB = 16
VMEM_LIMIT = 56 * 1024 * 1024
N_DEV = 8
MESH = pl.DeviceIdType.MESH


def _cparams(sem=None, **kw):
    if sem is not None:
        kw["dimension_semantics"] = sem
    return pltpu.CompilerParams(vmem_limit_bytes=VMEM_LIMIT, **kw)


def _tile(dim, target):
    best = None
    for t in range(LANES, min(dim, target) + 1, LANES):
        if dim % t == 0:
            best = t
    return best if best is not None else dim


def _sigmoid(x):
    return 1.0 / (1.0 + jnp.exp(-x))


def _silu(x):
    return x * _sigmoid(x)


def _dsilu(x):
    s = _sigmoid(x)
    return s * (1.0 + x * (1.0 - s))


MM_VMEM_BUDGET = 40 * 1024 * 1024
MM_MAX_TILE = 2048


def _mm_tiles(M, N, K, out_bytes):
    def cands(dim):
        c = [t for t in range(LANES, min(dim, MM_MAX_TILE) + 1, LANES) if dim % t == 0]
        return c or [dim]
    best = None
    for tm in cands(M):
        for tn in cands(N):
            vm = 2 * (2 * K * (tm + tn) + tm * tn * out_bytes)
            if vm <= MM_VMEM_BUDGET and (best is None or tm * tn > best[0] * best[1]):
                best = (tm, tn)
    return best if best is not None else (cands(M)[0], cands(N)[0])


def _matmul(a, b, mode, out_dtype, name, hosts=None, rows=None):
    if mode == "nn":
        (M, K), N = a.shape, b.shape[1]
    elif mode == "nt":
        (M, K), N = a.shape, b.shape[0]
    else:
        (K, M), N = a.shape, b.shape[1]
    first = 0
    if rows is not None:
        first, N = rows
    tm, tn = _mm_tiles(M, N, K, jnp.dtype(out_dtype).itemsize)
    assert first % tn == 0
    joff = first // tn
    if mode == "nn":
        a_spec = pl.BlockSpec((tm, K), lambda j, i: (i, 0))
        b_spec = pl.BlockSpec((K, tn), lambda j, i: (0, j))
        dn = (((1,), (0,)), ((), ()))
    elif mode == "nt":
        a_spec = pl.BlockSpec((tm, K), lambda j, i: (i, 0))
        b_spec = pl.BlockSpec((tn, K), lambda j, i: (j + joff, 0))
        dn = (((1,), (1,)), ((), ()))
    else:
        a_spec = pl.BlockSpec((K, tm), lambda j, i: (0, i))
        b_spec = pl.BlockSpec((K, tn), lambda j, i: (0, j))
        dn = (((0,), (0,)), ((), ()))

    def body(a_ref, b_ref, o_ref):
        o_ref[...] = lax.dot_general(a_ref[...], b_ref[...], dn, preferred_element_type=F32).astype(out_dtype)

    return _hosted_call(
        body, hosts, name=name, grid=(N // tn, M // tm), in_specs=[a_spec, b_spec],
        out_specs=[pl.BlockSpec((tm, tn), lambda j, i: (i, j))],
        out_shape=[jax.ShapeDtypeStruct((M, N), out_dtype)], scratch_shapes=[],
        args=(a, b), sem=("parallel", "parallel"))[0]


def _rowcall(name, body, T, tb, row_ins, par_ins, row_outs, acc_outs, hosts=None, wide=None):
    n_ri, n_pi, n_ro = len(row_ins), len(par_ins), len(row_outs)
    n_extra = 0 if (wide is None or wide[3] is None) else 1

    def kern(*refs):
        ri, pi = refs[:n_ri], refs[n_ri:n_ri + n_pi]
        refs = refs[n_ri + n_pi + n_extra:]
        ro, ao = refs[:n_ro], refs[n_ro:]
        if ao:
            @pl.when(pl.program_id(0) == 0)
            def _():
                for r in ao:
                    r[...] = jnp.zeros_like(r)
        body(ri, pi, ro, ao)

    in_specs = [pl.BlockSpec((tb, w), lambda i, cb=cb: (i, cb)) for (_, w, cb) in row_ins]
    in_specs += [pl.BlockSpec(p.shape, lambda i: (0, 0)) for p in par_ins]
    out_specs = [pl.BlockSpec((tb, w), lambda i: (i, 0)) for (w, _) in row_outs]
    out_specs += [pl.BlockSpec((1, w), lambda i: (0, 0)) for w in acc_outs]
    out_shape = [jax.ShapeDtypeStruct((T, w), dt) for (w, dt) in row_outs]
    out_shape += [jax.ShapeDtypeStruct((1, w), F32) for w in acc_outs]
    args, aliases = [*[a for (a, _, _) in row_ins], *par_ins], {}
    if wide is not None:
        k, total, cb, buf = wide
        w, dt = row_outs[k]
        out_specs[k] = pl.BlockSpec((tb, w), lambda i: (i, cb))
        out_shape[k] = jax.ShapeDtypeStruct((T, total), dt)
        if buf is not None:
            in_specs.append(pl.BlockSpec(memory_space=pl.ANY))
            args.append(buf)
            aliases = {len(args) - 1: k}
    return _hosted_call(
        kern, hosts, name=name, grid=(T // tb,), in_specs=in_specs, out_specs=out_specs, out_shape=out_shape,
        scratch_shapes=[], args=tuple(args), sem=("arbitrary",), aliases=aliases)


def _rsum(x):
    return jnp.sum(x, axis=0, keepdims=True)


def _rms_rstd(x):
    return lax.rsqrt(jnp.mean(x * x, axis=-1, keepdims=True) + RMS_EPS)


def _rms_fwd(x, w, name):
    T, D = x.shape

    def body(ri, pi, ro, ao):
        xv = ri[0][...]
        ro[0][...] = (xv * _rms_rstd(xv) * pi[0][...]).astype(BF16)

    return _rowcall(name, body, T, 256, [(x, D, 0)], [w], [(D, BF16)], [])[0]


def _rms_bwd_core(dy, x, w):
    rs = _rms_rstd(x)
    xh = x * rs
    gw = dy * w
    dx = rs * (gw - xh * jnp.mean(gw * xh, axis=-1, keepdims=True))
    return dx, dy * xh


def _pre_bwd(dh, x, w, dres, name, hosts=None):
    T, D = x.shape

    def body(ri, pi, ro, ao):
        dx, dwc = _rms_bwd_core(ri[0][...], ri[1][...], pi[0][...])
        ro[0][...] = ri[2][...] + dx
        ao[0][...] += _rsum(dwc)

    return _rowcall(name, body, T, 256, [(dh, D, 0), (x, D, 0), (dres, D, 0)], [w], [(D, F32)], [D], hosts=hosts)


def _post_fwd(x, f, w, r, name):
    T, D = x.shape

    def body(ri, pi, ro, ao):
        fv = ri[1][...]
        ro[0][...] = ri[0][...] + r * (fv * _rms_rstd(fv) * pi[0][...])

    return _rowcall(name, body, T, 256, [(x, D, 0), (f, D, 0)], [w], [(D, F32)], [])[0]


def _post_bwd(dxn, f, w, r, name):
    T, D = f.shape

    def body(ri, pi, ro, ao):
        df, dwc = _rms_bwd_core(r * ri[0][...], ri[1][...], pi[0][...])
        ro[0][...] = df.astype(BF16)
        ao[0][...] += _rsum(dwc)

    return _rowcall(name, body, T, 256, [(dxn, D, 0), (f, D, 0)], [w], [(D, BF16)], [D])


def _swiglu_fwd(a, name):
    T, F2 = a.shape
    F = F2 // 2

    def body(ri, pi, ro, ao):
        ro[0][...] = (_silu(ri[0][...].astype(F32)) * ri[1][...].astype(F32)).astype(BF16)

    return _rowcall(name, body, T, 256, [(a, F, 0), (a, F, 1)], [], [(F, BF16)], [])[0]


def _swiglu_bwd(ds, a, name, hosts=None):
    T, F2 = a.shape
    F = F2 // 2

    def body(ri, pi, ro, ao):
        dsv, g, u = ri[0][...].astype(F32), ri[1][...].astype(F32), ri[2][...].astype(F32)
        ro[0][:, :F] = (dsv * u * _dsilu(g)).astype(BF16)
        ro[0][:, F:] = (dsv * _silu(g)).astype(BF16)

    return _rowcall(name, body, T, 256, [(ds, F, 0), (a, F, 0), (a, F, 1)], [], [(F2, BF16)], [], hosts=hosts)[0]


def _loss_fwd_bwd(y, tgt, name):
    T, D = y.shape

    def body(ri, pi, ro, ao):
        e = ri[0][...] - ri[1][...]
        ro[0][...] = e * (1.0 / D)
        tot = jnp.sum(_rsum(e * e), axis=1, keepdims=True) * (0.5 / D)
        ao[0][...] += jnp.broadcast_to(tot, (1, LANES))

    return _rowcall(name, body, T, 256, [(y, D, 0), (tgt, D, 0)], [], [(D, F32)], [LANES])


def _gdn_gate_fwd(o, p, nw, name):
    T, D = o.shape
    H = D // GDN_DK

    def body(ri, pi, ro, ao):
        for h in range(H):
            sl = slice(h * GDN_DK, (h + 1) * GDN_DK)
            oh = ri[0][:, sl]
            ro[0][:, sl] = (oh * _rms_rstd(oh) * pi[0][...] * _silu(ri[1][:, sl].astype(F32))).astype(BF16)

    return _rowcall(name, body, T, 256, [(o, D, 0), (p, D, 3)], [nw], [(D, BF16)], [])[0]


def _gdn_gate_bwd(dog, o, p, nw, name, hosts=None, dp=None):
    T, D = o.shape
    H = D // GDN_DK

    def body(ri, pi, ro, ao):
        acc = jnp.zeros((1, GDN_DK), F32)
        for h in range(H):
            sl = slice(h * GDN_DK, (h + 1) * GDN_DK)
            dy, oh, z = ri[0][:, sl], ri[1][:, sl], ri[2][:, sl].astype(F32)
            sz = _silu(z)
            do, dwc = _rms_bwd_core(dy * sz, oh, pi[0][...])
            ro[0][:, sl] = do
            ro[1][:, sl] = (dy * oh * _rms_rstd(oh) * pi[0][...] * _dsilu(z)).astype(BF16)
            acc = acc + _rsum(dwc)
        ao[0][...] += acc

    return _rowcall(name, body, T, 256, [(dog, D, 0), (o, D, 0), (p, D, 3)], [nw], [(D, F32), (D, BF16)], [GDN_DK],
                    hosts=hosts, wide=None if dp is None else (1, dp.shape[1], 3, dp))


def _glu_fwd(p, b, name):
    T = p.shape[0]
    D = b.shape[1] // 2

    def body(ri, pi, ro, ao):
        ro[0][...] = (ri[0][...].astype(F32) + pi[0][:, :D]) * _sigmoid(ri[1][...].astype(F32) + pi[0][:, D:])

    return _rowcall(name, body, T, 256, [(p, D, 4), (p, D, 5)], [b], [(D, F32)], [])[0]


def _glu_bwd(dhc, p, b, name, dp=None):
    T = p.shape[0]
    D = b.shape[1] // 2

    def body(ri, pi, ro, ao):
        d, a, g = ri[0][...], ri[1][...].astype(F32) + pi[0][:, :D], ri[2][...].astype(F32) + pi[0][:, D:]
        sg = _sigmoid(g)
        da, dg = d * sg, d * a * sg * (1.0 - sg)
        ro[0][:, :D] = da.astype(BF16)
        ro[0][:, D:] = dg.astype(BF16)
        ao[0][:, :D] += _rsum(da)
        ao[0][:, D:] += _rsum(dg)

    return _rowcall(name, body, T, 256, [(dhc, D, 0), (p, D, 4), (p, D, 5)], [b], [(2 * D, BF16)], [2 * D],
                    wide=None if dp is None else (0, dp.shape[1], 2, dp))


def _ln_stats(x):
    mu = jnp.mean(x, axis=-1, keepdims=True)
    xc = x - mu
    rstd = lax.rsqrt(jnp.mean(xc * xc, axis=-1, keepdims=True) + LN_EPS)
    return xc * rstd, rstd


def _ln_silu_fwd(hcv, g, b, name):
    T, D = hcv.shape

    def body(ri, pi, ro, ao):
        xh, _ = _ln_stats(ri[0][...])
        ro[0][...] = _silu(xh * pi[0][...] + pi[1][...]).astype(BF16)

    return _rowcall(name, body, T, 256, [(hcv, D, 0)], [g, b], [(D, BF16)], [])[0]


def _ln_silu_bwd(dhl, hcv, g, b, name, hosts=None):
    T, D = hcv.shape

    def body(ri, pi, ro, ao):
        xh, rstd = _ln_stats(ri[1][...])
        dyl = ri[0][...] * _dsilu(xh * pi[0][...] + pi[1][...])
        dxh = dyl * pi[0][...]
        dx = rstd * (dxh - jnp.mean(dxh, axis=-1, keepdims=True) - xh * jnp.mean(dxh * xh, axis=-1, keepdims=True))
        ro[0][...] = dx
        ao[0][...] += _rsum(dyl * xh)
        ao[1][...] += _rsum(dyl)
        ao[2][...] += _rsum(dx)

    return _rowcall(name, body, T, 256, [(dhl, D, 0), (hcv, D, 0)], [g, b], [(D, F32)], [D, D, D], hosts=hosts)


def _merge_fwd(p, ya, yb, bo, name):
    T, D = ya.shape

    def body(ri, pi, ro, ao):
        ga, gb = _sigmoid(ri[0][...].astype(F32)), _sigmoid(ri[1][...].astype(F32))
        ro[0][...] = (ga * ri[2][...] + gb * (ri[3][...] + pi[0][...])).astype(BF16)

    return _rowcall(name, body, T, 256, [(p, D, 6), (p, D, 7), (ya, D, 0), (yb, D, 0)], [bo], [(D, BF16)], [])[0]


def _merge_bwd(dym, p, ya, yb, bo, name, hosts=None, dp_width=None):
    T, D = ya.shape

    def body(ri, pi, ro, ao):
        d = ri[0][...]
        ga, gb = _sigmoid(ri[1][...].astype(F32)), _sigmoid(ri[2][...].astype(F32))
        ybv = ri[4][...] + pi[0][...]
        dyb = d * gb
        ro[0][...] = (d * ga).astype(BF16)
        ro[1][...] = dyb.astype(BF16)
        ro[2][:, :D] = (d * ri[3][...] * ga * (1.0 - ga)).astype(BF16)
        ro[2][:, D:] = (d * ybv * gb * (1.0 - gb)).astype(BF16)
        ao[0][...] += _rsum(dyb)

    return _rowcall(name, body, T, 256, [(dym, D, 0), (p, D, 6), (p, D, 7), (ya, D, 0), (yb, D, 0)], [bo],
                    [(D, BF16), (D, BF16), (2 * D, BF16)], [D], hosts=hosts,
                    wide=None if dp_width is None else (2, dp_width, 3, None))


PAD = 32
RC = 256


def _tap_windows(ref, offs):
    groups = {}
    for j, o in enumerate(offs):
        groups.setdefault(o % 8, []).append((j, o))
    for grp in groups.values():
        lo, hi = min(o for _, o in grp), max(o for _, o in grp)
        win = ref[pl.ds(lo, RC + hi - lo), :]
        for j, o in grp:
            yield j, win[o - lo:o - lo + RC]


def _causal_taps(xp_ref, w, K, c0):
    acc = None
    for j, xs in _tap_windows(xp_ref, [PAD - (K - 1) + j + c0 for j in range(K)]):
        term = w[j:j + 1, :] * xs
        acc = term if acc is None else acc + term
    return acc


def _anticausal_taps(dp_ref, w, K, c0):
    acc = None
    for j, ds in _tap_windows(dp_ref, [(K - 1) - j + c0 for j in range(K)]):
        term = w[j:j + 1, :] * ds
        acc = term if acc is None else acc + term
    return acc


def _tap_grads(dw_ref, dc_ref, xp_ref, K, T):
    accs = [jnp.zeros((8, LANES), F32) for _ in range(K)]
    for c in range(T // RC):
        d = dc_ref[pl.ds(c * RC, RC), :]
        for j, xs in _tap_windows(xp_ref, [PAD - (K - 1) + j + c * RC for j in range(K)]):
            accs[j] = accs[j] + jnp.sum((d * xs).reshape(RC // 8, 8, LANES), axis=0)
    for j in range(K):
        dw_ref[j:j + 1, :] = _rsum(accs[j])


def _qkv_conv_fwd(p, cw, H, name, hosts=None):
    T = p.shape[0]
    K = cw.shape[0]

    def body(x_ref, w_ref, o_ref, xp_ref):
        j = pl.program_id(0)
        xp_ref[pl.ds(0, PAD), :] = jnp.zeros((PAD, LANES), F32)
        xp_ref[pl.ds(PAD, T), :] = x_ref[...].astype(F32)
        w = w_ref[...]
        scale = jnp.where(j < H, GDN_DK ** -0.5, 1.0).astype(F32)
        for c in range(T // RC):
            act = _silu(_causal_taps(xp_ref, w, K, c * RC))
            nrm = act * lax.rsqrt(jnp.sum(act * act, axis=-1, keepdims=True) + L2_EPS) * scale
            o_ref[pl.ds(c * RC, RC), :] = jnp.where(j < 2 * H, nrm, act)

    return _hosted_call(
        body, hosts, name=name, grid=(3 * H,),
        in_specs=[pl.BlockSpec((T, LANES), lambda j: (0, j)), pl.BlockSpec((K, LANES), lambda j: (0, j))],
        out_specs=[pl.BlockSpec((T, LANES), lambda j: (0, j))],
        out_shape=[jax.ShapeDtypeStruct((T, 3 * H * GDN_DK), F32)],
        scratch_shapes=[pltpu.VMEM((T + PAD, LANES), F32)], args=(p, cw), sem=("arbitrary",))[0]


def _qkv_conv_bwd(dn, p, cw, H, name, hosts=None, dp=None):
    T = p.shape[0]
    K = cw.shape[0]

    def body(dn_ref, x_ref, w_ref, _, dx_ref, dw_ref, xp_ref, dc_ref):
        j = pl.program_id(0)
        xp_ref[pl.ds(0, PAD), :] = jnp.zeros((PAD, LANES), F32)
        xp_ref[pl.ds(PAD, T), :] = x_ref[...].astype(F32)
        dc_ref[pl.ds(T, PAD), :] = jnp.zeros((PAD, LANES), F32)
        w = w_ref[...]
        scale = jnp.where(j < H, GDN_DK ** -0.5, 1.0).astype(F32)
        for c in range(T // RC):
            pre = _causal_taps(xp_ref, w, K, c * RC)
            act = _silu(pre)
            d = dn_ref[pl.ds(c * RC, RC), :]
            rs = lax.rsqrt(jnp.sum(act * act, axis=-1, keepdims=True) + L2_EPS)
            nh = act * rs
            dact_n = scale * rs * (d - nh * jnp.sum(d * nh, axis=-1, keepdims=True))
            dact = jnp.where(j < 2 * H, dact_n, d)
            dc_ref[pl.ds(c * RC, RC), :] = dact * _dsilu(pre)
        for c in range(T // RC):
            dx_ref[pl.ds(c * RC, RC), :] = _anticausal_taps(dc_ref, w, K, c * RC).astype(BF16)
        _tap_grads(dw_ref, dc_ref, xp_ref, K, T)

    return _hosted_call(
        body, hosts, name=name, grid=(3 * H,),
        in_specs=[pl.BlockSpec((T, LANES), lambda j: (0, j)), pl.BlockSpec((T, LANES), lambda j: (0, j)),
                  pl.BlockSpec((K, LANES), lambda j: (0, j)), pl.BlockSpec(memory_space=pl.ANY)],
        out_specs=[pl.BlockSpec((T, LANES), lambda j: (0, j)), pl.BlockSpec((K, LANES), lambda j: (0, j))],
        out_shape=[jax.ShapeDtypeStruct(dp.shape, BF16), jax.ShapeDtypeStruct(cw.shape, F32)],
        scratch_shapes=[pltpu.VMEM((T + PAD, LANES), F32), pltpu.VMEM((T + PAD, LANES), F32)],
        args=(dn, p, cw, dp), sem=("arbitrary",), aliases={3: 0})


def _dw_conv_fwd(hc, w, b, name):
    T, D = hc.shape
    K = w.shape[0]

    def body(x_ref, w_ref, b_ref, o_ref, xp_ref):
        xp_ref[pl.ds(0, PAD), :] = jnp.zeros((PAD, LANES), F32)
        xp_ref[pl.ds(PAD, T), :] = x_ref[...]
        wv = w_ref[...]
        for c in range(T // RC):
            o_ref[pl.ds(c * RC, RC), :] = _causal_taps(xp_ref, wv, K, c * RC) + b_ref[...]

    return pl.pallas_call(
        body, name=name, grid=(D // LANES,),
        in_specs=[pl.BlockSpec((T, LANES), lambda j: (0, j)), pl.BlockSpec((K, LANES), lambda j: (0, j)),
                  pl.BlockSpec((1, LANES), lambda j: (0, j))],
        out_specs=pl.BlockSpec((T, LANES), lambda j: (0, j)),
        out_shape=jax.ShapeDtypeStruct((T, D), F32),
        scratch_shapes=[pltpu.VMEM((T + PAD, LANES), F32)],
        compiler_params=_cparams(("arbitrary",)),
    )(hc, w, b)


def _dw_conv_bwd(dy, hc, w, name, hosts=None):
    T, D = hc.shape
    K = w.shape[0]

    def body(dy_ref, x_ref, w_ref, dx_ref, dw_ref, xp_ref, dc_ref):
        xp_ref[pl.ds(0, PAD), :] = jnp.zeros((PAD, LANES), F32)
        xp_ref[pl.ds(PAD, T), :] = x_ref[...]
        dc_ref[pl.ds(T, PAD), :] = jnp.zeros((PAD, LANES), F32)
        dc_ref[pl.ds(0, T), :] = dy_ref[...]
        wv = w_ref[...]
        for c in range(T // RC):
            dx_ref[pl.ds(c * RC, RC), :] = _anticausal_taps(dc_ref, wv, K, c * RC)
        _tap_grads(dw_ref, dc_ref, xp_ref, K, T)

    return _hosted_call(
        body, hosts, name=name, grid=(D // LANES,), sem=("arbitrary",),
        in_specs=[pl.BlockSpec((T, LANES), lambda j: (0, j)), pl.BlockSpec((T, LANES), lambda j: (0, j)),
                  pl.BlockSpec((K, LANES), lambda j: (0, j))],
        out_specs=[pl.BlockSpec((T, LANES), lambda j: (0, j)), pl.BlockSpec((K, LANES), lambda j: (0, j))],
        out_shape=[jax.ShapeDtypeStruct((T, D), F32), jax.ShapeDtypeStruct(w.shape, F32)],
        scratch_shapes=[pltpu.VMEM((T + PAD, LANES), F32), pltpu.VMEM((T + PAD, LANES), F32)],
        args=(dy, hc, w))


NN = (((1,), (0,)), ((), ()))
NT = (((1,), (1,)), ((), ()))
TN = (((0,), (0,)), ((), ()))


def _dotb(a, b, dn=NN):
    return lax.dot_general(a.astype(BF16), b.astype(BF16), dn, preferred_element_type=F32)


def _split_bf16(x, n):
    parts, r = [], x
    for _ in range(n):
        p = r.astype(BF16)
        parts.append(p)
        r = r - p.astype(F32)
    return parts


def _dot_sel(sel, x, pieces, sel_left=True):
    sb = sel.astype(BF16)
    acc = None
    for p in _split_bf16(x, pieces):
        t = (lax.dot_general(sb, p, NN, preferred_element_type=F32) if sel_left
             else lax.dot_general(p, sb, NN, preferred_element_type=F32))
        acc = t if acc is None else acc + t
    return acc


def _iota2(shape, axis):
    return lax.broadcasted_iota(jnp.int32, shape, axis)


def _to_row(col, eye):
    return jnp.sum(jnp.where(eye, col, 0.0), axis=0, keepdims=True)


def _to_col(row, eye):
    return jnp.sum(jnp.where(eye, row, 0.0), axis=1, keepdims=True)


def _gdn_gates(bl, al, alog, dtb):
    beta = _sigmoid(bl)
    x = al + dtb
    sp = jnp.maximum(x, 0.0) + jnp.log(1.0 + jnp.exp(-jnp.abs(x)))
    g = -jnp.exp(alog) * sp
    r, c = _iota2((CHUNK, CHUNK), 0), _iota2((CHUNK, CHUNK), 1)
    G = _dot_sel(r >= c, g, 3)
    return beta, g, G, x


def _unit_lower_inverses(As, Ats):
    n = len(As)
    nb = CHUNK // SUB
    lane = _iota2((SUB, CHUNK), 1)
    row = _iota2((SUB, CHUNK), 0)
    Atp = []
    for At in Ats:
        acc = jnp.zeros((SUB, CHUNK), F32)
        for b in range(nb):
            acc = jnp.where(lane // SUB == b, At[b * SUB:(b + 1) * SUB, :], acc)
        Atp.append(acc)
    gr, gc = _iota2((CHUNK, CHUNK), 0), _iota2((CHUNK, CHUNK), 1)
    ones_bd = gr // SUB == gc // SUB
    stack = jnp.concatenate(
        [jnp.where(lane % SUB == i, Atp[m], 0.0) for i in range(1, SUB) for m in range(n)], axis=0)
    Cm = _dot_sel(ones_bd, stack, 2, sel_left=False)
    Z = [(row == lane % SUB).astype(F32) for _ in range(n)]
    for i in range(1, SUB):
        for m in range(n):
            cm = Cm[((i - 1) * n + m) * SUB:((i - 1) * n + m + 1) * SUB, :]
            new = -jnp.sum(cm * Z[m], axis=0, keepdims=True)
            Z[m] = Z[m] + jnp.where(row == i, new, 0.0)
    bd = gr // SUB == gc // SUB
    Xs = [jnp.where(bd, jnp.concatenate([Z[m]] * nb, axis=0), 0.0) for m in range(n)]
    blk = SUB
    while blk < CHUNK:
        off = (gr // (2 * blk) == gc // (2 * blk)) & (gr // blk != gc // blk)
        Ys = [_dotb(Xs[m], jnp.where(off, As[m], 0.0)) for m in range(n)]
        Xs = [Xs[m] - _dotb(Ys[m], Xs[m]) for m in range(n)]
        blk *= 2
    return Xs


def _gdn_fwd(qkvn, p, alog, dtb, H, name, hosts=None):
    T = qkvn.shape[0]
    D = H * GDN_DK
    N = T // CHUNK
    bblk = 0

    def body(q_ref, k_ref, v_ref, b_ref, a_ref, alog_ref, dtb_ref, o_ref, t_ref, s_ref, S_scr):
        @pl.when(pl.program_id(0) == 0)
        def _():
            S_scr[...] = jnp.zeros_like(S_scr)

        beta, _, G, _ = _gdn_gates(b_ref[...], a_ref[...], alog_ref[...], dtb_ref[...])
        hs = range(H)
        sl = [slice(h * GDN_DK, (h + 1) * GDN_DK) for h in hs]
        q, k, v = [q_ref[:, s] for s in sl], [k_ref[:, s] for s in sl], [v_ref[:, s] for s in sl]
        Gc, bc = [G[:, h:h + 1] for h in hs], [beta[:, h:h + 1] for h in hs]
        r, c = _iota2((CHUNK, CHUNK), 0), _iota2((CHUNK, CHUNK), 1)
        eye, low, up = r == c, r >= c, r <= c
        Gr, br = [_to_row(Gc[h], eye) for h in hs], [_to_row(bc[h], eye) for h in hs]
        Dm = [jnp.where(low, jnp.exp(jnp.where(low, Gc[h] - Gr[h], 0.0)), 0.0) for h in hs]
        Dt = [jnp.where(up, jnp.exp(jnp.where(up, Gr[h] - Gc[h], 0.0)), 0.0) for h in hs]
        qk = [_dotb(jnp.concatenate([q[h], k[h]], axis=0), k[h], NT) for h in hs]
        QK = [qk[h][:CHUNK] * Dm[h] for h in hs]
        KK = [qk[h][CHUNK:] for h in hs]
        As = [jnp.where(r > c, KK[h] * Dm[h], 0.0) * bc[h] for h in hs]
        Ats = [jnp.where(r < c, KK[h] * Dt[h], 0.0) * br[h] for h in hs]
        Ts = _unit_lower_inverses(As, Ats)
        eG = [jnp.exp(Gc[h]) for h in hs]
        Gl = [Gc[h][CHUNK - 1:CHUNK, :] for h in hs]
        uw = [_dotb(Ts[h], jnp.concatenate([v[h] * bc[h], k[h] * (bc[h] * eG[h])], axis=1)) for h in hs]
        S = [S_scr[h] for h in hs]
        qw = [_dotb(jnp.concatenate([q[h] * eG[h], uw[h][:, GDN_DK:]], axis=0), S[h]) for h in hs]
        vn = [uw[h][:, :GDN_DK] - qw[h][CHUNK:] for h in hs]
        o = [qw[h][:CHUNK] + _dotb(QK[h], vn[h]) for h in hs]
        Sn = [S[h] * jnp.exp(Gl[h]) + _dotb(k[h] * jnp.exp(Gl[h] - Gc[h]), vn[h], TN) for h in hs]
        for h in hs:
            t_ref[0, h] = Ts[h]
            s_ref[0, h] = S[h]
            o_ref[:, sl[h]] = o[h]
            S_scr[h] = Sn[h]

    qkv_spec = [pl.BlockSpec((CHUNK, D), lambda n, cb=cb: (n, cb)) for cb in range(3)]
    return _hosted_call(
        body, hosts, name=name, grid=(N,), sem=("arbitrary",),
        in_specs=qkv_spec + [pl.BlockSpec((CHUNK, LANES), lambda n: (n, bblk)),
                             pl.BlockSpec((CHUNK, LANES), lambda n: (n, bblk + 1)),
                             pl.BlockSpec((1, LANES), lambda n: (0, 0)), pl.BlockSpec((1, LANES), lambda n: (0, 0))],
        out_specs=[pl.BlockSpec((CHUNK, D), lambda n: (n, 0)),
                   pl.BlockSpec((1, H, CHUNK, CHUNK), lambda n: (n, 0, 0, 0)),
                   pl.BlockSpec((1, H, GDN_DK, GDN_DK), lambda n: (n, 0, 0, 0))],
        out_shape=[jax.ShapeDtypeStruct((T, D), F32), jax.ShapeDtypeStruct((N, H, CHUNK, CHUNK), F32),
                   jax.ShapeDtypeStruct((N, H, GDN_DK, GDN_DK), F32)],
        scratch_shapes=[pltpu.VMEM((H, GDN_DK, GDN_DK), F32)],
        args=(qkvn, qkvn, qkvn, p, p, alog, dtb))


def _gdn_bwd(do, qkvn, p, alog, dtb, Tinv, Sin, H, name, hosts=None, dp=None):
    T = qkvn.shape[0]
    D = H * GDN_DK
    N = T // CHUNK
    bblk = 0

    def body(do_ref, q_ref, k_ref, v_ref, b_ref, a_ref, alog_ref, dtb_ref, t_ref, s_ref, _,
             dqkv_ref, dba_ref, dalog_ref, ddtb_ref, dS_scr):
        @pl.when(pl.program_id(0) == 0)
        def _():
            dS_scr[...] = jnp.zeros_like(dS_scr)
            dalog_ref[...] = jnp.zeros_like(dalog_ref)
            ddtb_ref[...] = jnp.zeros_like(ddtb_ref)

        beta, g, G, x = _gdn_gates(b_ref[...], a_ref[...], alog_ref[...], dtb_ref[...])
        r, c = _iota2((CHUNK, CHUNK), 0), _iota2((CHUNK, CHUNK), 1)
        eye, low, strict = r == c, r >= c, r > c
        lane = _iota2((CHUNK, LANES), 1)
        rsum1 = lambda a: jnp.sum(a, axis=1, keepdims=True)
        hs = range(H)
        sl = [slice(h * GDN_DK, (h + 1) * GDN_DK) for h in hs]
        q, k, v = [q_ref[:, s] for s in sl], [k_ref[:, s] for s in sl], [v_ref[:, s] for s in sl]
        dov = [do_ref[:, s] for s in sl]
        Gc, bc = [G[:, h:h + 1] for h in hs], [beta[:, h:h + 1] for h in hs]
        Tm, S, dSo = [t_ref[0, h] for h in hs], [s_ref[0, h] for h in hs], [dS_scr[h] for h in hs]
        Gr = [_to_row(Gc[h], eye) for h in hs]
        Dm = [jnp.where(low, jnp.exp(jnp.where(low, Gc[h] - Gr[h], 0.0)), 0.0) for h in hs]
        eG = [jnp.exp(Gc[h]) for h in hs]
        Gl = [Gc[h][CHUNK - 1:CHUNK, :] for h in hs]
        eR, dch = [jnp.exp(Gl[h] - Gc[h]) for h in hs], [jnp.exp(Gl[h]) for h in hs]
        qk = [_dotb(jnp.concatenate([q[h], k[h]], axis=0), k[h], NT) for h in hs]
        QKr, KK = [qk[h][:CHUNK] for h in hs], [qk[h][CHUNK:] for h in hs]
        QK = [QKr[h] * Dm[h] for h in hs]
        M = [jnp.where(strict, KK[h] * Dm[h], 0.0) for h in hs]
        uw = [_dotb(Tm[h], jnp.concatenate([v[h] * bc[h], k[h] * (bc[h] * eG[h])], axis=1)) for h in hs]
        u, w = [uw[h][:, :GDN_DK] for h in hs], [uw[h][:, GDN_DK:] for h in hs]
        qd, kd = [q[h] * eG[h] for h in hs], [k[h] * eR[h] for h in hs]
        vn = [u[h] - _dotb(w[h], S[h]) for h in hs]
        dvn = [_dotb(QK[h], dov[h], TN) + _dotb(kd[h], dSo[h]) for h in hs]
        dQK = [jnp.where(low, _dotb(dov[h], vn[h], NT), 0.0) for h in hs]
        dkd = [_dotb(vn[h], dSo[h], NT) for h in hs]
        ddch = [jnp.sum(rsum1(dSo[h] * S[h]), axis=0, keepdims=True) for h in hs]
        dd = [jnp.concatenate([dov[h], dvn[h]], axis=0) for h in hs]
        xs = [_dotb(dd[h], S[h], NT) for h in hs]
        dqd, dw = [xs[h][:CHUNK] for h in hs], [-xs[h][CHUNK:] for h in hs]
        dS = [_dotb(jnp.concatenate([qd[h], -w[h]], axis=0), dd[h], TN) + dch[h] * dSo[h] for h in hs]
        yb = [_dotb(Tm[h], jnp.concatenate([dvn[h], dw[h]], axis=1), TN) for h in hs]
        dvb, dkb = [yb[h][:, :GDN_DK] for h in hs], [yb[h][:, GDN_DK:] for h in hs]
        dA = [-jnp.where(strict, _dotb(yb[h], uw[h], NT), 0.0) for h in hs]
        rk = [rsum1(dkb[h] * k[h]) for h in hs]
        dbeta = [rsum1(dvb[h] * v[h]) + rk[h] * eG[h] + rsum1(dA[h] * M[h]) for h in hs]
        dM = [dA[h] * bc[h] for h in hs]
        dKK = [dM[h] * Dm[h] for h in hs]
        dQKr = [dQK[h] * Dm[h] for h in hs]
        E = [dM[h] * M[h] + dQK[h] * QK[h] for h in hs]
        zk = [_dotb(jnp.concatenate([dQKr[h], dKK[h]], axis=0), k[h]) for h in hs]
        dq = [zk[h][:CHUNK] + dqd[h] * eG[h] for h in hs]
        dk = [dkb[h] * (bc[h] * eG[h]) + zk[h][CHUNK:] + _dotb(dKK[h], k[h], TN) + _dotb(dQKr[h], q[h], TN)
              + dkd[h] * eR[h] for h in hs]
        deG = [rk[h] * bc[h] + rsum1(dqd[h] * q[h]) for h in hs]
        deR = [rsum1(dkd[h] * k[h]) for h in hs]
        dGl = [jnp.sum(deR[h] * eR[h], axis=0, keepdims=True) + ddch[h] * dch[h] for h in hs]
        dGc = [rsum1(E[h]) - _to_col(jnp.sum(E[h], axis=0, keepdims=True), eye) + deG[h] * eG[h] - deR[h] * eR[h]
               + jnp.where(r[:, :1] == CHUNK - 1, dGl[h], 0.0) for h in hs]
        dG_all = jnp.zeros((CHUNK, LANES), F32)
        dbeta_all = jnp.zeros((CHUNK, LANES), F32)
        for h in hs:
            dS_scr[h] = dS[h]
            dqkv_ref[:, sl[h]] = dq[h]
            dqkv_ref[:, D + h * GDN_DK:D + (h + 1) * GDN_DK] = dk[h]
            dqkv_ref[:, 2 * D + h * GDN_DK:2 * D + (h + 1) * GDN_DK] = dvb[h] * bc[h]
            dG_all = jnp.where(lane == h, dGc[h], dG_all)
            dbeta_all = jnp.where(lane == h, dbeta[h], dbeta_all)
        dg = _dot_sel(r <= c, dG_all, 3)
        da = dg * (-jnp.exp(alog_ref[...])) * _sigmoid(x)
        dba_ref[:, :LANES] = (dbeta_all * beta * (1.0 - beta)).astype(BF16)
        dba_ref[:, LANES:] = da.astype(BF16)
        dalog_ref[...] += _rsum(dg * g)
        ddtb_ref[...] += _rsum(da)

    rev = lambda n: N - 1 - n
    qkv_spec = [pl.BlockSpec((CHUNK, D), lambda n, cb=cb: (rev(n), cb)) for cb in range(3)]
    return _hosted_call(
        body, hosts, name=name, grid=(N,), sem=("arbitrary",),
        in_specs=[pl.BlockSpec((CHUNK, D), lambda n: (rev(n), 0))] + qkv_spec + [
            pl.BlockSpec((CHUNK, LANES), lambda n: (rev(n), bblk)),
            pl.BlockSpec((CHUNK, LANES), lambda n: (rev(n), bblk + 1)),
            pl.BlockSpec((1, LANES), lambda n: (0, 0)), pl.BlockSpec((1, LANES), lambda n: (0, 0)),
            pl.BlockSpec((1, H, CHUNK, CHUNK), lambda n: (rev(n), 0, 0, 0)),
            pl.BlockSpec((1, H, GDN_DK, GDN_DK), lambda n: (rev(n), 0, 0, 0)),
            pl.BlockSpec(memory_space=pl.ANY)],
        out_specs=[pl.BlockSpec((CHUNK, 3 * D), lambda n: (rev(n), 0)),
                   pl.BlockSpec((CHUNK, 2 * LANES), lambda n: (rev(n), 8 * D // (2 * LANES))),
                   pl.BlockSpec((1, LANES), lambda n: (0, 0)), pl.BlockSpec((1, LANES), lambda n: (0, 0))],
        out_shape=[jax.ShapeDtypeStruct((T, 3 * D), F32), jax.ShapeDtypeStruct(dp.shape, BF16),
                   jax.ShapeDtypeStruct((1, LANES), F32), jax.ShapeDtypeStruct((1, LANES), F32)],
        scratch_shapes=[pltpu.VMEM((H, GDN_DK, GDN_DK), F32)],
        args=(do, qkvn, qkvn, qkvn, p, p, alog, dtb, Tinv, Sin, dp), aliases={10: 1})


def _mix_in_reorder(wt, D, H):
    o1 = 4 * D
    o2, o3 = o1 + H, o1 + 2 * H
    z = jnp.zeros((LANES - H, wt.shape[1]), wt.dtype)
    return jnp.concatenate([wt[:o1], wt[o3:], wt[o1:o2], z, wt[o2:o3], z], axis=0)


def _mix_in_restore(dwt, D, H):
    b0 = 8 * D
    return jnp.concatenate([dwt[:4 * D], dwt[b0:b0 + H], dwt[b0 + LANES:b0 + LANES + H], dwt[4 * D:b0]], axis=0)


def _ffn_fwd(x, W, pre, tag, hosts=None):
    h = _rms_fwd(x, W[pre + "_norm_pre"], tag + "_pre")
    a = _matmul(h, W[pre + "_w_in"], "nt", BF16, tag + "_in", hosts)
    s = _swiglu_fwd(a, tag + "_act")
    f = _matmul(s, W[pre + "_w_out"], "nn", F32, tag + "_out", hosts)
    return _post_fwd(x, f, W[pre + "_norm_post"], 0.5, tag + "_post"), (x, h, a, s, f)


def _ffn_bwd(dxn, saved, W, pre, tag, g, hosts=None):
    x, h, a, s, f = saved
    df, g[pre + "_norm_post"] = _post_bwd(dxn, f, W[pre + "_norm_post"], 0.5, tag + "_dpost")
    ds = _matmul(df, W[pre + "_w_out"], "nt", BF16, tag + "_ds", hosts)
    g[pre + "_w_out"] = _matmul(s, df, "tn", BF16, tag + "_dwout")
    da = _swiglu_bwd(ds, a, tag + "_dact", hosts)
    dh = _matmul(da, W[pre + "_w_in"], "nn", F32, tag + "_dh", hosts)
    g[pre + "_w_in"] = _matmul(da, h, "tn", BF16, tag + "_dwin", hosts)
    dx, g[pre + "_norm_pre"] = _pre_bwd(dh, x, W[pre + "_norm_pre"], dxn, tag + "_dpre")
    return dx


def _mix_fwd(x, W, H, tag, hosts=None):
    h = _rms_fwd(x, W["mix_norm_pre"], tag + "_pre")
    D = x.shape[1]
    p = _matmul(h, W["mix_w_in"], "nt", BF16, tag + "_in", hosts, rows=(0, 8 * D))
    pba = _matmul(h, W["mix_w_in"], "nt", F32, tag + "_inba", rows=(8 * D, 2 * LANES))
    qkvn = _qkv_conv_fwd(p, W["gdn_conv_w"], H, tag + "_qkvconv", hosts)
    o, Tinv, Sin = _gdn_fwd(qkvn, pba, W["gdn_a_log"], W["gdn_dt_bias"], H, tag + "_gdn", hosts)
    og = _gdn_gate_fwd(o, p, W["gdn_norm_w"], tag + "_gdngate")
    ya = _matmul(og, W["gdn_w_o"], "nn", F32, tag + "_gdno")
    hc = _glu_fwd(p, W["cnv_pw1_b"], tag + "_glu")
    hcv = _dw_conv_fwd(hc, W["cnv_dw_w"], W["cnv_dw_b"], tag + "_dwconv")
    hl = _ln_silu_fwd(hcv, W["cnv_ln_g"], W["cnv_ln_b"], tag + "_ln")
    yb = _matmul(hl, W["cnv_w_o"], "nn", F32, tag + "_cnvo")
    ym = _merge_fwd(p, ya, yb, W["cnv_b_o"], tag + "_merge")
    y = _matmul(ym, W["mix_w_out"], "nn", F32, tag + "_out")
    xn = _post_fwd(x, y, W["mix_norm_post"], 1.0, tag + "_post")
    return xn, (x, h, p, pba, qkvn, o, Tinv, Sin, og, ya, hc, hcv, hl, yb, ym, y)


def _mix_bwd(dxn, saved, W, H, tag, g, hosts=None):
    x, h, p, pba, qkvn, o, Tinv, Sin, og, ya, hc, hcv, hl, yb, ym, y = saved
    dy, g["mix_norm_post"] = _post_bwd(dxn, y, W["mix_norm_post"], 1.0, tag + "_dpost")
    dym = _matmul(dy, W["mix_w_out"], "nt", F32, tag + "_dym")
    g["mix_w_out"] = _matmul(ym, dy, "tn", BF16, tag + "_dwout")
    dya, dyb, dp, g["cnv_b_o"] = _merge_bwd(dym, p, ya, yb, W["cnv_b_o"], tag + "_dmerge", hosts,
                                            dp_width=p.shape[1] + 2 * LANES)
    dhl = _matmul(dyb, W["cnv_w_o"], "nt", F32, tag + "_dhl")
    g["cnv_w_o"] = _matmul(hl, dyb, "tn", BF16, tag + "_dwcnvo")
    dhcv, g["cnv_ln_g"], g["cnv_ln_b"], g["cnv_dw_b"] = _ln_silu_bwd(
        dhl, hcv, W["cnv_ln_g"], W["cnv_ln_b"], tag + "_dln", hosts)
    dhc, g["cnv_dw_w"] = _dw_conv_bwd(dhcv, hc, W["cnv_dw_w"], tag + "_ddwconv", hosts)
    dp, g["cnv_pw1_b"] = _glu_bwd(dhc, p, W["cnv_pw1_b"], tag + "_dglu", dp)
    dog = _matmul(dya, W["gdn_w_o"], "nt", F32, tag + "_dog")
    g["gdn_w_o"] = _matmul(og, dya, "tn", BF16, tag + "_dwgdno")
    do, dp, g["gdn_norm_w"] = _gdn_gate_bwd(dog, o, p, W["gdn_norm_w"], tag + "_dgdngate", hosts, dp)
    dqkvn, dp, g["gdn_a_log"], g["gdn_dt_bias"] = _gdn_bwd(
        do, qkvn, pba, W["gdn_a_log"], W["gdn_dt_bias"], Tinv, Sin, H, tag + "_dgdn", hosts, dp)
    dp, g["gdn_conv_w"] = _qkv_conv_bwd(dqkvn, p, W["gdn_conv_w"], H, tag + "_dqkvconv", hosts, dp)
    dh = _matmul(dp, W["mix_w_in"], "nn", F32, tag + "_dh", hosts)
    g["mix_w_in"] = _matmul(dp, h, "tn", F32, tag + "_dwin")
    dx, g["mix_norm_pre"] = _pre_bwd(dh, x, W["mix_norm_pre"], dxn, tag + "_dpre", hosts)
    return dx


def _trunk_fwd_bwd(x, tgt, H, L, weights_of, grads, hosts=None):
    saved, Ws = [], []
    for i in range(L):
        W = weights_of(i)
        Ws.append(W)
        x, s1 = _ffn_fwd(x, W, "ffn1", f"l{i}_ffn1", hosts)
        x, s2 = _mix_fwd(x, W, H, f"l{i}_mix", hosts)
        x, s3 = _ffn_fwd(x, W, "ffn2", f"l{i}_ffn2", hosts)
        saved.append((s1, s2, s3))
    dx, loss = _loss_fwd_bwd(x, tgt, "loss")
    for i in reversed(range(L)):
        s1, s2, s3 = saved[i]
        dx = _ffn_bwd(dx, s3, Ws[i], "ffn2", f"l{i}_ffn2", grads[i], hosts)
        dx = _mix_bwd(dx, s2, Ws[i], H, f"l{i}_mix", grads[i], hosts)
        dx = _ffn_bwd(dx, s1, Ws[i], "ffn1", f"l{i}_ffn1", grads[i], hosts)
    return loss, dx


HBM_SPEC = pl.BlockSpec(memory_space=pltpu.HBM)


def _coords():
    return lax.axis_index("x"), lax.axis_index("y"), lax.axis_index("c")


class _GatherPlan:
    has_middle = True

    def __init__(self, shards):
        self.n = len(shards)
        self.out_shape = [jax.ShapeDtypeStruct((N_DEV,) + s.shape, s.dtype) for s in shards]
        self.sems = [pltpu.SemaphoreType.DMA((self.n, 7)), pltpu.SemaphoreType.DMA((self.n, 7)),
                     pltpu.SemaphoreType.DMA((self.n,))]

    def _parts(self, ins, outs, sems):
        send_sems, recv_sems, local_sems = sems
        x, y, c = _coords()
        me, sibling = (x, y, c), (x, y, 1 - c)
        chips = [(1 - x, y), (x, 1 - y), (1 - x, 1 - y)]

        def copy(w, k, block, to, src=None):
            dst = outs[w].at[4 * block[0] + 2 * block[1] + block[2]]
            return pltpu.make_async_remote_copy(
                src_ref=dst if src is None else src, dst_ref=dst, send_sem=send_sems.at[w, k],
                recv_sem=recv_sems.at[w, k], device_id=to, device_id_type=MESH)

        mine = [pltpu.make_async_copy(ins[w], outs[w].at[4 * x + 2 * y + c], local_sems.at[w]) for w in range(self.n)]
        first = []
        for w in range(self.n):
            first.append(copy(w, 0, me, sibling, src=ins[w]))
            first += [copy(w, 1 + j, me, (*chip, c), src=ins[w]) for j, chip in enumerate(chips)]
        passed = [copy(w, 4 + j, (*chip, c), sibling) for j, chip in enumerate(chips) for w in range(self.n)]
        return copy, mine, first, passed, chips, me, sibling, c

    def begin(self, ins, outs, sems):
        _, mine, first, _, _, _, _, _ = self._parts(ins, outs, sems)
        for cp in mine + first:
            cp.start()

    def middle(self, ins, outs, sems):
        copy, _, _, passed, chips, me, _, c = self._parts(ins, outs, sems)
        for j, chip in enumerate(chips):
            for w in range(self.n):
                copy(w, 1 + j, (*chip, c), me).wait_recv()
                passed[j * self.n + w].start()

    def finish(self, ins, outs, sems):
        copy, mine, first, passed, chips, me, sibling, c = self._parts(ins, outs, sems)
        for w in range(self.n):
            copy(w, 0, sibling, me).wait_recv()
            for j, chip in enumerate(chips):
                copy(w, 4 + j, (*chip, 1 - c), me).wait_recv()
        for cp in first + passed:
            cp.wait_send()
        for cp in mine:
            cp.wait()


class _SiblingPlan:
    has_middle = False

    def __init__(self, Gs):
        self.n = len(Gs)
        self.out_shape = [jax.ShapeDtypeStruct((4,) + g.shape[1:], g.dtype) for g in Gs]
        self.sems = [pltpu.SemaphoreType.DMA((self.n, 4)), pltpu.SemaphoreType.DMA((self.n, 4))]

    def _copies(self, ins, outs, sems):
        send_sems, recv_sems = sems
        x, y, c = _coords()
        return [pltpu.make_async_remote_copy(
            src_ref=ins[w].at[2 * q + (1 - c)], dst_ref=outs[w].at[q], send_sem=send_sems.at[w, q],
            recv_sem=recv_sems.at[w, q], device_id=(x, y, 1 - c), device_id_type=MESH)
            for w in range(self.n) for q in range(4)]

    def begin(self, ins, outs, sems):
        for cp in self._copies(ins, outs, sems):
            cp.start()

    def finish(self, ins, outs, sems):
        for cp in self._copies(ins, outs, sems):
            cp.wait()


class _ChipsPlan:
    has_middle = False

    def __init__(self, Ps):
        self.n = len(Ps)
        self.out_shape = [jax.ShapeDtypeStruct(p.shape, p.dtype) for p in Ps]
        self.sems = [pltpu.SemaphoreType.DMA((self.n, 3)), pltpu.SemaphoreType.DMA((self.n, 3)),
                     pltpu.SemaphoreType.DMA((self.n,))]

    def _copies(self, ins, outs, sems):
        send_sems, recv_sems, local_sems = sems
        x, y, c = _coords()
        me_q = 2 * x + y
        cps = []
        for w in range(self.n):
            cps.append(pltpu.make_async_copy(ins[w].at[me_q], outs[w].at[me_q], local_sems.at[w]))
            for j, (px, py) in enumerate([(1 - x, y), (x, 1 - y), (1 - x, 1 - y)]):
                cps.append(pltpu.make_async_remote_copy(
                    src_ref=ins[w].at[2 * px + py], dst_ref=outs[w].at[me_q], send_sem=send_sems.at[w, j],
                    recv_sem=recv_sems.at[w, j], device_id=(px, py, c), device_id_type=MESH))
        return cps

    def begin(self, ins, outs, sems):
        for cp in self._copies(ins, outs, sems):
            cp.start()

    def finish(self, ins, outs, sems):
        for cp in self._copies(ins, outs, sems):
            cp.wait()


def _comm_only(plan, arrays, name):
    n = plan.n

    def body(*refs):
        ins, outs, sems = refs[:n], refs[n:2 * n], refs[2 * n:]
        plan.begin(ins, outs, sems)
        if plan.has_middle:
            plan.middle(ins, outs, sems)
        plan.finish(ins, outs, sems)

    return pl.pallas_call(
        body, name=name, out_shape=plan.out_shape, in_specs=[HBM_SPEC] * n, out_specs=[HBM_SPEC] * n,
        scratch_shapes=plan.sems,
    )(*arrays)


class _MultiPlan:
    def __init__(self, plans):
        self.plans = plans
        self.n = sum(p.n for p in plans)
        self.out_shape = [s for p in plans for s in p.out_shape]
        self.sems = [s for p in plans for s in p.sems]
        self.has_middle = any(p.has_middle for p in plans)

    def _each(self, phase, ins, outs, sems):
        a = s = 0
        for p in self.plans:
            if phase != "middle" or p.has_middle:
                getattr(p, phase)(ins[a:a + p.n], outs[a:a + p.n], sems[s:s + len(p.sems)])
            a, s = a + p.n, s + len(p.sems)

    def begin(self, ins, outs, sems):
        self._each("begin", ins, outs, sems)

    def middle(self, ins, outs, sems):
        self._each("middle", ins, outs, sems)

    def finish(self, ins, outs, sems):
        self._each("finish", ins, outs, sems)


class _Hosts:
    def __init__(self):
        self.waiting = {}

    def add(self, host, make):
        self.waiting.setdefault(host, []).append(make)

    def take(self, host):
        makes = self.waiting.pop(host, None)
        if not makes:
            return None
        items = [m() for m in makes]
        return _MultiPlan([it[0] for it in items]), [a for it in items for a in it[1]], [it[2] for it in items]


def _hosted_call(body, hosts, *, name, grid, in_specs, out_specs, out_shape, scratch_shapes, args, sem, aliases=None):
    comm = hosts.take(name) if hosts is not None else None
    if comm is None:
        return pl.pallas_call(body, name=name, grid=grid, in_specs=in_specs, out_specs=out_specs, out_shape=out_shape,
                              scratch_shapes=scratch_shapes, input_output_aliases=aliases or {},
                              compiler_params=_cparams(sem))(*args)
    plan, arrays, deliver = comm
    n_in, n_out, n_scr, n = len(in_specs), len(out_specs), len(scratch_shapes), plan.n
    total = 1
    for g in grid:
        total *= g

    def kern(*refs):
        ins, cins = refs[:n_in], refs[n_in:n_in + n]
        outs, couts = refs[n_in + n:n_in + n + n_out], refs[n_in + n + n_out:n_in + 2 * n + n_out]
        scr, csems = refs[n_in + 2 * n + n_out:n_in + 2 * n + n_out + n_scr], refs[n_in + 2 * n + n_out + n_scr:]
        step = pl.program_id(0)
        for d in range(1, len(grid)):
            step = step * grid[d] + pl.program_id(d)

        @pl.when(step == 0)
        def _():
            plan.begin(cins, couts, csems)

        body(*ins, *outs, *scr)
        if plan.has_middle:
            @pl.when(step == (3 * total) // 4)
            def _():
                plan.middle(cins, couts, csems)

        @pl.when(step == total - 1)
        def _():
            plan.finish(cins, couts, csems)

    res = pl.pallas_call(
        kern, name=name, grid=grid, in_specs=list(in_specs) + [HBM_SPEC] * n,
        out_specs=list(out_specs) + [HBM_SPEC] * n, out_shape=list(out_shape) + plan.out_shape,
        scratch_shapes=list(scratch_shapes) + plan.sems, input_output_aliases=aliases or {},
        compiler_params=_cparams(("arbitrary",) * len(grid)),
    )(*args, *arrays)
    k = n_out
    for p, d in zip(plan.plans, deliver):
        d(res[k:k + p.n])
        k += p.n
    return res[:n_out]


def _row_tile(R, target=256):
    best = None
    for t in range(8, min(R, target) + 1, 8):
        if R % t == 0:
            best = t
    return best if best is not None else R


def _pair_add(G, R1, cidx, name):
    _, R, C = G.shape
    tb = _row_tile(R)

    def body(c_ref, g_ref, r_ref, o_ref):
        o_ref[...] = (g_ref[...].astype(F32) + r_ref[...].astype(F32)).astype(BF16)

    return pl.pallas_call(
        body, name=name,
        grid_spec=pltpu.PrefetchScalarGridSpec(
            num_scalar_prefetch=1, grid=(4, R // tb),
            in_specs=[pl.BlockSpec((None, tb, C), lambda q, i, cr: (2 * q + cr[0], i, 0)),
                      pl.BlockSpec((None, tb, C), lambda q, i, cr: (q, i, 0))],
            out_specs=pl.BlockSpec((None, tb, C), lambda q, i, cr: (q, i, 0))),
        out_shape=jax.ShapeDtypeStruct((4, R, C), BF16),
        compiler_params=_cparams(("arbitrary", "arbitrary")),
    )(cidx, G, R1)


def _sum_parts(parts, name):
    P, R, C = parts.shape

    def body(p_ref, o_ref):
        acc = p_ref[0]
        for j in range(1, P):
            acc = acc + p_ref[j]
        o_ref[...] = acc

    return pl.pallas_call(
        body, name=name, out_shape=jax.ShapeDtypeStruct((R, C), F32),
        in_specs=[pl.BlockSpec(memory_space=pltpu.VMEM)], out_specs=pl.BlockSpec(memory_space=pltpu.VMEM),
        compiler_params=_cparams(),
    )(parts)


def _adamw(w, m, v, parts, name):
    G, R, C = w.shape
    P = parts[0].shape[0]
    tb = _row_tile(R)
    nb = R // tb
    c1 = 1.0 / (1.0 - ADAM_B1 ** ADAM_STEP)
    c2 = 1.0 / (1.0 - ADAM_B2 ** ADAM_STEP)

    def body(w_ref, m_ref, v_ref, *rest):
        p_refs, (g_ref, d_ref, nm_ref, nv_ref) = rest[:G], rest[G:]
        l = pl.program_id(0)
        g = None
        for k in range(G):
            gk = p_refs[k][0].astype(F32)
            for j in range(1, P):
                gk = gk + p_refs[k][j].astype(F32)
            g = gk if g is None else jnp.where(l == k, gk, g)
        nm = ADAM_B1 * m_ref[...] + (1.0 - ADAM_B1) * g
        nv = ADAM_B2 * v_ref[...] + (1.0 - ADAM_B2) * (g * g)
        g_ref[...] = g
        nm_ref[...] = nm
        nv_ref[...] = nv
        d_ref[...] = -ADAM_LR * ((nm * c1) / (jnp.sqrt(nv * c2) + ADAM_EPS) + ADAM_WD * w_ref[...])

    blk = pl.BlockSpec((None, tb, C), lambda l, i: (l, i, 0))

    def part_spec(k):
        return pl.BlockSpec((P, tb, C), lambda l, i: (0, jnp.where(l < k, 0, jnp.where(l > k, nb - 1, i)), 0))

    return pl.pallas_call(
        body, name=name, grid=(G, nb),
        in_specs=[blk, blk, blk] + [part_spec(k) for k in range(G)],
        out_specs=[blk] * 4, out_shape=[jax.ShapeDtypeStruct((G, R, C), F32)] * 4,
        compiler_params=_cparams(("arbitrary", "arbitrary")),
    )(w, m, v, *parts)


BIG = ("ffn1_w_in", "ffn1_w_out", "mix_w_in", "gdn_w_o", "cnv_w_o", "mix_w_out", "ffn2_w_in", "ffn2_w_out")
COL_SHARDED = ("ffn1_w_in", "mix_w_in", "ffn2_w_in")
SMALL_SHARDED = ("gdn_conv_w", "cnv_dw_w")
NAMES = ("ffn1_norm_pre", "ffn1_norm_post", "ffn1_w_in", "ffn1_w_out", "mix_norm_pre", "mix_norm_post", "mix_w_in",
         "gdn_conv_w", "gdn_a_log", "gdn_dt_bias", "gdn_norm_w", "gdn_w_o", "cnv_pw1_b", "cnv_dw_w", "cnv_dw_b",
         "cnv_ln_g", "cnv_ln_b", "cnv_w_o", "cnv_b_o", "mix_w_out", "ffn2_norm_pre", "ffn2_norm_post", "ffn2_w_in",
         "ffn2_w_out")
SMALL = tuple(n for n in NAMES if n not in BIG)


class _LayerWeights:
    def __init__(self, got, params, i, D, H):
        self.got, self.params, self.i, self.D, self.H, self.made = got, params, i, D, H, {}

    def __getitem__(self, n):
        if n not in self.made:
            if n in BIG:
                g = self.got[(self.i, n)]
                g = g.reshape(-1, g.shape[-1])
                w = _mix_in_reorder(g, self.D, self.H) if n == "mix_w_in" else g
            elif n in SMALL_SHARDED:
                g = self.got[(self.i, n)]
                w = jnp.transpose(g, (1, 0, 2)).reshape(g.shape[1], -1)
            else:
                v = self.params[n][self.i]
                if n in ("gdn_a_log", "gdn_dt_bias"):
                    v = jnp.pad(v, (0, LANES - self.H))
                w = v.reshape(1, -1)
            self.made[n] = w
        return self.made[n]


MIX_SMALL = ("gdn_w_o", "cnv_w_o", "mix_w_out", "gdn_conv_w", "cnv_dw_w")
GATHER_HOSTS = (("l{j}_mix_in", ("ffn1_w_in", "ffn1_w_out")), ("l{j}_mix_qkvconv", MIX_SMALL),
                ("l{j}_mix_gdn", ("mix_w_in",)), ("l{i}_mix_gdn", ("ffn2_w_in", "ffn2_w_out")))
GATHER_HOSTS_FIRST = ((None, ("ffn1_w_in", "ffn1_w_out")), ("l0_ffn1_in", ("mix_w_in",)), ("l0_ffn1_out", MIX_SMALL),
                      ("l0_mix_gdn", ("ffn2_w_in", "ffn2_w_out")))
REDUCE_HOSTS = ((BIG, "l{j}_ffn2_ds", (("l{j}_ffn2_dh", ("ffn2_w_in",)),
                                       ("l{j}_ffn2_dwin", ("ffn2_w_out", "gdn_w_o", "cnv_w_o", "mix_w_out")),
                                       ("l{j}_mix_ddwconv", ("ffn1_w_in",)),
                                       ("l{j}_mix_dgdn", ("mix_w_in", "ffn1_w_out")))),)
REDUCE_HOSTS_FIRST = (
    (("ffn2_w_in", "ffn2_w_out"), "l0_mix_dmerge", (("l0_mix_dln", ("ffn2_w_out",)), ("l0_mix_dqkvconv", ("ffn2_w_in",)))),
    (("mix_w_out", "cnv_w_o", "gdn_w_o"), "l0_mix_dgdngate", (("l0_mix_dh", ("mix_w_out", "cnv_w_o", "gdn_w_o")),)),
    (("mix_w_in",), "l0_mix_dpre", (("l0_ffn1_dh", ("mix_w_in",)),)),
    (("ffn1_w_out",), "l0_ffn1_dact", (("l0_ffn1_dwin", ("ffn1_w_out",)),)),
    (("ffn1_w_in",), None, ((None, ("ffn1_w_in",)),)))


def kernel(x, ffn1_norm_pre, ffn1_norm_post, ffn1_w_in, ffn1_w_out, mix_norm_pre, mix_norm_post, mix_w_in, gdn_conv_w, gdn_a_log, gdn_dt_bias, gdn_norm_w, gdn_w_o, cnv_pw1_b, cnv_dw_w, cnv_dw_b, cnv_ln_g, cnv_ln_b, cnv_w_o, cnv_b_o, mix_w_out, ffn2_norm_pre, ffn2_norm_post, ffn2_w_in, ffn2_w_out, loss_target, m_ffn1_norm_pre, m_ffn1_norm_post, m_ffn1_w_in, m_ffn1_w_out, m_mix_norm_pre, m_mix_norm_post, m_mix_w_in, m_gdn_conv_w, m_gdn_a_log, m_gdn_dt_bias, m_gdn_norm_w, m_gdn_w_o, m_cnv_pw1_b, m_cnv_dw_w, m_cnv_dw_b, m_cnv_ln_g, m_cnv_ln_b, m_cnv_w_o, m_cnv_b_o, m_mix_w_out, m_ffn2_norm_pre, m_ffn2_norm_post, m_ffn2_w_in, m_ffn2_w_out, v_ffn1_norm_pre, v_ffn1_norm_post, v_ffn1_w_in, v_ffn1_w_out, v_mix_norm_pre, v_mix_norm_post, v_mix_w_in, v_gdn_conv_w, v_gdn_a_log, v_gdn_dt_bias, v_gdn_norm_w, v_gdn_w_o, v_cnv_pw1_b, v_cnv_dw_w, v_cnv_dw_b, v_cnv_ln_g, v_cnv_ln_b, v_cnv_w_o, v_cnv_b_o, v_mix_w_out, v_ffn2_norm_pre, v_ffn2_norm_post, v_ffn2_w_in, v_ffn2_w_out):
    params = dict(zip(NAMES, (ffn1_norm_pre, ffn1_norm_post, ffn1_w_in, ffn1_w_out, mix_norm_pre, mix_norm_post, mix_w_in, gdn_conv_w, gdn_a_log, gdn_dt_bias, gdn_norm_w, gdn_w_o, cnv_pw1_b, cnv_dw_w, cnv_dw_b, cnv_ln_g, cnv_ln_b, cnv_w_o, cnv_b_o, mix_w_out, ffn2_norm_pre, ffn2_norm_post, ffn2_w_in, ffn2_w_out)))
    mom1 = dict(zip(NAMES, (m_ffn1_norm_pre, m_ffn1_norm_post, m_ffn1_w_in, m_ffn1_w_out, m_mix_norm_pre, m_mix_norm_post, m_mix_w_in, m_gdn_conv_w, m_gdn_a_log, m_gdn_dt_bias, m_gdn_norm_w, m_gdn_w_o, m_cnv_pw1_b, m_cnv_dw_w, m_cnv_dw_b, m_cnv_ln_g, m_cnv_ln_b, m_cnv_w_o, m_cnv_b_o, m_mix_w_out, m_ffn2_norm_pre, m_ffn2_norm_post, m_ffn2_w_in, m_ffn2_w_out)))
    mom2 = dict(zip(NAMES, (v_ffn1_norm_pre, v_ffn1_norm_post, v_ffn1_w_in, v_ffn1_w_out, v_mix_norm_pre, v_mix_norm_post, v_mix_w_in, v_gdn_conv_w, v_gdn_a_log, v_gdn_dt_bias, v_gdn_norm_w, v_gdn_w_o, v_cnv_pw1_b, v_cnv_dw_w, v_cnv_dw_b, v_cnv_ln_g, v_cnv_ln_b, v_cnv_w_o, v_cnv_b_o, v_mix_w_out, v_ffn2_norm_pre, v_ffn2_norm_post, v_ffn2_w_in, v_ffn2_w_out)))
    T, D = x.shape[1], x.shape[2]
    H = D // GDN_DK
    L = ffn1_norm_pre.shape[0]
    xi, yi, ci = _coords()
    dev = 4 * xi + 2 * yi + ci

    ag_names = BIG + SMALL_SHARDED

    def shard_to_send(n):
        if n in COL_SHARDED:
            return jnp.swapaxes(params[n], 1, 2).astype(BF16)
        return params[n].astype(BF16) if n in BIG else params[n]

    send = {n: shard_to_send(n) for n in ag_names}

    hosts, got, later = _Hosts(), {}, []
    for i in range(L):
        for host, names in (GATHER_HOSTS_FIRST if i == 0 else GATHER_HOSTS):
            def make(i=i, names=names):
                blocks = [send[n][i] for n in names]

                def deliver(outs):
                    got.update({(i, n): o for n, o in zip(names, outs)})
                return _GatherPlan(blocks), blocks, deliver
            if host is None:
                plan, blocks, deliver = make()
                deliver(_comm_only(plan, blocks, f"ag_weights_l{i}"))
            else:
                hosts.add(host.format(i=i, j=i - 1), make)

    cidx = jnp.reshape(ci, (1,)).astype(jnp.int32)
    grads, reduced = [{} for _ in range(L)], {}
    for i in range(L):
        for names, sib_host, chip_hosts in (REDUCE_HOSTS_FIRST if i == 0 else REDUCE_HOSTS):
            stage = {}

            def make_sib(i=i, names=names, stage=stage):
                Gs = []
                for n in names:
                    g = _mix_in_restore(grads[i][n], D, H) if n == "mix_w_in" else grads[i][n]
                    Gs.append(g.reshape(N_DEV, -1, g.shape[-1]))
                stage["G"] = dict(zip(names, Gs))

                def deliver(outs):
                    stage["R1"] = dict(zip(names, outs))
                return _SiblingPlan(Gs), Gs, deliver

            def make_chips(ns, i=i, stage=stage):
                Ps = [_pair_add(stage["G"][n], stage["R1"][n], cidx, f"l{i}_pair_add_{n}") for n in ns]

                def deliver(outs):
                    reduced.update({(i, n): o for n, o in zip(ns, outs)})
                return _ChipsPlan(Ps), Ps, deliver

            if sib_host is None:
                later.append((f"l{i}_{names[0]}", make_sib, [(lambda ns=ns, mc=make_chips: mc(ns)) for _, ns in chip_hosts]))
                continue
            hosts.add(sib_host.format(i=i, j=i - 1), make_sib)
            for host, ns in chip_hosts:
                hosts.add(host.format(i=i, j=i - 1), lambda ns=ns, mc=make_chips: mc(ns))

    weights = [_LayerWeights(got, params, i, D, H) for i in range(L)]
    loss_row, dx = _trunk_fwd_bwd(x[0], loss_target[0], H, L, lambda i: weights[i], grads, hosts)
    loss = lax.psum(loss_row[0, 0], ("x", "y", "c"))
    assert not hosts.waiting, sorted(hosts.waiting)

    for tag, make_sib, chip_makes in later:
        plan, Gs, deliver = make_sib()
        deliver(_comm_only(plan, Gs, f"rs_sibling_{tag}"))
        for mk in chip_makes:
            plan, Ps, deliver = mk()
            deliver(_comm_only(plan, Ps, f"rs_chips_{tag}"))
    R2s = {n: [jnp.swapaxes(reduced[(i, n)], 1, 2) if n in COL_SHARDED else reduced[(i, n)] for i in range(L)]
           for n in BIG}

    pieces = []
    for i in range(L):
        for n in SMALL:
            piece = grads[i][n].reshape(-1, LANES)
            pieces.append(jnp.pad(piece, ((0, (-piece.shape[0]) % 8), (0, 0))))
    packed = jnp.concatenate(pieces, axis=0)
    small_all = _comm_only(_GatherPlan([packed]), [packed], "ag_small_grads")[0]
    small_sum = _sum_parts(small_all, "sum_small_grads")
    small_g = {n: [None] * L for n in SMALL}
    off = 0
    for i in range(L):
        for n in SMALL:
            shape = grads[i][n].shape
            cnt = shape[0] * shape[1] // LANES
            g = small_sum[off:off + cnt].reshape(shape)
            off += cnt + (-cnt) % 8
            if n in ("gdn_a_log", "gdn_dt_bias"):
                g = g[:, :H]
            if n in SMALL_SHARDED:
                wloc = params[n].shape[-1]
                g = lax.dynamic_slice_in_dim(g, dev * wloc, wloc, axis=1)
            small_g[n][i] = g

    outs = {}
    for n in NAMES:
        w, m, v = params[n], mom1[n], mom2[n]
        if n in BIG:
            shape3, parts = w.shape, R2s[n]
        else:
            rows, cols = (w.shape[0] * w.shape[1], w.shape[2]) if w.ndim == 3 else w.shape
            shape3, parts = (1, rows, cols), [jnp.stack(small_g[n], axis=0).reshape(1, rows, cols)]
        res = _adamw(w.reshape(shape3), m.reshape(shape3), v.reshape(shape3), parts, "adamw_" + n)
        outs[n] = [r.reshape(w.shape) for r in res]

    result = [loss, dx[None]]
    for k in range(4):
        result += [outs[n][k] for n in NAMES]
    return tuple(result)
```

```python
import jax
import jax.numpy as jnp
from jax import lax
from jax.experimental import pallas as pl
from jax.experimental.pallas import tpu as pltpu

F32 = jnp.float32
BF16 = jnp.bfloat16

GDN_DK = 128
CHUNK = 64
GDN_CONV = 4
CNV_K = 31
RMS_EPS = 1e-6
LN_EPS = 1e-5
L2_EPS = 1e-6
ADAM_LR = 0.001
ADAM_B1 = 0.9
ADAM_B2 = 0.999
ADAM_EPS = 1e-08
ADAM_WD = 0.01
ADAM_STEP = 10

LANES = 128
SUB = 16
VMEM_LIMIT = 56 * 1024 * 1024
N_DEV = 8
MESH = pl.DeviceIdType.MESH


def _cparams(sem=None, **kw):
    if sem is not None:
        kw["dimension_semantics"] = sem
    return pltpu.CompilerParams(vmem_limit_bytes=VMEM_LIMIT, **kw)


def _tile(dim, target):
    best = None
    for t in range(LANES, min(dim, target) + 1, LANES):
        if dim % t == 0:
            best = t
    return best if best is not None else dim


def _sigmoid(x):
    return 1.0 / (1.0 + jnp.exp(-x))


def _silu(x):
    return x * _sigmoid(x)


def _dsilu(x):
    s = _sigmoid(x)
    return s * (1.0 + x * (1.0 - s))


MM_VMEM_BUDGET = 40 * 1024 * 1024
MM_MAX_TILE = 2048


def _mm_tiles(M, N, K, out_bytes):
    def cands(dim):
        c = [t for t in range(LANES, min(dim, MM_MAX_TILE) + 1, LANES) if dim % t == 0]
        return c or [dim]
    best = None
    for tm in cands(M):
        for tn in cands(N):
            vm = 2 * (2 * K * (tm + tn) + tm * tn * out_bytes)
            if vm <= MM_VMEM_BUDGET and (best is None or tm * tn > best[0] * best[1]):
                best = (tm, tn)
    return best if best is not None else (cands(M)[0], cands(N)[0])


def _matmul(a, b, mode, out_dtype, name, hosts=None, rows=None):
    if mode == "nn":
        (M, K), N = a.shape, b.shape[1]
    elif mode == "nt":
        (M, K), N = a.shape, b.shape[0]
    else:
        (K, M), N = a.shape, b.shape[1]
    first = 0
    if rows is not None:
        first, N = rows
    tm, tn = _mm_tiles(M, N, K, jnp.dtype(out_dtype).itemsize)
    assert first % tn == 0
    joff = first // tn
    if mode == "nn":
        a_spec = pl.BlockSpec((tm, K), lambda j, i: (i, 0))
        b_spec = pl.BlockSpec((K, tn), lambda j, i: (0, j))
        dn = (((1,), (0,)), ((), ()))
    elif mode == "nt":
        a_spec = pl.BlockSpec((tm, K), lambda j, i: (i, 0))
        b_spec = pl.BlockSpec((tn, K), lambda j, i: (j + joff, 0))
        dn = (((1,), (1,)), ((), ()))
    else:
        a_spec = pl.BlockSpec((K, tm), lambda j, i: (0, i))
        b_spec = pl.BlockSpec((K, tn), lambda j, i: (0, j))
        dn = (((0,), (0,)), ((), ()))

    def body(a_ref, b_ref, o_ref):
        o_ref[...] = lax.dot_general(a_ref[...], b_ref[...], dn, preferred_element_type=F32).astype(out_dtype)

    return _hosted_call(
        body, hosts, name=name, grid=(N // tn, M // tm), in_specs=[a_spec, b_spec],
        out_specs=[pl.BlockSpec((tm, tn), lambda j, i: (i, j))],
        out_shape=[jax.ShapeDtypeStruct((M, N), out_dtype)], scratch_shapes=[],
        args=(a, b), sem=("parallel", "parallel"))[0]


def _rowcall(name, body, T, tb, row_ins, par_ins, row_outs, acc_outs, hosts=None, wide=None):
    n_ri, n_pi, n_ro = len(row_ins), len(par_ins), len(row_outs)
    n_extra = 0 if (wide is None or wide[3] is None) else 1

    def kern(*refs):
        ri, pi = refs[:n_ri], refs[n_ri:n_ri + n_pi]
        refs = refs[n_ri + n_pi + n_extra:]
        ro, ao = refs[:n_ro], refs[n_ro:]
        if ao:
            @pl.when(pl.program_id(0) == 0)
            def _():
                for r in ao:
                    r[...] = jnp.zeros_like(r)
        body(ri, pi, ro, ao)

    in_specs = [pl.BlockSpec((tb, w), lambda i, cb=cb: (i, cb)) for (_, w, cb) in row_ins]
    in_specs += [pl.BlockSpec(p.shape, lambda i: (0, 0)) for p in par_ins]
    out_specs = [pl.BlockSpec((tb, w), lambda i: (i, 0)) for (w, _) in row_outs]
    out_specs += [pl.BlockSpec((1, w), lambda i: (0, 0)) for w in acc_outs]
    out_shape = [jax.ShapeDtypeStruct((T, w), dt) for (w, dt) in row_outs]
    out_shape += [jax.ShapeDtypeStruct((1, w), F32) for w in acc_outs]
    args, aliases = [*[a for (a, _, _) in row_ins], *par_ins], {}
    if wide is not None:
        k, total, cb, buf = wide
        w, dt = row_outs[k]
        out_specs[k] = pl.BlockSpec((tb, w), lambda i: (i, cb))
        out_shape[k] = jax.ShapeDtypeStruct((T, total), dt)
        if buf is not None:
            in_specs.append(pl.BlockSpec(memory_space=pl.ANY))
            args.append(buf)
            aliases = {len(args) - 1: k}
    return _hosted_call(
        kern, hosts, name=name, grid=(T // tb,), in_specs=in_specs, out_specs=out_specs, out_shape=out_shape,
        scratch_shapes=[], args=tuple(args), sem=("arbitrary",), aliases=aliases)


def _rsum(x):
    return jnp.sum(x, axis=0, keepdims=True)


def _rms_rstd(x):
    return lax.rsqrt(jnp.mean(x * x, axis=-1, keepdims=True) + RMS_EPS)


def _rms_fwd(x, w, name):
    T, D = x.shape

    def body(ri, pi, ro, ao):
        xv = ri[0][...]
        ro[0][...] = (xv * _rms_rstd(xv) * pi[0][...]).astype(BF16)

    return _rowcall(name, body, T, 256, [(x, D, 0)], [w], [(D, BF16)], [])[0]


def _rms_bwd_core(dy, x, w):
    rs = _rms_rstd(x)
    xh = x * rs
    gw = dy * w
    dx = rs * (gw - xh * jnp.mean(gw * xh, axis=-1, keepdims=True))
    return dx, dy * xh


def _pre_bwd(dh, x, w, dres, name, hosts=None):
    T, D = x.shape

    def body(ri, pi, ro, ao):
        dx, dwc = _rms_bwd_core(ri[0][...], ri[1][...], pi[0][...])
        ro[0][...] = ri[2][...] + dx
        ao[0][...] += _rsum(dwc)

    return _rowcall(name, body, T, 256, [(dh, D, 0), (x, D, 0), (dres, D, 0)], [w], [(D, F32)], [D], hosts=hosts)


def _post_fwd(x, f, w, r, name):
    T, D = x.shape

    def body(ri, pi, ro, ao):
        fv = ri[1][...]
        ro[0][...] = ri[0][...] + r * (fv * _rms_rstd(fv) * pi[0][...])

    return _rowcall(name, body, T, 256, [(x, D, 0), (f, D, 0)], [w], [(D, F32)], [])[0]


def _post_bwd(dxn, f, w, r, name):
    T, D = f.shape

    def body(ri, pi, ro, ao):
        df, dwc = _rms_bwd_core(r * ri[0][...], ri[1][...], pi[0][...])
        ro[0][...] = df.astype(BF16)
        ao[0][...] += _rsum(dwc)

    return _rowcall(name, body, T, 256, [(dxn, D, 0), (f, D, 0)], [w], [(D, BF16)], [D])


def _swiglu_fwd(a, name):
    T, F2 = a.shape
    F = F2 // 2

    def body(ri, pi, ro, ao):
        ro[0][...] = (_silu(ri[0][...].astype(F32)) * ri[1][...].astype(F32)).astype(BF16)

    return _rowcall(name, body, T, 256, [(a, F, 0), (a, F, 1)], [], [(F, BF16)], [])[0]


def _swiglu_bwd(ds, a, name, hosts=None):
    T, F2 = a.shape
    F = F2 // 2

    def body(ri, pi, ro, ao):
        dsv, g, u = ri[0][...].astype(F32), ri[1][...].astype(F32), ri[2][...].astype(F32)
        ro[0][:, :F] = (dsv * u * _dsilu(g)).astype(BF16)
        ro[0][:, F:] = (dsv * _silu(g)).astype(BF16)

    return _rowcall(name, body, T, 256, [(ds, F, 0), (a, F, 0), (a, F, 1)], [], [(F2, BF16)], [], hosts=hosts)[0]


def _loss_fwd_bwd(y, tgt, name):
    T, D = y.shape

    def body(ri, pi, ro, ao):
        e = ri[0][...] - ri[1][...]
        ro[0][...] = e * (1.0 / D)
        tot = jnp.sum(_rsum(e * e), axis=1, keepdims=True) * (0.5 / D)
        ao[0][...] += jnp.broadcast_to(tot, (1, LANES))

    return _rowcall(name, body, T, 256, [(y, D, 0), (tgt, D, 0)], [], [(D, F32)], [LANES])


def _gdn_gate_fwd(o, p, nw, name):
    T, D = o.shape
    H = D // GDN_DK

    def body(ri, pi, ro, ao):
        for h in range(H):
            sl = slice(h * GDN_DK, (h + 1) * GDN_DK)
            oh = ri[0][:, sl]
            ro[0][:, sl] = (oh * _rms_rstd(oh) * pi[0][...] * _silu(ri[1][:, sl].astype(F32))).astype(BF16)

    return _rowcall(name, body, T, 256, [(o, D, 0), (p, D, 3)], [nw], [(D, BF16)], [])[0]


def _gdn_gate_bwd(dog, o, p, nw, name, hosts=None, dp=None):
    T, D = o.shape
    H = D // GDN_DK

    def body(ri, pi, ro, ao):
        acc = jnp.zeros((1, GDN_DK), F32)
        for h in range(H):
            sl = slice(h * GDN_DK, (h + 1) * GDN_DK)
            dy, oh, z = ri[0][:, sl], ri[1][:, sl], ri[2][:, sl].astype(F32)
            sz = _silu(z)
            do, dwc = _rms_bwd_core(dy * sz, oh, pi[0][...])
            ro[0][:, sl] = do
            ro[1][:, sl] = (dy * oh * _rms_rstd(oh) * pi[0][...] * _dsilu(z)).astype(BF16)
            acc = acc + _rsum(dwc)
        ao[0][...] += acc

    return _rowcall(name, body, T, 256, [(dog, D, 0), (o, D, 0), (p, D, 3)], [nw], [(D, F32), (D, BF16)], [GDN_DK],
                    hosts=hosts, wide=None if dp is None else (1, dp.shape[1], 3, dp))


def _glu_fwd(p, b, name):
    T = p.shape[0]
    D = b.shape[1] // 2

    def body(ri, pi, ro, ao):
        ro[0][...] = (ri[0][...].astype(F32) + pi[0][:, :D]) * _sigmoid(ri[1][...].astype(F32) + pi[0][:, D:])

    return _rowcall(name, body, T, 256, [(p, D, 4), (p, D, 5)], [b], [(D, F32)], [])[0]


def _glu_bwd(dhc, p, b, name, dp=None):
    T = p.shape[0]
    D = b.shape[1] // 2

    def body(ri, pi, ro, ao):
        d, a, g = ri[0][...], ri[1][...].astype(F32) + pi[0][:, :D], ri[2][...].astype(F32) + pi[0][:, D:]
        sg = _sigmoid(g)
        da, dg = d * sg, d * a * sg * (1.0 - sg)
        ro[0][:, :D] = da.astype(BF16)
        ro[0][:, D:] = dg.astype(BF16)
        ao[0][:, :D] += _rsum(da)
        ao[0][:, D:] += _rsum(dg)

    return _rowcall(name, body, T, 256, [(dhc, D, 0), (p, D, 4), (p, D, 5)], [b], [(2 * D, BF16)], [2 * D],
                    wide=None if dp is None else (0, dp.shape[1], 2, dp))


def _ln_stats(x):
    mu = jnp.mean(x, axis=-1, keepdims=True)
    xc = x - mu
    rstd = lax.rsqrt(jnp.mean(xc * xc, axis=-1, keepdims=True) + LN_EPS)
    return xc * rstd, rstd


def _ln_silu_fwd(hcv, g, b, name):
    T, D = hcv.shape

    def body(ri, pi, ro, ao):
        xh, _ = _ln_stats(ri[0][...])
        ro[0][...] = _silu(xh * pi[0][...] + pi[1][...]).astype(BF16)

    return _rowcall(name, body, T, 256, [(hcv, D, 0)], [g, b], [(D, BF16)], [])[0]


def _ln_silu_bwd(dhl, hcv, g, b, name, hosts=None):
    T, D = hcv.shape

    def body(ri, pi, ro, ao):
        xh, rstd = _ln_stats(ri[1][...])
        dyl = ri[0][...] * _dsilu(xh * pi[0][...] + pi[1][...])
        dxh = dyl * pi[0][...]
        dx = rstd * (dxh - jnp.mean(dxh, axis=-1, keepdims=True) - xh * jnp.mean(dxh * xh, axis=-1, keepdims=True))
        ro[0][...] = dx
        ao[0][...] += _rsum(dyl * xh)
        ao[1][...] += _rsum(dyl)
        ao[2][...] += _rsum(dx)

    return _rowcall(name, body, T, 256, [(dhl, D, 0), (hcv, D, 0)], [g, b], [(D, F32)], [D, D, D], hosts=hosts)


def _merge_fwd(p, ya, yb, bo, name):
    T, D = ya.shape

    def body(ri, pi, ro, ao):
        ga, gb = _sigmoid(ri[0][...].astype(F32)), _sigmoid(ri[1][...].astype(F32))
        ro[0][...] = (ga * ri[2][...] + gb * (ri[3][...] + pi[0][...])).astype(BF16)

    return _rowcall(name, body, T, 256, [(p, D, 6), (p, D, 7), (ya, D, 0), (yb, D, 0)], [bo], [(D, BF16)], [])[0]


def _merge_bwd(dym, p, ya, yb, bo, name, hosts=None, dp_width=None):
    T, D = ya.shape

    def body(ri, pi, ro, ao):
        d = ri[0][...]
        ga, gb = _sigmoid(ri[1][...].astype(F32)), _sigmoid(ri[2][...].astype(F32))
        ybv = ri[4][...] + pi[0][...]
        dyb = d * gb
        ro[0][...] = (d * ga).astype(BF16)
        ro[1][...] = dyb.astype(BF16)
        ro[2][:, :D] = (d * ri[3][...] * ga * (1.0 - ga)).astype(BF16)
        ro[2][:, D:] = (d * ybv * gb * (1.0 - gb)).astype(BF16)
        ao[0][...] += _rsum(dyb)

    return _rowcall(name, body, T, 256, [(dym, D, 0), (p, D, 6), (p, D, 7), (ya, D, 0), (yb, D, 0)], [bo],
                    [(D, BF16), (D, BF16), (2 * D, BF16)], [D], hosts=hosts,
                    wide=None if dp_width is None else (2, dp_width, 3, None))


PAD = 32
RC = 256


def _tap_windows(ref, offs):
    groups = {}
    for j, o in enumerate(offs):
        groups.setdefault(o % 8, []).append((j, o))
    for grp in groups.values():
        lo, hi = min(o for _, o in grp), max(o for _, o in grp)
        win = ref[pl.ds(lo, RC + hi - lo), :]
        for j, o in grp:
            yield j, win[o - lo:o - lo + RC]


def _causal_taps(xp_ref, w, K, c0):
    acc = None
    for j, xs in _tap_windows(xp_ref, [PAD - (K - 1) + j + c0 for j in range(K)]):
        term = w[j:j + 1, :] * xs
        acc = term if acc is None else acc + term
    return acc


def _anticausal_taps(dp_ref, w, K, c0):
    acc = None
    for j, ds in _tap_windows(dp_ref, [(K - 1) - j + c0 for j in range(K)]):
        term = w[j:j + 1, :] * ds
        acc = term if acc is None else acc + term
    return acc


def _tap_grads(dw_ref, dc_ref, xp_ref, K, T):
    accs = [jnp.zeros((8, LANES), F32) for _ in range(K)]
    for c in range(T // RC):
        d = dc_ref[pl.ds(c * RC, RC), :]
        for j, xs in _tap_windows(xp_ref, [PAD - (K - 1) + j + c * RC for j in range(K)]):
            accs[j] = accs[j] + jnp.sum((d * xs).reshape(RC // 8, 8, LANES), axis=0)
    for j in range(K):
        dw_ref[j:j + 1, :] = _rsum(accs[j])


def _qkv_conv_fwd(p, cw, H, name, hosts=None):
    T = p.shape[0]
    K = cw.shape[0]

    def body(x_ref, w_ref, o_ref, xp_ref):
        j = pl.program_id(0)
        xp_ref[pl.ds(0, PAD), :] = jnp.zeros((PAD, LANES), F32)
        xp_ref[pl.ds(PAD, T), :] = x_ref[...].astype(F32)
        w = w_ref[...]
        scale = jnp.where(j < H, GDN_DK ** -0.5, 1.0).astype(F32)
        for c in range(T // RC):
            act = _silu(_causal_taps(xp_ref, w, K, c * RC))
            nrm = act * lax.rsqrt(jnp.sum(act * act, axis=-1, keepdims=True) + L2_EPS) * scale
            o_ref[pl.ds(c * RC, RC), :] = jnp.where(j < 2 * H, nrm, act)

    return _hosted_call(
        body, hosts, name=name, grid=(3 * H,),
        in_specs=[pl.BlockSpec((T, LANES), lambda j: (0, j)), pl.BlockSpec((K, LANES), lambda j: (0, j))],
        out_specs=[pl.BlockSpec((T, LANES), lambda j: (0, j))],
        out_shape=[jax.ShapeDtypeStruct((T, 3 * H * GDN_DK), F32)],
        scratch_shapes=[pltpu.VMEM((T + PAD, LANES), F32)], args=(p, cw), sem=("arbitrary",))[0]


def _qkv_conv_bwd(dn, p, cw, H, name, hosts=None, dp=None):
    T = p.shape[0]
    K = cw.shape[0]

    def body(dn_ref, x_ref, w_ref, _, dx_ref, dw_ref, xp_ref, dc_ref):
        j = pl.program_id(0)
        xp_ref[pl.ds(0, PAD), :] = jnp.zeros((PAD, LANES), F32)
        xp_ref[pl.ds(PAD, T), :] = x_ref[...].astype(F32)
        dc_ref[pl.ds(T, PAD), :] = jnp.zeros((PAD, LANES), F32)
        w = w_ref[...]
        scale = jnp.where(j < H, GDN_DK ** -0.5, 1.0).astype(F32)
        for c in range(T // RC):
            pre = _causal_taps(xp_ref, w, K, c * RC)
            act = _silu(pre)
            d = dn_ref[pl.ds(c * RC, RC), :]
            rs = lax.rsqrt(jnp.sum(act * act, axis=-1, keepdims=True) + L2_EPS)
            nh = act * rs
            dact_n = scale * rs * (d - nh * jnp.sum(d * nh, axis=-1, keepdims=True))
            dact = jnp.where(j < 2 * H, dact_n, d)
            dc_ref[pl.ds(c * RC, RC), :] = dact * _dsilu(pre)
        for c in range(T // RC):
            dx_ref[pl.ds(c * RC, RC), :] = _anticausal_taps(dc_ref, w, K, c * RC).astype(BF16)
        _tap_grads(dw_ref, dc_ref, xp_ref, K, T)

    return _hosted_call(
        body, hosts, name=name, grid=(3 * H,),
        in_specs=[pl.BlockSpec((T, LANES), lambda j: (0, j)), pl.BlockSpec((T, LANES), lambda j: (0, j)),
                  pl.BlockSpec((K, LANES), lambda j: (0, j)), pl.BlockSpec(memory_space=pl.ANY)],
        out_specs=[pl.BlockSpec((T, LANES), lambda j: (0, j)), pl.BlockSpec((K, LANES), lambda j: (0, j))],
        out_shape=[jax.ShapeDtypeStruct(dp.shape, BF16), jax.ShapeDtypeStruct(cw.shape, F32)],
        scratch_shapes=[pltpu.VMEM((T + PAD, LANES), F32), pltpu.VMEM((T + PAD, LANES), F32)],
        args=(dn, p, cw, dp), sem=("arbitrary",), aliases={3: 0})


def _dw_conv_fwd(hc, w, b, name):
    T, D = hc.shape
    K = w.shape[0]

    def body(x_ref, w_ref, b_ref, o_ref, xp_ref):
        xp_ref[pl.ds(0, PAD), :] = jnp.zeros((PAD, LANES), F32)
        xp_ref[pl.ds(PAD, T), :] = x_ref[...]
        wv = w_ref[...]
        for c in range(T // RC):
            o_ref[pl.ds(c * RC, RC), :] = _causal_taps(xp_ref, wv, K, c * RC) + b_ref[...]

    return pl.pallas_call(
        body, name=name, grid=(D // LANES,),
        in_specs=[pl.BlockSpec((T, LANES), lambda j: (0, j)), pl.BlockSpec((K, LANES), lambda j: (0, j)),
                  pl.BlockSpec((1, LANES), lambda j: (0, j))],
        out_specs=pl.BlockSpec((T, LANES), lambda j: (0, j)),
        out_shape=jax.ShapeDtypeStruct((T, D), F32),
        scratch_shapes=[pltpu.VMEM((T + PAD, LANES), F32)],
        compiler_params=_cparams(("arbitrary",)),
    )(hc, w, b)


def _dw_conv_bwd(dy, hc, w, name, hosts=None):
    T, D = hc.shape
    K = w.shape[0]

    def body(dy_ref, x_ref, w_ref, dx_ref, dw_ref, xp_ref, dc_ref):
        xp_ref[pl.ds(0, PAD), :] = jnp.zeros((PAD, LANES), F32)
        xp_ref[pl.ds(PAD, T), :] = x_ref[...]
        dc_ref[pl.ds(T, PAD), :] = jnp.zeros((PAD, LANES), F32)
        dc_ref[pl.ds(0, T), :] = dy_ref[...]
        wv = w_ref[...]
        for c in range(T // RC):
            dx_ref[pl.ds(c * RC, RC), :] = _anticausal_taps(dc_ref, wv, K, c * RC)
        _tap_grads(dw_ref, dc_ref, xp_ref, K, T)

    return _hosted_call(
        body, hosts, name=name, grid=(D // LANES,), sem=("arbitrary",),
        in_specs=[pl.BlockSpec((T, LANES), lambda j: (0, j)), pl.BlockSpec((T, LANES), lambda j: (0, j)),
                  pl.BlockSpec((K, LANES), lambda j: (0, j))],
        out_specs=[pl.BlockSpec((T, LANES), lambda j: (0, j)), pl.BlockSpec((K, LANES), lambda j: (0, j))],
        out_shape=[jax.ShapeDtypeStruct((T, D), F32), jax.ShapeDtypeStruct(w.shape, F32)],
        scratch_shapes=[pltpu.VMEM((T + PAD, LANES), F32), pltpu.VMEM((T + PAD, LANES), F32)],
        args=(dy, hc, w))


NN = (((1,), (0,)), ((), ()))
NT = (((1,), (1,)), ((), ()))
TN = (((0,), (0,)), ((), ()))


def _dotb(a, b, dn=NN):
    return lax.dot_general(a.astype(BF16), b.astype(BF16), dn, preferred_element_type=F32)


def _split_bf16(x, n):
    parts, r = [], x
    for _ in range(n):
        p = r.astype(BF16)
        parts.append(p)
        r = r - p.astype(F32)
    return parts


def _dot_sel(sel, x, pieces, sel_left=True):
    sb = sel.astype(BF16)
    acc = None
    for p in _split_bf16(x, pieces):
        t = (lax.dot_general(sb, p, NN, preferred_element_type=F32) if sel_left
             else lax.dot_general(p, sb, NN, preferred_element_type=F32))
        acc = t if acc is None else acc + t
    return acc


def _iota2(shape, axis):
    return lax.broadcasted_iota(jnp.int32, shape, axis)


def _to_row(col, eye):
    return jnp.sum(jnp.where(eye, col, 0.0), axis=0, keepdims=True)


def _to_col(row, eye):
    return jnp.sum(jnp.where(eye, row, 0.0), axis=1, keepdims=True)


def _gdn_gates(bl, al, alog, dtb):
    beta = _sigmoid(bl)
    x = al + dtb
    sp = jnp.maximum(x, 0.0) + jnp.log(1.0 + jnp.exp(-jnp.abs(x)))
    g = -jnp.exp(alog) * sp
    r, c = _iota2((CHUNK, CHUNK), 0), _iota2((CHUNK, CHUNK), 1)
    G = _dot_sel(r >= c, g, 3)
    return beta, g, G, x


def _unit_lower_inverses(As, Ats):
    n = len(As)
    nb = CHUNK // SUB
    lane = _iota2((SUB, CHUNK), 1)
    row = _iota2((SUB, CHUNK), 0)
    Atp = []
    for At in Ats:
        acc = jnp.zeros((SUB, CHUNK), F32)
        for b in range(nb):
            acc = jnp.where(lane // SUB == b, At[b * SUB:(b + 1) * SUB, :], acc)
        Atp.append(acc)
    gr, gc = _iota2((CHUNK, CHUNK), 0), _iota2((CHUNK, CHUNK), 1)
    ones_bd = gr // SUB == gc // SUB
    stack = jnp.concatenate(
        [jnp.where(lane % SUB == i, Atp[m], 0.0) for i in range(1, SUB) for m in range(n)], axis=0)
    Cm = _dot_sel(ones_bd, stack, 2, sel_left=False)
    Z = [(row == lane % SUB).astype(F32) for _ in range(n)]
    for i in range(1, SUB):
        for m in range(n):
            cm = Cm[((i - 1) * n + m) * SUB:((i - 1) * n + m + 1) * SUB, :]
            new = -jnp.sum(cm * Z[m], axis=0, keepdims=True)
            Z[m] = Z[m] + jnp.where(row == i, new, 0.0)
    bd = gr // SUB == gc // SUB
    Xs = [jnp.where(bd, jnp.concatenate([Z[m]] * nb, axis=0), 0.0) for m in range(n)]
    blk = SUB
    while blk < CHUNK:
        off = (gr // (2 * blk) == gc // (2 * blk)) & (gr // blk != gc // blk)
        Ys = [_dotb(Xs[m], jnp.where(off, As[m], 0.0)) for m in range(n)]
        Xs = [Xs[m] - _dotb(Ys[m], Xs[m]) for m in range(n)]
        blk *= 2
    return Xs


def _gdn_fwd(qkvn, p, alog, dtb, H, name, hosts=None):
    T = qkvn.shape[0]
    D = H * GDN_DK
    N = T // CHUNK
    bblk = 0

    def body(q_ref, k_ref, v_ref, b_ref, a_ref, alog_ref, dtb_ref, o_ref, t_ref, s_ref, S_scr):
        @pl.when(pl.program_id(0) == 0)
        def _():
            S_scr[...] = jnp.zeros_like(S_scr)

        beta, _, G, _ = _gdn_gates(b_ref[...], a_ref[...], alog_ref[...], dtb_ref[...])
        hs = range(H)
        sl = [slice(h * GDN_DK, (h + 1) * GDN_DK) for h in hs]
        q, k, v = [q_ref[:, s] for s in sl], [k_ref[:, s] for s in sl], [v_ref[:, s] for s in sl]
        Gc, bc = [G[:, h:h + 1] for h in hs], [beta[:, h:h + 1] for h in hs]
        r, c = _iota2((CHUNK, CHUNK), 0), _iota2((CHUNK, CHUNK), 1)
        eye, low, up = r == c, r >= c, r <= c
        Gr, br = [_to_row(Gc[h], eye) for h in hs], [_to_row(bc[h], eye) for h in hs]
        Dm = [jnp.where(low, jnp.exp(jnp.where(low, Gc[h] - Gr[h], 0.0)), 0.0) for h in hs]
        Dt = [jnp.where(up, jnp.exp(jnp.where(up, Gr[h] - Gc[h], 0.0)), 0.0) for h in hs]
        qk = [_dotb(jnp.concatenate([q[h], k[h]], axis=0), k[h], NT) for h in hs]
        QK = [qk[h][:CHUNK] * Dm[h] for h in hs]
        KK = [qk[h][CHUNK:] for h in hs]
        As = [jnp.where(r > c, KK[h] * Dm[h], 0.0) * bc[h] for h in hs]
        Ats = [jnp.where(r < c, KK[h] * Dt[h], 0.0) * br[h] for h in hs]
        Ts = _unit_lower_inverses(As, Ats)
        eG = [jnp.exp(Gc[h]) for h in hs]
        Gl = [Gc[h][CHUNK - 1:CHUNK, :] for h in hs]
        uw = [_dotb(Ts[h], jnp.concatenate([v[h] * bc[h], k[h] * (bc[h] * eG[h])], axis=1)) for h in hs]
        S = [S_scr[h] for h in hs]
        qw = [_dotb(jnp.concatenate([q[h] * eG[h], uw[h][:, GDN_DK:]], axis=0), S[h]) for h in hs]
        vn = [uw[h][:, :GDN_DK] - qw[h][CHUNK:] for h in hs]
        o = [qw[h][:CHUNK] + _dotb(QK[h], vn[h]) for h in hs]
        Sn = [S[h] * jnp.exp(Gl[h]) + _dotb(k[h] * jnp.exp(Gl[h] - Gc[h]), vn[h], TN) for h in hs]
        for h in hs:
            t_ref[0, h] = Ts[h]
            s_ref[0, h] = S[h]
            o_ref[:, sl[h]] = o[h]
            S_scr[h] = Sn[h]

    qkv_spec = [pl.BlockSpec((CHUNK, D), lambda n, cb=cb: (n, cb)) for cb in range(3)]
    return _hosted_call(
        body, hosts, name=name, grid=(N,), sem=("arbitrary",),
        in_specs=qkv_spec + [pl.BlockSpec((CHUNK, LANES), lambda n: (n, bblk)),
                             pl.BlockSpec((CHUNK, LANES), lambda n: (n, bblk + 1)),
                             pl.BlockSpec((1, LANES), lambda n: (0, 0)), pl.BlockSpec((1, LANES), lambda n: (0, 0))],
        out_specs=[pl.BlockSpec((CHUNK, D), lambda n: (n, 0)),
                   pl.BlockSpec((1, H, CHUNK, CHUNK), lambda n: (n, 0, 0, 0)),
                   pl.BlockSpec((1, H, GDN_DK, GDN_DK), lambda n: (n, 0, 0, 0))],
        out_shape=[jax.ShapeDtypeStruct((T, D), F32), jax.ShapeDtypeStruct((N, H, CHUNK, CHUNK), F32),
                   jax.ShapeDtypeStruct((N, H, GDN_DK, GDN_DK), F32)],
        scratch_shapes=[pltpu.VMEM((H, GDN_DK, GDN_DK), F32)],
        args=(qkvn, qkvn, qkvn, p, p, alog, dtb))


def _gdn_bwd(do, qkvn, p, alog, dtb, Tinv, Sin, H, name, hosts=None, dp=None):
    T = qkvn.shape[0]
    D = H * GDN_DK
    N = T // CHUNK
    bblk = 0

    def body(do_ref, q_ref, k_ref, v_ref, b_ref, a_ref, alog_ref, dtb_ref, t_ref, s_ref, _,
             dqkv_ref, dba_ref, dalog_ref, ddtb_ref, dS_scr):
        @pl.when(pl.program_id(0) == 0)
        def _():
            dS_scr[...] = jnp.zeros_like(dS_scr)
            dalog_ref[...] = jnp.zeros_like(dalog_ref)
            ddtb_ref[...] = jnp.zeros_like(ddtb_ref)

        beta, g, G, x = _gdn_gates(b_ref[...], a_ref[...], alog_ref[...], dtb_ref[...])
        r, c = _iota2((CHUNK, CHUNK), 0), _iota2((CHUNK, CHUNK), 1)
        eye, low, strict = r == c, r >= c, r > c
        lane = _iota2((CHUNK, LANES), 1)
        rsum1 = lambda a: jnp.sum(a, axis=1, keepdims=True)
        hs = range(H)
        sl = [slice(h * GDN_DK, (h + 1) * GDN_DK) for h in hs]
        q, k, v = [q_ref[:, s] for s in sl], [k_ref[:, s] for s in sl], [v_ref[:, s] for s in sl]
        dov = [do_ref[:, s] for s in sl]
        Gc, bc = [G[:, h:h + 1] for h in hs], [beta[:, h:h + 1] for h in hs]
        Tm, S, dSo = [t_ref[0, h] for h in hs], [s_ref[0, h] for h in hs], [dS_scr[h] for h in hs]
        Gr = [_to_row(Gc[h], eye) for h in hs]
        Dm = [jnp.where(low, jnp.exp(jnp.where(low, Gc[h] - Gr[h], 0.0)), 0.0) for h in hs]
        eG = [jnp.exp(Gc[h]) for h in hs]
        Gl = [Gc[h][CHUNK - 1:CHUNK, :] for h in hs]
        eR, dch = [jnp.exp(Gl[h] - Gc[h]) for h in hs], [jnp.exp(Gl[h]) for h in hs]
        qk = [_dotb(jnp.concatenate([q[h], k[h]], axis=0), k[h], NT) for h in hs]
        QKr, KK = [qk[h][:CHUNK] for h in hs], [qk[h][CHUNK:] for h in hs]
        QK = [QKr[h] * Dm[h] for h in hs]
        M = [jnp.where(strict, KK[h] * Dm[h], 0.0) for h in hs]
        uw = [_dotb(Tm[h], jnp.concatenate([v[h] * bc[h], k[h] * (bc[h] * eG[h])], axis=1)) for h in hs]
        u, w = [uw[h][:, :GDN_DK] for h in hs], [uw[h][:, GDN_DK:] for h in hs]
        qd, kd = [q[h] * eG[h] for h in hs], [k[h] * eR[h] for h in hs]
        vn = [u[h] - _dotb(w[h], S[h]) for h in hs]
        dvn = [_dotb(QK[h], dov[h], TN) + _dotb(kd[h], dSo[h]) for h in hs]
        dQK = [jnp.where(low, _dotb(dov[h], vn[h], NT), 0.0) for h in hs]
        dkd = [_dotb(vn[h], dSo[h], NT) for h in hs]
        ddch = [jnp.sum(rsum1(dSo[h] * S[h]), axis=0, keepdims=True) for h in hs]
        dd = [jnp.concatenate([dov[h], dvn[h]], axis=0) for h in hs]
        xs = [_dotb(dd[h], S[h], NT) for h in hs]
        dqd, dw = [xs[h][:CHUNK] for h in hs], [-xs[h][CHUNK:] for h in hs]
        dS = [_dotb(jnp.concatenate([qd[h], -w[h]], axis=0), dd[h], TN) + dch[h] * dSo[h] for h in hs]
        yb = [_dotb(Tm[h], jnp.concatenate([dvn[h], dw[h]], axis=1), TN) for h in hs]
        dvb, dkb = [yb[h][:, :GDN_DK] for h in hs], [yb[h][:, GDN_DK:] for h in hs]
        dA = [-jnp.where(strict, _dotb(yb[h], uw[h], NT), 0.0) for h in hs]
        rk = [rsum1(dkb[h] * k[h]) for h in hs]
        dbeta = [rsum1(dvb[h] * v[h]) + rk[h] * eG[h] + rsum1(dA[h] * M[h]) for h in hs]
        dM = [dA[h] * bc[h] for h in hs]
        dKK = [dM[h] * Dm[h] for h in hs]
        dQKr = [dQK[h] * Dm[h] for h in hs]
        E = [dM[h] * M[h] + dQK[h] * QK[h] for h in hs]
        zk = [_dotb(jnp.concatenate([dQKr[h], dKK[h]], axis=0), k[h]) for h in hs]
        dq = [zk[h][:CHUNK] + dqd[h] * eG[h] for h in hs]
        dk = [dkb[h] * (bc[h] * eG[h]) + zk[h][CHUNK:] + _dotb(dKK[h], k[h], TN) + _dotb(dQKr[h], q[h], TN)
              + dkd[h] * eR[h] for h in hs]
        deG = [rk[h] * bc[h] + rsum1(dqd[h] * q[h]) for h in hs]
        deR = [rsum1(dkd[h] * k[h]) for h in hs]
        dGl = [jnp.sum(deR[h] * eR[h], axis=0, keepdims=True) + ddch[h] * dch[h] for h in hs]
        dGc = [rsum1(E[h]) - _to_col(jnp.sum(E[h], axis=0, keepdims=True), eye) + deG[h] * eG[h] - deR[h] * eR[h]
               + jnp.where(r[:, :1] == CHUNK - 1, dGl[h], 0.0) for h in hs]
        dG_all = jnp.zeros((CHUNK, LANES), F32)
        dbeta_all = jnp.zeros((CHUNK, LANES), F32)
        for h in hs:
            dS_scr[h] = dS[h]
            dqkv_ref[:, sl[h]] = dq[h]
            dqkv_ref[:, D + h * GDN_DK:D + (h + 1) * GDN_DK] = dk[h]
            dqkv_ref[:, 2 * D + h * GDN_DK:2 * D + (h + 1) * GDN_DK] = dvb[h] * bc[h]
            dG_all = jnp.where(lane == h, dGc[h], dG_all)
            dbeta_all = jnp.where(lane == h, dbeta[h], dbeta_all)
        dg = _dot_sel(r <= c, dG_all, 3)
        da = dg * (-jnp.exp(alog_ref[...])) * _sigmoid(x)
        dba_ref[:, :LANES] = (dbeta_all * beta * (1.0 - beta)).astype(BF16)
        dba_ref[:, LANES:] = da.astype(BF16)
        dalog_ref[...] += _rsum(dg * g)
        ddtb_ref[...] += _rsum(da)

    rev = lambda n: N - 1 - n
    qkv_spec = [pl.BlockSpec((CHUNK, D), lambda n, cb=cb: (rev(n), cb)) for cb in range(3)]
    return _hosted_call(
        body, hosts, name=name, grid=(N,), sem=("arbitrary",),
        in_specs=[pl.BlockSpec((CHUNK, D), lambda n: (rev(n), 0))] + qkv_spec + [
            pl.BlockSpec((CHUNK, LANES), lambda n: (rev(n), bblk)),
            pl.BlockSpec((CHUNK, LANES), lambda n: (rev(n), bblk + 1)),
            pl.BlockSpec((1, LANES), lambda n: (0, 0)), pl.BlockSpec((1, LANES), lambda n: (0, 0)),
            pl.BlockSpec((1, H, CHUNK, CHUNK), lambda n: (rev(n), 0, 0, 0)),
            pl.BlockSpec((1, H, GDN_DK, GDN_DK), lambda n: (rev(n), 0, 0, 0)),
            pl.BlockSpec(memory_space=pl.ANY)],
        out_specs=[pl.BlockSpec((CHUNK, 3 * D), lambda n: (rev(n), 0)),
                   pl.BlockSpec((CHUNK, 2 * LANES), lambda n: (rev(n), 8 * D // (2 * LANES))),
                   pl.BlockSpec((1, LANES), lambda n: (0, 0)), pl.BlockSpec((1, LANES), lambda n: (0, 0))],
        out_shape=[jax.ShapeDtypeStruct((T, 3 * D), F32), jax.ShapeDtypeStruct(dp.shape, BF16),
                   jax.ShapeDtypeStruct((1, LANES), F32), jax.ShapeDtypeStruct((1, LANES), F32)],
        scratch_shapes=[pltpu.VMEM((H, GDN_DK, GDN_DK), F32)],
        args=(do, qkvn, qkvn, qkvn, p, p, alog, dtb, Tinv, Sin, dp), aliases={10: 1})


def _mix_in_reorder(wt, D, H):
    o1 = 4 * D
    o2, o3 = o1 + H, o1 + 2 * H
    z = jnp.zeros((LANES - H, wt.shape[1]), wt.dtype)
    return jnp.concatenate([wt[:o1], wt[o3:], wt[o1:o2], z, wt[o2:o3], z], axis=0)


def _mix_in_restore(dwt, D, H):
    b0 = 8 * D
    return jnp.concatenate([dwt[:4 * D], dwt[b0:b0 + H], dwt[b0 + LANES:b0 + LANES + H], dwt[4 * D:b0]], axis=0)


def _ffn_fwd(x, W, pre, tag, hosts=None):
    h = _rms_fwd(x, W[pre + "_norm_pre"], tag + "_pre")
    a = _matmul(h, W[pre + "_w_in"], "nt", BF16, tag + "_in", hosts)
    s = _swiglu_fwd(a, tag + "_act")
    f = _matmul(s, W[pre + "_w_out"], "nn", F32, tag + "_out", hosts)
    return _post_fwd(x, f, W[pre + "_norm_post"], 0.5, tag + "_post"), (x, h, a, s, f)


def _ffn_bwd(dxn, saved, W, pre, tag, g, hosts=None):
    x, h, a, s, f = saved
    df, g[pre + "_norm_post"] = _post_bwd(dxn, f, W[pre + "_norm_post"], 0.5, tag + "_dpost")
    ds = _matmul(df, W[pre + "_w_out"], "nt", BF16, tag + "_ds", hosts)
    g[pre + "_w_out"] = _matmul(s, df, "tn", BF16, tag + "_dwout")
    da = _swiglu_bwd(ds, a, tag + "_dact", hosts)
    g[pre + "_w_in"] = _matmul(da, h, "tn", BF16, tag + "_dwin", hosts)
    dh = _matmul(da, W[pre + "_w_in"], "nn", F32, tag + "_dh", hosts)
    dx, g[pre + "_norm_pre"] = _pre_bwd(dh, x, W[pre + "_norm_pre"], dxn, tag + "_dpre", hosts)
    return dx


def _mix_fwd(x, W, H, tag, hosts=None):
    h = _rms_fwd(x, W["mix_norm_pre"], tag + "_pre")
    D = x.shape[1]
    p = _matmul(h, W["mix_w_in"], "nt", BF16, tag + "_in", hosts, rows=(0, 8 * D))
    pba = _matmul(h, W["mix_w_in"], "nt", F32, tag + "_inba", rows=(8 * D, 2 * LANES))
    qkvn = _qkv_conv_fwd(p, W["gdn_conv_w"], H, tag + "_qkvconv", hosts)
    o, Tinv, Sin = _gdn_fwd(qkvn, pba, W["gdn_a_log"], W["gdn_dt_bias"], H, tag + "_gdn", hosts)
    og = _gdn_gate_fwd(o, p, W["gdn_norm_w"], tag + "_gdngate")
    ya = _matmul(og, W["gdn_w_o"], "nn", F32, tag + "_gdno")
    hc = _glu_fwd(p, W["cnv_pw1_b"], tag + "_glu")
    hcv = _dw_conv_fwd(hc, W["cnv_dw_w"], W["cnv_dw_b"], tag + "_dwconv")
    hl = _ln_silu_fwd(hcv, W["cnv_ln_g"], W["cnv_ln_b"], tag + "_ln")
    yb = _matmul(hl, W["cnv_w_o"], "nn", F32, tag + "_cnvo")
    ym = _merge_fwd(p, ya, yb, W["cnv_b_o"], tag + "_merge")
    y = _matmul(ym, W["mix_w_out"], "nn", F32, tag + "_out")
    xn = _post_fwd(x, y, W["mix_norm_post"], 1.0, tag + "_post")
    return xn, (x, h, p, pba, qkvn, o, Tinv, Sin, og, ya, hc, hcv, hl, yb, ym, y)


def _mix_bwd(dxn, saved, W, H, tag, g, hosts=None):
    x, h, p, pba, qkvn, o, Tinv, Sin, og, ya, hc, hcv, hl, yb, ym, y = saved
    dy, g["mix_norm_post"] = _post_bwd(dxn, y, W["mix_norm_post"], 1.0, tag + "_dpost")
    dym = _matmul(dy, W["mix_w_out"], "nt", F32, tag + "_dym")
    g["mix_w_out"] = _matmul(ym, dy, "tn", BF16, tag + "_dwout")
    dya, dyb, dp, g["cnv_b_o"] = _merge_bwd(dym, p, ya, yb, W["cnv_b_o"], tag + "_dmerge", hosts,
                                            dp_width=p.shape[1] + 2 * LANES)
    dhl = _matmul(dyb, W["cnv_w_o"], "nt", F32, tag + "_dhl")
    g["cnv_w_o"] = _matmul(hl, dyb, "tn", BF16, tag + "_dwcnvo")
    dhcv, g["cnv_ln_g"], g["cnv_ln_b"], g["cnv_dw_b"] = _ln_silu_bwd(
        dhl, hcv, W["cnv_ln_g"], W["cnv_ln_b"], tag + "_dln", hosts)
    dhc, g["cnv_dw_w"] = _dw_conv_bwd(dhcv, hc, W["cnv_dw_w"], tag + "_ddwconv", hosts)
    dp, g["cnv_pw1_b"] = _glu_bwd(dhc, p, W["cnv_pw1_b"], tag + "_dglu", dp)
    dog = _matmul(dya, W["gdn_w_o"], "nt", F32, tag + "_dog")
    g["gdn_w_o"] = _matmul(og, dya, "tn", BF16, tag + "_dwgdno")
    do, dp, g["gdn_norm_w"] = _gdn_gate_bwd(dog, o, p, W["gdn_norm_w"], tag + "_dgdngate", hosts, dp)
    dqkvn, dp, g["gdn_a_log"], g["gdn_dt_bias"] = _gdn_bwd(
        do, qkvn, pba, W["gdn_a_log"], W["gdn_dt_bias"], Tinv, Sin, H, tag + "_dgdn", hosts, dp)
    dp, g["gdn_conv_w"] = _qkv_conv_bwd(dqkvn, p, W["gdn_conv_w"], H, tag + "_dqkvconv", hosts, dp)
    dh = _matmul(dp, W["mix_w_in"], "nn", F32, tag + "_dh", hosts)
    g["mix_w_in"] = _matmul(dp, h, "tn", F32, tag + "_dwin")
    dx, g["mix_norm_pre"] = _pre_bwd(dh, x, W["mix_norm_pre"], dxn, tag + "_dpre", hosts)
    return dx


def _trunk_fwd_bwd(x, tgt, H, L, weights_of, grads, hosts=None):
    saved, Ws = [], []
    for i in range(L):
        W = weights_of(i)
        Ws.append(W)
        x, s1 = _ffn_fwd(x, W, "ffn1", f"l{i}_ffn1", hosts)
        x, s2 = _mix_fwd(x, W, H, f"l{i}_mix", hosts)
        x, s3 = _ffn_fwd(x, W, "ffn2", f"l{i}_ffn2", hosts)
        saved.append((s1, s2, s3))
    dx, loss = _loss_fwd_bwd(x, tgt, "loss")
    for i in reversed(range(L)):
        s1, s2, s3 = saved[i]
        dx = _ffn_bwd(dx, s3, Ws[i], "ffn2", f"l{i}_ffn2", grads[i], hosts)
        dx = _mix_bwd(dx, s2, Ws[i], H, f"l{i}_mix", grads[i], hosts)
        dx = _ffn_bwd(dx, s1, Ws[i], "ffn1", f"l{i}_ffn1", grads[i], hosts)
    return loss, dx


HBM_SPEC = pl.BlockSpec(memory_space=pltpu.HBM)


def _coords():
    return lax.axis_index("x"), lax.axis_index("y"), lax.axis_index("c")


class _GatherPlan:
    has_middle = True

    def __init__(self, shards):
        self.n = len(shards)
        self.out_shape = [jax.ShapeDtypeStruct((N_DEV,) + s.shape, s.dtype) for s in shards]
        self.sems = [pltpu.SemaphoreType.DMA((self.n, 7)), pltpu.SemaphoreType.DMA((self.n, 7)),
                     pltpu.SemaphoreType.DMA((self.n,))]

    def _parts(self, ins, outs, sems):
        send_sems, recv_sems, local_sems = sems
        x, y, c = _coords()
        me, sibling = (x, y, c), (x, y, 1 - c)
        chips = [(1 - x, y), (x, 1 - y), (1 - x, 1 - y)]

        def copy(w, k, block, to, src=None):
            dst = outs[w].at[4 * block[0] + 2 * block[1] + block[2]]
            return pltpu.make_async_remote_copy(
                src_ref=dst if src is None else src, dst_ref=dst, send_sem=send_sems.at[w, k],
                recv_sem=recv_sems.at[w, k], device_id=to, device_id_type=MESH)

        mine = [pltpu.make_async_copy(ins[w], outs[w].at[4 * x + 2 * y + c], local_sems.at[w]) for w in range(self.n)]
        first = []
        for w in range(self.n):
            first.append(copy(w, 0, me, sibling, src=ins[w]))
            first += [copy(w, 1 + j, me, (*chip, c), src=ins[w]) for j, chip in enumerate(chips)]
        passed = [copy(w, 4 + j, (*chip, c), sibling) for j, chip in enumerate(chips) for w in range(self.n)]
        return copy, mine, first, passed, chips, me, sibling, c

    def begin(self, ins, outs, sems):
        _, mine, first, _, _, _, _, _ = self._parts(ins, outs, sems)
        for cp in mine + first:
            cp.start()

    def middle(self, ins, outs, sems):
        copy, _, _, passed, chips, me, _, c = self._parts(ins, outs, sems)
        for j, chip in enumerate(chips):
            for w in range(self.n):
                copy(w, 1 + j, (*chip, c), me).wait_recv()
                passed[j * self.n + w].start()

    def finish(self, ins, outs, sems):
        copy, mine, first, passed, chips, me, sibling, c = self._parts(ins, outs, sems)
        for w in range(self.n):
            copy(w, 0, sibling, me).wait_recv()
            for j, chip in enumerate(chips):
                copy(w, 4 + j, (*chip, 1 - c), me).wait_recv()
        for cp in first + passed:
            cp.wait_send()
        for cp in mine:
            cp.wait()


class _SiblingPlan:
    has_middle = False

    def __init__(self, Gs):
        self.n = len(Gs)
        self.out_shape = [jax.ShapeDtypeStruct((4,) + g.shape[1:], g.dtype) for g in Gs]
        self.sems = [pltpu.SemaphoreType.DMA((self.n, 4)), pltpu.SemaphoreType.DMA((self.n, 4))]

    def _copies(self, ins, outs, sems):
        send_sems, recv_sems = sems
        x, y, c = _coords()
        return [pltpu.make_async_remote_copy(
            src_ref=ins[w].at[2 * q + (1 - c)], dst_ref=outs[w].at[q], send_sem=send_sems.at[w, q],
            recv_sem=recv_sems.at[w, q], device_id=(x, y, 1 - c), device_id_type=MESH)
            for w in range(self.n) for q in range(4)]

    def begin(self, ins, outs, sems):
        for cp in self._copies(ins, outs, sems):
            cp.start()

    def finish(self, ins, outs, sems):
        for cp in self._copies(ins, outs, sems):
            cp.wait()


class _ChipsPlan:
    has_middle = False

    def __init__(self, Ps):
        self.n = len(Ps)
        self.out_shape = [jax.ShapeDtypeStruct(p.shape, p.dtype) for p in Ps]
        self.sems = [pltpu.SemaphoreType.DMA((self.n, 3)), pltpu.SemaphoreType.DMA((self.n, 3)),
                     pltpu.SemaphoreType.DMA((self.n,))]

    def _copies(self, ins, outs, sems):
        send_sems, recv_sems, local_sems = sems
        x, y, c = _coords()
        me_q = 2 * x + y
        cps = []
        for w in range(self.n):
            cps.append(pltpu.make_async_copy(ins[w].at[me_q], outs[w].at[me_q], local_sems.at[w]))
            for j, (px, py) in enumerate([(1 - x, y), (x, 1 - y), (1 - x, 1 - y)]):
                cps.append(pltpu.make_async_remote_copy(
                    src_ref=ins[w].at[2 * px + py], dst_ref=outs[w].at[me_q], send_sem=send_sems.at[w, j],
                    recv_sem=recv_sems.at[w, j], device_id=(px, py, c), device_id_type=MESH))
        return cps

    def begin(self, ins, outs, sems):
        for cp in self._copies(ins, outs, sems):
            cp.start()

    def finish(self, ins, outs, sems):
        for cp in self._copies(ins, outs, sems):
            cp.wait()


def _comm_only(plan, arrays, name):
    n = plan.n

    def body(*refs):
        ins, outs, sems = refs[:n], refs[n:2 * n], refs[2 * n:]
        plan.begin(ins, outs, sems)
        if plan.has_middle:
            plan.middle(ins, outs, sems)
        plan.finish(ins, outs, sems)

    return pl.pallas_call(
        body, name=name, out_shape=plan.out_shape, in_specs=[HBM_SPEC] * n, out_specs=[HBM_SPEC] * n,
        scratch_shapes=plan.sems,
    )(*arrays)


class _MultiPlan:
    def __init__(self, plans):
        self.plans = plans
        self.n = sum(p.n for p in plans)
        self.out_shape = [s for p in plans for s in p.out_shape]
        self.sems = [s for p in plans for s in p.sems]
        self.has_middle = any(p.has_middle for p in plans)

    def _each(self, phase, ins, outs, sems):
        a = s = 0
        for p in self.plans:
            if phase != "middle" or p.has_middle:
                getattr(p, phase)(ins[a:a + p.n], outs[a:a + p.n], sems[s:s + len(p.sems)])
            a, s = a + p.n, s + len(p.sems)

    def begin(self, ins, outs, sems):
        self._each("begin", ins, outs, sems)

    def middle(self, ins, outs, sems):
        self._each("middle", ins, outs, sems)

    def finish(self, ins, outs, sems):
        self._each("finish", ins, outs, sems)


class _Hosts:
    def __init__(self):
        self.waiting = {}

    def add(self, host, make):
        self.waiting.setdefault(host, []).append(make)

    def take(self, host):
        makes = self.waiting.pop(host, None)
        if not makes:
            return None
        items = [m() for m in makes]
        return _MultiPlan([it[0] for it in items]), [a for it in items for a in it[1]], [it[2] for it in items]


def _hosted_call(body, hosts, *, name, grid, in_specs, out_specs, out_shape, scratch_shapes, args, sem, aliases=None):
    comm = hosts.take(name) if hosts is not None else None
    if comm is None:
        return pl.pallas_call(body, name=name, grid=grid, in_specs=in_specs, out_specs=out_specs, out_shape=out_shape,
                              scratch_shapes=scratch_shapes, input_output_aliases=aliases or {},
                              compiler_params=_cparams(sem))(*args)
    plan, arrays, deliver = comm
    n_in, n_out, n_scr, n = len(in_specs), len(out_specs), len(scratch_shapes), plan.n
    total = 1
    for g in grid:
        total *= g

    def kern(*refs):
        ins, cins = refs[:n_in], refs[n_in:n_in + n]
        outs, couts = refs[n_in + n:n_in + n + n_out], refs[n_in + n + n_out:n_in + 2 * n + n_out]
        scr, csems = refs[n_in + 2 * n + n_out:n_in + 2 * n + n_out + n_scr], refs[n_in + 2 * n + n_out + n_scr:]
        step = pl.program_id(0)
        for d in range(1, len(grid)):
            step = step * grid[d] + pl.program_id(d)

        @pl.when(step == 0)
        def _():
            plan.begin(cins, couts, csems)

        body(*ins, *outs, *scr)
        if plan.has_middle:
            @pl.when(step == (3 * total) // 4)
            def _():
                plan.middle(cins, couts, csems)

        @pl.when(step == total - 1)
        def _():
            plan.finish(cins, couts, csems)

    res = pl.pallas_call(
        kern, name=name, grid=grid, in_specs=list(in_specs) + [HBM_SPEC] * n,
        out_specs=list(out_specs) + [HBM_SPEC] * n, out_shape=list(out_shape) + plan.out_shape,
        scratch_shapes=list(scratch_shapes) + plan.sems, input_output_aliases=aliases or {},
        compiler_params=_cparams(("arbitrary",) * len(grid)),
    )(*args, *arrays)
    k = n_out
    for p, d in zip(plan.plans, deliver):
        d(res[k:k + p.n])
        k += p.n
    return res[:n_out]


def _row_tile(R, target=256):
    best = None
    for t in range(8, min(R, target) + 1, 8):
        if R % t == 0:
            best = t
    return best if best is not None else R


def _pair_add(G, R1, cidx, name):
    _, R, C = G.shape
    tb = _row_tile(R)

    def body(c_ref, g_ref, r_ref, o_ref):
        o_ref[...] = (g_ref[...].astype(F32) + r_ref[...].astype(F32)).astype(BF16)

    return pl.pallas_call(
        body, name=name,
        grid_spec=pltpu.PrefetchScalarGridSpec(
            num_scalar_prefetch=1, grid=(4, R // tb),
            in_specs=[pl.BlockSpec((None, tb, C), lambda q, i, cr: (2 * q + cr[0], i, 0)),
                      pl.BlockSpec((None, tb, C), lambda q, i, cr: (q, i, 0))],
            out_specs=pl.BlockSpec((None, tb, C), lambda q, i, cr: (q, i, 0))),
        out_shape=jax.ShapeDtypeStruct((4, R, C), BF16),
        compiler_params=_cparams(("arbitrary", "arbitrary")),
    )(cidx, G, R1)


def _sum_parts(parts, name):
    P, R, C = parts.shape

    def body(p_ref, o_ref):
        acc = p_ref[0]
        for j in range(1, P):
            acc = acc + p_ref[j]
        o_ref[...] = acc

    return pl.pallas_call(
        body, name=name, out_shape=jax.ShapeDtypeStruct((R, C), F32),
        in_specs=[pl.BlockSpec(memory_space=pltpu.VMEM)], out_specs=pl.BlockSpec(memory_space=pltpu.VMEM),
        compiler_params=_cparams(),
    )(parts)


def _adamw(w, m, v, parts, name):
    G, R, C = w.shape
    P = parts[0].shape[0]
    tb = _row_tile(R)
    nb = R // tb
    c1 = 1.0 / (1.0 - ADAM_B1 ** ADAM_STEP)
    c2 = 1.0 / (1.0 - ADAM_B2 ** ADAM_STEP)

    def body(w_ref, m_ref, v_ref, *rest):
        p_refs, (g_ref, d_ref, nm_ref, nv_ref) = rest[:G], rest[G:]
        l = pl.program_id(0)
        g = None
        for k in range(G):
            gk = p_refs[k][0].astype(F32)
            for j in range(1, P):
                gk = gk + p_refs[k][j].astype(F32)
            g = gk if g is None else jnp.where(l == k, gk, g)
        nm = ADAM_B1 * m_ref[...] + (1.0 - ADAM_B1) * g
        nv = ADAM_B2 * v_ref[...] + (1.0 - ADAM_B2) * (g * g)
        g_ref[...] = g
        nm_ref[...] = nm
        nv_ref[...] = nv
        d_ref[...] = -ADAM_LR * ((nm * c1) / (jnp.sqrt(nv * c2) + ADAM_EPS) + ADAM_WD * w_ref[...])

    blk = pl.BlockSpec((None, tb, C), lambda l, i: (l, i, 0))

    def part_spec(k):
        return pl.BlockSpec((P, tb, C), lambda l, i: (0, jnp.where(l < k, 0, jnp.where(l > k, nb - 1, i)), 0))

    return pl.pallas_call(
        body, name=name, grid=(G, nb),
        in_specs=[blk, blk, blk] + [part_spec(k) for k in range(G)],
        out_specs=[blk] * 4, out_shape=[jax.ShapeDtypeStruct((G, R, C), F32)] * 4,
        compiler_params=_cparams(("arbitrary", "arbitrary")),
    )(w, m, v, *parts)


BIG = ("ffn1_w_in", "ffn1_w_out", "mix_w_in", "gdn_w_o", "cnv_w_o", "mix_w_out", "ffn2_w_in", "ffn2_w_out")
COL_SHARDED = ("ffn1_w_in", "mix_w_in", "ffn2_w_in")
SMALL_SHARDED = ("gdn_conv_w", "cnv_dw_w")
NAMES = ("ffn1_norm_pre", "ffn1_norm_post", "ffn1_w_in", "ffn1_w_out", "mix_norm_pre", "mix_norm_post", "mix_w_in",
         "gdn_conv_w", "gdn_a_log", "gdn_dt_bias", "gdn_norm_w", "gdn_w_o", "cnv_pw1_b", "cnv_dw_w", "cnv_dw_b",
         "cnv_ln_g", "cnv_ln_b", "cnv_w_o", "cnv_b_o", "mix_w_out", "ffn2_norm_pre", "ffn2_norm_post", "ffn2_w_in",
         "ffn2_w_out")
SMALL = tuple(n for n in NAMES if n not in BIG)


class _LayerWeights:
    def __init__(self, got, params, i, D, H):
        self.got, self.params, self.i, self.D, self.H, self.made = got, params, i, D, H, {}

    def __getitem__(self, n):
        if n not in self.made:
            if n in BIG:
                g = self.got[(self.i, n)]
                g = g.reshape(-1, g.shape[-1])
                w = _mix_in_reorder(g, self.D, self.H) if n == "mix_w_in" else g
            elif n in SMALL_SHARDED:
                g = self.got[(self.i, n)]
                w = jnp.transpose(g, (1, 0, 2)).reshape(g.shape[1], -1)
            else:
                v = self.params[n][self.i]
                if n in ("gdn_a_log", "gdn_dt_bias"):
                    v = jnp.pad(v, (0, LANES - self.H))
                w = v.reshape(1, -1)
            self.made[n] = w
        return self.made[n]


MIX_SMALL = ("gdn_w_o", "cnv_w_o", "mix_w_out", "gdn_conv_w", "cnv_dw_w")
GATHER_HOSTS = (("l{j}_mix_in", ("ffn1_w_in", "ffn1_w_out")), ("l{j}_mix_qkvconv", MIX_SMALL),
                ("l{j}_ffn2_in", ("mix_w_in",)), ("l{i}_mix_gdn", ("ffn2_w_in", "ffn2_w_out")))
GATHER_HOSTS_FIRST = ((None, ("ffn1_w_in", "ffn1_w_out")), ("l0_ffn1_in", ("mix_w_in",)), ("l0_ffn1_out", MIX_SMALL),
                      ("l0_mix_gdn", ("ffn2_w_in", "ffn2_w_out")))
REDUCE_HOSTS = ((BIG, "l{j}_ffn2_ds", (("l{j}_ffn2_dh", ("ffn2_w_in",)),
                                       ("l{j}_ffn2_dwin", ("ffn2_w_out", "gdn_w_o", "cnv_w_o", "mix_w_out")),
                                       ("l{j}_mix_ddwconv", ("ffn1_w_in",)),
                                       ("l{j}_mix_dgdn", ("mix_w_in", "ffn1_w_out")))),)
REDUCE_HOSTS_FIRST = (
    (("ffn2_w_in", "ffn2_w_out"), "l0_mix_dmerge", (("l0_mix_dln", ("ffn2_w_out",)), ("l0_mix_dqkvconv", ("ffn2_w_in",)))),
    (("mix_w_out", "cnv_w_o", "gdn_w_o"), "l0_mix_dgdngate", (("l0_mix_dh", ("mix_w_out", "cnv_w_o", "gdn_w_o")),)),
    (("mix_w_in",), "l0_mix_dpre", (("l0_ffn1_dwin", ("mix_w_in",)),)),
    (("ffn1_w_out",), "l0_ffn1_dact", (("l0_ffn1_dh", ("ffn1_w_out",)),)),
    (("ffn1_w_in",), "l0_ffn1_dh", (("l0_ffn1_dpre", ("ffn1_w_in",)),)))


def kernel(x, ffn1_norm_pre, ffn1_norm_post, ffn1_w_in, ffn1_w_out, mix_norm_pre, mix_norm_post, mix_w_in, gdn_conv_w, gdn_a_log, gdn_dt_bias, gdn_norm_w, gdn_w_o, cnv_pw1_b, cnv_dw_w, cnv_dw_b, cnv_ln_g, cnv_ln_b, cnv_w_o, cnv_b_o, mix_w_out, ffn2_norm_pre, ffn2_norm_post, ffn2_w_in, ffn2_w_out, loss_target, m_ffn1_norm_pre, m_ffn1_norm_post, m_ffn1_w_in, m_ffn1_w_out, m_mix_norm_pre, m_mix_norm_post, m_mix_w_in, m_gdn_conv_w, m_gdn_a_log, m_gdn_dt_bias, m_gdn_norm_w, m_gdn_w_o, m_cnv_pw1_b, m_cnv_dw_w, m_cnv_dw_b, m_cnv_ln_g, m_cnv_ln_b, m_cnv_w_o, m_cnv_b_o, m_mix_w_out, m_ffn2_norm_pre, m_ffn2_norm_post, m_ffn2_w_in, m_ffn2_w_out, v_ffn1_norm_pre, v_ffn1_norm_post, v_ffn1_w_in, v_ffn1_w_out, v_mix_norm_pre, v_mix_norm_post, v_mix_w_in, v_gdn_conv_w, v_gdn_a_log, v_gdn_dt_bias, v_gdn_norm_w, v_gdn_w_o, v_cnv_pw1_b, v_cnv_dw_w, v_cnv_dw_b, v_cnv_ln_g, v_cnv_ln_b, v_cnv_w_o, v_cnv_b_o, v_mix_w_out, v_ffn2_norm_pre, v_ffn2_norm_post, v_ffn2_w_in, v_ffn2_w_out):
    params = dict(zip(NAMES, (ffn1_norm_pre, ffn1_norm_post, ffn1_w_in, ffn1_w_out, mix_norm_pre, mix_norm_post, mix_w_in, gdn_conv_w, gdn_a_log, gdn_dt_bias, gdn_norm_w, gdn_w_o, cnv_pw1_b, cnv_dw_w, cnv_dw_b, cnv_ln_g, cnv_ln_b, cnv_w_o, cnv_b_o, mix_w_out, ffn2_norm_pre, ffn2_norm_post, ffn2_w_in, ffn2_w_out)))
    mom1 = dict(zip(NAMES, (m_ffn1_norm_pre, m_ffn1_norm_post, m_ffn1_w_in, m_ffn1_w_out, m_mix_norm_pre, m_mix_norm_post, m_mix_w_in, m_gdn_conv_w, m_gdn_a_log, m_gdn_dt_bias, m_gdn_norm_w, m_gdn_w_o, m_cnv_pw1_b, m_cnv_dw_w, m_cnv_dw_b, m_cnv_ln_g, m_cnv_ln_b, m_cnv_w_o, m_cnv_b_o, m_mix_w_out, m_ffn2_norm_pre, m_ffn2_norm_post, m_ffn2_w_in, m_ffn2_w_out)))
    mom2 = dict(zip(NAMES, (v_ffn1_norm_pre, v_ffn1_norm_post, v_ffn1_w_in, v_ffn1_w_out, v_mix_norm_pre, v_mix_norm_post, v_mix_w_in, v_gdn_conv_w, v_gdn_a_log, v_gdn_dt_bias, v_gdn_norm_w, v_gdn_w_o, v_cnv_pw1_b, v_cnv_dw_w, v_cnv_dw_b, v_cnv_ln_g, v_cnv_ln_b, v_cnv_w_o, v_cnv_b_o, v_mix_w_out, v_ffn2_norm_pre, v_ffn2_norm_post, v_ffn2_w_in, v_ffn2_w_out)))
    T, D = x.shape[1], x.shape[2]
    H = D // GDN_DK
    L = ffn1_norm_pre.shape[0]
    xi, yi, ci = _coords()
    dev = 4 * xi + 2 * yi + ci

    ag_names = BIG + SMALL_SHARDED

    def shard_to_send(n):
        if n in COL_SHARDED:
            return jnp.swapaxes(params[n], 1, 2).astype(BF16)
        return params[n].astype(BF16) if n in BIG else params[n]

    send = {n: shard_to_send(n) for n in ag_names}

    hosts, got, later = _Hosts(), {}, []
    for i in range(L):
        for host, names in (GATHER_HOSTS_FIRST if i == 0 else GATHER_HOSTS):
            def make(i=i, names=names):
                blocks = [send[n][i] for n in names]

                def deliver(outs):
                    got.update({(i, n): o for n, o in zip(names, outs)})
                return _GatherPlan(blocks), blocks, deliver
            if host is None:
                plan, blocks, deliver = make()
                deliver(_comm_only(plan, blocks, f"ag_weights_l{i}"))
            else:
                hosts.add(host.format(i=i, j=i - 1), make)

    cidx = jnp.reshape(ci, (1,)).astype(jnp.int32)
    grads, reduced = [{} for _ in range(L)], {}
    for i in range(L):
        for names, sib_host, chip_hosts in (REDUCE_HOSTS_FIRST if i == 0 else REDUCE_HOSTS):
            stage = {}

            def make_sib(i=i, names=names, stage=stage):
                Gs = []
                for n in names:
                    g = _mix_in_restore(grads[i][n], D, H) if n == "mix_w_in" else grads[i][n]
                    Gs.append(g.reshape(N_DEV, -1, g.shape[-1]))
                stage["G"] = dict(zip(names, Gs))

                def deliver(outs):
                    stage["R1"] = dict(zip(names, outs))
                return _SiblingPlan(Gs), Gs, deliver

            def make_chips(ns, i=i, stage=stage):
                Ps = [_pair_add(stage["G"][n], stage["R1"][n], cidx, f"l{i}_pair_add_{n}") for n in ns]

                def deliver(outs):
                    reduced.update({(i, n): o for n, o in zip(ns, outs)})
                return _ChipsPlan(Ps), Ps, deliver

            if sib_host is None:
                later.append((f"l{i}_{names[0]}", make_sib, [(lambda ns=ns, mc=make_chips: mc(ns)) for _, ns in chip_hosts]))
                continue
            hosts.add(sib_host.format(i=i, j=i - 1), make_sib)
            for host, ns in chip_hosts:
                hosts.add(host.format(i=i, j=i - 1), lambda ns=ns, mc=make_chips: mc(ns))

    weights = [_LayerWeights(got, params, i, D, H) for i in range(L)]
    loss_row, dx = _trunk_fwd_bwd(x[0], loss_target[0], H, L, lambda i: weights[i], grads, hosts)
    loss = lax.psum(loss_row[0, 0], ("x", "y", "c"))
    assert not hosts.waiting, sorted(hosts.waiting)

    for tag, make_sib, chip_makes in later:
        plan, Gs, deliver = make_sib()
        deliver(_comm_only(plan, Gs, f"rs_sibling_{tag}"))
        for mk in chip_makes:
            plan, Ps, deliver = mk()
            deliver(_comm_only(plan, Ps, f"rs_chips_{tag}"))
    R2s = {n: [jnp.swapaxes(reduced[(i, n)], 1, 2) if n in COL_SHARDED else reduced[(i, n)] for i in range(L)]
           for n in BIG}

    pieces = []
    for i in range(L):
        for n in SMALL:
            piece = grads[i][n].reshape(-1, LANES)
            pieces.append(jnp.pad(piece, ((0, (-piece.shape[0]) % 8), (0, 0))))
    packed = jnp.concatenate(pieces, axis=0)
    small_all = _comm_only(_GatherPlan([packed]), [packed], "ag_small_grads")[0]
    small_sum = _sum_parts(small_all, "sum_small_grads")
    small_g = {n: [None] * L for n in SMALL}
    off = 0
    for i in range(L):
        for n in SMALL:
            shape = grads[i][n].shape
            cnt = shape[0] * shape[1] // LANES
            g = small_sum[off:off + cnt].reshape(shape)
            off += cnt + (-cnt) % 8
            if n in ("gdn_a_log", "gdn_dt_bias"):
                g = g[:, :H]
            if n in SMALL_SHARDED:
                wloc = params[n].shape[-1]
                g = lax.dynamic_slice_in_dim(g, dev * wloc, wloc, axis=1)
            small_g[n][i] = g

    outs = {}
    for n in NAMES:
        w, m, v = params[n], mom1[n], mom2[n]
        if n in BIG:
            shape3, parts = w.shape, R2s[n]
        else:
            rows, cols = (w.shape[0] * w.shape[1], w.shape[2]) if w.ndim == 3 else w.shape
            shape3, parts = (1, rows, cols), [jnp.stack(small_g[n], axis=0).reshape(1, rows, cols)]
        res = _adamw(w.reshape(shape3), m.reshape(shape3), v.reshape(shape3), parts, "adamw_" + n)
        outs[n] = [r.reshape(w.shape) for r in res]

    result = [loss, dx[None]]
    for k in range(4):
        result += [outs[n][k] for n in NAMES]
    return tuple(result)
```

```python
import jax
import jax.numpy as jnp
from jax import lax
from jax.experimental import pallas as pl
from jax.experimental.pallas import tpu as pltpu

F32 = jnp.float32
BF16 = jnp.bfloat16

GDN_DK = 128
CHUNK = 64
GDN_CONV = 4
CNV_K = 31
RMS_EPS = 1e-6
LN_EPS = 1e-5
L2_EPS = 1e-6
ADAM_LR = 0.001
ADAM_B1 = 0.9
ADAM_B2 = 0.999
ADAM_EPS = 1e-08
ADAM_WD = 0.01
ADAM_STEP = 10

LANES = 128
SUB = 16
VMEM_LIMIT = 56 * 1024 * 1024
N_DEV = 8
MESH = pl.DeviceIdType.MESH


def _cparams(sem=None, **kw):
    if sem is not None:
        kw["dimension_semantics"] = sem
    return pltpu.CompilerParams(vmem_limit_bytes=VMEM_LIMIT, **kw)


def _tile(dim, target):
    best = None
    for t in range(LANES, min(dim, target) + 1, LANES):
        if dim % t == 0:
            best = t
    return best if best is not None else dim


def _sigmoid(x):
    return 1.0 / (1.0 + jnp.exp(-x))


def _silu(x):
    return x * _sigmoid(x)


def _dsilu(x):
    s = _sigmoid(x)
    return s * (1.0 + x * (1.0 - s))


MM_VMEM_BUDGET = 40 * 1024 * 1024
MM_MAX_TILE = 2048


def _mm_tiles(M, N, K, out_bytes):
    def cands(dim):
        c = [t for t in range(LANES, min(dim, MM_MAX_TILE) + 1, LANES) if dim % t == 0]
        return c or [dim]
    best = None
    for tm in cands(M):
        for tn in cands(N):
            vm = 2 * (2 * K * (tm + tn) + tm * tn * out_bytes)
            if vm <= MM_VMEM_BUDGET and (best is None or tm * tn > best[0] * best[1]):
                best = (tm, tn)
    return best if best is not None else (cands(M)[0], cands(N)[0])


def _matmul(a, b, mode, out_dtype, name, hosts=None, rows=None):
    if mode == "nn":
        (M, K), N = a.shape, b.shape[1]
    elif mode == "nt":
        (M, K), N = a.shape, b.shape[0]
    else:
        (K, M), N = a.shape, b.shape[1]
    first = 0
    if rows is not None:
        first, N = rows
    tm, tn = _mm_tiles(M, N, K, jnp.dtype(out_dtype).itemsize)
    assert first % tn == 0
    joff = first // tn
    if mode == "nn":
        a_spec = pl.BlockSpec((tm, K), lambda j, i: (i, 0))
        b_spec = pl.BlockSpec((K, tn), lambda j, i: (0, j))
        dn = (((1,), (0,)), ((), ()))
    elif mode == "nt":
        a_spec = pl.BlockSpec((tm, K), lambda j, i: (i, 0))
        b_spec = pl.BlockSpec((tn, K), lambda j, i: (j + joff, 0))
        dn = (((1,), (1,)), ((), ()))
    else:
        a_spec = pl.BlockSpec((K, tm), lambda j, i: (0, i))
        b_spec = pl.BlockSpec((K, tn), lambda j, i: (0, j))
        dn = (((0,), (0,)), ((), ()))

    def body(a_ref, b_ref, o_ref):
        o_ref[...] = lax.dot_general(a_ref[...], b_ref[...], dn, preferred_element_type=F32).astype(out_dtype)

    return _hosted_call(
        body, hosts, name=name, grid=(N // tn, M // tm), in_specs=[a_spec, b_spec],
        out_specs=[pl.BlockSpec((tm, tn), lambda j, i: (i, j))],
        out_shape=[jax.ShapeDtypeStruct((M, N), out_dtype)], scratch_shapes=[],
        args=(a, b), sem=("parallel", "parallel"))[0]


ROW_BLOCK = 512


def _rowcall(name, body, T, tb, row_ins, par_ins, row_outs, acc_outs, hosts=None, wide=None):
    tb = min(T, ROW_BLOCK)
    n_ri, n_pi, n_ro = len(row_ins), len(par_ins), len(row_outs)
    n_extra = 0 if (wide is None or wide[3] is None) else 1

    def kern(*refs):
        ri, pi = refs[:n_ri], refs[n_ri:n_ri + n_pi]
        refs = refs[n_ri + n_pi + n_extra:]
        ro, ao = refs[:n_ro], refs[n_ro:]
        if ao:
            @pl.when(pl.program_id(0) == 0)
            def _():
                for r in ao:
                    r[...] = jnp.zeros_like(r)
        body(ri, pi, ro, ao)

    in_specs = [pl.BlockSpec((tb, w), lambda i, cb=cb: (i, cb)) for (_, w, cb) in row_ins]
    in_specs += [pl.BlockSpec(p.shape, lambda i: (0, 0)) for p in par_ins]
    out_specs = [pl.BlockSpec((tb, w), lambda i: (i, 0)) for (w, _) in row_outs]
    out_specs += [pl.BlockSpec((1, w), lambda i: (0, 0)) for w in acc_outs]
    out_shape = [jax.ShapeDtypeStruct((T, w), dt) for (w, dt) in row_outs]
    out_shape += [jax.ShapeDtypeStruct((1, w), F32) for w in acc_outs]
    args, aliases = [*[a for (a, _, _) in row_ins], *par_ins], {}
    if wide is not None:
        k, total, cb, buf = wide
        w, dt = row_outs[k]
        out_specs[k] = pl.BlockSpec((tb, w), lambda i: (i, cb))
        out_shape[k] = jax.ShapeDtypeStruct((T, total), dt)
        if buf is not None:
            in_specs.append(pl.BlockSpec(memory_space=pl.ANY))
            args.append(buf)
            aliases = {len(args) - 1: k}
    return _hosted_call(
        kern, hosts, name=name, grid=(T // tb,), in_specs=in_specs, out_specs=out_specs, out_shape=out_shape,
        scratch_shapes=[], args=tuple(args), sem=("arbitrary",), aliases=aliases)


def _rsum(x):
    return jnp.sum(x, axis=0, keepdims=True)


def _rms_rstd(x):
    return lax.rsqrt(jnp.mean(x * x, axis=-1, keepdims=True) + RMS_EPS)


def _rms_fwd(x, w, name):
    T, D = x.shape

    def body(ri, pi, ro, ao):
        xv = ri[0][...]
        ro[0][...] = (xv * _rms_rstd(xv) * pi[0][...]).astype(BF16)

    return _rowcall(name, body, T, 256, [(x, D, 0)], [w], [(D, BF16)], [])[0]


def _rms_bwd_core(dy, x, w):
    rs = _rms_rstd(x)
    xh = x * rs
    gw = dy * w
    dx = rs * (gw - xh * jnp.mean(gw * xh, axis=-1, keepdims=True))
    return dx, dy * xh


def _pre_bwd(dh, x, w, dres, name, hosts=None):
    T, D = x.shape

    def body(ri, pi, ro, ao):
        dx, dwc = _rms_bwd_core(ri[0][...], ri[1][...], pi[0][...])
        ro[0][...] = ri[2][...] + dx
        ao[0][...] += _rsum(dwc)

    return _rowcall(name, body, T, 256, [(dh, D, 0), (x, D, 0), (dres, D, 0)], [w], [(D, F32)], [D], hosts=hosts)


def _post_fwd(x, f, w, r, name):
    T, D = x.shape

    def body(ri, pi, ro, ao):
        fv = ri[1][...]
        ro[0][...] = ri[0][...] + r * (fv * _rms_rstd(fv) * pi[0][...])

    return _rowcall(name, body, T, 256, [(x, D, 0), (f, D, 0)], [w], [(D, F32)], [])[0]


def _post_bwd(dxn, f, w, r, name):
    T, D = f.shape

    def body(ri, pi, ro, ao):
        df, dwc = _rms_bwd_core(r * ri[0][...], ri[1][...], pi[0][...])
        ro[0][...] = df.astype(BF16)
        ao[0][...] += _rsum(dwc)

    return _rowcall(name, body, T, 256, [(dxn, D, 0), (f, D, 0)], [w], [(D, BF16)], [D])


def _swiglu_fwd(a, name):
    T, F2 = a.shape
    F = F2 // 2

    def body(ri, pi, ro, ao):
        ro[0][...] = (_silu(ri[0][...].astype(F32)) * ri[1][...].astype(F32)).astype(BF16)

    return _rowcall(name, body, T, 256, [(a, F, 0), (a, F, 1)], [], [(F, BF16)], [])[0]


def _swiglu_bwd(ds, a, name, hosts=None):
    T, F2 = a.shape
    F = F2 // 2

    def body(ri, pi, ro, ao):
        dsv, g, u = ri[0][...].astype(F32), ri[1][...].astype(F32), ri[2][...].astype(F32)
        ro[0][:, :F] = (dsv * u * _dsilu(g)).astype(BF16)
        ro[0][:, F:] = (dsv * _silu(g)).astype(BF16)

    return _rowcall(name, body, T, 256, [(ds, F, 0), (a, F, 0), (a, F, 1)], [], [(F2, BF16)], [], hosts=hosts)[0]


def _loss_fwd_bwd(y, tgt, name):
    T, D = y.shape

    def body(ri, pi, ro, ao):
        e = ri[0][...] - ri[1][...]
        ro[0][...] = e * (1.0 / D)
        tot = jnp.sum(_rsum(e * e), axis=1, keepdims=True) * (0.5 / D)
        ao[0][...] += jnp.broadcast_to(tot, (1, LANES))

    return _rowcall(name, body, T, 256, [(y, D, 0), (tgt, D, 0)], [], [(D, F32)], [LANES])


def _gdn_gate_fwd(o, p, nw, name):
    T, D = o.shape
    H = D // GDN_DK

    def body(ri, pi, ro, ao):
        for h in range(H):
            sl = slice(h * GDN_DK, (h + 1) * GDN_DK)
            oh = ri[0][:, sl]
            ro[0][:, sl] = (oh * _rms_rstd(oh) * pi[0][...] * _silu(ri[1][:, sl].astype(F32))).astype(BF16)

    return _rowcall(name, body, T, 256, [(o, D, 0), (p, D, 3)], [nw], [(D, BF16)], [])[0]


def _gdn_gate_bwd(dog, o, p, nw, name, hosts=None, dp=None):
    T, D = o.shape
    H = D // GDN_DK

    def body(ri, pi, ro, ao):
        acc = jnp.zeros((1, GDN_DK), F32)
        for h in range(H):
            sl = slice(h * GDN_DK, (h + 1) * GDN_DK)
            dy, oh, z = ri[0][:, sl], ri[1][:, sl], ri[2][:, sl].astype(F32)
            sz = _silu(z)
            do, dwc = _rms_bwd_core(dy * sz, oh, pi[0][...])
            ro[0][:, sl] = do
            ro[1][:, sl] = (dy * oh * _rms_rstd(oh) * pi[0][...] * _dsilu(z)).astype(BF16)
            acc = acc + _rsum(dwc)
        ao[0][...] += acc

    return _rowcall(name, body, T, 256, [(dog, D, 0), (o, D, 0), (p, D, 3)], [nw], [(D, F32), (D, BF16)], [GDN_DK],
                    hosts=hosts, wide=None if dp is None else (1, dp.shape[1], 3, dp))


def _glu_fwd(p, b, name):
    T = p.shape[0]
    D = b.shape[1] // 2

    def body(ri, pi, ro, ao):
        ro[0][...] = (ri[0][...].astype(F32) + pi[0][:, :D]) * _sigmoid(ri[1][...].astype(F32) + pi[0][:, D:])

    return _rowcall(name, body, T, 256, [(p, D, 4), (p, D, 5)], [b], [(D, F32)], [])[0]


def _glu_bwd(dhc, p, b, name, dp=None):
    T = p.shape[0]
    D = b.shape[1] // 2

    def body(ri, pi, ro, ao):
        d, a, g = ri[0][...], ri[1][...].astype(F32) + pi[0][:, :D], ri[2][...].astype(F32) + pi[0][:, D:]
        sg = _sigmoid(g)
        da, dg = d * sg, d * a * sg * (1.0 - sg)
        ro[0][:, :D] = da.astype(BF16)
        ro[0][:, D:] = dg.astype(BF16)
        ao[0][:, :D] += _rsum(da)
        ao[0][:, D:] += _rsum(dg)

    return _rowcall(name, body, T, 256, [(dhc, D, 0), (p, D, 4), (p, D, 5)], [b], [(2 * D, BF16)], [2 * D],
                    wide=None if dp is None else (0, dp.shape[1], 2, dp))


def _ln_stats(x):
    mu = jnp.mean(x, axis=-1, keepdims=True)
    xc = x - mu
    rstd = lax.rsqrt(jnp.mean(xc * xc, axis=-1, keepdims=True) + LN_EPS)
    return xc * rstd, rstd


def _ln_silu_fwd(hcv, g, b, name):
    T, D = hcv.shape

    def body(ri, pi, ro, ao):
        xh, _ = _ln_stats(ri[0][...])
        ro[0][...] = _silu(xh * pi[0][...] + pi[1][...]).astype(BF16)

    return _rowcall(name, body, T, 256, [(hcv, D, 0)], [g, b], [(D, BF16)], [])[0]


def _ln_silu_bwd(dhl, hcv, g, b, name, hosts=None):
    T, D = hcv.shape

    def body(ri, pi, ro, ao):
        xh, rstd = _ln_stats(ri[1][...])
        dyl = ri[0][...] * _dsilu(xh * pi[0][...] + pi[1][...])
        dxh = dyl * pi[0][...]
        dx = rstd * (dxh - jnp.mean(dxh, axis=-1, keepdims=True) - xh * jnp.mean(dxh * xh, axis=-1, keepdims=True))
        ro[0][...] = dx
        ao[0][...] += _rsum(dyl * xh)
        ao[1][...] += _rsum(dyl)
        ao[2][...] += _rsum(dx)

    return _rowcall(name, body, T, 256, [(dhl, D, 0), (hcv, D, 0)], [g, b], [(D, F32)], [D, D, D], hosts=hosts)


def _merge_fwd(p, ya, yb, bo, name):
    T, D = ya.shape

    def body(ri, pi, ro, ao):
        ga, gb = _sigmoid(ri[0][...].astype(F32)), _sigmoid(ri[1][...].astype(F32))
        ro[0][...] = (ga * ri[2][...] + gb * (ri[3][...] + pi[0][...])).astype(BF16)

    return _rowcall(name, body, T, 256, [(p, D, 6), (p, D, 7), (ya, D, 0), (yb, D, 0)], [bo], [(D, BF16)], [])[0]


def _merge_bwd(dym, p, ya, yb, bo, name, hosts=None, dp_width=None):
    T, D = ya.shape

    def body(ri, pi, ro, ao):
        d = ri[0][...]
        ga, gb = _sigmoid(ri[1][...].astype(F32)), _sigmoid(ri[2][...].astype(F32))
        ybv = ri[4][...] + pi[0][...]
        dyb = d * gb
        ro[0][...] = (d * ga).astype(BF16)
        ro[1][...] = dyb.astype(BF16)
        ro[2][:, :D] = (d * ri[3][...] * ga * (1.0 - ga)).astype(BF16)
        ro[2][:, D:] = (d * ybv * gb * (1.0 - gb)).astype(BF16)
        ao[0][...] += _rsum(dyb)

    return _rowcall(name, body, T, 256, [(dym, D, 0), (p, D, 6), (p, D, 7), (ya, D, 0), (yb, D, 0)], [bo],
                    [(D, BF16), (D, BF16), (2 * D, BF16)], [D], hosts=hosts,
                    wide=None if dp_width is None else (2, dp_width, 3, None))


PAD = 32
RC = 256


def _tap_windows(ref, offs):
    groups = {}
    for j, o in enumerate(offs):
        groups.setdefault(o % 8, []).append((j, o))
    for grp in groups.values():
        lo, hi = min(o for _, o in grp), max(o for _, o in grp)
        win = ref[pl.ds(lo, RC + hi - lo), :]
        for j, o in grp:
            yield j, win[o - lo:o - lo + RC]


def _causal_taps(xp_ref, w, K, c0):
    acc = None
    for j, xs in _tap_windows(xp_ref, [PAD - (K - 1) + j + c0 for j in range(K)]):
        term = w[j:j + 1, :] * xs
        acc = term if acc is None else acc + term
    return acc


def _anticausal_taps(dp_ref, w, K, c0):
    acc = None
    for j, ds in _tap_windows(dp_ref, [(K - 1) - j + c0 for j in range(K)]):
        term = w[j:j + 1, :] * ds
        acc = term if acc is None else acc + term
    return acc


def _tap_grads(dw_ref, dc_ref, xp_ref, K, T):
    accs = [jnp.zeros((8, LANES), F32) for _ in range(K)]
    for c in range(T // RC):
        d = dc_ref[pl.ds(c * RC, RC), :]
        for j, xs in _tap_windows(xp_ref, [PAD - (K - 1) + j + c * RC for j in range(K)]):
            accs[j] = accs[j] + jnp.sum((d * xs).reshape(RC // 8, 8, LANES), axis=0)
    for j in range(K):
        dw_ref[j:j + 1, :] = _rsum(accs[j])


def _qkv_conv_fwd(p, cw, H, name, hosts=None):
    T = p.shape[0]
    K = cw.shape[0]

    def body(x_ref, w_ref, o_ref, xp_ref):
        j = pl.program_id(0)
        xp_ref[pl.ds(0, PAD), :] = jnp.zeros((PAD, LANES), F32)
        xp_ref[pl.ds(PAD, T), :] = x_ref[...].astype(F32)
        w = w_ref[...]
        scale = jnp.where(j < H, GDN_DK ** -0.5, 1.0).astype(F32)
        for c in range(T // RC):
            act = _silu(_causal_taps(xp_ref, w, K, c * RC))
            nrm = act * lax.rsqrt(jnp.sum(act * act, axis=-1, keepdims=True) + L2_EPS) * scale
            o_ref[pl.ds(c * RC, RC), :] = jnp.where(j < 2 * H, nrm, act)

    return _hosted_call(
        body, hosts, name=name, grid=(3 * H,),
        in_specs=[pl.BlockSpec((T, LANES), lambda j: (0, j)), pl.BlockSpec((K, LANES), lambda j: (0, j))],
        out_specs=[pl.BlockSpec((T, LANES), lambda j: (0, j))],
        out_shape=[jax.ShapeDtypeStruct((T, 3 * H * GDN_DK), F32)],
        scratch_shapes=[pltpu.VMEM((T + PAD, LANES), F32)], args=(p, cw), sem=("arbitrary",))[0]


def _qkv_conv_bwd(dn, p, cw, H, name, hosts=None, dp=None):
    T = p.shape[0]
    K = cw.shape[0]

    def body(dn_ref, x_ref, w_ref, _, dx_ref, dw_ref, xp_ref, dc_ref):
        j = pl.program_id(0)
        xp_ref[pl.ds(0, PAD), :] = jnp.zeros((PAD, LANES), F32)
        xp_ref[pl.ds(PAD, T), :] = x_ref[...].astype(F32)
        dc_ref[pl.ds(T, PAD), :] = jnp.zeros((PAD, LANES), F32)
        w = w_ref[...]
        scale = jnp.where(j < H, GDN_DK ** -0.5, 1.0).astype(F32)
        for c in range(T // RC):
            pre = _causal_taps(xp_ref, w, K, c * RC)
            act = _silu(pre)
            d = dn_ref[pl.ds(c * RC, RC), :]
            rs = lax.rsqrt(jnp.sum(act * act, axis=-1, keepdims=True) + L2_EPS)
            nh = act * rs
            dact_n = scale * rs * (d - nh * jnp.sum(d * nh, axis=-1, keepdims=True))
            dact = jnp.where(j < 2 * H, dact_n, d)
            dc_ref[pl.ds(c * RC, RC), :] = dact * _dsilu(pre)
        for c in range(T // RC):
            dx_ref[pl.ds(c * RC, RC), :] = _anticausal_taps(dc_ref, w, K, c * RC).astype(BF16)
        _tap_grads(dw_ref, dc_ref, xp_ref, K, T)

    return _hosted_call(
        body, hosts, name=name, grid=(3 * H,),
        in_specs=[pl.BlockSpec((T, LANES), lambda j: (0, j)), pl.BlockSpec((T, LANES), lambda j: (0, j)),
                  pl.BlockSpec((K, LANES), lambda j: (0, j)), pl.BlockSpec(memory_space=pl.ANY)],
        out_specs=[pl.BlockSpec((T, LANES), lambda j: (0, j)), pl.BlockSpec((K, LANES), lambda j: (0, j))],
        out_shape=[jax.ShapeDtypeStruct(dp.shape, BF16), jax.ShapeDtypeStruct(cw.shape, F32)],
        scratch_shapes=[pltpu.VMEM((T + PAD, LANES), F32), pltpu.VMEM((T + PAD, LANES), F32)],
        args=(dn, p, cw, dp), sem=("arbitrary",), aliases={3: 0})


def _dw_conv_fwd(hc, w, b, name):
    T, D = hc.shape
    K = w.shape[0]

    def body(x_ref, w_ref, b_ref, o_ref, xp_ref):
        xp_ref[pl.ds(0, PAD), :] = jnp.zeros((PAD, LANES), F32)
        xp_ref[pl.ds(PAD, T), :] = x_ref[...]
        wv = w_ref[...]
        for c in range(T // RC):
            o_ref[pl.ds(c * RC, RC), :] = _causal_taps(xp_ref, wv, K, c * RC) + b_ref[...]

    return pl.pallas_call(
        body, name=name, grid=(D // LANES,),
        in_specs=[pl.BlockSpec((T, LANES), lambda j: (0, j)), pl.BlockSpec((K, LANES), lambda j: (0, j)),
                  pl.BlockSpec((1, LANES), lambda j: (0, j))],
        out_specs=pl.BlockSpec((T, LANES), lambda j: (0, j)),
        out_shape=jax.ShapeDtypeStruct((T, D), F32),
        scratch_shapes=[pltpu.VMEM((T + PAD, LANES), F32)],
        compiler_params=_cparams(("arbitrary",)),
    )(hc, w, b)


def _dw_conv_bwd(dy, hc, w, name, hosts=None):
    T, D = hc.shape
    K = w.shape[0]

    def body(dy_ref, x_ref, w_ref, dx_ref, dw_ref, xp_ref, dc_ref):
        xp_ref[pl.ds(0, PAD), :] = jnp.zeros((PAD, LANES), F32)
        xp_ref[pl.ds(PAD, T), :] = x_ref[...]
        dc_ref[pl.ds(T, PAD), :] = jnp.zeros((PAD, LANES), F32)
        dc_ref[pl.ds(0, T), :] = dy_ref[...]
        wv = w_ref[...]
        for c in range(T // RC):
            dx_ref[pl.ds(c * RC, RC), :] = _anticausal_taps(dc_ref, wv, K, c * RC)
        _tap_grads(dw_ref, dc_ref, xp_ref, K, T)

    return _hosted_call(
        body, hosts, name=name, grid=(D // LANES,), sem=("arbitrary",),
        in_specs=[pl.BlockSpec((T, LANES), lambda j: (0, j)), pl.BlockSpec((T, LANES), lambda j: (0, j)),
                  pl.BlockSpec((K, LANES), lambda j: (0, j))],
        out_specs=[pl.BlockSpec((T, LANES), lambda j: (0, j)), pl.BlockSpec((K, LANES), lambda j: (0, j))],
        out_shape=[jax.ShapeDtypeStruct((T, D), F32), jax.ShapeDtypeStruct(w.shape, F32)],
        scratch_shapes=[pltpu.VMEM((T + PAD, LANES), F32), pltpu.VMEM((T + PAD, LANES), F32)],
        args=(dy, hc, w))


NN = (((1,), (0,)), ((), ()))
NT = (((1,), (1,)), ((), ()))
TN = (((0,), (0,)), ((), ()))


def _dotb(a, b, dn=NN):
    return lax.dot_general(a.astype(BF16), b.astype(BF16), dn, preferred_element_type=F32)


def _split_bf16(x, n):
    parts, r = [], x
    for _ in range(n):
        p = r.astype(BF16)
        parts.append(p)
        r = r - p.astype(F32)
    return parts


def _dot_sel(sel, x, pieces, sel_left=True):
    sb = sel.astype(BF16)
    acc = None
    for p in _split_bf16(x, pieces):
        t = (lax.dot_general(sb, p, NN, preferred_element_type=F32) if sel_left
             else lax.dot_general(p, sb, NN, preferred_element_type=F32))
        acc = t if acc is None else acc + t
    return acc


def _iota2(shape, axis):
    return lax.broadcasted_iota(jnp.int32, shape, axis)


def _to_row(col, eye):
    return jnp.sum(jnp.where(eye, col, 0.0), axis=0, keepdims=True)


def _to_col(row, eye):
    return jnp.sum(jnp.where(eye, row, 0.0), axis=1, keepdims=True)


def _gdn_gates(bl, al, alog, dtb):
    beta = _sigmoid(bl)
    x = al + dtb
    sp = jnp.maximum(x, 0.0) + jnp.log(1.0 + jnp.exp(-jnp.abs(x)))
    g = -jnp.exp(alog) * sp
    r, c = _iota2((CHUNK, CHUNK), 0), _iota2((CHUNK, CHUNK), 1)
    G = _dot_sel(r >= c, g, 3)
    return beta, g, G, x


def _unit_lower_inverses(As, Ats):
    n = len(As)
    nb = CHUNK // SUB
    lane = _iota2((SUB, CHUNK), 1)
    row = _iota2((SUB, CHUNK), 0)
    Atp = []
    for At in Ats:
        acc = jnp.zeros((SUB, CHUNK), F32)
        for b in range(nb):
            acc = jnp.where(lane // SUB == b, At[b * SUB:(b + 1) * SUB, :], acc)
        Atp.append(acc)
    gr, gc = _iota2((CHUNK, CHUNK), 0), _iota2((CHUNK, CHUNK), 1)
    ones_bd = gr // SUB == gc // SUB
    stack = jnp.concatenate(
        [jnp.where(lane % SUB == i, Atp[m], 0.0) for i in range(1, SUB) for m in range(n)], axis=0)
    Cm = _dot_sel(ones_bd, stack, 2, sel_left=False)
    Z = [(row == lane % SUB).astype(F32) for _ in range(n)]
    for i in range(1, SUB):
        for m in range(n):
            cm = Cm[((i - 1) * n + m) * SUB:((i - 1) * n + m + 1) * SUB, :]
            new = -jnp.sum(cm * Z[m], axis=0, keepdims=True)
            Z[m] = Z[m] + jnp.where(row == i, new, 0.0)
    bd = gr // SUB == gc // SUB
    Xs = [jnp.where(bd, jnp.concatenate([Z[m]] * nb, axis=0), 0.0) for m in range(n)]
    blk = SUB
    while blk < CHUNK:
        off = (gr // (2 * blk) == gc // (2 * blk)) & (gr // blk != gc // blk)
        Ys = [_dotb(Xs[m], jnp.where(off, As[m], 0.0)) for m in range(n)]
        Xs = [Xs[m] - _dotb(Ys[m], Xs[m]) for m in range(n)]
        blk *= 2
    return Xs


def _gdn_fwd(qkvn, p, alog, dtb, H, name, hosts=None):
    T = qkvn.shape[0]
    D = H * GDN_DK
    N = T // CHUNK
    bblk = 0

    def body(q_ref, k_ref, v_ref, b_ref, a_ref, alog_ref, dtb_ref, o_ref, t_ref, s_ref, S_scr):
        @pl.when(pl.program_id(0) == 0)
        def _():
            S_scr[...] = jnp.zeros_like(S_scr)

        beta, _, G, _ = _gdn_gates(b_ref[...], a_ref[...], alog_ref[...], dtb_ref[...])
        hs = range(H)
        sl = [slice(h * GDN_DK, (h + 1) * GDN_DK) for h in hs]
        q, k, v = [q_ref[:, s] for s in sl], [k_ref[:, s] for s in sl], [v_ref[:, s] for s in sl]
        Gc, bc = [G[:, h:h + 1] for h in hs], [beta[:, h:h + 1] for h in hs]
        r, c = _iota2((CHUNK, CHUNK), 0), _iota2((CHUNK, CHUNK), 1)
        eye, low, up = r == c, r >= c, r <= c
        Gr, br = [_to_row(Gc[h], eye) for h in hs], [_to_row(bc[h], eye) for h in hs]
        Dm = [jnp.where(low, jnp.exp(jnp.where(low, Gc[h] - Gr[h], 0.0)), 0.0) for h in hs]
        Dt = [jnp.where(up, jnp.exp(jnp.where(up, Gr[h] - Gc[h], 0.0)), 0.0) for h in hs]
        qk = [_dotb(jnp.concatenate([q[h], k[h]], axis=0), k[h], NT) for h in hs]
        QK = [qk[h][:CHUNK] * Dm[h] for h in hs]
        KK = [qk[h][CHUNK:] for h in hs]
        As = [jnp.where(r > c, KK[h] * Dm[h], 0.0) * bc[h] for h in hs]
        Ats = [jnp.where(r < c, KK[h] * Dt[h], 0.0) * br[h] for h in hs]
        Ts = _unit_lower_inverses(As, Ats)
        eG = [jnp.exp(Gc[h]) for h in hs]
        Gl = [Gc[h][CHUNK - 1:CHUNK, :] for h in hs]
        uw = [_dotb(Ts[h], jnp.concatenate([v[h] * bc[h], k[h] * (bc[h] * eG[h])], axis=1)) for h in hs]
        S = [S_scr[h] for h in hs]
        qw = [_dotb(jnp.concatenate([q[h] * eG[h], uw[h][:, GDN_DK:]], axis=0), S[h]) for h in hs]
        vn = [uw[h][:, :GDN_DK] - qw[h][CHUNK:] for h in hs]
        o = [qw[h][:CHUNK] + _dotb(QK[h], vn[h]) for h in hs]
        Sn = [S[h] * jnp.exp(Gl[h]) + _dotb(k[h] * jnp.exp(Gl[h] - Gc[h]), vn[h], TN) for h in hs]
        for h in hs:
            t_ref[0, h] = Ts[h]
            s_ref[0, h] = S[h]
            o_ref[:, sl[h]] = o[h]
            S_scr[h] = Sn[h]

    qkv_spec = [pl.BlockSpec((CHUNK, D), lambda n, cb=cb: (n, cb)) for cb in range(3)]
    return _hosted_call(
        body, hosts, name=name, grid=(N,), sem=("arbitrary",),
        in_specs=qkv_spec + [pl.BlockSpec((CHUNK, LANES), lambda n: (n, bblk)),
                             pl.BlockSpec((CHUNK, LANES), lambda n: (n, bblk + 1)),
                             pl.BlockSpec((1, LANES), lambda n: (0, 0)), pl.BlockSpec((1, LANES), lambda n: (0, 0))],
        out_specs=[pl.BlockSpec((CHUNK, D), lambda n: (n, 0)),
                   pl.BlockSpec((1, H, CHUNK, CHUNK), lambda n: (n, 0, 0, 0)),
                   pl.BlockSpec((1, H, GDN_DK, GDN_DK), lambda n: (n, 0, 0, 0))],
        out_shape=[jax.ShapeDtypeStruct((T, D), F32), jax.ShapeDtypeStruct((N, H, CHUNK, CHUNK), F32),
                   jax.ShapeDtypeStruct((N, H, GDN_DK, GDN_DK), F32)],
        scratch_shapes=[pltpu.VMEM((H, GDN_DK, GDN_DK), F32)],
        args=(qkvn, qkvn, qkvn, p, p, alog, dtb))


def _gdn_bwd(do, qkvn, p, alog, dtb, Tinv, Sin, H, name, hosts=None, dp=None):
    T = qkvn.shape[0]
    D = H * GDN_DK
    N = T // CHUNK
    bblk = 0

    def body(do_ref, q_ref, k_ref, v_ref, b_ref, a_ref, alog_ref, dtb_ref, t_ref, s_ref, _,
             dqkv_ref, dba_ref, dalog_ref, ddtb_ref, dS_scr):
        @pl.when(pl.program_id(0) == 0)
        def _():
            dS_scr[...] = jnp.zeros_like(dS_scr)
            dalog_ref[...] = jnp.zeros_like(dalog_ref)
            ddtb_ref[...] = jnp.zeros_like(ddtb_ref)

        beta, g, G, x = _gdn_gates(b_ref[...], a_ref[...], alog_ref[...], dtb_ref[...])
        r, c = _iota2((CHUNK, CHUNK), 0), _iota2((CHUNK, CHUNK), 1)
        eye, low, strict = r == c, r >= c, r > c
        lane = _iota2((CHUNK, LANES), 1)
        rsum1 = lambda a: jnp.sum(a, axis=1, keepdims=True)
        hs = range(H)
        sl = [slice(h * GDN_DK, (h + 1) * GDN_DK) for h in hs]
        q, k, v = [q_ref[:, s] for s in sl], [k_ref[:, s] for s in sl], [v_ref[:, s] for s in sl]
        dov = [do_ref[:, s] for s in sl]
        Gc, bc = [G[:, h:h + 1] for h in hs], [beta[:, h:h + 1] for h in hs]
        Tm, S, dSo = [t_ref[0, h] for h in hs], [s_ref[0, h] for h in hs], [dS_scr[h] for h in hs]
        Gr = [_to_row(Gc[h], eye) for h in hs]
        Dm = [jnp.where(low, jnp.exp(jnp.where(low, Gc[h] - Gr[h], 0.0)), 0.0) for h in hs]
        eG = [jnp.exp(Gc[h]) for h in hs]
        Gl = [Gc[h][CHUNK - 1:CHUNK, :] for h in hs]
        eR, dch = [jnp.exp(Gl[h] - Gc[h]) for h in hs], [jnp.exp(Gl[h]) for h in hs]
        qk = [_dotb(jnp.concatenate([q[h], k[h]], axis=0), k[h], NT) for h in hs]
        QKr, KK = [qk[h][:CHUNK] for h in hs], [qk[h][CHUNK:] for h in hs]
        QK = [QKr[h] * Dm[h] for h in hs]
        M = [jnp.where(strict, KK[h] * Dm[h], 0.0) for h in hs]
        uw = [_dotb(Tm[h], jnp.concatenate([v[h] * bc[h], k[h] * (bc[h] * eG[h])], axis=1)) for h in hs]
        u, w = [uw[h][:, :GDN_DK] for h in hs], [uw[h][:, GDN_DK:] for h in hs]
        qd, kd = [q[h] * eG[h] for h in hs], [k[h] * eR[h] for h in hs]
        vn = [u[h] - _dotb(w[h], S[h]) for h in hs]
        dvn = [_dotb(QK[h], dov[h], TN) + _dotb(kd[h], dSo[h]) for h in hs]
        dQK = [jnp.where(low, _dotb(dov[h], vn[h], NT), 0.0) for h in hs]
        dkd = [_dotb(vn[h], dSo[h], NT) for h in hs]
        ddch = [jnp.sum(rsum1(dSo[h] * S[h]), axis=0, keepdims=True) for h in hs]
        dd = [jnp.concatenate([dov[h], dvn[h]], axis=0) for h in hs]
        xs = [_dotb(dd[h], S[h], NT) for h in hs]
        dqd, dw = [xs[h][:CHUNK] for h in hs], [-xs[h][CHUNK:] for h in hs]
        dS = [_dotb(jnp.concatenate([qd[h], -w[h]], axis=0), dd[h], TN) + dch[h] * dSo[h] for h in hs]
        yb = [_dotb(Tm[h], jnp.concatenate([dvn[h], dw[h]], axis=1), TN) for h in hs]
        dvb, dkb = [yb[h][:, :GDN_DK] for h in hs], [yb[h][:, GDN_DK:] for h in hs]
        dA = [-jnp.where(strict, _dotb(yb[h], uw[h], NT), 0.0) for h in hs]
        rk = [rsum1(dkb[h] * k[h]) for h in hs]
        dbeta = [rsum1(dvb[h] * v[h]) + rk[h] * eG[h] + rsum1(dA[h] * M[h]) for h in hs]
        dM = [dA[h] * bc[h] for h in hs]
        dKK = [dM[h] * Dm[h] for h in hs]
        dQKr = [dQK[h] * Dm[h] for h in hs]
        E = [dM[h] * M[h] + dQK[h] * QK[h] for h in hs]
        zk = [_dotb(jnp.concatenate([dQKr[h], dKK[h]], axis=0), k[h]) for h in hs]
        dq = [zk[h][:CHUNK] + dqd[h] * eG[h] for h in hs]
        dk = [dkb[h] * (bc[h] * eG[h]) + zk[h][CHUNK:] + _dotb(dKK[h], k[h], TN) + _dotb(dQKr[h], q[h], TN)
              + dkd[h] * eR[h] for h in hs]
        deG = [rk[h] * bc[h] + rsum1(dqd[h] * q[h]) for h in hs]
        deR = [rsum1(dkd[h] * k[h]) for h in hs]
        dGl = [jnp.sum(deR[h] * eR[h], axis=0, keepdims=True) + ddch[h] * dch[h] for h in hs]
        dGc = [rsum1(E[h]) - _to_col(jnp.sum(E[h], axis=0, keepdims=True), eye) + deG[h] * eG[h] - deR[h] * eR[h]
               + jnp.where(r[:, :1] == CHUNK - 1, dGl[h], 0.0) for h in hs]
        dG_all = jnp.zeros((CHUNK, LANES), F32)
        dbeta_all = jnp.zeros((CHUNK, LANES), F32)
        for h in hs:
            dS_scr[h] = dS[h]
            dqkv_ref[:, sl[h]] = dq[h]
            dqkv_ref[:, D + h * GDN_DK:D + (h + 1) * GDN_DK] = dk[h]
            dqkv_ref[:, 2 * D + h * GDN_DK:2 * D + (h + 1) * GDN_DK] = dvb[h] * bc[h]
            dG_all = jnp.where(lane == h, dGc[h], dG_all)
            dbeta_all = jnp.where(lane == h, dbeta[h], dbeta_all)
        dg = _dot_sel(r <= c, dG_all, 3)
        da = dg * (-jnp.exp(alog_ref[...])) * _sigmoid(x)
        dba_ref[:, :LANES] = (dbeta_all * beta * (1.0 - beta)).astype(BF16)
        dba_ref[:, LANES:] = da.astype(BF16)
        dalog_ref[...] += _rsum(dg * g)
        ddtb_ref[...] += _rsum(da)

    rev = lambda n: N - 1 - n
    qkv_spec = [pl.BlockSpec((CHUNK, D), lambda n, cb=cb: (rev(n), cb)) for cb in range(3)]
    return _hosted_call(
        body, hosts, name=name, grid=(N,), sem=("arbitrary",),
        in_specs=[pl.BlockSpec((CHUNK, D), lambda n: (rev(n), 0))] + qkv_spec + [
            pl.BlockSpec((CHUNK, LANES), lambda n: (rev(n), bblk)),
            pl.BlockSpec((CHUNK, LANES), lambda n: (rev(n), bblk + 1)),
            pl.BlockSpec((1, LANES), lambda n: (0, 0)), pl.BlockSpec((1, LANES), lambda n: (0, 0)),
            pl.BlockSpec((1, H, CHUNK, CHUNK), lambda n: (rev(n), 0, 0, 0)),
            pl.BlockSpec((1, H, GDN_DK, GDN_DK), lambda n: (rev(n), 0, 0, 0)),
            pl.BlockSpec(memory_space=pl.ANY)],
        out_specs=[pl.BlockSpec((CHUNK, 3 * D), lambda n: (rev(n), 0)),
                   pl.BlockSpec((CHUNK, 2 * LANES), lambda n: (rev(n), 8 * D // (2 * LANES))),
                   pl.BlockSpec((1, LANES), lambda n: (0, 0)), pl.BlockSpec((1, LANES), lambda n: (0, 0))],
        out_shape=[jax.ShapeDtypeStruct((T, 3 * D), F32), jax.ShapeDtypeStruct(dp.shape, BF16),
                   jax.ShapeDtypeStruct((1, LANES), F32), jax.ShapeDtypeStruct((1, LANES), F32)],
        scratch_shapes=[pltpu.VMEM((H, GDN_DK, GDN_DK), F32)],
        args=(do, qkvn, qkvn, qkvn, p, p, alog, dtb, Tinv, Sin, dp), aliases={10: 1})


def _mix_in_reorder(wt, D, H):
    o1 = 4 * D
    o2, o3 = o1 + H, o1 + 2 * H
    z = jnp.zeros((LANES - H, wt.shape[1]), wt.dtype)
    return jnp.concatenate([wt[:o1], wt[o3:], wt[o1:o2], z, wt[o2:o3], z], axis=0)


def _mix_in_restore(dwt, D, H):
    b0 = 8 * D
    return jnp.concatenate([dwt[:4 * D], dwt[b0:b0 + H], dwt[b0 + LANES:b0 + LANES + H], dwt[4 * D:b0]], axis=0)


def _ffn_fwd(x, W, pre, tag, hosts=None):
    h = _rms_fwd(x, W[pre + "_norm_pre"], tag + "_pre")
    a = _matmul(h, W[pre + "_w_in"], "nt", BF16, tag + "_in", hosts)
    s = _swiglu_fwd(a, tag + "_act")
    f = _matmul(s, W[pre + "_w_out"], "nn", F32, tag + "_out", hosts)
    return _post_fwd(x, f, W[pre + "_norm_post"], 0.5, tag + "_post"), (x, h, a, s, f)


def _ffn_bwd(dxn, saved, W, pre, tag, g, hosts=None):
    x, h, a, s, f = saved
    df, g[pre + "_norm_post"] = _post_bwd(dxn, f, W[pre + "_norm_post"], 0.5, tag + "_dpost")
    ds = _matmul(df, W[pre + "_w_out"], "nt", BF16, tag + "_ds", hosts)
    g[pre + "_w_out"] = _matmul(s, df, "tn", BF16, tag + "_dwout")
    da = _swiglu_bwd(ds, a, tag + "_dact", hosts)
    g[pre + "_w_in"] = _matmul(da, h, "tn", BF16, tag + "_dwin", hosts)
    dh = _matmul(da, W[pre + "_w_in"], "nn", F32, tag + "_dh", hosts)
    dx, g[pre + "_norm_pre"] = _pre_bwd(dh, x, W[pre + "_norm_pre"], dxn, tag + "_dpre", hosts)
    return dx


def _mix_fwd(x, W, H, tag, hosts=None):
    h = _rms_fwd(x, W["mix_norm_pre"], tag + "_pre")
    D = x.shape[1]
    p = _matmul(h, W["mix_w_in"], "nt", BF16, tag + "_in", hosts, rows=(0, 8 * D))
    pba = _matmul(h, W["mix_w_in"], "nt", F32, tag + "_inba", rows=(8 * D, 2 * LANES))
    qkvn = _qkv_conv_fwd(p, W["gdn_conv_w"], H, tag + "_qkvconv", hosts)
    o, Tinv, Sin = _gdn_fwd(qkvn, pba, W["gdn_a_log"], W["gdn_dt_bias"], H, tag + "_gdn", hosts)
    og = _gdn_gate_fwd(o, p, W["gdn_norm_w"], tag + "_gdngate")
    ya = _matmul(og, W["gdn_w_o"], "nn", F32, tag + "_gdno")
    hc = _glu_fwd(p, W["cnv_pw1_b"], tag + "_glu")
    hcv = _dw_conv_fwd(hc, W["cnv_dw_w"], W["cnv_dw_b"], tag + "_dwconv")
    hl = _ln_silu_fwd(hcv, W["cnv_ln_g"], W["cnv_ln_b"], tag + "_ln")
    yb = _matmul(hl, W["cnv_w_o"], "nn", F32, tag + "_cnvo")
    ym = _merge_fwd(p, ya, yb, W["cnv_b_o"], tag + "_merge")
    y = _matmul(ym, W["mix_w_out"], "nn", F32, tag + "_out")
    xn = _post_fwd(x, y, W["mix_norm_post"], 1.0, tag + "_post")
    return xn, (x, h, p, pba, qkvn, o, Tinv, Sin, og, ya, hc, hcv, hl, yb, ym, y)


def _mix_bwd(dxn, saved, W, H, tag, g, hosts=None):
    x, h, p, pba, qkvn, o, Tinv, Sin, og, ya, hc, hcv, hl, yb, ym, y = saved
    dy, g["mix_norm_post"] = _post_bwd(dxn, y, W["mix_norm_post"], 1.0, tag + "_dpost")
    dym = _matmul(dy, W["mix_w_out"], "nt", F32, tag + "_dym")
    g["mix_w_out"] = _matmul(ym, dy, "tn", BF16, tag + "_dwout")
    dya, dyb, dp, g["cnv_b_o"] = _merge_bwd(dym, p, ya, yb, W["cnv_b_o"], tag + "_dmerge", hosts,
                                            dp_width=p.shape[1] + 2 * LANES)
    dhl = _matmul(dyb, W["cnv_w_o"], "nt", F32, tag + "_dhl")
    g["cnv_w_o"] = _matmul(hl, dyb, "tn", BF16, tag + "_dwcnvo")
    dhcv, g["cnv_ln_g"], g["cnv_ln_b"], g["cnv_dw_b"] = _ln_silu_bwd(
        dhl, hcv, W["cnv_ln_g"], W["cnv_ln_b"], tag + "_dln", hosts)
    dhc, g["cnv_dw_w"] = _dw_conv_bwd(dhcv, hc, W["cnv_dw_w"], tag + "_ddwconv", hosts)
    dp, g["cnv_pw1_b"] = _glu_bwd(dhc, p, W["cnv_pw1_b"], tag + "_dglu", dp)
    dog = _matmul(dya, W["gdn_w_o"], "nt", F32, tag + "_dog")
    g["gdn_w_o"] = _matmul(og, dya, "tn", BF16, tag + "_dwgdno")
    do, dp, g["gdn_norm_w"] = _gdn_gate_bwd(dog, o, p, W["gdn_norm_w"], tag + "_dgdngate", hosts, dp)
    dqkvn, dp, g["gdn_a_log"], g["gdn_dt_bias"] = _gdn_bwd(
        do, qkvn, pba, W["gdn_a_log"], W["gdn_dt_bias"], Tinv, Sin, H, tag + "_dgdn", hosts, dp)
    dp, g["gdn_conv_w"] = _qkv_conv_bwd(dqkvn, p, W["gdn_conv_w"], H, tag + "_dqkvconv", hosts, dp)
    dh = _matmul(dp, W["mix_w_in"], "nn", F32, tag + "_dh", hosts)
    g["mix_w_in"] = _matmul(dp, h, "tn", F32, tag + "_dwin")
    dx, g["mix_norm_pre"] = _pre_bwd(dh, x, W["mix_norm_pre"], dxn, tag + "_dpre", hosts)
    return dx


def _trunk_fwd_bwd(x, tgt, H, L, weights_of, grads, hosts=None):
    saved, Ws = [], []
    for i in range(L):
        W = weights_of(i)
        Ws.append(W)
        x, s1 = _ffn_fwd(x, W, "ffn1", f"l{i}_ffn1", hosts)
        x, s2 = _mix_fwd(x, W, H, f"l{i}_mix", hosts)
        x, s3 = _ffn_fwd(x, W, "ffn2", f"l{i}_ffn2", hosts)
        saved.append((s1, s2, s3))
    dx, loss = _loss_fwd_bwd(x, tgt, "loss")
    for i in reversed(range(L)):
        s1, s2, s3 = saved[i]
        dx = _ffn_bwd(dx, s3, Ws[i], "ffn2", f"l{i}_ffn2", grads[i], hosts)
        dx = _mix_bwd(dx, s2, Ws[i], H, f"l{i}_mix", grads[i], hosts)
        dx = _ffn_bwd(dx, s1, Ws[i], "ffn1", f"l{i}_ffn1", grads[i], hosts)
    return loss, dx


HBM_SPEC = pl.BlockSpec(memory_space=pltpu.HBM)


def _coords():
    return lax.axis_index("x"), lax.axis_index("y"), lax.axis_index("c")


class _GatherPlan:
    has_middle = True

    def __init__(self, shards):
        self.n = len(shards)
        self.out_shape = [jax.ShapeDtypeStruct((N_DEV,) + s.shape, s.dtype) for s in shards]
        self.sems = [pltpu.SemaphoreType.DMA((self.n, 7)), pltpu.SemaphoreType.DMA((self.n, 7)),
                     pltpu.SemaphoreType.DMA((self.n,))]

    def _parts(self, ins, outs, sems):
        send_sems, recv_sems, local_sems = sems
        x, y, c = _coords()
        me, sibling = (x, y, c), (x, y, 1 - c)
        chips = [(1 - x, y), (x, 1 - y), (1 - x, 1 - y)]

        def copy(w, k, block, to, src=None):
            dst = outs[w].at[4 * block[0] + 2 * block[1] + block[2]]
            return pltpu.make_async_remote_copy(
                src_ref=dst if src is None else src, dst_ref=dst, send_sem=send_sems.at[w, k],
                recv_sem=recv_sems.at[w, k], device_id=to, device_id_type=MESH)

        mine = [pltpu.make_async_copy(ins[w], outs[w].at[4 * x + 2 * y + c], local_sems.at[w]) for w in range(self.n)]
        first = []
        for w in range(self.n):
            first.append(copy(w, 0, me, sibling, src=ins[w]))
            first += [copy(w, 1 + j, me, (*chip, c), src=ins[w]) for j, chip in enumerate(chips)]
        passed = [copy(w, 4 + j, (*chip, c), sibling) for j, chip in enumerate(chips) for w in range(self.n)]
        return copy, mine, first, passed, chips, me, sibling, c

    def begin(self, ins, outs, sems):
        _, mine, first, _, _, _, _, _ = self._parts(ins, outs, sems)
        for cp in mine + first:
            cp.start()

    def middle(self, ins, outs, sems):
        copy, _, _, passed, chips, me, _, c = self._parts(ins, outs, sems)
        for j, chip in enumerate(chips):
            for w in range(self.n):
                copy(w, 1 + j, (*chip, c), me).wait_recv()
                passed[j * self.n + w].start()

    def finish(self, ins, outs, sems):
        copy, mine, first, passed, chips, me, sibling, c = self._parts(ins, outs, sems)
        for w in range(self.n):
            copy(w, 0, sibling, me).wait_recv()
            for j, chip in enumerate(chips):
                copy(w, 4 + j, (*chip, 1 - c), me).wait_recv()
        for cp in first + passed:
            cp.wait_send()
        for cp in mine:
            cp.wait()


class _SiblingPlan:
    has_middle = False

    def __init__(self, Gs):
        self.n = len(Gs)
        self.out_shape = [jax.ShapeDtypeStruct((4,) + g.shape[1:], g.dtype) for g in Gs]
        self.sems = [pltpu.SemaphoreType.DMA((self.n, 4)), pltpu.SemaphoreType.DMA((self.n, 4))]

    def _copies(self, ins, outs, sems):
        send_sems, recv_sems = sems
        x, y, c = _coords()
        return [pltpu.make_async_remote_copy(
            src_ref=ins[w].at[2 * q + (1 - c)], dst_ref=outs[w].at[q], send_sem=send_sems.at[w, q],
            recv_sem=recv_sems.at[w, q], device_id=(x, y, 1 - c), device_id_type=MESH)
            for w in range(self.n) for q in range(4)]

    def begin(self, ins, outs, sems):
        for cp in self._copies(ins, outs, sems):
            cp.start()

    def finish(self, ins, outs, sems):
        for cp in self._copies(ins, outs, sems):
            cp.wait()


class _ChipsPlan:
    has_middle = False

    def __init__(self, Ps):
        self.n = len(Ps)
        self.out_shape = [jax.ShapeDtypeStruct(p.shape, p.dtype) for p in Ps]
        self.sems = [pltpu.SemaphoreType.DMA((self.n, 3)), pltpu.SemaphoreType.DMA((self.n, 3)),
                     pltpu.SemaphoreType.DMA((self.n,))]

    def _copies(self, ins, outs, sems):
        send_sems, recv_sems, local_sems = sems
        x, y, c = _coords()
        me_q = 2 * x + y
        cps = []
        for w in range(self.n):
            cps.append(pltpu.make_async_copy(ins[w].at[me_q], outs[w].at[me_q], local_sems.at[w]))
            for j, (px, py) in enumerate([(1 - x, y), (x, 1 - y), (1 - x, 1 - y)]):
                cps.append(pltpu.make_async_remote_copy(
                    src_ref=ins[w].at[2 * px + py], dst_ref=outs[w].at[me_q], send_sem=send_sems.at[w, j],
                    recv_sem=recv_sems.at[w, j], device_id=(px, py, c), device_id_type=MESH))
        return cps

    def begin(self, ins, outs, sems):
        for cp in self._copies(ins, outs, sems):
            cp.start()

    def finish(self, ins, outs, sems):
        for cp in self._copies(ins, outs, sems):
            cp.wait()


def _comm_only(plan, arrays, name):
    n = plan.n

    def body(*refs):
        ins, outs, sems = refs[:n], refs[n:2 * n], refs[2 * n:]
        plan.begin(ins, outs, sems)
        if plan.has_middle:
            plan.middle(ins, outs, sems)
        plan.finish(ins, outs, sems)

    return pl.pallas_call(
        body, name=name, out_shape=plan.out_shape, in_specs=[HBM_SPEC] * n, out_specs=[HBM_SPEC] * n,
        scratch_shapes=plan.sems,
    )(*arrays)


class _MultiPlan:
    def __init__(self, plans):
        self.plans = plans
        self.n = sum(p.n for p in plans)
        self.out_shape = [s for p in plans for s in p.out_shape]
        self.sems = [s for p in plans for s in p.sems]
        self.has_middle = any(p.has_middle for p in plans)

    def _each(self, phase, ins, outs, sems):
        a = s = 0
        for p in self.plans:
            if phase != "middle" or p.has_middle:
                getattr(p, phase)(ins[a:a + p.n], outs[a:a + p.n], sems[s:s + len(p.sems)])
            a, s = a + p.n, s + len(p.sems)

    def begin(self, ins, outs, sems):
        self._each("begin", ins, outs, sems)

    def middle(self, ins, outs, sems):
        self._each("middle", ins, outs, sems)

    def finish(self, ins, outs, sems):
        self._each("finish", ins, outs, sems)


class _Hosts:
    def __init__(self):
        self.waiting = {}

    def add(self, host, make):
        self.waiting.setdefault(host, []).append(make)

    def take(self, host):
        makes = self.waiting.pop(host, None)
        if not makes:
            return None
        items = [m() for m in makes]
        return _MultiPlan([it[0] for it in items]), [a for it in items for a in it[1]], [it[2] for it in items]


def _hosted_call(body, hosts, *, name, grid, in_specs, out_specs, out_shape, scratch_shapes, args, sem, aliases=None):
    comm = hosts.take(name) if hosts is not None else None
    if comm is None:
        return pl.pallas_call(body, name=name, grid=grid, in_specs=in_specs, out_specs=out_specs, out_shape=out_shape,
                              scratch_shapes=scratch_shapes, input_output_aliases=aliases or {},
                              compiler_params=_cparams(sem))(*args)
    plan, arrays, deliver = comm
    n_in, n_out, n_scr, n = len(in_specs), len(out_specs), len(scratch_shapes), plan.n
    total = 1
    for g in grid:
        total *= g

    def kern(*refs):
        ins, cins = refs[:n_in], refs[n_in:n_in + n]
        outs, couts = refs[n_in + n:n_in + n + n_out], refs[n_in + n + n_out:n_in + 2 * n + n_out]
        scr, csems = refs[n_in + 2 * n + n_out:n_in + 2 * n + n_out + n_scr], refs[n_in + 2 * n + n_out + n_scr:]
        step = pl.program_id(0)
        for d in range(1, len(grid)):
            step = step * grid[d] + pl.program_id(d)

        @pl.when(step == 0)
        def _():
            plan.begin(cins, couts, csems)

        body(*ins, *outs, *scr)
        if plan.has_middle:
            @pl.when(step == (3 * total) // 4)
            def _():
                plan.middle(cins, couts, csems)

        @pl.when(step == total - 1)
        def _():
            plan.finish(cins, couts, csems)

    res = pl.pallas_call(
        kern, name=name, grid=grid, in_specs=list(in_specs) + [HBM_SPEC] * n,
        out_specs=list(out_specs) + [HBM_SPEC] * n, out_shape=list(out_shape) + plan.out_shape,
        scratch_shapes=list(scratch_shapes) + plan.sems, input_output_aliases=aliases or {},
        compiler_params=_cparams(("arbitrary",) * len(grid)),
    )(*args, *arrays)
    k = n_out
    for p, d in zip(plan.plans, deliver):
        d(res[k:k + p.n])
        k += p.n
    return res[:n_out]


def _row_tile(R, target=256):
    best = None
    for t in range(8, min(R, target) + 1, 8):
        if R % t == 0:
            best = t
    return best if best is not None else R


def _pair_add(G, R1, cidx, name):
    _, R, C = G.shape
    tb = _row_tile(R)

    def body(c_ref, g_ref, r_ref, o_ref):
        o_ref[...] = (g_ref[...].astype(F32) + r_ref[...].astype(F32)).astype(BF16)

    return pl.pallas_call(
        body, name=name,
        grid_spec=pltpu.PrefetchScalarGridSpec(
            num_scalar_prefetch=1, grid=(4, R // tb),
            in_specs=[pl.BlockSpec((None, tb, C), lambda q, i, cr: (2 * q + cr[0], i, 0)),
                      pl.BlockSpec((None, tb, C), lambda q, i, cr: (q, i, 0))],
            out_specs=pl.BlockSpec((None, tb, C), lambda q, i, cr: (q, i, 0))),
        out_shape=jax.ShapeDtypeStruct((4, R, C), BF16),
        compiler_params=_cparams(("arbitrary", "arbitrary")),
    )(cidx, G, R1)


def _sum_parts(parts, name):
    P, R, C = parts.shape

    def body(p_ref, o_ref):
        acc = p_ref[0]
        for j in range(1, P):
            acc = acc + p_ref[j]
        o_ref[...] = acc

    return pl.pallas_call(
        body, name=name, out_shape=jax.ShapeDtypeStruct((R, C), F32),
        in_specs=[pl.BlockSpec(memory_space=pltpu.VMEM)], out_specs=pl.BlockSpec(memory_space=pltpu.VMEM),
        compiler_params=_cparams(),
    )(parts)


def _adamw(w, m, v, parts, name):
    G, R, C = w.shape
    P = parts[0].shape[0]
    tb = _row_tile(R)
    nb = R // tb
    c1 = 1.0 / (1.0 - ADAM_B1 ** ADAM_STEP)
    c2 = 1.0 / (1.0 - ADAM_B2 ** ADAM_STEP)

    def body(w_ref, m_ref, v_ref, *rest):
        p_refs, (g_ref, d_ref, nm_ref, nv_ref) = rest[:G], rest[G:]
        l = pl.program_id(0)
        g = None
        for k in range(G):
            gk = p_refs[k][0].astype(F32)
            for j in range(1, P):
                gk = gk + p_refs[k][j].astype(F32)
            g = gk if g is None else jnp.where(l == k, gk, g)
        nm = ADAM_B1 * m_ref[...] + (1.0 - ADAM_B1) * g
        nv = ADAM_B2 * v_ref[...] + (1.0 - ADAM_B2) * (g * g)
        g_ref[...] = g
        nm_ref[...] = nm
        nv_ref[...] = nv
        d_ref[...] = -ADAM_LR * ((nm * c1) / (jnp.sqrt(nv * c2) + ADAM_EPS) + ADAM_WD * w_ref[...])

    blk = pl.BlockSpec((None, tb, C), lambda l, i: (l, i, 0))

    def part_spec(k):
        return pl.BlockSpec((P, tb, C), lambda l, i: (0, jnp.where(l < k, 0, jnp.where(l > k, nb - 1, i)), 0))

    return pl.pallas_call(
        body, name=name, grid=(G, nb),
        in_specs=[blk, blk, blk] + [part_spec(k) for k in range(G)],
        out_specs=[blk] * 4, out_shape=[jax.ShapeDtypeStruct((G, R, C), F32)] * 4,
        compiler_params=_cparams(("arbitrary", "arbitrary")),
    )(w, m, v, *parts)


BIG = ("ffn1_w_in", "ffn1_w_out", "mix_w_in", "gdn_w_o", "cnv_w_o", "mix_w_out", "ffn2_w_in", "ffn2_w_out")
COL_SHARDED = ("ffn1_w_in", "mix_w_in", "ffn2_w_in")
SMALL_SHARDED = ("gdn_conv_w", "cnv_dw_w")
NAMES = ("ffn1_norm_pre", "ffn1_norm_post", "ffn1_w_in", "ffn1_w_out", "mix_norm_pre", "mix_norm_post", "mix_w_in",
         "gdn_conv_w", "gdn_a_log", "gdn_dt_bias", "gdn_norm_w", "gdn_w_o", "cnv_pw1_b", "cnv_dw_w", "cnv_dw_b",
         "cnv_ln_g", "cnv_ln_b", "cnv_w_o", "cnv_b_o", "mix_w_out", "ffn2_norm_pre", "ffn2_norm_post", "ffn2_w_in",
         "ffn2_w_out")
SMALL = tuple(n for n in NAMES if n not in BIG)


class _LayerWeights:
    def __init__(self, got, params, i, D, H):
        self.got, self.params, self.i, self.D, self.H, self.made = got, params, i, D, H, {}

    def __getitem__(self, n):
        if n not in self.made:
            if n in BIG:
                g = self.got[(self.i, n)]
                g = g.reshape(-1, g.shape[-1])
                w = _mix_in_reorder(g, self.D, self.H) if n == "mix_w_in" else g
            elif n in SMALL_SHARDED:
                g = self.got[(self.i, n)]
                w = jnp.transpose(g, (1, 0, 2)).reshape(g.shape[1], -1)
            else:
                v = self.params[n][self.i]
                if n in ("gdn_a_log", "gdn_dt_bias"):
                    v = jnp.pad(v, (0, LANES - self.H))
                w = v.reshape(1, -1)
            self.made[n] = w
        return self.made[n]


MIX_SMALL = ("gdn_w_o", "cnv_w_o", "mix_w_out", "gdn_conv_w", "cnv_dw_w")
GATHER_HOSTS = (("l{j}_mix_in", ("ffn1_w_in", "ffn1_w_out")), ("l{j}_mix_qkvconv", MIX_SMALL),
                ("l{j}_ffn2_in", ("mix_w_in",)), ("l{i}_mix_gdn", ("ffn2_w_in", "ffn2_w_out")))
GATHER_HOSTS_FIRST = ((None, ("ffn1_w_in", "ffn1_w_out")), ("l0_ffn1_in", ("mix_w_in",)), ("l0_ffn1_out", MIX_SMALL),
                      ("l0_mix_gdn", ("ffn2_w_in", "ffn2_w_out")))
REDUCE_HOSTS = ((BIG, "l{j}_ffn2_ds", (("l{j}_ffn2_dh", ("ffn2_w_in",)),
                                       ("l{j}_ffn2_dwin", ("ffn2_w_out", "gdn_w_o", "cnv_w_o", "mix_w_out")),
                                       ("l{j}_mix_ddwconv", ("ffn1_w_in",)),
                                       ("l{j}_mix_dgdn", ("mix_w_in", "ffn1_w_out")))),)
REDUCE_HOSTS_FIRST = (
    (("ffn2_w_in", "ffn2_w_out"), "l0_mix_dmerge", (("l0_mix_dln", ("ffn2_w_out",)), ("l0_mix_dqkvconv", ("ffn2_w_in",)))),
    (("mix_w_out", "cnv_w_o", "gdn_w_o"), "l0_mix_dgdngate", (("l0_mix_dh", ("mix_w_out", "cnv_w_o", "gdn_w_o")),)),
    (("mix_w_in",), "l0_mix_dpre", (("l0_ffn1_dwin", ("mix_w_in",)),)),
    (("ffn1_w_out",), "l0_ffn1_dact", (("l0_ffn1_dh", ("ffn1_w_out",)),)),
    (("ffn1_w_in",), "l0_ffn1_dh", (("l0_ffn1_dpre", ("ffn1_w_in",)),)))


def kernel(x, ffn1_norm_pre, ffn1_norm_post, ffn1_w_in, ffn1_w_out, mix_norm_pre, mix_norm_post, mix_w_in, gdn_conv_w, gdn_a_log, gdn_dt_bias, gdn_norm_w, gdn_w_o, cnv_pw1_b, cnv_dw_w, cnv_dw_b, cnv_ln_g, cnv_ln_b, cnv_w_o, cnv_b_o, mix_w_out, ffn2_norm_pre, ffn2_norm_post, ffn2_w_in, ffn2_w_out, loss_target, m_ffn1_norm_pre, m_ffn1_norm_post, m_ffn1_w_in, m_ffn1_w_out, m_mix_norm_pre, m_mix_norm_post, m_mix_w_in, m_gdn_conv_w, m_gdn_a_log, m_gdn_dt_bias, m_gdn_norm_w, m_gdn_w_o, m_cnv_pw1_b, m_cnv_dw_w, m_cnv_dw_b, m_cnv_ln_g, m_cnv_ln_b, m_cnv_w_o, m_cnv_b_o, m_mix_w_out, m_ffn2_norm_pre, m_ffn2_norm_post, m_ffn2_w_in, m_ffn2_w_out, v_ffn1_norm_pre, v_ffn1_norm_post, v_ffn1_w_in, v_ffn1_w_out, v_mix_norm_pre, v_mix_norm_post, v_mix_w_in, v_gdn_conv_w, v_gdn_a_log, v_gdn_dt_bias, v_gdn_norm_w, v_gdn_w_o, v_cnv_pw1_b, v_cnv_dw_w, v_cnv_dw_b, v_cnv_ln_g, v_cnv_ln_b, v_cnv_w_o, v_cnv_b_o, v_mix_w_out, v_ffn2_norm_pre, v_ffn2_norm_post, v_ffn2_w_in, v_ffn2_w_out):
    params = dict(zip(NAMES, (ffn1_norm_pre, ffn1_norm_post, ffn1_w_in, ffn1_w_out, mix_norm_pre, mix_norm_post, mix_w_in, gdn_conv_w, gdn_a_log, gdn_dt_bias, gdn_norm_w, gdn_w_o, cnv_pw1_b, cnv_dw_w, cnv_dw_b, cnv_ln_g, cnv_ln_b, cnv_w_o, cnv_b_o, mix_w_out, ffn2_norm_pre, ffn2_norm_post, ffn2_w_in, ffn2_w_out)))
    mom1 = dict(zip(NAMES, (m_ffn1_norm_pre, m_ffn1_norm_post, m_ffn1_w_in, m_ffn1_w_out, m_mix_norm_pre, m_mix_norm_post, m_mix_w_in, m_gdn_conv_w, m_gdn_a_log, m_gdn_dt_bias, m_gdn_norm_w, m_gdn_w_o, m_cnv_pw1_b, m_cnv_dw_w, m_cnv_dw_b, m_cnv_ln_g, m_cnv_ln_b, m_cnv_w_o, m_cnv_b_o, m_mix_w_out, m_ffn2_norm_pre, m_ffn2_norm_post, m_ffn2_w_in, m_ffn2_w_out)))
    mom2 = dict(zip(NAMES, (v_ffn1_norm_pre, v_ffn1_norm_post, v_ffn1_w_in, v_ffn1_w_out, v_mix_norm_pre, v_mix_norm_post, v_mix_w_in, v_gdn_conv_w, v_gdn_a_log, v_gdn_dt_bias, v_gdn_norm_w, v_gdn_w_o, v_cnv_pw1_b, v_cnv_dw_w, v_cnv_dw_b, v_cnv_ln_g, v_cnv_ln_b, v_cnv_w_o, v_cnv_b_o, v_mix_w_out, v_ffn2_norm_pre, v_ffn2_norm_post, v_ffn2_w_in, v_ffn2_w_out)))
    T, D = x.shape[1], x.shape[2]
    H = D // GDN_DK
    L = ffn1_norm_pre.shape[0]
    xi, yi, ci = _coords()
    dev = 4 * xi + 2 * yi + ci

    ag_names = BIG + SMALL_SHARDED

    def shard_to_send(n):
        if n in COL_SHARDED:
            return jnp.swapaxes(params[n], 1, 2).astype(BF16)
        return params[n].astype(BF16) if n in BIG else params[n]

    send = {n: shard_to_send(n) for n in ag_names}

    hosts, got, later = _Hosts(), {}, []
    for i in range(L):
        for host, names in (GATHER_HOSTS_FIRST if i == 0 else GATHER_HOSTS):
            def make(i=i, names=names):
                blocks = [send[n][i] for n in names]

                def deliver(outs):
                    got.update({(i, n): o for n, o in zip(names, outs)})
                return _GatherPlan(blocks), blocks, deliver
            if host is None:
                plan, blocks, deliver = make()
                deliver(_comm_only(plan, blocks, f"ag_weights_l{i}"))
            else:
                hosts.add(host.format(i=i, j=i - 1), make)

    cidx = jnp.reshape(ci, (1,)).astype(jnp.int32)
    grads, reduced = [{} for _ in range(L)], {}
    for i in range(L):
        for names, sib_host, chip_hosts in (REDUCE_HOSTS_FIRST if i == 0 else REDUCE_HOSTS):
            stage = {}

            def make_sib(i=i, names=names, stage=stage):
                Gs = []
                for n in names:
                    g = _mix_in_restore(grads[i][n], D, H) if n == "mix_w_in" else grads[i][n]
                    Gs.append(g.reshape(N_DEV, -1, g.shape[-1]))
                stage["G"] = dict(zip(names, Gs))

                def deliver(outs):
                    stage["R1"] = dict(zip(names, outs))
                return _SiblingPlan(Gs), Gs, deliver

            def make_chips(ns, i=i, stage=stage):
                Ps = [_pair_add(stage["G"][n], stage["R1"][n], cidx, f"l{i}_pair_add_{n}") for n in ns]

                def deliver(outs):
                    reduced.update({(i, n): o for n, o in zip(ns, outs)})
                return _ChipsPlan(Ps), Ps, deliver

            if sib_host is None:
                later.append((f"l{i}_{names[0]}", make_sib, [(lambda ns=ns, mc=make_chips: mc(ns)) for _, ns in chip_hosts]))
                continue
            hosts.add(sib_host.format(i=i, j=i - 1), make_sib)
            for host, ns in chip_hosts:
                hosts.add(host.format(i=i, j=i - 1), lambda ns=ns, mc=make_chips: mc(ns))

    weights = [_LayerWeights(got, params, i, D, H) for i in range(L)]
    loss_row, dx = _trunk_fwd_bwd(x[0], loss_target[0], H, L, lambda i: weights[i], grads, hosts)
    loss = lax.psum(loss_row[0, 0], ("x", "y", "c"))
    assert not hosts.waiting, sorted(hosts.waiting)

    for tag, make_sib, chip_makes in later:
        plan, Gs, deliver = make_sib()
        deliver(_comm_only(plan, Gs, f"rs_sibling_{tag}"))
        for mk in chip_makes:
            plan, Ps, deliver = mk()
            deliver(_comm_only(plan, Ps, f"rs_chips_{tag}"))
    R2s = {n: [jnp.swapaxes(reduced[(i, n)], 1, 2) if n in COL_SHARDED else reduced[(i, n)] for i in range(L)]
           for n in BIG}

    pieces = []
    for i in range(L):
        for n in SMALL:
            piece = grads[i][n].reshape(-1, LANES)
            pieces.append(jnp.pad(piece, ((0, (-piece.shape[0]) % 8), (0, 0))))
    packed = jnp.concatenate(pieces, axis=0)
    small_all = _comm_only(_GatherPlan([packed]), [packed], "ag_small_grads")[0]
    small_sum = _sum_parts(small_all, "sum_small_grads")
    small_g = {n: [None] * L for n in SMALL}
    off = 0
    for i in range(L):
        for n in SMALL:
            shape = grads[i][n].shape
            cnt = shape[0] * shape[1] // LANES
            g = small_sum[off:off + cnt].reshape(shape)
            off += cnt + (-cnt) % 8
            if n in ("gdn_a_log", "gdn_dt_bias"):
                g = g[:, :H]
            if n in SMALL_SHARDED:
                wloc = params[n].shape[-1]
                g = lax.dynamic_slice_in_dim(g, dev * wloc, wloc, axis=1)
            small_g[n][i] = g

    outs = {}
    for n in NAMES:
        w, m, v = params[n], mom1[n], mom2[n]
        if n in BIG:
            shape3, parts = w.shape, R2s[n]
        else:
            rows, cols = (w.shape[0] * w.shape[1], w.shape[2]) if w.ndim == 3 else w.shape
            shape3, parts = (1, rows, cols), [jnp.stack(small_g[n], axis=0).reshape(1, rows, cols)]
        res = _adamw(w.reshape(shape3), m.reshape(shape3), v.reshape(shape3), parts, "adamw_" + n)
        outs[n] = [r.reshape(w.shape) for r in res]

    result = [loss, dx[None]]
    for k in range(4):
        result += [outs[n][k] for n in NAMES]
    return tuple(result)
```

```python
import jax
import jax.numpy as jnp
from jax import lax
from jax.experimental import pallas as pl
from jax.experimental.pallas import tpu as pltpu

F32 = jnp.float32
BF16 = jnp.bfloat16

GDN_DK = 128
CHUNK = 64
GDN_CONV = 4
CNV_K = 31
RMS_EPS = 1e-6
LN_EPS = 1e-5
L2_EPS = 1e-6
ADAM_LR = 0.001
ADAM_B1 = 0.9
ADAM_B2 = 0.999
ADAM_EPS = 1e-08
ADAM_WD = 0.01
ADAM_STEP = 10

LANES = 128
SUB = 16
VMEM_LIMIT = 56 * 1024 * 1024
N_DEV = 8
MESH = pl.DeviceIdType.MESH


def _cparams(sem=None, **kw):
    if sem is not None:
        kw["dimension_semantics"] = sem
    return pltpu.CompilerParams(vmem_limit_bytes=VMEM_LIMIT, **kw)


def _tile(dim, target):
    best = None
    for t in range(LANES, min(dim, target) + 1, LANES):
        if dim % t == 0:
            best = t
    return best if best is not None else dim


def _sigmoid(x):
    return 1.0 / (1.0 + jnp.exp(-x))


def _silu(x):
    return x * _sigmoid(x)


def _dsilu(x):
    s = _sigmoid(x)
    return s * (1.0 + x * (1.0 - s))


MM_VMEM_BUDGET = 40 * 1024 * 1024
MM_MAX_TILE = 2048


def _mm_tiles(M, N, K, out_bytes):
    def cands(dim):
        c = [t for t in range(LANES, min(dim, MM_MAX_TILE) + 1, LANES) if dim % t == 0]
        return c or [dim]
    best = None
    for tm in cands(M):
        for tn in cands(N):
            vm = 2 * (2 * K * (tm + tn) + tm * tn * out_bytes)
            if vm <= MM_VMEM_BUDGET and (best is None or tm * tn > best[0] * best[1]):
                best = (tm, tn)
    return best if best is not None else (cands(M)[0], cands(N)[0])


def _matmul(a, b, mode, out_dtype, name, hosts=None, rows=None):
    if mode == "nn":
        (M, K), N = a.shape, b.shape[1]
    elif mode == "nt":
        (M, K), N = a.shape, b.shape[0]
    else:
        (K, M), N = a.shape, b.shape[1]
    first = 0
    if rows is not None:
        first, N = rows
    tm, tn = _mm_tiles(M, N, K, jnp.dtype(out_dtype).itemsize)
    assert first % tn == 0
    joff = first // tn
    if mode == "nn":
        a_spec = pl.BlockSpec((tm, K), lambda j, i: (i, 0))
        b_spec = pl.BlockSpec((K, tn), lambda j, i: (0, j))
        dn = (((1,), (0,)), ((), ()))
    elif mode == "nt":
        a_spec = pl.BlockSpec((tm, K), lambda j, i: (i, 0))
        b_spec = pl.BlockSpec((tn, K), lambda j, i: (j + joff, 0))
        dn = (((1,), (1,)), ((), ()))
    else:
        a_spec = pl.BlockSpec((K, tm), lambda j, i: (0, i))
        b_spec = pl.BlockSpec((K, tn), lambda j, i: (0, j))
        dn = (((0,), (0,)), ((), ()))

    def body(a_ref, b_ref, o_ref):
        o_ref[...] = lax.dot_general(a_ref[...], b_ref[...], dn, preferred_element_type=F32).astype(out_dtype)

    return _hosted_call(
        body, hosts, name=name, grid=(N // tn, M // tm), in_specs=[a_spec, b_spec],
        out_specs=[pl.BlockSpec((tm, tn), lambda j, i: (i, j))],
        out_shape=[jax.ShapeDtypeStruct((M, N), out_dtype)], scratch_shapes=[],
        args=(a, b), sem=("parallel", "parallel"))[0]


ROW_BLOCK = 512


def _rowcall(name, body, T, tb, row_ins, par_ins, row_outs, acc_outs, hosts=None, wide=None):
    tb = min(T, ROW_BLOCK)
    n_ri, n_pi, n_ro = len(row_ins), len(par_ins), len(row_outs)
    n_extra = 0 if (wide is None or wide[3] is None) else 1

    def kern(*refs):
        ri, pi = refs[:n_ri], refs[n_ri:n_ri + n_pi]
        refs = refs[n_ri + n_pi + n_extra:]
        ro, ao = refs[:n_ro], refs[n_ro:]
        if ao:
            @pl.when(pl.program_id(0) == 0)
            def _():
                for r in ao:
                    r[...] = jnp.zeros_like(r)
        body(ri, pi, ro, ao)

    in_specs = [pl.BlockSpec((tb, w), lambda i, cb=cb: (i, cb)) for (_, w, cb) in row_ins]
    in_specs += [pl.BlockSpec(p.shape, lambda i: (0, 0)) for p in par_ins]
    out_specs = [pl.BlockSpec((tb, w), lambda i: (i, 0)) for (w, _) in row_outs]
    out_specs += [pl.BlockSpec((1, w), lambda i: (0, 0)) for w in acc_outs]
    out_shape = [jax.ShapeDtypeStruct((T, w), dt) for (w, dt) in row_outs]
    out_shape += [jax.ShapeDtypeStruct((1, w), F32) for w in acc_outs]
    args, aliases = [*[a for (a, _, _) in row_ins], *par_ins], {}
    if wide is not None:
        k, total, cb, buf = wide
        w, dt = row_outs[k]
        out_specs[k] = pl.BlockSpec((tb, w), lambda i: (i, cb))
        out_shape[k] = jax.ShapeDtypeStruct((T, total), dt)
        if buf is not None:
            in_specs.append(pl.BlockSpec(memory_space=pl.ANY))
            args.append(buf)
            aliases = {len(args) - 1: k}
    return _hosted_call(
        kern, hosts, name=name, grid=(T // tb,), in_specs=in_specs, out_specs=out_specs, out_shape=out_shape,
        scratch_shapes=[], args=tuple(args), sem=("arbitrary",), aliases=aliases)


def _rsum(x):
    return jnp.sum(x, axis=0, keepdims=True)


def _rms_rstd(x):
    return lax.rsqrt(jnp.mean(x * x, axis=-1, keepdims=True) + RMS_EPS)


def _rms_fwd(x, w, name):
    T, D = x.shape

    def body(ri, pi, ro, ao):
        xv = ri[0][...]
        ro[0][...] = (xv * _rms_rstd(xv) * pi[0][...]).astype(BF16)

    return _rowcall(name, body, T, 256, [(x, D, 0)], [w], [(D, BF16)], [])[0]


def _rms_bwd_core(dy, x, w):
    rs = _rms_rstd(x)
    xh = x * rs
    gw = dy * w
    dx = rs * (gw - xh * jnp.mean(gw * xh, axis=-1, keepdims=True))
    return dx, dy * xh


def _pre_bwd(dh, x, w, dres, name, hosts=None):
    T, D = x.shape

    def body(ri, pi, ro, ao):
        dx, dwc = _rms_bwd_core(ri[0][...], ri[1][...], pi[0][...])
        ro[0][...] = ri[2][...] + dx
        ao[0][...] += _rsum(dwc)

    return _rowcall(name, body, T, 256, [(dh, D, 0), (x, D, 0), (dres, D, 0)], [w], [(D, F32)], [D], hosts=hosts)


def _post_fwd(x, f, w, r, name):
    T, D = x.shape

    def body(ri, pi, ro, ao):
        fv = ri[1][...]
        ro[0][...] = ri[0][...] + r * (fv * _rms_rstd(fv) * pi[0][...])

    return _rowcall(name, body, T, 256, [(x, D, 0), (f, D, 0)], [w], [(D, F32)], [])[0]


def _post_bwd(dxn, f, w, r, name):
    T, D = f.shape

    def body(ri, pi, ro, ao):
        df, dwc = _rms_bwd_core(r * ri[0][...], ri[1][...], pi[0][...])
        ro[0][...] = df.astype(BF16)
        ao[0][...] += _rsum(dwc)

    return _rowcall(name, body, T, 256, [(dxn, D, 0), (f, D, 0)], [w], [(D, BF16)], [D])


def _swiglu_fwd(a, name):
    T, F2 = a.shape
    F = F2 // 2

    def body(ri, pi, ro, ao):
        ro[0][...] = (_silu(ri[0][...].astype(F32)) * ri[1][...].astype(F32)).astype(BF16)

    return _rowcall(name, body, T, 256, [(a, F, 0), (a, F, 1)], [], [(F, BF16)], [])[0]


def _swiglu_bwd(ds, a, name, hosts=None):
    T, F2 = a.shape
    F = F2 // 2

    def body(ri, pi, ro, ao):
        dsv, g, u = ri[0][...].astype(F32), ri[1][...].astype(F32), ri[2][...].astype(F32)
        ro[0][:, :F] = (dsv * u * _dsilu(g)).astype(BF16)
        ro[0][:, F:] = (dsv * _silu(g)).astype(BF16)

    return _rowcall(name, body, T, 256, [(ds, F, 0), (a, F, 0), (a, F, 1)], [], [(F2, BF16)], [], hosts=hosts)[0]


def _loss_fwd_bwd(y, tgt, name):
    T, D = y.shape

    def body(ri, pi, ro, ao):
        e = ri[0][...] - ri[1][...]
        ro[0][...] = e * (1.0 / D)
        tot = jnp.sum(_rsum(e * e), axis=1, keepdims=True) * (0.5 / D)
        ao[0][...] += jnp.broadcast_to(tot, (1, LANES))

    return _rowcall(name, body, T, 256, [(y, D, 0), (tgt, D, 0)], [], [(D, F32)], [LANES])


def _gdn_gate_fwd(o, p, nw, name):
    T, D = o.shape
    H = D // GDN_DK

    def body(ri, pi, ro, ao):
        for h in range(H):
            sl = slice(h * GDN_DK, (h + 1) * GDN_DK)
            oh = ri[0][:, sl]
            ro[0][:, sl] = (oh * _rms_rstd(oh) * pi[0][...] * _silu(ri[1][:, sl].astype(F32))).astype(BF16)

    return _rowcall(name, body, T, 256, [(o, D, 0), (p, D, 3)], [nw], [(D, BF16)], [])[0]


def _gdn_gate_bwd(dog, o, p, nw, name, hosts=None, dp=None):
    T, D = o.shape
    H = D // GDN_DK

    def body(ri, pi, ro, ao):
        acc = jnp.zeros((1, GDN_DK), F32)
        for h in range(H):
            sl = slice(h * GDN_DK, (h + 1) * GDN_DK)
            dy, oh, z = ri[0][:, sl], ri[1][:, sl], ri[2][:, sl].astype(F32)
            sz = _silu(z)
            do, dwc = _rms_bwd_core(dy * sz, oh, pi[0][...])
            ro[0][:, sl] = do
            ro[1][:, sl] = (dy * oh * _rms_rstd(oh) * pi[0][...] * _dsilu(z)).astype(BF16)
            acc = acc + _rsum(dwc)
        ao[0][...] += acc

    return _rowcall(name, body, T, 256, [(dog, D, 0), (o, D, 0), (p, D, 3)], [nw], [(D, F32), (D, BF16)], [GDN_DK],
                    hosts=hosts, wide=None if dp is None else (1, dp.shape[1], 3, dp))


def _glu_fwd(p, b, name):
    T = p.shape[0]
    D = b.shape[1] // 2

    def body(ri, pi, ro, ao):
        ro[0][...] = (ri[0][...].astype(F32) + pi[0][:, :D]) * _sigmoid(ri[1][...].astype(F32) + pi[0][:, D:])

    return _rowcall(name, body, T, 256, [(p, D, 4), (p, D, 5)], [b], [(D, F32)], [])[0]


def _glu_bwd(dhc, p, b, name, dp=None):
    T = p.shape[0]
    D = b.shape[1] // 2

    def body(ri, pi, ro, ao):
        d, a, g = ri[0][...], ri[1][...].astype(F32) + pi[0][:, :D], ri[2][...].astype(F32) + pi[0][:, D:]
        sg = _sigmoid(g)
        da, dg = d * sg, d * a * sg * (1.0 - sg)
        ro[0][:, :D] = da.astype(BF16)
        ro[0][:, D:] = dg.astype(BF16)
        ao[0][:, :D] += _rsum(da)
        ao[0][:, D:] += _rsum(dg)

    return _rowcall(name, body, T, 256, [(dhc, D, 0), (p, D, 4), (p, D, 5)], [b], [(2 * D, BF16)], [2 * D],
                    wide=None if dp is None else (0, dp.shape[1], 2, dp))


def _ln_stats(x):
    mu = jnp.mean(x, axis=-1, keepdims=True)
    xc = x - mu
    rstd = lax.rsqrt(jnp.mean(xc * xc, axis=-1, keepdims=True) + LN_EPS)
    return xc * rstd, rstd


def _ln_silu_fwd(hcv, g, b, name):
    T, D = hcv.shape

    def body(ri, pi, ro, ao):
        xh, _ = _ln_stats(ri[0][...])
        ro[0][...] = _silu(xh * pi[0][...] + pi[1][...]).astype(BF16)

    return _rowcall(name, body, T, 256, [(hcv, D, 0)], [g, b], [(D, BF16)], [])[0]


def _ln_silu_bwd(dhl, hcv, g, b, name, hosts=None):
    T, D = hcv.shape

    def body(ri, pi, ro, ao):
        xh, rstd = _ln_stats(ri[1][...])
        dyl = ri[0][...] * _dsilu(xh * pi[0][...] + pi[1][...])
        dxh = dyl * pi[0][...]
        dx = rstd * (dxh - jnp.mean(dxh, axis=-1, keepdims=True) - xh * jnp.mean(dxh * xh, axis=-1, keepdims=True))
        ro[0][...] = dx
        ao[0][...] += _rsum(dyl * xh)
        ao[1][...] += _rsum(dyl)
        ao[2][...] += _rsum(dx)

    return _rowcall(name, body, T, 256, [(dhl, D, 0), (hcv, D, 0)], [g, b], [(D, F32)], [D, D, D], hosts=hosts)


def _merge_fwd(p, ya, yb, bo, name):
    T, D = ya.shape

    def body(ri, pi, ro, ao):
        ga, gb = _sigmoid(ri[0][...].astype(F32)), _sigmoid(ri[1][...].astype(F32))
        ro[0][...] = (ga * ri[2][...] + gb * (ri[3][...] + pi[0][...])).astype(BF16)

    return _rowcall(name, body, T, 256, [(p, D, 6), (p, D, 7), (ya, D, 0), (yb, D, 0)], [bo], [(D, BF16)], [])[0]


def _merge_bwd(dym, p, ya, yb, bo, name, hosts=None, dp_width=None):
    T, D = ya.shape

    def body(ri, pi, ro, ao):
        d = ri[0][...]
        ga, gb = _sigmoid(ri[1][...].astype(F32)), _sigmoid(ri[2][...].astype(F32))
        ybv = ri[4][...] + pi[0][...]
        dyb = d * gb
        ro[0][...] = (d * ga).astype(BF16)
        ro[1][...] = dyb.astype(BF16)
        ro[2][:, :D] = (d * ri[3][...] * ga * (1.0 - ga)).astype(BF16)
        ro[2][:, D:] = (d * ybv * gb * (1.0 - gb)).astype(BF16)
        ao[0][...] += _rsum(dyb)

    return _rowcall(name, body, T, 256, [(dym, D, 0), (p, D, 6), (p, D, 7), (ya, D, 0), (yb, D, 0)], [bo],
                    [(D, BF16), (D, BF16), (2 * D, BF16)], [D], hosts=hosts,
                    wide=None if dp_width is None else (2, dp_width, 3, None))


PAD = 32
RC = 256


def _tap_windows(ref, offs):
    groups = {}
    for j, o in enumerate(offs):
        groups.setdefault(o % 8, []).append((j, o))
    for grp in groups.values():
        lo, hi = min(o for _, o in grp), max(o for _, o in grp)
        win = ref[pl.ds(lo, RC + hi - lo), :]
        for j, o in grp:
            yield j, win[o - lo:o - lo + RC]


def _causal_taps(xp_ref, w, K, c0):
    acc = None
    for j, xs in _tap_windows(xp_ref, [PAD - (K - 1) + j + c0 for j in range(K)]):
        term = w[j:j + 1, :] * xs
        acc = term if acc is None else acc + term
    return acc


def _anticausal_taps(dp_ref, w, K, c0):
    acc = None
    for j, ds in _tap_windows(dp_ref, [(K - 1) - j + c0 for j in range(K)]):
        term = w[j:j + 1, :] * ds
        acc = term if acc is None else acc + term
    return acc


def _tap_grads(dw_ref, dc_ref, xp_ref, K, T):
    accs = [jnp.zeros((8, LANES), F32) for _ in range(K)]
    for c in range(T // RC):
        d = dc_ref[pl.ds(c * RC, RC), :]
        for j, xs in _tap_windows(xp_ref, [PAD - (K - 1) + j + c * RC for j in range(K)]):
            accs[j] = accs[j] + jnp.sum((d * xs).reshape(RC // 8, 8, LANES), axis=0)
    for j in range(K):
        dw_ref[j:j + 1, :] = _rsum(accs[j])


def _qkv_conv_fwd(p, cw, H, name, hosts=None):
    T = p.shape[0]
    K = cw.shape[0]

    def body(x_ref, w_ref, o_ref, xp_ref):
        j = pl.program_id(0)
        xp_ref[pl.ds(0, PAD), :] = jnp.zeros((PAD, LANES), F32)
        xp_ref[pl.ds(PAD, T), :] = x_ref[...].astype(F32)
        w = w_ref[...]
        scale = jnp.where(j < H, GDN_DK ** -0.5, 1.0).astype(F32)
        for c in range(T // RC):
            act = _silu(_causal_taps(xp_ref, w, K, c * RC))
            nrm = act * lax.rsqrt(jnp.sum(act * act, axis=-1, keepdims=True) + L2_EPS) * scale
            o_ref[pl.ds(c * RC, RC), :] = jnp.where(j < 2 * H, nrm, act)

    return _hosted_call(
        body, hosts, name=name, grid=(3 * H,),
        in_specs=[pl.BlockSpec((T, LANES), lambda j: (0, j)), pl.BlockSpec((K, LANES), lambda j: (0, j))],
        out_specs=[pl.BlockSpec((T, LANES), lambda j: (0, j))],
        out_shape=[jax.ShapeDtypeStruct((T, 3 * H * GDN_DK), F32)],
        scratch_shapes=[pltpu.VMEM((T + PAD, LANES), F32)], args=(p, cw), sem=("arbitrary",))[0]


def _qkv_conv_bwd(dn, p, cw, H, name, hosts=None, dp=None):
    T = p.shape[0]
    K = cw.shape[0]

    def body(dn_ref, x_ref, w_ref, _, dx_ref, dw_ref, xp_ref, dc_ref):
        j = pl.program_id(0)
        xp_ref[pl.ds(0, PAD), :] = jnp.zeros((PAD, LANES), F32)
        xp_ref[pl.ds(PAD, T), :] = x_ref[...].astype(F32)
        dc_ref[pl.ds(T, PAD), :] = jnp.zeros((PAD, LANES), F32)
        w = w_ref[...]
        scale = jnp.where(j < H, GDN_DK ** -0.5, 1.0).astype(F32)
        for c in range(T // RC):
            pre = _causal_taps(xp_ref, w, K, c * RC)
            act = _silu(pre)
            d = dn_ref[pl.ds(c * RC, RC), :]
            rs = lax.rsqrt(jnp.sum(act * act, axis=-1, keepdims=True) + L2_EPS)
            nh = act * rs
            dact_n = scale * rs * (d - nh * jnp.sum(d * nh, axis=-1, keepdims=True))
            dact = jnp.where(j < 2 * H, dact_n, d)
            dc_ref[pl.ds(c * RC, RC), :] = dact * _dsilu(pre)
        for c in range(T // RC):
            dx_ref[pl.ds(c * RC, RC), :] = _anticausal_taps(dc_ref, w, K, c * RC).astype(BF16)
        _tap_grads(dw_ref, dc_ref, xp_ref, K, T)

    return _hosted_call(
        body, hosts, name=name, grid=(3 * H,),
        in_specs=[pl.BlockSpec((T, LANES), lambda j: (0, j)), pl.BlockSpec((T, LANES), lambda j: (0, j)),
                  pl.BlockSpec((K, LANES), lambda j: (0, j)), pl.BlockSpec(memory_space=pl.ANY)],
        out_specs=[pl.BlockSpec((T, LANES), lambda j: (0, j)), pl.BlockSpec((K, LANES), lambda j: (0, j))],
        out_shape=[jax.ShapeDtypeStruct(dp.shape, BF16), jax.ShapeDtypeStruct(cw.shape, F32)],
        scratch_shapes=[pltpu.VMEM((T + PAD, LANES), F32), pltpu.VMEM((T + PAD, LANES), F32)],
        args=(dn, p, cw, dp), sem=("arbitrary",), aliases={3: 0})


def _dw_conv_fwd(hc, w, b, name):
    T, D = hc.shape
    K = w.shape[0]

    def body(x_ref, w_ref, b_ref, o_ref, xp_ref):
        xp_ref[pl.ds(0, PAD), :] = jnp.zeros((PAD, LANES), F32)
        xp_ref[pl.ds(PAD, T), :] = x_ref[...]
        wv = w_ref[...]
        for c in range(T // RC):
            o_ref[pl.ds(c * RC, RC), :] = _causal_taps(xp_ref, wv, K, c * RC) + b_ref[...]

    return pl.pallas_call(
        body, name=name, grid=(D // LANES,),
        in_specs=[pl.BlockSpec((T, LANES), lambda j: (0, j)), pl.BlockSpec((K, LANES), lambda j: (0, j)),
                  pl.BlockSpec((1, LANES), lambda j: (0, j))],
        out_specs=pl.BlockSpec((T, LANES), lambda j: (0, j)),
        out_shape=jax.ShapeDtypeStruct((T, D), F32),
        scratch_shapes=[pltpu.VMEM((T + PAD, LANES), F32)],
        compiler_params=_cparams(("arbitrary",)),
    )(hc, w, b)


def _dw_conv_bwd(dy, hc, w, name, hosts=None):
    T, D = hc.shape
    K = w.shape[0]

    def body(dy_ref, x_ref, w_ref, dx_ref, dw_ref, xp_ref, dc_ref):
        xp_ref[pl.ds(0, PAD), :] = jnp.zeros((PAD, LANES), F32)
        xp_ref[pl.ds(PAD, T), :] = x_ref[...]
        dc_ref[pl.ds(T, PAD), :] = jnp.zeros((PAD, LANES), F32)
        dc_ref[pl.ds(0, T), :] = dy_ref[...]
        wv = w_ref[...]
        for c in range(T // RC):
            dx_ref[pl.ds(c * RC, RC), :] = _anticausal_taps(dc_ref, wv, K, c * RC)
        _tap_grads(dw_ref, dc_ref, xp_ref, K, T)

    return _hosted_call(
        body, hosts, name=name, grid=(D // LANES,), sem=("arbitrary",),
        in_specs=[pl.BlockSpec((T, LANES), lambda j: (0, j)), pl.BlockSpec((T, LANES), lambda j: (0, j)),
                  pl.BlockSpec((K, LANES), lambda j: (0, j))],
        out_specs=[pl.BlockSpec((T, LANES), lambda j: (0, j)), pl.BlockSpec((K, LANES), lambda j: (0, j))],
        out_shape=[jax.ShapeDtypeStruct((T, D), F32), jax.ShapeDtypeStruct(w.shape, F32)],
        scratch_shapes=[pltpu.VMEM((T + PAD, LANES), F32), pltpu.VMEM((T + PAD, LANES), F32)],
        args=(dy, hc, w))


NN = (((1,), (0,)), ((), ()))
NT = (((1,), (1,)), ((), ()))
TN = (((0,), (0,)), ((), ()))


def _dotb(a, b, dn=NN):
    return lax.dot_general(a.astype(BF16), b.astype(BF16), dn, preferred_element_type=F32)


def _split_bf16(x, n):
    parts, r = [], x
    for _ in range(n):
        p = r.astype(BF16)
        parts.append(p)
        r = r - p.astype(F32)
    return parts


def _dot_sel(sel, x, pieces, sel_left=True):
    sb = sel.astype(BF16)
    acc = None
    for p in _split_bf16(x, pieces):
        t = (lax.dot_general(sb, p, NN, preferred_element_type=F32) if sel_left
             else lax.dot_general(p, sb, NN, preferred_element_type=F32))
        acc = t if acc is None else acc + t
    return acc


def _iota2(shape, axis):
    return lax.broadcasted_iota(jnp.int32, shape, axis)


def _to_row(col, eye):
    return jnp.sum(jnp.where(eye, col, 0.0), axis=0, keepdims=True)


def _to_col(row, eye):
    return jnp.sum(jnp.where(eye, row, 0.0), axis=1, keepdims=True)


def _gdn_gates(bl, al, alog, dtb):
    beta = _sigmoid(bl)
    x = al + dtb
    sp = jnp.maximum(x, 0.0) + jnp.log(1.0 + jnp.exp(-jnp.abs(x)))
    g = -jnp.exp(alog) * sp
    r, c = _iota2((CHUNK, CHUNK), 0), _iota2((CHUNK, CHUNK), 1)
    G = _dot_sel(r >= c, g, 3)
    return beta, g, G, x


def _unit_lower_inverses(As, Ats):
    n = len(As)
    nb = CHUNK // SUB
    lane = _iota2((SUB, CHUNK), 1)
    row = _iota2((SUB, CHUNK), 0)
    Atp = []
    for At in Ats:
        acc = jnp.zeros((SUB, CHUNK), F32)
        for b in range(nb):
            acc = jnp.where(lane // SUB == b, At[b * SUB:(b + 1) * SUB, :], acc)
        Atp.append(acc)
    gr, gc = _iota2((CHUNK, CHUNK), 0), _iota2((CHUNK, CHUNK), 1)
    ones_bd = gr // SUB == gc // SUB
    stack = jnp.concatenate(
        [jnp.where(lane % SUB == i, Atp[m], 0.0) for i in range(1, SUB) for m in range(n)], axis=0)
    Cm = _dot_sel(ones_bd, stack, 2, sel_left=False)
    Z = [(row == lane % SUB).astype(F32) for _ in range(n)]
    for i in range(1, SUB):
        for m in range(n):
            cm = Cm[((i - 1) * n + m) * SUB:((i - 1) * n + m + 1) * SUB, :]
            new = -jnp.sum(cm * Z[m], axis=0, keepdims=True)
            Z[m] = Z[m] + jnp.where(row == i, new, 0.0)
    bd = gr // SUB == gc // SUB
    Xs = [jnp.where(bd, jnp.concatenate([Z[m]] * nb, axis=0), 0.0) for m in range(n)]
    blk = SUB
    while blk < CHUNK:
        off = (gr // (2 * blk) == gc // (2 * blk)) & (gr // blk != gc // blk)
        Ys = [_dotb(Xs[m], jnp.where(off, As[m], 0.0)) for m in range(n)]
        Xs = [Xs[m] - _dotb(Ys[m], Xs[m]) for m in range(n)]
        blk *= 2
    return Xs


def _gdn_fwd(qkvn, p, alog, dtb, H, name, hosts=None):
    T = qkvn.shape[0]
    D = H * GDN_DK
    N = T // CHUNK
    bblk = 0

    def body(q_ref, k_ref, v_ref, b_ref, a_ref, alog_ref, dtb_ref, o_ref, t_ref, s_ref, S_scr):
        @pl.when(pl.program_id(0) == 0)
        def _():
            S_scr[...] = jnp.zeros_like(S_scr)

        beta, _, G, _ = _gdn_gates(b_ref[...], a_ref[...], alog_ref[...], dtb_ref[...])
        hs = range(H)
        sl = [slice(h * GDN_DK, (h + 1) * GDN_DK) for h in hs]
        q, k, v = [q_ref[:, s] for s in sl], [k_ref[:, s] for s in sl], [v_ref[:, s] for s in sl]
        Gc, bc = [G[:, h:h + 1] for h in hs], [beta[:, h:h + 1] for h in hs]
        r, c = _iota2((CHUNK, CHUNK), 0), _iota2((CHUNK, CHUNK), 1)
        eye, low, up = r == c, r >= c, r <= c
        Gr, br = [_to_row(Gc[h], eye) for h in hs], [_to_row(bc[h], eye) for h in hs]
        Dm = [jnp.where(low, jnp.exp(jnp.where(low, Gc[h] - Gr[h], 0.0)), 0.0) for h in hs]
        Dt = [jnp.where(up, jnp.exp(jnp.where(up, Gr[h] - Gc[h], 0.0)), 0.0) for h in hs]
        qk = [_dotb(jnp.concatenate([q[h], k[h]], axis=0), k[h], NT) for h in hs]
        QK = [qk[h][:CHUNK] * Dm[h] for h in hs]
        KK = [qk[h][CHUNK:] for h in hs]
        As = [jnp.where(r > c, KK[h] * Dm[h], 0.0) * bc[h] for h in hs]
        Ats = [jnp.where(r < c, KK[h] * Dt[h], 0.0) * br[h] for h in hs]
        Ts = _unit_lower_inverses(As, Ats)
        eG = [jnp.exp(Gc[h]) for h in hs]
        Gl = [Gc[h][CHUNK - 1:CHUNK, :] for h in hs]
        uw = [_dotb(Ts[h], jnp.concatenate([v[h] * bc[h], k[h] * (bc[h] * eG[h])], axis=1)) for h in hs]
        S = [S_scr[h] for h in hs]
        qw = [_dotb(jnp.concatenate([q[h] * eG[h], uw[h][:, GDN_DK:]], axis=0), S[h]) for h in hs]
        vn = [uw[h][:, :GDN_DK] - qw[h][CHUNK:] for h in hs]
        o = [qw[h][:CHUNK] + _dotb(QK[h], vn[h]) for h in hs]
        Sn = [S[h] * jnp.exp(Gl[h]) + _dotb(k[h] * jnp.exp(Gl[h] - Gc[h]), vn[h], TN) for h in hs]
        for h in hs:
            t_ref[0, h] = Ts[h]
            s_ref[0, h] = S[h]
            o_ref[:, sl[h]] = o[h]
            S_scr[h] = Sn[h]

    qkv_spec = [pl.BlockSpec((CHUNK, D), lambda n, cb=cb: (n, cb)) for cb in range(3)]
    return _hosted_call(
        body, hosts, name=name, grid=(N,), sem=("arbitrary",),
        in_specs=qkv_spec + [pl.BlockSpec((CHUNK, LANES), lambda n: (n, bblk)),
                             pl.BlockSpec((CHUNK, LANES), lambda n: (n, bblk + 1)),
                             pl.BlockSpec((1, LANES), lambda n: (0, 0)), pl.BlockSpec((1, LANES), lambda n: (0, 0))],
        out_specs=[pl.BlockSpec((CHUNK, D), lambda n: (n, 0)),
                   pl.BlockSpec((1, H, CHUNK, CHUNK), lambda n: (n, 0, 0, 0)),
                   pl.BlockSpec((1, H, GDN_DK, GDN_DK), lambda n: (n, 0, 0, 0))],
        out_shape=[jax.ShapeDtypeStruct((T, D), F32), jax.ShapeDtypeStruct((N, H, CHUNK, CHUNK), F32),
                   jax.ShapeDtypeStruct((N, H, GDN_DK, GDN_DK), F32)],
        scratch_shapes=[pltpu.VMEM((H, GDN_DK, GDN_DK), F32)],
        args=(qkvn, qkvn, qkvn, p, p, alog, dtb))


def _gdn_bwd(do, qkvn, p, alog, dtb, Tinv, Sin, H, name, hosts=None, dp=None):
    T = qkvn.shape[0]
    D = H * GDN_DK
    N = T // CHUNK
    bblk = 0

    def body(do_ref, q_ref, k_ref, v_ref, b_ref, a_ref, alog_ref, dtb_ref, t_ref, s_ref, _,
             dqkv_ref, dba_ref, dalog_ref, ddtb_ref, dS_scr):
        @pl.when(pl.program_id(0) == 0)
        def _():
            dS_scr[...] = jnp.zeros_like(dS_scr)
            dalog_ref[...] = jnp.zeros_like(dalog_ref)
            ddtb_ref[...] = jnp.zeros_like(ddtb_ref)

        beta, g, G, x = _gdn_gates(b_ref[...], a_ref[...], alog_ref[...], dtb_ref[...])
        r, c = _iota2((CHUNK, CHUNK), 0), _iota2((CHUNK, CHUNK), 1)
        eye, low, strict = r == c, r >= c, r > c
        lane = _iota2((CHUNK, LANES), 1)
        rsum1 = lambda a: jnp.sum(a, axis=1, keepdims=True)
        hs = range(H)
        sl = [slice(h * GDN_DK, (h + 1) * GDN_DK) for h in hs]
        q, k, v = [q_ref[:, s] for s in sl], [k_ref[:, s] for s in sl], [v_ref[:, s] for s in sl]
        dov = [do_ref[:, s] for s in sl]
        Gc, bc = [G[:, h:h + 1] for h in hs], [beta[:, h:h + 1] for h in hs]
        Tm, S, dSo = [t_ref[0, h] for h in hs], [s_ref[0, h] for h in hs], [dS_scr[h] for h in hs]
        Gr = [_to_row(Gc[h], eye) for h in hs]
        Dm = [jnp.where(low, jnp.exp(jnp.where(low, Gc[h] - Gr[h], 0.0)), 0.0) for h in hs]
        eG = [jnp.exp(Gc[h]) for h in hs]
        Gl = [Gc[h][CHUNK - 1:CHUNK, :] for h in hs]
        eR, dch = [jnp.exp(Gl[h] - Gc[h]) for h in hs], [jnp.exp(Gl[h]) for h in hs]
        qk = [_dotb(jnp.concatenate([q[h], k[h]], axis=0), k[h], NT) for h in hs]
        QKr, KK = [qk[h][:CHUNK] for h in hs], [qk[h][CHUNK:] for h in hs]
        QK = [QKr[h] * Dm[h] for h in hs]
        M = [jnp.where(strict, KK[h] * Dm[h], 0.0) for h in hs]
        uw = [_dotb(Tm[h], jnp.concatenate([v[h] * bc[h], k[h] * (bc[h] * eG[h])], axis=1)) for h in hs]
        u, w = [uw[h][:, :GDN_DK] for h in hs], [uw[h][:, GDN_DK:] for h in hs]
        qd, kd = [q[h] * eG[h] for h in hs], [k[h] * eR[h] for h in hs]
        vn = [u[h] - _dotb(w[h], S[h]) for h in hs]
        dvn = [_dotb(QK[h], dov[h], TN) + _dotb(kd[h], dSo[h]) for h in hs]
        dQK = [jnp.where(low, _dotb(dov[h], vn[h], NT), 0.0) for h in hs]
        dkd = [_dotb(vn[h], dSo[h], NT) for h in hs]
        ddch = [jnp.sum(rsum1(dSo[h] * S[h]), axis=0, keepdims=True) for h in hs]
        dd = [jnp.concatenate([dov[h], dvn[h]], axis=0) for h in hs]
        xs = [_dotb(dd[h], S[h], NT) for h in hs]
        dqd, dw = [xs[h][:CHUNK] for h in hs], [-xs[h][CHUNK:] for h in hs]
        dS = [_dotb(jnp.concatenate([qd[h], -w[h]], axis=0), dd[h], TN) + dch[h] * dSo[h] for h in hs]
        yb = [_dotb(Tm[h], jnp.concatenate([dvn[h], dw[h]], axis=1), TN) for h in hs]
        dvb, dkb = [yb[h][:, :GDN_DK] for h in hs], [yb[h][:, GDN_DK:] for h in hs]
        dA = [-jnp.where(strict, _dotb(yb[h], uw[h], NT), 0.0) for h in hs]
        rk = [rsum1(dkb[h] * k[h]) for h in hs]
        dbeta = [rsum1(dvb[h] * v[h]) + rk[h] * eG[h] + rsum1(dA[h] * M[h]) for h in hs]
        dM = [dA[h] * bc[h] for h in hs]
        dKK = [dM[h] * Dm[h] for h in hs]
        dQKr = [dQK[h] * Dm[h] for h in hs]
        E = [dM[h] * M[h] + dQK[h] * QK[h] for h in hs]
        zk = [_dotb(jnp.concatenate([dQKr[h], dKK[h]], axis=0), k[h]) for h in hs]
        dq = [zk[h][:CHUNK] + dqd[h] * eG[h] for h in hs]
        dk = [dkb[h] * (bc[h] * eG[h]) + zk[h][CHUNK:] + _dotb(dKK[h], k[h], TN) + _dotb(dQKr[h], q[h], TN)
              + dkd[h] * eR[h] for h in hs]
        deG = [rk[h] * bc[h] + rsum1(dqd[h] * q[h]) for h in hs]
        deR = [rsum1(dkd[h] * k[h]) for h in hs]
        dGl = [jnp.sum(deR[h] * eR[h], axis=0, keepdims=True) + ddch[h] * dch[h] for h in hs]
        dGc = [rsum1(E[h]) - _to_col(jnp.sum(E[h], axis=0, keepdims=True), eye) + deG[h] * eG[h] - deR[h] * eR[h]
               + jnp.where(r[:, :1] == CHUNK - 1, dGl[h], 0.0) for h in hs]
        dG_all = jnp.zeros((CHUNK, LANES), F32)
        dbeta_all = jnp.zeros((CHUNK, LANES), F32)
        for h in hs:
            dS_scr[h] = dS[h]
            dqkv_ref[:, sl[h]] = dq[h]
            dqkv_ref[:, D + h * GDN_DK:D + (h + 1) * GDN_DK] = dk[h]
            dqkv_ref[:, 2 * D + h * GDN_DK:2 * D + (h + 1) * GDN_DK] = dvb[h] * bc[h]
            dG_all = jnp.where(lane == h, dGc[h], dG_all)
            dbeta_all = jnp.where(lane == h, dbeta[h], dbeta_all)
        dg = _dot_sel(r <= c, dG_all, 3)
        da = dg * (-jnp.exp(alog_ref[...])) * _sigmoid(x)
        dba_ref[:, :LANES] = (dbeta_all * beta * (1.0 - beta)).astype(BF16)
        dba_ref[:, LANES:] = da.astype(BF16)
        dalog_ref[...] += _rsum(dg * g)
        ddtb_ref[...] += _rsum(da)

    rev = lambda n: N - 1 - n
    qkv_spec = [pl.BlockSpec((CHUNK, D), lambda n, cb=cb: (rev(n), cb)) for cb in range(3)]
    return _hosted_call(
        body, hosts, name=name, grid=(N,), sem=("arbitrary",),
        in_specs=[pl.BlockSpec((CHUNK, D), lambda n: (rev(n), 0))] + qkv_spec + [
            pl.BlockSpec((CHUNK, LANES), lambda n: (rev(n), bblk)),
            pl.BlockSpec((CHUNK, LANES), lambda n: (rev(n), bblk + 1)),
            pl.BlockSpec((1, LANES), lambda n: (0, 0)), pl.BlockSpec((1, LANES), lambda n: (0, 0)),
            pl.BlockSpec((1, H, CHUNK, CHUNK), lambda n: (rev(n), 0, 0, 0)),
            pl.BlockSpec((1, H, GDN_DK, GDN_DK), lambda n: (rev(n), 0, 0, 0)),
            pl.BlockSpec(memory_space=pl.ANY)],
        out_specs=[pl.BlockSpec((CHUNK, 3 * D), lambda n: (rev(n), 0)),
                   pl.BlockSpec((CHUNK, 2 * LANES), lambda n: (rev(n), 8 * D // (2 * LANES))),
                   pl.BlockSpec((1, LANES), lambda n: (0, 0)), pl.BlockSpec((1, LANES), lambda n: (0, 0))],
        out_shape=[jax.ShapeDtypeStruct((T, 3 * D), F32), jax.ShapeDtypeStruct(dp.shape, BF16),
                   jax.ShapeDtypeStruct((1, LANES), F32), jax.ShapeDtypeStruct((1, LANES), F32)],
        scratch_shapes=[pltpu.VMEM((H, GDN_DK, GDN_DK), F32)],
        args=(do, qkvn, qkvn, qkvn, p, p, alog, dtb, Tinv, Sin, dp), aliases={10: 1})


def _mix_in_reorder(wt, D, H):
    o1 = 4 * D
    o2, o3 = o1 + H, o1 + 2 * H
    z = jnp.zeros((LANES - H, wt.shape[1]), wt.dtype)
    return jnp.concatenate([wt[:o1], wt[o3:], wt[o1:o2], z, wt[o2:o3], z], axis=0)


def _mix_in_restore(dwt, D, H):
    b0 = 8 * D
    return jnp.concatenate([dwt[:4 * D], dwt[b0:b0 + H], dwt[b0 + LANES:b0 + LANES + H], dwt[4 * D:b0]], axis=0)


def _ffn_fwd(x, W, pre, tag, hosts=None):
    h = _rms_fwd(x, W[pre + "_norm_pre"], tag + "_pre")
    a = _matmul(h, W[pre + "_w_in"], "nt", BF16, tag + "_in", hosts)
    s = _swiglu_fwd(a, tag + "_act")
    f = _matmul(s, W[pre + "_w_out"], "nn", F32, tag + "_out", hosts)
    return _post_fwd(x, f, W[pre + "_norm_post"], 0.5, tag + "_post"), (x, h, a, s, f)


def _ffn_bwd(dxn, saved, W, pre, tag, g, hosts=None):
    x, h, a, s, f = saved
    df, g[pre + "_norm_post"] = _post_bwd(dxn, f, W[pre + "_norm_post"], 0.5, tag + "_dpost")
    ds = _matmul(df, W[pre + "_w_out"], "nt", BF16, tag + "_ds", hosts)
    g[pre + "_w_out"] = _matmul(s, df, "tn", BF16, tag + "_dwout")
    da = _swiglu_bwd(ds, a, tag + "_dact", hosts)
    g[pre + "_w_in"] = _matmul(da, h, "tn", BF16, tag + "_dwin", hosts)
    dh = _matmul(da, W[pre + "_w_in"], "nn", F32, tag + "_dh", hosts)
    dx, g[pre + "_norm_pre"] = _pre_bwd(dh, x, W[pre + "_norm_pre"], dxn, tag + "_dpre", hosts)
    return dx


def _mix_fwd(x, W, H, tag, hosts=None):
    h = _rms_fwd(x, W["mix_norm_pre"], tag + "_pre")
    D = x.shape[1]
    p = _matmul(h, W["mix_w_in"], "nt", BF16, tag + "_in", hosts, rows=(0, 8 * D))
    pba = _matmul(h, W["mix_w_in"], "nt", F32, tag + "_inba", rows=(8 * D, 2 * LANES))
    qkvn = _qkv_conv_fwd(p, W["gdn_conv_w"], H, tag + "_qkvconv", hosts)
    o, Tinv, Sin = _gdn_fwd(qkvn, pba, W["gdn_a_log"], W["gdn_dt_bias"], H, tag + "_gdn", hosts)
    og = _gdn_gate_fwd(o, p, W["gdn_norm_w"], tag + "_gdngate")
    ya = _matmul(og, W["gdn_w_o"], "nn", F32, tag + "_gdno")
    hc = _glu_fwd(p, W["cnv_pw1_b"], tag + "_glu")
    hcv = _dw_conv_fwd(hc, W["cnv_dw_w"], W["cnv_dw_b"], tag + "_dwconv")
    hl = _ln_silu_fwd(hcv, W["cnv_ln_g"], W["cnv_ln_b"], tag + "_ln")
    yb = _matmul(hl, W["cnv_w_o"], "nn", F32, tag + "_cnvo")
    ym = _merge_fwd(p, ya, yb, W["cnv_b_o"], tag + "_merge")
    y = _matmul(ym, W["mix_w_out"], "nn", F32, tag + "_out")
    xn = _post_fwd(x, y, W["mix_norm_post"], 1.0, tag + "_post")
    return xn, (x, h, p, pba, qkvn, o, Tinv, Sin, og, ya, hc, hcv, hl, yb, ym, y)


def _mix_bwd(dxn, saved, W, H, tag, g, hosts=None):
    x, h, p, pba, qkvn, o, Tinv, Sin, og, ya, hc, hcv, hl, yb, ym, y = saved
    dy, g["mix_norm_post"] = _post_bwd(dxn, y, W["mix_norm_post"], 1.0, tag + "_dpost")
    dym = _matmul(dy, W["mix_w_out"], "nt", F32, tag + "_dym")
    g["mix_w_out"] = _matmul(ym, dy, "tn", BF16, tag + "_dwout")
    dya, dyb, dp, g["cnv_b_o"] = _merge_bwd(dym, p, ya, yb, W["cnv_b_o"], tag + "_dmerge", hosts,
                                            dp_width=p.shape[1] + 2 * LANES)
    dhl = _matmul(dyb, W["cnv_w_o"], "nt", F32, tag + "_dhl")
    g["cnv_w_o"] = _matmul(hl, dyb, "tn", BF16, tag + "_dwcnvo")
    dhcv, g["cnv_ln_g"], g["cnv_ln_b"], g["cnv_dw_b"] = _ln_silu_bwd(
        dhl, hcv, W["cnv_ln_g"], W["cnv_ln_b"], tag + "_dln", hosts)
    dhc, g["cnv_dw_w"] = _dw_conv_bwd(dhcv, hc, W["cnv_dw_w"], tag + "_ddwconv", hosts)
    dp, g["cnv_pw1_b"] = _glu_bwd(dhc, p, W["cnv_pw1_b"], tag + "_dglu", dp)
    dog = _matmul(dya, W["gdn_w_o"], "nt", F32, tag + "_dog")
    g["gdn_w_o"] = _matmul(og, dya, "tn", BF16, tag + "_dwgdno")
    do, dp, g["gdn_norm_w"] = _gdn_gate_bwd(dog, o, p, W["gdn_norm_w"], tag + "_dgdngate", hosts, dp)
    dqkvn, dp, g["gdn_a_log"], g["gdn_dt_bias"] = _gdn_bwd(
        do, qkvn, pba, W["gdn_a_log"], W["gdn_dt_bias"], Tinv, Sin, H, tag + "_dgdn", hosts, dp)
    dp, g["gdn_conv_w"] = _qkv_conv_bwd(dqkvn, p, W["gdn_conv_w"], H, tag + "_dqkvconv", hosts, dp)
    dh = _matmul(dp, W["mix_w_in"], "nn", F32, tag + "_dh", hosts)
    g["mix_w_in"] = _matmul(dp, h, "tn", F32, tag + "_dwin")
    dx, g["mix_norm_pre"] = _pre_bwd(dh, x, W["mix_norm_pre"], dxn, tag + "_dpre", hosts)
    return dx


def _trunk_fwd_bwd(x, tgt, H, L, weights_of, grads, hosts=None):
    saved, Ws = [], []
    for i in range(L):
        W = weights_of(i)
        Ws.append(W)
        x, s1 = _ffn_fwd(x, W, "ffn1", f"l{i}_ffn1", hosts)
        x, s2 = _mix_fwd(x, W, H, f"l{i}_mix", hosts)
        x, s3 = _ffn_fwd(x, W, "ffn2", f"l{i}_ffn2", hosts)
        saved.append((s1, s2, s3))
    dx, loss = _loss_fwd_bwd(x, tgt, "loss")
    for i in reversed(range(L)):
        s1, s2, s3 = saved[i]
        dx = _ffn_bwd(dx, s3, Ws[i], "ffn2", f"l{i}_ffn2", grads[i], hosts)
        dx = _mix_bwd(dx, s2, Ws[i], H, f"l{i}_mix", grads[i], hosts)
        dx = _ffn_bwd(dx, s1, Ws[i], "ffn1", f"l{i}_ffn1", grads[i], hosts)
    return loss, dx


HBM_SPEC = pl.BlockSpec(memory_space=pltpu.HBM)


def _coords():
    return lax.axis_index("x"), lax.axis_index("y"), lax.axis_index("c")


class _GatherPlan:
    has_middle = True

    def __init__(self, shards):
        self.n = len(shards)
        self.out_shape = [jax.ShapeDtypeStruct((N_DEV,) + s.shape, s.dtype) for s in shards]
        self.sems = [pltpu.SemaphoreType.DMA((self.n, 7)), pltpu.SemaphoreType.DMA((self.n, 7)),
                     pltpu.SemaphoreType.DMA((self.n,))]

    def _parts(self, ins, outs, sems):
        send_sems, recv_sems, local_sems = sems
        x, y, c = _coords()
        me, sibling = (x, y, c), (x, y, 1 - c)
        chips = [(1 - x, y), (x, 1 - y), (1 - x, 1 - y)]

        def copy(w, k, block, to, src=None):
            dst = outs[w].at[4 * block[0] + 2 * block[1] + block[2]]
            return pltpu.make_async_remote_copy(
                src_ref=dst if src is None else src, dst_ref=dst, send_sem=send_sems.at[w, k],
                recv_sem=recv_sems.at[w, k], device_id=to, device_id_type=MESH)

        mine = [pltpu.make_async_copy(ins[w], outs[w].at[4 * x + 2 * y + c], local_sems.at[w]) for w in range(self.n)]
        first = []
        for w in range(self.n):
            first.append(copy(w, 0, me, sibling, src=ins[w]))
            first += [copy(w, 1 + j, me, (*chip, c), src=ins[w]) for j, chip in enumerate(chips)]
        passed = [copy(w, 4 + j, (*chip, c), sibling) for j, chip in enumerate(chips) for w in range(self.n)]
        return copy, mine, first, passed, chips, me, sibling, c

    def begin(self, ins, outs, sems):
        _, mine, first, _, _, _, _, _ = self._parts(ins, outs, sems)
        for cp in mine + first:
            cp.start()

    def middle(self, ins, outs, sems):
        copy, _, _, passed, chips, me, _, c = self._parts(ins, outs, sems)
        for j, chip in enumerate(chips):
            for w in range(self.n):
                copy(w, 1 + j, (*chip, c), me).wait_recv()
                passed[j * self.n + w].start()

    def finish(self, ins, outs, sems):
        copy, mine, first, passed, chips, me, sibling, c = self._parts(ins, outs, sems)
        for w in range(self.n):
            copy(w, 0, sibling, me).wait_recv()
            for j, chip in enumerate(chips):
                copy(w, 4 + j, (*chip, 1 - c), me).wait_recv()
        for cp in first + passed:
            cp.wait_send()
        for cp in mine:
            cp.wait()


class _SiblingPlan:
    has_middle = False

    def __init__(self, Gs):
        self.n = len(Gs)
        self.out_shape = [jax.ShapeDtypeStruct((4,) + g.shape[1:], g.dtype) for g in Gs]
        self.sems = [pltpu.SemaphoreType.DMA((self.n, 4)), pltpu.SemaphoreType.DMA((self.n, 4))]

    def _copies(self, ins, outs, sems):
        send_sems, recv_sems = sems
        x, y, c = _coords()
        return [pltpu.make_async_remote_copy(
            src_ref=ins[w].at[2 * q + (1 - c)], dst_ref=outs[w].at[q], send_sem=send_sems.at[w, q],
            recv_sem=recv_sems.at[w, q], device_id=(x, y, 1 - c), device_id_type=MESH)
            for w in range(self.n) for q in range(4)]

    def begin(self, ins, outs, sems):
        for cp in self._copies(ins, outs, sems):
            cp.start()

    def finish(self, ins, outs, sems):
        for cp in self._copies(ins, outs, sems):
            cp.wait()


class _ChipsPlan:
    has_middle = False

    def __init__(self, Ps):
        self.n = len(Ps)
        self.out_shape = [jax.ShapeDtypeStruct(p.shape, p.dtype) for p in Ps]
        self.sems = [pltpu.SemaphoreType.DMA((self.n, 3)), pltpu.SemaphoreType.DMA((self.n, 3)),
                     pltpu.SemaphoreType.DMA((self.n,))]

    def _copies(self, ins, outs, sems):
        send_sems, recv_sems, local_sems = sems
        x, y, c = _coords()
        me_q = 2 * x + y
        cps = []
        for w in range(self.n):
            cps.append(pltpu.make_async_copy(ins[w].at[me_q], outs[w].at[me_q], local_sems.at[w]))
            for j, (px, py) in enumerate([(1 - x, y), (x, 1 - y), (1 - x, 1 - y)]):
                cps.append(pltpu.make_async_remote_copy(
                    src_ref=ins[w].at[2 * px + py], dst_ref=outs[w].at[me_q], send_sem=send_sems.at[w, j],
                    recv_sem=recv_sems.at[w, j], device_id=(px, py, c), device_id_type=MESH))
        return cps

    def begin(self, ins, outs, sems):
        for cp in self._copies(ins, outs, sems):
            cp.start()

    def finish(self, ins, outs, sems):
        for cp in self._copies(ins, outs, sems):
            cp.wait()


def _comm_only(plan, arrays, name):
    n = plan.n

    def body(*refs):
        ins, outs, sems = refs[:n], refs[n:2 * n], refs[2 * n:]
        plan.begin(ins, outs, sems)
        if plan.has_middle:
            plan.middle(ins, outs, sems)
        plan.finish(ins, outs, sems)

    return pl.pallas_call(
        body, name=name, out_shape=plan.out_shape, in_specs=[HBM_SPEC] * n, out_specs=[HBM_SPEC] * n,
        scratch_shapes=plan.sems,
    )(*arrays)


class _MultiPlan:
    def __init__(self, plans):
        self.plans = plans
        self.n = sum(p.n for p in plans)
        self.out_shape = [s for p in plans for s in p.out_shape]
        self.sems = [s for p in plans for s in p.sems]
        self.has_middle = any(p.has_middle for p in plans)

    def _each(self, phase, ins, outs, sems):
        a = s = 0
        for p in self.plans:
            if phase != "middle" or p.has_middle:
                getattr(p, phase)(ins[a:a + p.n], outs[a:a + p.n], sems[s:s + len(p.sems)])
            a, s = a + p.n, s + len(p.sems)

    def begin(self, ins, outs, sems):
        self._each("begin", ins, outs, sems)

    def middle(self, ins, outs, sems):
        self._each("middle", ins, outs, sems)

    def finish(self, ins, outs, sems):
        self._each("finish", ins, outs, sems)


class _Hosts:
    def __init__(self):
        self.waiting = {}

    def add(self, host, make):
        self.waiting.setdefault(host, []).append(make)

    def take(self, host):
        makes = self.waiting.pop(host, None)
        if not makes:
            return None
        items = [m() for m in makes]
        return _MultiPlan([it[0] for it in items]), [a for it in items for a in it[1]], [it[2] for it in items]


def _hosted_call(body, hosts, *, name, grid, in_specs, out_specs, out_shape, scratch_shapes, args, sem, aliases=None):
    comm = hosts.take(name) if hosts is not None else None
    if comm is None:
        return pl.pallas_call(body, name=name, grid=grid, in_specs=in_specs, out_specs=out_specs, out_shape=out_shape,
                              scratch_shapes=scratch_shapes, input_output_aliases=aliases or {},
                              compiler_params=_cparams(sem))(*args)
    plan, arrays, deliver = comm
    n_in, n_out, n_scr, n = len(in_specs), len(out_specs), len(scratch_shapes), plan.n
    total = 1
    for g in grid:
        total *= g

    def kern(*refs):
        ins, cins = refs[:n_in], refs[n_in:n_in + n]
        outs, couts = refs[n_in + n:n_in + n + n_out], refs[n_in + n + n_out:n_in + 2 * n + n_out]
        scr, csems = refs[n_in + 2 * n + n_out:n_in + 2 * n + n_out + n_scr], refs[n_in + 2 * n + n_out + n_scr:]
        step = pl.program_id(0)
        for d in range(1, len(grid)):
            step = step * grid[d] + pl.program_id(d)

        @pl.when(step == 0)
        def _():
            plan.begin(cins, couts, csems)

        body(*ins, *outs, *scr)
        if plan.has_middle:
            @pl.when(step == (3 * total) // 4)
            def _():
                plan.middle(cins, couts, csems)

        @pl.when(step == total - 1)
        def _():
            plan.finish(cins, couts, csems)

    res = pl.pallas_call(
        kern, name=name, grid=grid, in_specs=list(in_specs) + [HBM_SPEC] * n,
        out_specs=list(out_specs) + [HBM_SPEC] * n, out_shape=list(out_shape) + plan.out_shape,
        scratch_shapes=list(scratch_shapes) + plan.sems, input_output_aliases=aliases or {},
        compiler_params=_cparams(("arbitrary",) * len(grid)),
    )(*args, *arrays)
    k = n_out
    for p, d in zip(plan.plans, deliver):
        d(res[k:k + p.n])
        k += p.n
    return res[:n_out]


def _row_tile(R, target=256):
    best = None
    for t in range(8, min(R, target) + 1, 8):
        if R % t == 0:
            best = t
    return best if best is not None else R


def _pair_add(G, R1, cidx, name):
    _, R, C = G.shape
    tb = _row_tile(R, ROW_BLOCK)

    def body(c_ref, g_ref, r_ref, o_ref):
        o_ref[...] = (g_ref[...].astype(F32) + r_ref[...].astype(F32)).astype(BF16)

    return pl.pallas_call(
        body, name=name,
        grid_spec=pltpu.PrefetchScalarGridSpec(
            num_scalar_prefetch=1, grid=(4, R // tb),
            in_specs=[pl.BlockSpec((None, tb, C), lambda q, i, cr: (2 * q + cr[0], i, 0)),
                      pl.BlockSpec((None, tb, C), lambda q, i, cr: (q, i, 0))],
            out_specs=pl.BlockSpec((None, tb, C), lambda q, i, cr: (q, i, 0))),
        out_shape=jax.ShapeDtypeStruct((4, R, C), BF16),
        compiler_params=_cparams(("arbitrary", "arbitrary")),
    )(cidx, G, R1)


def _sum_parts(parts, name):
    P, R, C = parts.shape

    def body(p_ref, o_ref):
        acc = p_ref[0]
        for j in range(1, P):
            acc = acc + p_ref[j]
        o_ref[...] = acc

    return pl.pallas_call(
        body, name=name, out_shape=jax.ShapeDtypeStruct((R, C), F32),
        in_specs=[pl.BlockSpec(memory_space=pltpu.VMEM)], out_specs=pl.BlockSpec(memory_space=pltpu.VMEM),
        compiler_params=_cparams(),
    )(parts)


def _adamw(w, m, v, parts, name):
    G, R, C = w.shape
    P = parts[0].shape[0]
    tb = _row_tile(R, ROW_BLOCK)
    nb = R // tb
    c1 = 1.0 / (1.0 - ADAM_B1 ** ADAM_STEP)
    c2 = 1.0 / (1.0 - ADAM_B2 ** ADAM_STEP)

    def body(w_ref, m_ref, v_ref, *rest):
        p_refs, (g_ref, d_ref, nm_ref, nv_ref) = rest[:G], rest[G:]
        l = pl.program_id(0)
        g = None
        for k in range(G):
            gk = p_refs[k][0].astype(F32)
            for j in range(1, P):
                gk = gk + p_refs[k][j].astype(F32)
            g = gk if g is None else jnp.where(l == k, gk, g)
        nm = ADAM_B1 * m_ref[...] + (1.0 - ADAM_B1) * g
        nv = ADAM_B2 * v_ref[...] + (1.0 - ADAM_B2) * (g * g)
        g_ref[...] = g
        nm_ref[...] = nm
        nv_ref[...] = nv
        d_ref[...] = -ADAM_LR * ((nm * c1) / (jnp.sqrt(nv * c2) + ADAM_EPS) + ADAM_WD * w_ref[...])

    blk = pl.BlockSpec((None, tb, C), lambda l, i: (l, i, 0))

    def part_spec(k):
        return pl.BlockSpec((P, tb, C), lambda l, i: (0, jnp.where(l < k, 0, jnp.where(l > k, nb - 1, i)), 0))

    return pl.pallas_call(
        body, name=name, grid=(G, nb),
        in_specs=[blk, blk, blk] + [part_spec(k) for k in range(G)],
        out_specs=[blk] * 4, out_shape=[jax.ShapeDtypeStruct((G, R, C), F32)] * 4,
        compiler_params=_cparams(("arbitrary", "arbitrary")),
    )(w, m, v, *parts)


BIG = ("ffn1_w_in", "ffn1_w_out", "mix_w_in", "gdn_w_o", "cnv_w_o", "mix_w_out", "ffn2_w_in", "ffn2_w_out")
COL_SHARDED = ("ffn1_w_in", "mix_w_in", "ffn2_w_in")
SMALL_SHARDED = ("gdn_conv_w", "cnv_dw_w")
NAMES = ("ffn1_norm_pre", "ffn1_norm_post", "ffn1_w_in", "ffn1_w_out", "mix_norm_pre", "mix_norm_post", "mix_w_in",
         "gdn_conv_w", "gdn_a_log", "gdn_dt_bias", "gdn_norm_w", "gdn_w_o", "cnv_pw1_b", "cnv_dw_w", "cnv_dw_b",
         "cnv_ln_g", "cnv_ln_b", "cnv_w_o", "cnv_b_o", "mix_w_out", "ffn2_norm_pre", "ffn2_norm_post", "ffn2_w_in",
         "ffn2_w_out")
SMALL = tuple(n for n in NAMES if n not in BIG)


class _LayerWeights:
    def __init__(self, got, params, i, D, H):
        self.got, self.params, self.i, self.D, self.H, self.made = got, params, i, D, H, {}

    def __getitem__(self, n):
        if n not in self.made:
            if n in BIG:
                g = self.got[(self.i, n)]
                g = g.reshape(-1, g.shape[-1])
                w = _mix_in_reorder(g, self.D, self.H) if n == "mix_w_in" else g
            elif n in SMALL_SHARDED:
                g = self.got[(self.i, n)]
                w = jnp.transpose(g, (1, 0, 2)).reshape(g.shape[1], -1)
            else:
                v = self.params[n][self.i]
                if n in ("gdn_a_log", "gdn_dt_bias"):
                    v = jnp.pad(v, (0, LANES - self.H))
                w = v.reshape(1, -1)
            self.made[n] = w
        return self.made[n]


MIX_SMALL = ("gdn_w_o", "cnv_w_o", "mix_w_out", "gdn_conv_w", "cnv_dw_w")
GATHER_HOSTS = (("l{j}_mix_in", ("ffn1_w_in", "ffn1_w_out")), ("l{j}_mix_qkvconv", MIX_SMALL),
                ("l{j}_ffn2_in", ("mix_w_in",)), ("l{i}_mix_gdn", ("ffn2_w_in", "ffn2_w_out")))
GATHER_HOSTS_FIRST = ((None, ("ffn1_w_in", "ffn1_w_out")), ("l0_ffn1_in", ("mix_w_in",)), ("l0_ffn1_out", MIX_SMALL),
                      ("l0_mix_gdn", ("ffn2_w_in", "ffn2_w_out")))
REDUCE_HOSTS = ((BIG, "l{j}_ffn2_ds", (("l{j}_ffn2_dh", ("ffn2_w_in",)),
                                       ("l{j}_ffn2_dwin", ("ffn2_w_out", "gdn_w_o", "cnv_w_o", "mix_w_out")),
                                       ("l{j}_mix_ddwconv", ("ffn1_w_in",)),
                                       ("l{j}_mix_dgdn", ("mix_w_in", "ffn1_w_out")))),)
REDUCE_HOSTS_FIRST = (
    (("ffn2_w_in", "ffn2_w_out"), "l0_mix_dmerge", (("l0_mix_dln", ("ffn2_w_out",)), ("l0_mix_dqkvconv", ("ffn2_w_in",)))),
    (("mix_w_out", "cnv_w_o", "gdn_w_o"), "l0_mix_dgdngate", (("l0_mix_dh", ("mix_w_out", "cnv_w_o", "gdn_w_o")),)),
    (("mix_w_in",), "l0_mix_dpre", (("l0_ffn1_dwin", ("mix_w_in",)),)),
    (("ffn1_w_out",), "l0_ffn1_dact", (("l0_ffn1_dh", ("ffn1_w_out",)),)),
    (("ffn1_w_in",), "l0_ffn1_dh", (("l0_ffn1_dpre", ("ffn1_w_in",)),)))


def kernel(x, ffn1_norm_pre, ffn1_norm_post, ffn1_w_in, ffn1_w_out, mix_norm_pre, mix_norm_post, mix_w_in, gdn_conv_w, gdn_a_log, gdn_dt_bias, gdn_norm_w, gdn_w_o, cnv_pw1_b, cnv_dw_w, cnv_dw_b, cnv_ln_g, cnv_ln_b, cnv_w_o, cnv_b_o, mix_w_out, ffn2_norm_pre, ffn2_norm_post, ffn2_w_in, ffn2_w_out, loss_target, m_ffn1_norm_pre, m_ffn1_norm_post, m_ffn1_w_in, m_ffn1_w_out, m_mix_norm_pre, m_mix_norm_post, m_mix_w_in, m_gdn_conv_w, m_gdn_a_log, m_gdn_dt_bias, m_gdn_norm_w, m_gdn_w_o, m_cnv_pw1_b, m_cnv_dw_w, m_cnv_dw_b, m_cnv_ln_g, m_cnv_ln_b, m_cnv_w_o, m_cnv_b_o, m_mix_w_out, m_ffn2_norm_pre, m_ffn2_norm_post, m_ffn2_w_in, m_ffn2_w_out, v_ffn1_norm_pre, v_ffn1_norm_post, v_ffn1_w_in, v_ffn1_w_out, v_mix_norm_pre, v_mix_norm_post, v_mix_w_in, v_gdn_conv_w, v_gdn_a_log, v_gdn_dt_bias, v_gdn_norm_w, v_gdn_w_o, v_cnv_pw1_b, v_cnv_dw_w, v_cnv_dw_b, v_cnv_ln_g, v_cnv_ln_b, v_cnv_w_o, v_cnv_b_o, v_mix_w_out, v_ffn2_norm_pre, v_ffn2_norm_post, v_ffn2_w_in, v_ffn2_w_out):
    params = dict(zip(NAMES, (ffn1_norm_pre, ffn1_norm_post, ffn1_w_in, ffn1_w_out, mix_norm_pre, mix_norm_post, mix_w_in, gdn_conv_w, gdn_a_log, gdn_dt_bias, gdn_norm_w, gdn_w_o, cnv_pw1_b, cnv_dw_w, cnv_dw_b, cnv_ln_g, cnv_ln_b, cnv_w_o, cnv_b_o, mix_w_out, ffn2_norm_pre, ffn2_norm_post, ffn2_w_in, ffn2_w_out)))
    mom1 = dict(zip(NAMES, (m_ffn1_norm_pre, m_ffn1_norm_post, m_ffn1_w_in, m_ffn1_w_out, m_mix_norm_pre, m_mix_norm_post, m_mix_w_in, m_gdn_conv_w, m_gdn_a_log, m_gdn_dt_bias, m_gdn_norm_w, m_gdn_w_o, m_cnv_pw1_b, m_cnv_dw_w, m_cnv_dw_b, m_cnv_ln_g, m_cnv_ln_b, m_cnv_w_o, m_cnv_b_o, m_mix_w_out, m_ffn2_norm_pre, m_ffn2_norm_post, m_ffn2_w_in, m_ffn2_w_out)))
    mom2 = dict(zip(NAMES, (v_ffn1_norm_pre, v_ffn1_norm_post, v_ffn1_w_in, v_ffn1_w_out, v_mix_norm_pre, v_mix_norm_post, v_mix_w_in, v_gdn_conv_w, v_gdn_a_log, v_gdn_dt_bias, v_gdn_norm_w, v_gdn_w_o, v_cnv_pw1_b, v_cnv_dw_w, v_cnv_dw_b, v_cnv_ln_g, v_cnv_ln_b, v_cnv_w_o, v_cnv_b_o, v_mix_w_out, v_ffn2_norm_pre, v_ffn2_norm_post, v_ffn2_w_in, v_ffn2_w_out)))
    T, D = x.shape[1], x.shape[2]
    H = D // GDN_DK
    L = ffn1_norm_pre.shape[0]
    xi, yi, ci = _coords()
    dev = 4 * xi + 2 * yi + ci

    ag_names = BIG + SMALL_SHARDED

    def shard_to_send(n):
        if n in COL_SHARDED:
            return jnp.swapaxes(params[n], 1, 2).astype(BF16)
        return params[n].astype(BF16) if n in BIG else params[n]

    send = {n: shard_to_send(n) for n in ag_names}

    hosts, got, later = _Hosts(), {}, []
    for i in range(L):
        for host, names in (GATHER_HOSTS_FIRST if i == 0 else GATHER_HOSTS):
            def make(i=i, names=names):
                blocks = [send[n][i] for n in names]

                def deliver(outs):
                    got.update({(i, n): o for n, o in zip(names, outs)})
                return _GatherPlan(blocks), blocks, deliver
            if host is None:
                plan, blocks, deliver = make()
                deliver(_comm_only(plan, blocks, f"ag_weights_l{i}"))
            else:
                hosts.add(host.format(i=i, j=i - 1), make)

    cidx = jnp.reshape(ci, (1,)).astype(jnp.int32)
    grads, reduced = [{} for _ in range(L)], {}
    for i in range(L):
        for names, sib_host, chip_hosts in (REDUCE_HOSTS_FIRST if i == 0 else REDUCE_HOSTS):
            stage = {}

            def make_sib(i=i, names=names, stage=stage):
                Gs = []
                for n in names:
                    g = _mix_in_restore(grads[i][n], D, H) if n == "mix_w_in" else grads[i][n]
                    Gs.append(g.reshape(N_DEV, -1, g.shape[-1]))
                stage["G"] = dict(zip(names, Gs))

                def deliver(outs):
                    stage["R1"] = dict(zip(names, outs))
                return _SiblingPlan(Gs), Gs, deliver

            def make_chips(ns, i=i, stage=stage):
                Ps = [_pair_add(stage["G"][n], stage["R1"][n], cidx, f"l{i}_pair_add_{n}") for n in ns]

                def deliver(outs):
                    reduced.update({(i, n): o for n, o in zip(ns, outs)})
                return _ChipsPlan(Ps), Ps, deliver

            if sib_host is None:
                later.append((f"l{i}_{names[0]}", make_sib, [(lambda ns=ns, mc=make_chips: mc(ns)) for _, ns in chip_hosts]))
                continue
            hosts.add(sib_host.format(i=i, j=i - 1), make_sib)
            for host, ns in chip_hosts:
                hosts.add(host.format(i=i, j=i - 1), lambda ns=ns, mc=make_chips: mc(ns))

    weights = [_LayerWeights(got, params, i, D, H) for i in range(L)]
    loss_row, dx = _trunk_fwd_bwd(x[0], loss_target[0], H, L, lambda i: weights[i], grads, hosts)
    loss = lax.psum(loss_row[0, 0], ("x", "y", "c"))
    assert not hosts.waiting, sorted(hosts.waiting)

    for tag, make_sib, chip_makes in later:
        plan, Gs, deliver = make_sib()
        deliver(_comm_only(plan, Gs, f"rs_sibling_{tag}"))
        for mk in chip_makes:
            plan, Ps, deliver = mk()
            deliver(_comm_only(plan, Ps, f"rs_chips_{tag}"))
    R2s = {n: [jnp.swapaxes(reduced[(i, n)], 1, 2) if n in COL_SHARDED else reduced[(i, n)] for i in range(L)]
           for n in BIG}

    pieces = []
    for i in range(L):
        for n in SMALL:
            piece = grads[i][n].reshape(-1, LANES)
            pieces.append(jnp.pad(piece, ((0, (-piece.shape[0]) % 8), (0, 0))))
    packed = jnp.concatenate(pieces, axis=0)
    small_all = _comm_only(_GatherPlan([packed]), [packed], "ag_small_grads")[0]
    small_sum = _sum_parts(small_all, "sum_small_grads")
    small_g = {n: [None] * L for n in SMALL}
    off = 0
    for i in range(L):
        for n in SMALL:
            shape = grads[i][n].shape
            cnt = shape[0] * shape[1] // LANES
            g = small_sum[off:off + cnt].reshape(shape)
            off += cnt + (-cnt) % 8
            if n in ("gdn_a_log", "gdn_dt_bias"):
                g = g[:, :H]
            if n in SMALL_SHARDED:
                wloc = params[n].shape[-1]
                g = lax.dynamic_slice_in_dim(g, dev * wloc, wloc, axis=1)
            small_g[n][i] = g

    outs = {}
    for n in NAMES:
        w, m, v = params[n], mom1[n], mom2[n]
        if n in BIG:
            shape3, parts = w.shape, R2s[n]
        else:
            rows, cols = (w.shape[0] * w.shape[1], w.shape[2]) if w.ndim == 3 else w.shape
            shape3, parts = (1, rows, cols), [jnp.stack(small_g[n], axis=0).reshape(1, rows, cols)]
        res = _adamw(w.reshape(shape3), m.reshape(shape3), v.reshape(shape3), parts, "adamw_" + n)
        outs[n] = [r.reshape(w.shape) for r in res]

    result = [loss, dx[None]]
    for k in range(4):
        result += [outs[n][k] for n in NAMES]
    return tuple(result)
```
